```python
import math
import jax
import jax.numpy as jnp
from jax import lax
import numpy as np


D_MODEL = 1024
BATCH = 2
SEQ = 8192
DEPTH = 2

GRID_W = 64
CTX_LEN = 256
HEAD_DIM = 64
N_BRANCH = 4
BRANCH_W = 512
Q_BLOCK = 128
ROPE_BASE = 10000.0
EPS = 1e-6
NEG = -1e30

WIN_HEADS = 8
WIN_KV_HEADS = 2
WIN_RADIUS = 128
WIN_BLOCK = 128
DIF_HEADS = 4
DIF_QK_DIM = 64
DIF_V_DIM = 128
NAT_HEADS = 8
NAT_WIN_ROWS = 8
NAT_WIN_COLS = 16
MLA_HEADS = 8
MLA_NOPE = 64
MLA_ROPE = 32
MLA_V = 64
MLA_Q_LORA = 256
MLA_KV_LORA = 128
N_EXPERTS = 16
N_GROUPS = 4
TOP_K = 2
D_EXPERT = 512
MOE_BLOCK = 256

IN_SIZES = (WIN_HEADS * HEAD_DIM, WIN_KV_HEADS * HEAD_DIM, WIN_KV_HEADS * HEAD_DIM,
            DIF_HEADS * 2 * DIF_QK_DIM, DIF_HEADS * 2 * DIF_QK_DIM, DIF_HEADS * DIF_V_DIM,
            NAT_HEADS * HEAD_DIM, NAT_HEADS * HEAD_DIM, NAT_HEADS * HEAD_DIM,
            MLA_Q_LORA, MLA_KV_LORA + MLA_ROPE, N_BRANCH * D_MODEL)
D_IN = sum(IN_SIZES)

kernel_name = 'hybrid_gated_mixers_grouped_moe_diffusion_block'


def rms_norm(x, g):
    xf = x.astype(jnp.float32)
    y = xf * lax.rsqrt(jnp.mean(xf * xf, axis=-1, keepdims=True) + EPS)
    return (y * g.astype(jnp.float32)).astype(x.dtype)


def rope_cos_sin(n_tok, dim, dtype):
    t = jnp.arange(n_tok)
    rows = (t // GRID_W).astype(jnp.float32)
    cols = (t % GRID_W).astype(jnp.float32)
    quarter = dim // 4
    inv_freq = jnp.exp(-math.log(ROPE_BASE) * jnp.arange(quarter, dtype=jnp.float32) / quarter)
    ang = jnp.concatenate([rows[:, None] * inv_freq, cols[:, None] * inv_freq], axis=-1)
    return jnp.cos(ang).astype(dtype), jnp.sin(ang).astype(dtype)


def apply_rope(x, cos, sin):
    half = x.shape[-1] // 2
    x1, x2 = x[..., :half], x[..., half:]
    c = cos[:, None, :]
    s = sin[:, None, :]
    return jnp.concatenate([x1 * c - x2 * s, x1 * s + x2 * c], axis=-1)


def joint_softmax(parts):
    m = parts[0].max(axis=-1, keepdims=True)
    for p in parts[1:]:
        m = jnp.maximum(m, p.max(axis=-1, keepdims=True))
    ex = [jnp.exp(p - m) for p in parts]
    z = ex[0].sum(axis=-1, keepdims=True)
    for e in ex[1:]:
        z = z + e.sum(axis=-1, keepdims=True)
    return [e / z for e in ex]


def sweep_query_blocks(fn, q):
    b, t = q.shape[:2]
    nb = t // Q_BLOCK
    qb = jnp.moveaxis(q.reshape((b, nb, Q_BLOCK) + q.shape[2:]), 1, 0)
    out = lax.map(fn, qb)
    return jnp.moveaxis(out, 0, 1).reshape((b, t) + out.shape[3:])


def flat_heads(y):
    return y.reshape(y.shape[0], y.shape[1], -1)


def dense_attention(q, k, v):
    scale = q.shape[-1] ** -0.5

    def block(qb):
        logits = jnp.einsum('bqhd,bkhd->bhqk', qb, k).astype(jnp.float32) * scale
        p = jax.nn.softmax(logits, axis=-1)
        return jnp.einsum('bhqk,bkhd->bqhd', p.astype(v.dtype), v)

    return sweep_query_blocks(block, q)


def diff_attention(q, k, v, lam):
    scale = q.shape[-1] ** -0.5

    def block(qb):
        logits = jnp.einsum('bqhid,bkhid->bhiqk', qb, k).astype(jnp.float32) * scale
        p = jax.nn.softmax(logits, axis=-1)
        a = p[:, :, 0] - lam * p[:, :, 1]
        return jnp.einsum('bhqk,bkhd->bqhd', a.astype(v.dtype), v)

    return sweep_query_blocks(block, q)


def window_attention(q, k, v, kc, vc, sink):
    b, s, _, hd = q.shape
    g = WIN_HEADS // WIN_KV_HEADS
    nb = s // WIN_BLOCK
    scale = hd ** -0.5
    qb = q.reshape(b, nb, WIN_BLOCK, WIN_KV_HEADS, g, hd)

    def band(t):
        tp = jnp.pad(t, ((0, 0), (WIN_BLOCK, WIN_BLOCK), (0, 0), (0, 0)))
        tp = tp.reshape(b, nb + 2, WIN_BLOCK, WIN_KV_HEADS, hd)
        return jnp.concatenate([tp[:, :-2], tp[:, 1:-1], tp[:, 2:]], axis=2)

    kb, vb = band(k), band(v)
    qi = jnp.arange(WIN_BLOCK)
    kj = jnp.arange(3 * WIN_BLOCK)
    rel = kj[None, :] - WIN_BLOCK - qi[:, None]
    key_pos = (jnp.arange(nb)[:, None] - 1) * WIN_BLOCK + kj[None, :]
    valid = (jnp.abs(rel) <= WIN_RADIUS)[None] & ((key_pos >= 0) & (key_pos < s))[:, None, :]
    s_band = jnp.einsum('bnqkgd,bnjkd->bnkgqj', qb, kb).astype(jnp.float32) * scale
    s_band = jnp.where(valid[None, :, None, None], s_band, NEG)
    s_ctx = jnp.einsum('bnqkgd,bckd->bnkgqc', qb, kc).astype(jnp.float32) * scale
    s_sink = jnp.broadcast_to(sink.astype(jnp.float32).reshape(1, 1, WIN_KV_HEADS, g, 1, 1),
                              s_ctx.shape[:-1] + (1,))
    p_band, p_ctx, _ = joint_softmax([s_band, s_ctx, s_sink])
    out = (jnp.einsum('bnkgqj,bnjkd->bnqkgd', p_band.astype(v.dtype), vb)
           + jnp.einsum('bnkgqc,bckd->bnqkgd', p_ctx.astype(v.dtype), vc))
    return out.reshape(b, s, WIN_HEADS * hd)


def window_attention_context(qc, kc, vc, sink):
    b, n, _, hd = qc.shape
    g = WIN_HEADS // WIN_KV_HEADS
    scale = hd ** -0.5
    qg = qc.reshape(b, n, WIN_KV_HEADS, g, hd)
    sc = jnp.einsum('bqkgd,bckd->bkgqc', qg, kc).astype(jnp.float32) * scale
    s_sink = jnp.broadcast_to(sink.astype(jnp.float32).reshape(1, WIN_KV_HEADS, g, 1, 1), sc.shape[:-1] + (1,))
    p, _ = joint_softmax([sc, s_sink])
    out = jnp.einsum('bkgqc,bckd->bqkgd', p.astype(vc.dtype), vc)
    return out.reshape(b, n, WIN_HEADS * hd)


def neighbourhood_attention(q, k, v, kc, vc, rpb):
    b, s, h, hd = q.shape
    rows = s // GRID_W
    wr = min(NAT_WIN_ROWS, rows)
    wc = NAT_WIN_COLS
    scale = hd ** -0.5
    r = jnp.arange(rows)
    row_idx = jnp.clip(r - wr // 2, 0, rows - wr)[:, None] + jnp.arange(wr)[None, :]
    col = jnp.arange(GRID_W)
    col_start = jnp.clip(col - wc // 2, 0, GRID_W - wc)
    col_ok = (col[None, :] >= col_start[:, None]) & (col[None, :] < col_start[:, None] + wc)
    qg = q.reshape(b, rows, GRID_W, h, hd)
    kg = k.reshape(b, rows, GRID_W, h, hd)[:, row_idx]
    vg = v.reshape(b, rows, GRID_W, h, hd)[:, row_idx]
    d_row = row_idx - r[:, None] + (NAT_WIN_ROWS - 1)
    d_col = jnp.clip(col[None, :] - col[:, None] + (wc - 1), 0, 2 * wc - 2)
    bias = rpb[:, d_row[:, None, :, None], d_col[None, :, None, :]].astype(jnp.float32)
    s_nb = jnp.einsum('brqhd,brakhd->bhrqak', qg, kg).astype(jnp.float32) * scale + bias[None]
    s_nb = jnp.where(col_ok[:, None, :], s_nb, NEG).reshape(b, h, rows, GRID_W, wr * GRID_W)
    s_cx = jnp.einsum('brqhd,bchd->bhrqc', qg, kc).astype(jnp.float32) * scale
    p_nb, p_cx = joint_softmax([s_nb, s_cx])
    p_nb = p_nb.reshape(b, h, rows, GRID_W, wr, GRID_W).astype(v.dtype)
    out = (jnp.einsum('bhrqak,brakhd->brqhd', p_nb, vg)
           + jnp.einsum('bhrqc,bchd->brqhd', p_cx.astype(v.dtype), vc))
    return out.reshape(b, s, h * hd)


def project_heads(proj, rope, win_q_norm, win_k_norm, dif_q_norm, dif_k_norm, nat_q_norm, nat_k_norm,
                  mla_q_a_norm, mla_wq_b, mla_kv_a_norm, mla_wkv_b, mla_q_norm, mla_k_norm):
    split_at = np.cumsum(IN_SIZES)[:-1].tolist()
    (wq, wk, wv, dq, dk, dv, nq, nk, nv, mqa, mkva, gates) = jnp.split(proj, split_at, axis=-1)
    b, t = proj.shape[:2]
    wq = rms_norm(wq.reshape(b, t, WIN_HEADS, HEAD_DIM), win_q_norm)
    wk = rms_norm(wk.reshape(b, t, WIN_KV_HEADS, HEAD_DIM), win_k_norm)
    wv = wv.reshape(b, t, WIN_KV_HEADS, HEAD_DIM)
    dq = rms_norm(dq.reshape(b, t, 2 * DIF_HEADS, DIF_QK_DIM), dif_q_norm)
    dk = rms_norm(dk.reshape(b, t, 2 * DIF_HEADS, DIF_QK_DIM), dif_k_norm)
    dv = dv.reshape(b, t, DIF_HEADS, DIF_V_DIM)
    nq = rms_norm(nq.reshape(b, t, NAT_HEADS, HEAD_DIM), nat_q_norm)
    nk = rms_norm(nk.reshape(b, t, NAT_HEADS, HEAD_DIM), nat_k_norm)
    nv = nv.reshape(b, t, NAT_HEADS, HEAD_DIM)
    mq = (rms_norm(mqa, mla_q_a_norm) @ mla_wq_b).reshape(b, t, MLA_HEADS, MLA_NOPE + MLA_ROPE)
    ckv, k_rope = mkva[..., :MLA_KV_LORA], mkva[..., MLA_KV_LORA:]
    kv = (rms_norm(ckv, mla_kv_a_norm) @ mla_wkv_b).reshape(b, t, MLA_HEADS, MLA_NOPE + MLA_V)
    mk = jnp.concatenate([kv[..., :MLA_NOPE],
                          jnp.broadcast_to(k_rope[:, :, None, :], (b, t, MLA_HEADS, MLA_ROPE))], axis=-1)
    mv = kv[..., MLA_NOPE:]
    mq = rms_norm(mq, mla_q_norm)
    mk = rms_norm(mk, mla_k_norm)
    if rope is not None:
        cos_h, sin_h, cos_m, sin_m = rope
        wq = apply_rope(wq, cos_h, sin_h)
        wk = apply_rope(wk, cos_h, sin_h)
        dq = apply_rope(dq, cos_h, sin_h)
        dk = apply_rope(dk, cos_h, sin_h)
        mq = jnp.concatenate([mq[..., :MLA_NOPE], apply_rope(mq[..., MLA_NOPE:], cos_m, sin_m)], axis=-1)
        mk = jnp.concatenate([mk[..., :MLA_NOPE], apply_rope(mk[..., MLA_NOPE:], cos_m, sin_m)], axis=-1)
    dq = dq.reshape(b, t, DIF_HEADS, 2, DIF_QK_DIM)
    dk = dk.reshape(b, t, DIF_HEADS, 2, DIF_QK_DIM)
    return (wq, wk, wv, dq, dk, dv, nq, nk, nv, mq, mk, mv, gates)


def merge_branches(ys, gates, w_branch, w_out):
    b, t = gates.shape[:2]
    yb = jnp.einsum('btnw,nwd->btnd', jnp.stack(ys, axis=2), w_branch)
    g = jax.nn.sigmoid(gates.reshape(b, t, N_BRANCH, D_MODEL))
    return (g * yb).sum(axis=2) @ w_out


def moe(tok, router_w, router_b, w1, w3, w2):
    n, d = tok.shape
    per_group = N_EXPERTS // N_GROUPS
    scores = jax.nn.sigmoid((tok @ router_w).astype(jnp.float32))
    biased = scores + router_b.astype(jnp.float32)
    group_score = lax.top_k(biased.reshape(n, N_GROUPS, per_group), 2)[0].sum(-1)
    group = jnp.argmax(group_score, axis=-1)
    in_group = (jnp.arange(N_EXPERTS) // per_group)[None, :] == group[:, None]
    _, idx = lax.top_k(jnp.where(in_group, biased, -jnp.inf), TOP_K)
    wts = jnp.take_along_axis(scores, idx, axis=-1)
    wts = wts / wts.sum(-1, keepdims=True)
    flat_e = idx.reshape(-1).astype(jnp.int32)
    flat_w = wts.reshape(-1)
    flat_t = jnp.repeat(jnp.arange(n, dtype=jnp.int32), TOP_K)
    order = jnp.argsort(flat_e)
    se, st, sw = flat_e[order], flat_t[order], flat_w[order]
    counts = jnp.bincount(flat_e, length=N_EXPERTS)
    padded = (counts + MOE_BLOCK - 1) // MOE_BLOCK * MOE_BLOCK
    start = jnp.cumsum(counts) - counts
    pad_end = jnp.cumsum(padded)
    pad_start = pad_end - padded
    dest = pad_start[se] + jnp.arange(n * TOP_K) - start[se]
    n_blk = (n * TOP_K + N_EXPERTS * (MOE_BLOCK - 1) + MOE_BLOCK - 1) // MOE_BLOCK
    cap = n_blk * MOE_BLOCK
    buf_t = jnp.full((cap,), n, jnp.int32).at[dest].set(st)
    buf_w = jnp.zeros((cap,), tok.dtype).at[dest].set(sw.astype(tok.dtype))
    blk_e = jnp.minimum(jnp.searchsorted(pad_end, jnp.arange(n_blk) * MOE_BLOCK, side='right'), N_EXPERTS - 1)
    xs = jnp.concatenate([tok, jnp.zeros((1, d), tok.dtype)], axis=0)[buf_t].reshape(n_blk, MOE_BLOCK, d)

    def expert_block(args):
        xb, e = args
        return (jax.nn.silu(xb @ w1[e]) * (xb @ w3[e])) @ w2[e]

    ys = lax.map(expert_block, (xs, blk_e)).reshape(cap, d) * buf_w[:, None]
    return jnp.zeros((n + 1, d), tok.dtype).at[buf_t].add(ys)[:n]


def trunk_layer(x, xc, c, c_ctx, rope, lam_init, want_ctx,
                ada_w, ada_b, norm1_g, norm2_g, w_in,
                win_q_norm, win_k_norm, win_sink,
                dif_q_norm, dif_k_norm, dif_lambda, dif_subln,
                nat_q_norm, nat_k_norm, nat_rpb,
                mla_q_a_norm, mla_wq_b, mla_kv_a_norm, mla_wkv_b, mla_q_norm, mla_k_norm,
                w_branch, w_out, router_w, router_b, moe_w1, moe_w3, moe_w2):
    b, s, d = x.shape
    mod = jax.nn.silu(c) @ ada_w + ada_b
    mod_c = jax.nn.silu(c_ctx) @ ada_w + ada_b
    sh1, sc1, g1, sh2, sc2, g2 = [m[:, None, :] for m in jnp.split(mod, 6, axis=-1)]
    sh1c, sc1c, g1c, sh2c, sc2c, g2c = jnp.split(mod_c, 6, axis=-1)

    h = rms_norm(x, norm1_g) * (1.0 + sc1) + sh1
    hc = rms_norm(xc, norm1_g) * (1.0 + sc1c) + sh1c
    head_params = (win_q_norm, win_k_norm, dif_q_norm, dif_k_norm, nat_q_norm, nat_k_norm,
                   mla_q_a_norm, mla_wq_b, mla_kv_a_norm, mla_wkv_b, mla_q_norm, mla_k_norm)
    (wq, wk, wv, dq, dk, dv, nq, nk, nv, mq, mk, mv, gates) = project_heads(h @ w_in, rope, *head_params)
    (wqc, wkc, wvc, dqc, dkc, dvc, nqc, nkc, nvc, mqc, mkc, mvc, gates_c) = project_heads(hc @ w_in, None, *head_params)

    lam_f = dif_lambda.astype(jnp.float32)
    lam = jnp.exp(jnp.sum(lam_f[0] * lam_f[1])) - jnp.exp(jnp.sum(lam_f[2] * lam_f[3])) + lam_init

    def dif_out(y):
        return flat_heads(rms_norm(y, dif_subln) * (1.0 - lam_init))

    ys = [
        window_attention(wq, wk, wv, wkc, wvc, win_sink),
        dif_out(diff_attention(dq, jnp.concatenate([dkc, dk], axis=1), jnp.concatenate([dvc, dv], axis=1), lam)),
        neighbourhood_attention(nq, nk, nv, nkc, nvc, nat_rpb),
        flat_heads(dense_attention(mq, jnp.concatenate([mkc, mk], axis=1), jnp.concatenate([mvc, mv], axis=1))),
    ]
    x = x + g1 * merge_branches(ys, gates, w_branch, w_out)
    h2 = rms_norm(x, norm2_g) * (1.0 + sc2) + sh2
    tokens = h2.reshape(b * s, d)
    if want_ctx:
        ysc = [
            window_attention_context(wqc, wkc, wvc, win_sink),
            dif_out(diff_attention(dqc, dkc, dvc, lam)),
            flat_heads(dense_attention(nqc, nkc, nvc)),
            flat_heads(dense_attention(mqc, mkc, mvc)),
        ]
        xc = xc + g1c * merge_branches(ysc, gates_c, w_branch, w_out)
        h2c = rms_norm(xc, norm2_g) * (1.0 + sc2c) + sh2c
        tokens = jnp.concatenate([tokens, h2c.reshape(-1, d)], axis=0)
    f = moe(tokens, router_w, router_b, moe_w1, moe_w3, moe_w2)
    x = x + g2 * f[: b * s].reshape(b, s, d)
    if want_ctx:
        xc = xc + g2c * f[b * s:].reshape(xc.shape)
    return x, xc


def _normal(k, shape, std):
    return jax.random.normal(k, shape, jnp.float32) * std


def setup_inputs(seed: int = 0) -> dict:
    key = jax.random.key(seed)
    k = jax.random.split(key, 32)
    L, D = DEPTH, D_MODEL
    return {
        'x': _normal(k[0], (BATCH, SEQ, D), 1.0),
        'c': _normal(k[1], (BATCH, D), 1.0),
        'ctx': _normal(k[2], (BATCH, CTX_LEN, D), 1.0),
        'c_ctx': _normal(k[3], (D,), 1.0),
        'ada_w': _normal(k[4], (L, D, 6 * D), 0.5 * D ** -0.5),
        'ada_b': _normal(k[5], (L, 6 * D), 0.02),
        'norm1_g': 1.0 + _normal(k[6], (L, D), 0.05),
        'norm2_g': 1.0 + _normal(k[7], (L, D), 0.05),
        'w_in': _normal(k[8], (L, D, D_IN), D ** -0.5),
        'win_q_norm': 1.0 + _normal(k[9], (L, HEAD_DIM), 0.05),
        'win_k_norm': 1.0 + _normal(k[10], (L, HEAD_DIM), 0.05),
        'win_sink': _normal(k[11], (L, WIN_HEADS), 0.5),
        'dif_q_norm': 1.0 + _normal(k[12], (L, DIF_QK_DIM), 0.05),
        'dif_k_norm': 1.0 + _normal(k[13], (L, DIF_QK_DIM), 0.05),
        'dif_lambda': _normal(k[14], (L, 4, DIF_QK_DIM), 0.1),
        'dif_subln': 1.0 + _normal(k[15], (L, DIF_V_DIM), 0.05),
        'nat_q_norm': 1.0 + _normal(k[16], (L, HEAD_DIM), 0.05),
        'nat_k_norm': 1.0 + _normal(k[17], (L, HEAD_DIM), 0.05),
        'nat_rpb': _normal(k[18], (L, NAT_HEADS, 2 * NAT_WIN_ROWS - 1, 2 * NAT_WIN_COLS - 1), 0.1),
        'mla_q_a_norm': 1.0 + _normal(k[19], (L, MLA_Q_LORA), 0.05),
        'mla_wq_b': _normal(k[20], (L, MLA_Q_LORA, MLA_HEADS * (MLA_NOPE + MLA_ROPE)), MLA_Q_LORA ** -0.5),
        'mla_kv_a_norm': 1.0 + _normal(k[21], (L, MLA_KV_LORA), 0.05),
        'mla_wkv_b': _normal(k[22], (L, MLA_KV_LORA, MLA_HEADS * (MLA_NOPE + MLA_V)), MLA_KV_LORA ** -0.5),
        'mla_q_norm': 1.0 + _normal(k[23], (L, MLA_NOPE + MLA_ROPE), 0.05),
        'mla_k_norm': 1.0 + _normal(k[24], (L, MLA_NOPE + MLA_ROPE), 0.05),
        'w_branch': _normal(k[25], (L, N_BRANCH, BRANCH_W, D), BRANCH_W ** -0.5),
        'w_out': _normal(k[26], (L, D, D), D ** -0.5),
        'router_w': _normal(k[27], (D, N_EXPERTS), D ** -0.5),
        'router_b': _normal(k[28], (N_EXPERTS,), 0.01),
        'moe_w1': _normal(k[29], (L, N_EXPERTS, D, D_EXPERT), D ** -0.5),
        'moe_w3': _normal(k[30], (L, N_EXPERTS, D, D_EXPERT), D ** -0.5),
        'moe_w2': _normal(k[31], (L, N_EXPERTS, D_EXPERT, D), D_EXPERT ** -0.5),
    }


def reference(x, c, ctx, c_ctx, ada_w, ada_b, norm1_g, norm2_g, w_in,
              win_q_norm, win_k_norm, win_sink,
              dif_q_norm, dif_k_norm, dif_lambda, dif_subln,
              nat_q_norm, nat_k_norm, nat_rpb,
              mla_q_a_norm, mla_wq_b, mla_kv_a_norm, mla_wkv_b, mla_q_norm, mla_k_norm,
              w_branch, w_out, router_w, router_b, moe_w1, moe_w3, moe_w2):
    s = x.shape[1]
    rope = rope_cos_sin(s, HEAD_DIM, x.dtype) + rope_cos_sin(s, MLA_ROPE, x.dtype)
    xc = ctx
    for l in range(DEPTH):
        lam_init = 0.8 - 0.6 * math.exp(-0.3 * l)
        x, xc = trunk_layer(
            x, xc, c, c_ctx, rope, lam_init, l < DEPTH - 1,
            ada_w[l], ada_b[l], norm1_g[l], norm2_g[l], w_in[l],
            win_q_norm[l], win_k_norm[l], win_sink[l],
            dif_q_norm[l], dif_k_norm[l], dif_lambda[l], dif_subln[l],
            nat_q_norm[l], nat_k_norm[l], nat_rpb[l],
            mla_q_a_norm[l], mla_wq_b[l], mla_kv_a_norm[l], mla_wkv_b[l], mla_q_norm[l], mla_k_norm[l],
            w_branch[l], w_out[l], router_w, router_b, moe_w1[l], moe_w3[l], moe_w2[l])
    return x
```

```python
import functools
import math

import jax
import jax.numpy as jnp
import numpy as np
from jax import lax
from jax.experimental import pallas as pl
from jax.experimental.pallas import tpu as pltpu

F32 = jnp.float32
BF = jnp.bfloat16

GRID_W = 64
HEAD_DIM = 64
N_BRANCH = 4
BRANCH_W = 512
ROPE_BASE = 10000.0
EPS = 1e-6
NEG = -1e30
WIN_HEADS, WIN_KV_HEADS, WIN_RADIUS = 8, 2, 128
DIF_HEADS, DIF_QK_DIM, DIF_V_DIM = 4, 64, 128
NAT_HEADS, NAT_WIN_ROWS, NAT_WIN_COLS = 8, 8, 16
MLA_HEADS, MLA_NOPE, MLA_ROPE, MLA_V, MLA_Q_LORA, MLA_KV_LORA = 8, 64, 32, 64, 256, 128
MLA_QK = MLA_NOPE + MLA_ROPE
N_EXPERTS, N_GROUPS, TOP_K, D_EXPERT, MOE_BLOCK = 16, 4, 2, 512, 256

LANES = 128
TM = 512
WIN_TQ = 256
NAT_ROWS_PER_STEP = 4
NAT_KEY_ROWS = 12
DENSE_TQ = 512
DENSE_TK = 1024
VMEM_LIMIT = 48 * 1024 * 1024

A_GATES, A_DV, A_NV, A_MQA, A_MKVA, A_WV = 0, 4096, 4608, 5120, 5376, 5632
A_COLS = 5888
Q_COLS = 1536
KT_DK, KT_NK, KT_WK = 0, 512, 1024
KT_ROWS = 1280
B_ROWS = Q_COLS + KT_ROWS


def _cparams(sem, vmem=VMEM_LIMIT):
    return pltpu.CompilerParams(dimension_semantics=sem, vmem_limit_bytes=vmem)


def _nt_dot(a, b):
    return lax.dot_general(a, b, (((1,), (1,)), ((), ())), preferred_element_type=F32)


def _norm_mod(x, g, sc, sh):
    ms = jnp.mean(x * x, axis=-1, keepdims=True)
    return (x * lax.rsqrt(ms + EPS) * g) * (1.0 + sc) + sh


def _lane_tile(a, n):
    reps = n // a.shape[1]
    return a if reps == 1 else jnp.concatenate([a] * reps, axis=1)


def _mod_kernel(c_ref, w_ref, b_ref, o_ref):
    cc = c_ref[...]
    a = cc * jax.nn.sigmoid(cc)
    o_ref[0] = jnp.dot(a, w_ref[0], preferred_element_type=F32,
                       precision=lax.Precision.HIGHEST) + b_ref[0]


def _modulation(cc, ada_w, ada_b):
    n_layers, d, d6 = ada_w.shape
    tn = 1536
    return pl.pallas_call(
        _mod_kernel,
        out_shape=jax.ShapeDtypeStruct((n_layers, 8, d6), F32),
        grid=(n_layers, d6 // tn),
        in_specs=[pl.BlockSpec((8, d), lambda l, j: (0, 0)),
                  pl.BlockSpec((1, d, tn), lambda l, j: (l, 0, j)),
                  pl.BlockSpec((1, 1, tn), lambda l, j: (l, 0, j))],
        out_specs=pl.BlockSpec((1, 8, tn), lambda l, j: (l, 0, j)),
        compiler_params=_cparams(("parallel", "parallel")),
        name="adaln_mod",
    )(cc, ada_w, ada_b.reshape(n_layers, 1, d6))


def _in_a_kernel(x_ref, mod_ref, g_ref, w_ref, o_ref, h_sc):
    @pl.when(pl.program_id(1) == 0)
    def _():
        m = mod_ref[0]
        h_sc[...] = _norm_mod(x_ref[...], g_ref[...], m[1:2], m[0:1]).astype(BF)

    o_ref[...] = jnp.dot(h_sc[...], w_ref[...], preferred_element_type=F32).astype(BF)


def _in_proj_a(xa, mod3, g1, w_a, group_of_tile):
    n, d = xa.shape
    tn = A_COLS // 2
    return pl.pallas_call(
        _in_a_kernel,
        out_shape=jax.ShapeDtypeStruct((n, A_COLS), BF),
        grid=(n // TM, A_COLS // tn),
        in_specs=[pl.BlockSpec((TM, d), lambda i, j: (i, 0)),
                  pl.BlockSpec((1, 8, d), lambda i, j: (group_of_tile(i), 0, 0)),
                  pl.BlockSpec((1, d), lambda i, j: (0, 0)),
                  pl.BlockSpec((d, tn), lambda i, j: (0, j))],
        out_specs=pl.BlockSpec((TM, tn), lambda i, j: (i, j)),
        scratch_shapes=[pltpu.VMEM((TM, d), BF)],
        compiler_params=_cparams(("parallel", "arbitrary")),
        name="in_proj_tok",
    )(xa, mod3, g1, w_a)


def _head_norm_rope(x, g, cos, sin, rope):
    ss = jnp.sum(x * x, axis=0, keepdims=True)
    y = x * lax.rsqrt(ss * (1.0 / HEAD_DIM) + EPS) * g
    if not rope:
        return y
    half = HEAD_DIM // 2
    y1, y2 = y[:half], y[half:]
    return jnp.concatenate([y1 * cos - y2 * sin, y1 * sin + y2 * cos], axis=0)


def _in_b_kernel(x_ref, mod_ref, g_ref, wt_ref, gain_ref, cos_ref, sin_ref, q_ref, kt_ref, acc_sc):
    m = mod_ref[0]
    h = _norm_mod(x_ref[...], g_ref[...], m[1:2], m[0:1]).astype(BF)
    acc_sc[...] = _nt_dot(wt_ref[...], h)
    tm = h.shape[0]
    cos = cos_ref[...]
    sin = sin_ref[...]

    def pair(r0, rope):
        hs = []
        for e in range(2):
            r = r0 + e * HEAD_DIM
            g = _lane_tile(gain_ref[r:r + HEAD_DIM, :], tm)
            hs.append(_head_norm_rope(acc_sc[r:r + HEAD_DIM, :], g, cos, sin, rope))
        return jnp.concatenate(hs, axis=0)

    for p in range(Q_COLS // LANES):
        y = pair(p * LANES, rope=p < 8)
        q_ref[:, p * LANES:(p + 1) * LANES] = y.T.astype(BF)
    for p in range(KT_ROWS // LANES):
        y = pair(Q_COLS + p * LANES, rope=not (4 <= p < 8))
        kt_ref[p * LANES:(p + 1) * LANES, :] = y.astype(BF)


def _in_proj_b(xa, mod3, g1, w_bt, gain_b, cos_t, sin_t, group_of_tile, pos_of_tile):
    n, d = xa.shape
    return pl.pallas_call(
        _in_b_kernel,
        out_shape=(jax.ShapeDtypeStruct((n, Q_COLS), BF),
                   jax.ShapeDtypeStruct((KT_ROWS, n), BF)),
        grid=(n // TM,),
        in_specs=[pl.BlockSpec((TM, d), lambda i: (i, 0)),
                  pl.BlockSpec((1, 8, d), lambda i: (group_of_tile(i), 0, 0)),
                  pl.BlockSpec((1, d), lambda i: (0, 0)),
                  pl.BlockSpec((B_ROWS, d), lambda i: (0, 0)),
                  pl.BlockSpec((B_ROWS, LANES), lambda i: (0, 0)),
                  pl.BlockSpec((HEAD_DIM // 2, TM), lambda i: (0, pos_of_tile(i))),
                  pl.BlockSpec((HEAD_DIM // 2, TM), lambda i: (0, pos_of_tile(i)))],
        out_specs=(pl.BlockSpec((TM, Q_COLS), lambda i: (i, 0)),
                   pl.BlockSpec((KT_ROWS, TM), lambda i: (0, i))),
        scratch_shapes=[pltpu.VMEM((B_ROWS, TM), F32)],
        compiler_params=_cparams(("parallel",)),
        name="in_proj_heads",
    )(xa, mod3, g1, w_bt, gain_b, cos_t, sin_t)


def _mla_kernel(qa_ref, kva_ref, gqa_ref, gkva_ref, wqt_ref, wkt_ref, wv_ref, gq_ref, gk_ref,
                cos_ref, sin_ref, mq_ref, mkt_ref, mv_ref):
    tm = qa_ref.shape[0]
    cos = cos_ref[...]
    sin = sin_ref[...]
    rh = MLA_ROPE // 2

    def rms_rows(x, g):
        ms = jnp.mean(x * x, axis=-1, keepdims=True)
        return x * lax.rsqrt(ms + EPS) * g

    def rope_rows(x):
        x1, x2 = x[:rh], x[rh:]
        return jnp.concatenate([x1 * cos - x2 * sin, x1 * sin + x2 * cos], axis=0)

    qa = rms_rows(qa_ref[...].astype(F32), gqa_ref[...]).astype(BF)
    qt = _nt_dot(wqt_ref[...], qa)
    kva = kva_ref[...].astype(F32)
    cn = rms_rows(kva[:, :MLA_KV_LORA], gkva_ref[...]).astype(BF)
    knt = _nt_dot(wkt_ref[...], cn)
    mv_ref[...] = jnp.dot(cn, wv_ref[...], preferred_element_type=F32).astype(BF)
    krope = kva[:, MLA_KV_LORA:].T[:MLA_ROPE]
    kr_ss = jnp.sum(krope * krope, axis=0, keepdims=True)
    gq = _lane_tile(gq_ref[...], tm)
    gk = _lane_tile(gk_ref[...], tm)
    zpad = jnp.zeros((LANES - MLA_QK, tm), F32)
    for hd in range(MLA_HEADS):
        x = qt[hd * MLA_QK:(hd + 1) * MLA_QK]
        ss = jnp.sum(x * x, axis=0, keepdims=True)
        y = x * lax.rsqrt(ss * (1.0 / MLA_QK) + EPS) * gq
        y = jnp.concatenate([y[:MLA_NOPE], rope_rows(y[MLA_NOPE:]), zpad], axis=0)
        mq_ref[:, hd * LANES:(hd + 1) * LANES] = y.T.astype(BF)
        kn = knt[hd * MLA_NOPE:(hd + 1) * MLA_NOPE]
        ss = jnp.sum(kn * kn, axis=0, keepdims=True) + kr_ss
        r = lax.rsqrt(ss * (1.0 / MLA_QK) + EPS)
        yk = jnp.concatenate([kn * r * gk[:MLA_NOPE], rope_rows(krope * r * gk[MLA_NOPE:]), zpad], axis=0)
        mkt_ref[hd * LANES:(hd + 1) * LANES, :] = yk.astype(BF)


def _mla_prep(tok_a, gqa, gkva, wq_t, wk_t, wv, gq, gk, cos_t, sin_t, pos_of_tile):
    n = tok_a.shape[0]
    hw = MLA_HEADS * LANES
    return pl.pallas_call(
        _mla_kernel,
        out_shape=(jax.ShapeDtypeStruct((n, hw), BF),
                   jax.ShapeDtypeStruct((hw, n), BF),
                   jax.ShapeDtypeStruct((n, MLA_HEADS * MLA_V), BF)),
        grid=(n // TM,),
        in_specs=[pl.BlockSpec((TM, 256), lambda i: (i, A_MQA // 256)),
                  pl.BlockSpec((TM, 256), lambda i: (i, A_MKVA // 256)),
                  pl.BlockSpec((1, MLA_Q_LORA), lambda i: (0, 0)),
                  pl.BlockSpec((1, MLA_KV_LORA), lambda i: (0, 0)),
                  pl.BlockSpec(wq_t.shape, lambda i: (0, 0)),
                  pl.BlockSpec(wk_t.shape, lambda i: (0, 0)),
                  pl.BlockSpec(wv.shape, lambda i: (0, 0)),
                  pl.BlockSpec((MLA_QK, LANES), lambda i: (0, 0)),
                  pl.BlockSpec((MLA_QK, LANES), lambda i: (0, 0)),
                  pl.BlockSpec((MLA_ROPE // 2, TM), lambda i: (0, pos_of_tile(i))),
                  pl.BlockSpec((MLA_ROPE // 2, TM), lambda i: (0, pos_of_tile(i)))],
        out_specs=(pl.BlockSpec((TM, hw), lambda i: (i, 0)),
                   pl.BlockSpec((hw, TM), lambda i: (0, i)),
                   pl.BlockSpec((TM, MLA_HEADS * MLA_V), lambda i: (i, 0))),
        compiler_params=_cparams(("parallel",)),
        name="mla_prep",
    )(tok_a, tok_a, gqa, gkva, wq_t, wk_t, wv, gq, gk, cos_t, sin_t)


def _half_mask(shape):
    return lax.broadcasted_iota(jnp.int32, shape, 1) < (LANES // 2)


def _select_half(q, e, lo_mask):
    zero = jnp.zeros_like(q)
    return jnp.where(lo_mask, q, zero) if e == 0 else jnp.where(lo_mask, zero, q)


def _local_softmax_out(parts, extra_logit):
    m = parts[0][0].max(axis=-1, keepdims=True)
    for s, _ in parts[1:]:
        m = jnp.maximum(m, s.max(axis=-1, keepdims=True))
    if extra_logit is not None:
        m = jnp.maximum(m, extra_logit)
    z = None
    o = None
    for s, v in parts:
        p = jnp.exp(s - m)
        zs = p.sum(axis=-1, keepdims=True)
        os_ = jnp.dot(p.astype(BF), v, preferred_element_type=F32)
        z = zs if z is None else z + zs
        o = os_ if o is None else o + os_
    if extra_logit is not None:
        z = z + jnp.exp(extra_logit - m)
    return o / z


def _win_kernel(sink_ref, q_ref, *refs, band):
    if band:
        k0, k1, k2, k3, v0, v1, v2, v3, kc_ref, vc_ref, mask_ref, o_ref = refs
        kb = jnp.concatenate([k0[...], k1[...], k2[...], k3[...]], axis=1)
        vb = jnp.concatenate([v0[...], v1[...], v2[...], v3[...]], axis=0)
        mask = mask_ref[0]
    else:
        kc_ref, vc_ref, o_ref = refs
    q = q_ref[...]
    lo = _half_mask((q.shape[0], LANES))
    group = WIN_HEADS // WIN_KV_HEADS
    for j in range(WIN_HEADS // 2):
        qp = q[:, j * LANES:(j + 1) * LANES]
        g = (2 * j) // group
        kc = kc_ref[g * LANES:(g + 1) * LANES, :]
        vc = vc_ref[:, g * LANES:(g + 1) * LANES]
        outs = []
        for e in range(2):
            qm = _select_half(qp, e, lo)
            parts = []
            if band:
                s = jnp.dot(qm, kb[g * LANES:(g + 1) * LANES, :], preferred_element_type=F32) + mask
                parts.append((s, vb[:, g * LANES:(g + 1) * LANES]))
            parts.append((jnp.dot(qm, kc, preferred_element_type=F32), vc))
            outs.append(_local_softmax_out(parts, sink_ref[0, 2 * j + e]))
        o_ref[:, j * LANES:(j + 1) * LANES] = jnp.where(lo, outs[0], outs[1]).astype(BF)


def _win_attention(sink, q_tok, kt, tok_a, mask_tbl, n_batch, seq, n_ctx, latent):
    ctx_blk = (n_batch * seq) // n_ctx
    kc_spec = lambda f: pl.BlockSpec((2 * LANES, n_ctx), f)
    vc_spec = lambda f: pl.BlockSpec((n_ctx, 2 * LANES), f)
    smem = pl.BlockSpec(memory_space=pltpu.SMEM)
    n_out = n_batch * (seq if latent else n_ctx)
    out_shape = jax.ShapeDtypeStruct((n_out, WIN_HEADS * HEAD_DIM), BF)
    if not latent:
        return pl.pallas_call(
            functools.partial(_win_kernel, band=False),
            out_shape=out_shape,
            grid=(n_batch,),
            in_specs=[smem,
                      pl.BlockSpec((n_ctx, 512), lambda b: (ctx_blk + b, 0)),
                      kc_spec(lambda b: (KT_WK // 256, ctx_blk + b)),
                      vc_spec(lambda b: (ctx_blk + b, A_WV // 256))],
            out_specs=pl.BlockSpec((n_ctx, 512), lambda b: (b, 0)),
            compiler_params=_cparams(("parallel",)),
            name="win_attn_ctx",
        )(sink, q_tok, kt, tok_a)
    nq = seq // WIN_TQ
    nkb = seq // LANES

    def kidx(j):
        return lambda b, i: (KT_WK // 256, b * nkb + jnp.clip(2 * i - 1 + j, 0, nkb - 1))

    def vidx(j):
        return lambda b, i: (b * nkb + jnp.clip(2 * i - 1 + j, 0, nkb - 1), A_WV // 256)

    def variant(b, i):
        return (jnp.where(i == 0, 0, jnp.where(i == nq - 1, 2, 1)), 0, 0)

    in_specs = ([smem, pl.BlockSpec((WIN_TQ, 512), lambda b, i: (b * nq + i, 0))]
                + [pl.BlockSpec((2 * LANES, LANES), kidx(j)) for j in range(4)]
                + [pl.BlockSpec((LANES, 2 * LANES), vidx(j)) for j in range(4)]
                + [kc_spec(lambda b, i: (KT_WK // 256, ctx_blk + b)),
                   vc_spec(lambda b, i: (ctx_blk + b, A_WV // 256)),
                   pl.BlockSpec((1, WIN_TQ, 4 * LANES), variant)])
    return pl.pallas_call(
        functools.partial(_win_kernel, band=True),
        out_shape=out_shape,
        grid=(n_batch, nq),
        in_specs=in_specs,
        out_specs=pl.BlockSpec((WIN_TQ, 512), lambda b, i: (b * nq + i, 0)),
        compiler_params=_cparams(("parallel", "parallel")),
        name="win_attn",
    )(sink, q_tok, kt, kt, kt, kt, tok_a, tok_a, tok_a, tok_a, kt, tok_a, mask_tbl)


def _nat_kernel(q_ref, k0, k1, k2, v0, v1, v2, kc_ref, vc_ref, bias_ref, o_ref):
    q = q_ref[...]
    kb = jnp.concatenate([k0[...], k1[...], k2[...]], axis=1)
    vb = jnp.concatenate([v0[...], v1[...], v2[...]], axis=0)
    lo = _half_mask((q.shape[0], LANES))
    for j in range(NAT_HEADS // 2):
        sl = slice(j * LANES, (j + 1) * LANES)
        qp = q[:, sl]
        outs = []
        for e in range(2):
            qm = _select_half(qp, e, lo)
            s = jnp.dot(qm, kb[sl, :], preferred_element_type=F32) + bias_ref[0, 2 * j + e]
            sc = jnp.dot(qm, kc_ref[sl, :], preferred_element_type=F32)
            outs.append(_local_softmax_out([(s, vb[:, sl]), (sc, vc_ref[:, sl])], None))
        o_ref[:, sl] = jnp.where(lo, outs[0], outs[1]).astype(BF)


def _nat_attention(q_tok, kt, tok_a, bias_tbl, n_batch, seq, n_ctx):
    tq =NAT_ROWS_PER_STEP * GRID_W
    nq = seq // tq
    rows = seq // GRID_W
    ctx_blk = (n_batch * seq) // n_ctx
    q_col = 2
    k_row = KT_NK // 512
    v_col = A_NV // 512

    def wstart(i):
        return jnp.clip(NAT_ROWS_PER_STEP * i - NAT_WIN_ROWS // 2, 0, rows - NAT_KEY_ROWS) // NAT_ROWS_PER_STEP

    def kidx(j):
        return lambda b, i: (k_row, b * nq + wstart(i) + j)

    def vidx(j):
        return lambda b, i: (b * nq + wstart(i) + j, v_col)

    def variant(b, i):
        return (jnp.where(i == 0, 0, jnp.where(i == nq - 1, 2, 1)), 0, 0, 0)

    nk = NAT_KEY_ROWS * GRID_W
    in_specs = ([pl.BlockSpec((tq, 512), lambda b, i: (b * nq + i, q_col))]
                + [pl.BlockSpec((512, tq), kidx(j)) for j in range(3)]
                + [pl.BlockSpec((tq, 512), vidx(j)) for j in range(3)]
                + [pl.BlockSpec((512, n_ctx), lambda b, i: (k_row, ctx_blk + b)),
                   pl.BlockSpec((n_ctx, 512), lambda b, i: (ctx_blk + b, v_col)),
                   pl.BlockSpec((1, NAT_HEADS, tq, nk), variant)])
    return pl.pallas_call(
        _nat_kernel,
        out_shape=jax.ShapeDtypeStruct((n_batch * seq, NAT_HEADS * HEAD_DIM), BF),
        grid=(n_batch, nq),
        in_specs=in_specs,
        out_specs=pl.BlockSpec((tq, 512), lambda b, i: (b * nq + i, 0)),
        compiler_params=_cparams(("parallel", "parallel")),
        name="nat_attn",
    )(q_tok, kt, kt, kt, tok_a, tok_a, tok_a, kt, tok_a, bias_tbl)


def _dense_kernel(lam_ref, q_ref, kc_ref, vc_ref, *refs, n_heads, packed, diff, latent, lam_scale):
    if latent:
        k_ref, v_ref = refs[0], refs[1]
        refs = refs[2:]
    if diff:
        subln_ref, o_ref, qm_sc, m_sc, l_sc, acc_sc = refs
    else:
        o_ref, qm_sc, m_sc, l_sc, acc_sc = refs
    kt_step = pl.program_id(2) if latent else 0
    tq = q_ref.shape[0]

    def kv_slices(h):
        blk = h // 2 if packed else h
        ks = slice(blk * LANES, (blk + 1) * LANES)
        if diff:
            vs = ks
        elif packed:
            vs = ks
        else:
            vs = slice((h // 2) * LANES, (h // 2 + 1) * LANES)
        return ks, vs

    def step(h, k_blk, v_blk, first):
        ks, vs = kv_slices(h)
        s = jnp.dot(qm_sc[h], k_blk[ks, :], preferred_element_type=F32)
        smax = s.max(axis=-1, keepdims=True)
        if first:
            m_new = smax
        else:
            m_old = m_sc[h][:, :1]
            m_new = jnp.maximum(m_old, smax)
            alpha = jnp.exp(m_old - m_new)
        p = jnp.exp(s - m_new)
        ps = p.sum(axis=-1, keepdims=True)
        pv = jnp.dot(p.astype(BF), v_blk[:, vs], preferred_element_type=F32)
        if first:
            l_new, acc_new = ps, pv
        else:
            l_new = alpha * l_sc[h][:, :1] + ps
            acc_new = alpha * acc_sc[h] + pv
        m_sc[h] = jnp.broadcast_to(m_new, (tq, LANES))
        l_sc[h] = jnp.broadcast_to(l_new, (tq, LANES))
        acc_sc[h] = acc_new

    @pl.when(kt_step == 0)
    def _():
        q = q_ref[...]
        lo = _half_mask((tq, LANES))
        for h in range(n_heads):
            if packed:
                qp = q[:, (h // 2) * LANES:(h // 2 + 1) * LANES]
                qm_sc[h] = _select_half(qp, h % 2, lo)
            else:
                qm_sc[h] = q[:, h * LANES:(h + 1) * LANES]
        kc = kc_ref[...]
        vc = vc_ref[...]
        for h in range(n_heads):
            step(h, kc, vc, True)

    if latent:
        k_blk = k_ref[...]
        v_blk = v_ref[...]
        for h in range(n_heads):
            step(h, k_blk, v_blk, False)
        last = kt_step == pl.num_programs(2) - 1
    else:
        last = True

    def finish():
        lo = _half_mask((tq, LANES))
        if diff:
            lam = lam_ref[0, 0]
            for hv in range(n_heads // 2):
                y = (acc_sc[2 * hv] / l_sc[2 * hv][:, :1]
                     - lam * (acc_sc[2 * hv + 1] / l_sc[2 * hv + 1][:, :1]))
                ms = jnp.mean(y * y, axis=-1, keepdims=True)
                y = y * lax.rsqrt(ms + EPS) * subln_ref[...] * lam_scale
                o_ref[:, hv * LANES:(hv + 1) * LANES] = y.astype(BF)
        else:
            for hp in range(n_heads // 2):
                o0 = acc_sc[2 * hp] / l_sc[2 * hp][:, :1]
                o1 = acc_sc[2 * hp + 1] / l_sc[2 * hp + 1][:, :1]
                o_ref[:, hp * LANES:(hp + 1) * LANES] = jnp.where(lo, o0, o1).astype(BF)

    if latent:
        pl.when(last)(finish)
    else:
        finish()


def _dense_attention(lam, q_arr, q_col, q_w, k_arr, k_row, k_w, v_arr, v_col, v_w, subln,
                     n_batch, seq, n_ctx, *, n_heads, packed, diff, latent, lam_scale, name):
    n = n_batch * (seq if latent else n_ctx)
    ctx_blk = (n_batch * seq) // n_ctx
    out_w = v_w
    smem = pl.BlockSpec(memory_space=pltpu.SMEM)
    kern = functools.partial(_dense_kernel, n_heads=n_heads, packed=packed, diff=diff,
                             latent=latent, lam_scale=lam_scale)
    tq = DENSE_TQ if latent else n_ctx
    scratch = [pltpu.VMEM((n_heads, tq, LANES), BF),
               pltpu.VMEM((n_heads, tq, LANES), F32),
               pltpu.VMEM((n_heads, tq, LANES), F32),
               pltpu.VMEM((n_heads, tq, LANES), F32)]
    out_shape = jax.ShapeDtypeStruct((n, out_w), BF)
    if latent:
        nq = seq // tq
        nk = seq // DENSE_TK
        grid = (n_batch, nq, nk)
        in_specs = [smem,
                    pl.BlockSpec((tq, q_w), lambda b, i, k: (b * nq + i, q_col)),
                    pl.BlockSpec((k_w, n_ctx), lambda b, i, k: (k_row, ctx_blk + b)),
                    pl.BlockSpec((n_ctx, v_w), lambda b, i, k: (ctx_blk + b, v_col)),
                    pl.BlockSpec((k_w, DENSE_TK), lambda b, i, k: (k_row, b * nk + k)),
                    pl.BlockSpec((DENSE_TK, v_w), lambda b, i, k: (b * nk + k, v_col))]
        args = [lam, q_arr, k_arr, v_arr, k_arr, v_arr]
        if diff:
            in_specs.append(pl.BlockSpec((1, LANES), lambda b, i, k: (0, 0)))
            args.append(subln)
        out_specs = pl.BlockSpec((tq, out_w), lambda b, i, k: (b * nq + i, 0))
        sem = ("parallel", "parallel", "arbitrary")
    else:
        grid = (n_batch,)
        in_specs = [smem,
                    pl.BlockSpec((tq, q_w), lambda b: (ctx_blk + b, q_col)),
                    pl.BlockSpec((k_w, n_ctx), lambda b: (k_row, ctx_blk + b)),
                    pl.BlockSpec((n_ctx, v_w), lambda b: (ctx_blk + b, v_col))]
        args = [lam, q_arr, k_arr, v_arr]
        if diff:
            in_specs.append(pl.BlockSpec((1, LANES), lambda b: (0, 0)))
            args.append(subln)
        out_specs = pl.BlockSpec((tq, out_w), lambda b: (b, 0))
        sem = ("parallel",)
    return pl.pallas_call(
        kern, out_shape=out_shape, grid=grid, in_specs=in_specs, out_specs=out_specs,
        scratch_shapes=scratch, compiler_params=_cparams(sem), name=name,
    )(*args)


def _merge_kernel(x_ref, mod_ref, g2_ref, *refs, n_lat_tiles, has_ctx):
    ys = refs[:N_BRANCH]
    refs = refs[N_BRANCH:]
    if has_ctx:
        ycs = refs[:N_BRANCH]
        refs = refs[N_BRANCH:]
        is_ctx = pl.program_id(0) >= n_lat_tiles
    gts = refs[:N_BRANCH]
    wb_ref, wo_ref, rw_ref, xo_ref, h2_ref, sc_ref = refs[N_BRANCH:]
    m = mod_ref[0]
    mix = None
    for n_ in range(N_BRANCH):
        y = ys[n_][...]
        if has_ctx:
            y = jnp.where(is_ctx, ycs[n_][...], y)
        yb = jnp.dot(y, wb_ref[n_], preferred_element_type=F32)
        t = jax.nn.sigmoid(gts[n_][...].astype(F32)) * yb
        mix = t if mix is None else mix + t
    att = jnp.dot(mix.astype(BF), wo_ref[...], preferred_element_type=F32)
    xn = x_ref[...] + m[2:3] * att
    xo_ref[...] = xn
    h2 = _norm_mod(xn, g2_ref[...], m[4:5], m[3:4])
    h2_ref[...] = h2
    logits = jnp.dot(h2, rw_ref[...], preferred_element_type=F32, precision=lax.Precision.HIGHEST)
    sc_ref[...] = jax.nn.sigmoid(logits)


def _merge(xa, mod3, g2, ys, ys_ctx, tok_a, wb, wo, rw, n_rows, group_of_tile):
    d = xa.shape[1]
    n_lat_tiles = ys[0].shape[0] // TM
    has_ctx = ys_ctx is not None
    row = lambda w, c: pl.BlockSpec((TM, w), lambda i, c=c: (i, c))
    lat_row = pl.BlockSpec((TM, BRANCH_W), lambda i: (jnp.minimum(i, n_lat_tiles - 1), 0))
    in_specs = ([row(d, 0),
                 pl.BlockSpec((1, 8, d), lambda i: (group_of_tile(i), 0, 0)),
                 pl.BlockSpec((1, d), lambda i: (0, 0))]
                + [lat_row for _ in range(N_BRANCH)]
                + ([pl.BlockSpec((TM, BRANCH_W), lambda i: (0, 0)) for _ in range(N_BRANCH)] if has_ctx else [])
                + [row(d, c) for c in range(N_BRANCH)]
                + [pl.BlockSpec(wb.shape, lambda i: (0, 0, 0)),
                   pl.BlockSpec(wo.shape, lambda i: (0, 0)),
                   pl.BlockSpec(rw.shape, lambda i: (0, 0))])
    ys = list(ys) + (list(ys_ctx) if has_ctx else [])
    return pl.pallas_call(
        functools.partial(_merge_kernel, n_lat_tiles=n_lat_tiles, has_ctx=has_ctx),
        out_shape=(jax.ShapeDtypeStruct((n_rows, d), F32),
                   jax.ShapeDtypeStruct((n_rows, d), F32),
                   jax.ShapeDtypeStruct((n_rows, LANES), F32)),
        grid=(n_rows // TM,),
        in_specs=in_specs,
        out_specs=(row(d, 0), row(d, 0), row(LANES, 0)),
        compiler_params=_cparams(("parallel",)),
        name="merge",
    )(xa, mod3, g2, *ys, tok_a, tok_a, tok_a, tok_a, wb, wo, rw)


def _expert_kernel(blk_e_ref, src_ref, dst_ref, h2_hbm, w_ref, w1_ref, w3_ref, w2_ref, out_hbm,
                   xbuf, ybuf, sem_in, sem_out):
    i = pl.program_id(0)
    base = i * MOE_BLOCK

    def gather(r, c):
        t = src_ref[base + r]
        pltpu.make_async_copy(h2_hbm.at[pl.ds(t, 1)], xbuf.at[pl.ds(r, 1)], sem_in).start()
        return c

    lax.fori_loop(0, MOE_BLOCK, gather, 0)
    pltpu.make_async_copy(h2_hbm.at[pl.ds(0, MOE_BLOCK)], xbuf, sem_in).wait()
    xb = xbuf[...].astype(BF)
    a = jnp.dot(xb, w1_ref[0], preferred_element_type=F32)
    b = jnp.dot(xb, w3_ref[0], preferred_element_type=F32)
    hmid = (a * jax.nn.sigmoid(a) * b).astype(BF)
    ybuf[...] = jnp.dot(hmid, w2_ref[0], preferred_element_type=F32) * w_ref[...]

    def scatter(r, c):
        t = dst_ref[base + r]
        pltpu.make_async_copy(ybuf.at[pl.ds(r, 1)], out_hbm.at[pl.ds(t, 1)], sem_out).start()
        return c

    lax.fori_loop(0, MOE_BLOCK, scatter, 0)
    pltpu.make_async_copy(ybuf, out_hbm.at[pl.ds(0, MOE_BLOCK)], sem_out).wait()


def _experts(blk_e, src, dst, h2, wcol, w1, w3, w2):
    cap = src.shape[0]
    d = h2.shape[1]
    n_blk = cap // MOE_BLOCK
    wspec = lambda shp: pl.BlockSpec((1,) + shp, lambda i, be, s, t: (be[i], 0, 0))
    grid_spec = pltpu.PrefetchScalarGridSpec(
        num_scalar_prefetch=3,
        grid=(n_blk,),
        in_specs=[pl.BlockSpec(memory_space=pl.ANY),
                  pl.BlockSpec((MOE_BLOCK, 1), lambda i, be, s, t: (i, 0)),
                  wspec((d, D_EXPERT)), wspec((d, D_EXPERT)), wspec((D_EXPERT, d))],
        out_specs=pl.BlockSpec(memory_space=pl.ANY),
        scratch_shapes=[pltpu.VMEM((MOE_BLOCK, d), F32),
                        pltpu.VMEM((MOE_BLOCK, d), F32),
                        pltpu.SemaphoreType.DMA,
                        pltpu.SemaphoreType.DMA],
    )
    return pl.pallas_call(
        _expert_kernel,
        out_shape=jax.ShapeDtypeStruct((cap, d), F32),
        grid_spec=grid_spec,
        compiler_params=_cparams(("arbitrary",)),
        name="moe_experts",
    )(blk_e, src, dst, h2, wcol, w1, w3, w2)


def _route(scores, router_b, n):
    per_group = N_EXPERTS // N_GROUPS
    biased = scores + router_b.astype(F32)
    group_score = lax.top_k(biased.reshape(n, N_GROUPS, per_group), 2)[0].sum(-1)
    group = jnp.argmax(group_score, axis=-1)
    in_group = (jnp.arange(N_EXPERTS) // per_group)[None, :] == group[:, None]
    _, idx = lax.top_k(jnp.where(in_group, biased, -jnp.inf), TOP_K)
    wts = jnp.take_along_axis(scores, idx, axis=-1)
    wts = wts / wts.sum(-1, keepdims=True)
    flat_e = idx.reshape(-1).astype(jnp.int32)
    order = jnp.argsort(flat_e)
    se = flat_e[order]
    counts = jnp.bincount(flat_e, length=N_EXPERTS)
    padded = (counts + MOE_BLOCK - 1) // MOE_BLOCK * MOE_BLOCK
    start = jnp.cumsum(counts) - counts
    pad_end = jnp.cumsum(padded)
    pad_start = pad_end - padded
    dest = pad_start[se] + jnp.arange(n * TOP_K) - start[se]
    n_blk = (n * TOP_K + N_EXPERTS * (MOE_BLOCK - 1) + MOE_BLOCK - 1) // MOE_BLOCK
    cap = n_blk * MOE_BLOCK
    slot_a = jnp.full((cap,), -1, jnp.int32).at[dest].set(order.astype(jnp.int32))
    real = slot_a >= 0
    a = jnp.maximum(slot_a, 0)
    src = jnp.where(real, a // TOP_K, 0).astype(jnp.int32)
    pad_rank = jnp.cumsum(jnp.logical_not(real).astype(jnp.int32)) - 1
    dst = jnp.where(real, (a % TOP_K) * n + a // TOP_K, n * TOP_K + pad_rank).astype(jnp.int32)
    wcol = jnp.where(real, wts.reshape(-1)[a], 0.0).astype(F32).reshape(cap, 1)
    blk_e = jnp.minimum(jnp.searchsorted(pad_end, jnp.arange(n_blk) * MOE_BLOCK, side='right'),
                        N_EXPERTS - 1).astype(jnp.int32)
    return blk_e, src, dst, wcol


def _combine_kernel(x_ref, mod_ref, f0_ref, f1_ref, o_ref):
    o_ref[...] = x_ref[...] + mod_ref[0][5:6] * (f0_ref[...] + f1_ref[...])


def _combine(xn, mod3, f, n_rows, group_of_tile):
    d = xn.shape[1]
    nt = n_rows // TM
    return pl.pallas_call(
        _combine_kernel,
        out_shape=jax.ShapeDtypeStruct((n_rows, d), F32),
        grid=(nt,),
        in_specs=[pl.BlockSpec((TM, d), lambda i: (i, 0)),
                  pl.BlockSpec((1, 8, d), lambda i: (group_of_tile(i), 0, 0)),
                  pl.BlockSpec((TM, d), lambda i: (i, 0)),
                  pl.BlockSpec((TM, d), lambda i: (nt + i, 0))],
        out_specs=pl.BlockSpec((TM, d), lambda i: (i, 0)),
        compiler_params=_cparams(("parallel",)),
        name="moe_combine",
    )(xn, mod3, f, f)


def _rope_tables(seq, dim, pad):
    t = jnp.arange(seq)
    rows = (t // GRID_W).astype(F32)
    cols = (t % GRID_W).astype(F32)
    quarter = dim // 4
    inv_freq = jnp.exp(-math.log(ROPE_BASE) * jnp.arange(quarter, dtype=F32) / quarter)
    ang = jnp.concatenate([inv_freq[:, None] * rows[None, :], inv_freq[:, None] * cols[None, :]], axis=0)
    cos = jnp.concatenate([jnp.cos(ang), jnp.ones((dim // 2, pad), F32)], axis=1)
    sin = jnp.concatenate([jnp.sin(ang), jnp.zeros((dim // 2, pad), F32)], axis=1)
    return cos, sin


def _win_mask_table(seq):
    nkb = seq // LANES
    nq = seq // WIN_TQ
    tabs = []
    for i in (0, 1, nq - 1):
        t = i * WIN_TQ + np.arange(WIN_TQ)[:, None]
        blk = np.clip(2 * i - 1 + np.arange(4), 0, nkb - 1)
        want = 2 * i - 1 + np.arange(4)
        s = (blk[:, None] * LANES + np.arange(LANES)[None, :]).reshape(-1)[None, :]
        ok = (np.abs(t - s) <= WIN_RADIUS) & np.repeat(blk == want, LANES)[None, :]
        tabs.append(np.where(ok, 0.0, NEG))
    return jnp.asarray(np.stack(tabs), F32)


def _nat_bias_table(rpb, seq):
    rows = seq // GRID_W
    nq = rows // NAT_ROWS_PER_STEP
    wc = NAT_WIN_COLS
    col = np.arange(GRID_W)
    col_start = np.clip(col - wc // 2, 0, GRID_W - wc)
    col_ok = (col[None, :] >= col_start[:, None]) & (col[None, :] < col_start[:, None] + wc)
    d_col = np.clip(col[None, :] - col[:, None] + (wc - 1), 0, 2 * wc - 2)
    tabs = []
    for i in (0, 1, nq - 1):
        r0 = NAT_ROWS_PER_STEP * i
        ws = np.clip(r0 - NAT_WIN_ROWS // 2, 0, rows - NAT_KEY_ROWS)
        r = r0 + np.arange(NAT_ROWS_PER_STEP)
        rs = np.clip(r - NAT_WIN_ROWS // 2, 0, rows - NAT_WIN_ROWS)
        krow = ws + np.arange(NAT_KEY_ROWS)
        row_ok = (krow[None, :] >= rs[:, None]) & (krow[None, :] < rs[:, None] + NAT_WIN_ROWS)
        d_row = np.clip(krow[None, :] - r[:, None] + (NAT_WIN_ROWS - 1), 0, 2 * NAT_WIN_ROWS - 2)
        ok = row_ok[:, None, :, None] & col_ok[None, :, None, :]
        bias = rpb[:, d_row[:, None, :, None], d_col[None, :, None, :]].astype(F32)
        bias = jnp.where(jnp.asarray(ok)[None], bias, NEG)
        tabs.append(bias.reshape(rpb.shape[0], NAT_ROWS_PER_STEP * GRID_W, NAT_KEY_ROWS * GRID_W))
    return jnp.stack(tabs)


def _bcast_rows(v, reps=1):
    return jnp.tile(jnp.broadcast_to(v.astype(F32)[:, None], (v.shape[0], LANES)), (reps, 1))


def _layer_params(l, p):
    w = p['w_in'][l]
    sizes = (512, 128, 128, 512, 512, 512, 512, 512, 512, 256, 160, 4096)
    offs = np.concatenate([[0], np.cumsum(sizes)])
    seg = lambda k: w[:, offs[k]:offs[k + 1]]
    wq, wk, wv, dq, dk, dv, nq, nk, nv, mqa, mkva, gates = [seg(k) for k in range(12)]
    d = w.shape[0]
    dup = lambda m: jnp.concatenate([m[:, :64], m[:, :64], m[:, 64:], m[:, 64:]], axis=1)
    mkva_p = jnp.concatenate([mkva, jnp.zeros((d, 256 - mkva.shape[1]), F32)], axis=1)
    w_a = jnp.concatenate([gates, dv, nv, mqa, mkva_p, dup(wv)], axis=1).astype(BF)
    w_bt = jnp.concatenate([wq, dq, nq, dk, nk, dup(wk)], axis=1).T.astype(BF)
    scale = HEAD_DIM ** -0.5
    gain_b = jnp.concatenate([
        _bcast_rows(p['win_q_norm'][l] * scale, 8), _bcast_rows(p['dif_q_norm'][l] * scale, 8),
        _bcast_rows(p['nat_q_norm'][l] * scale, 8), _bcast_rows(p['dif_k_norm'][l], 8),
        _bcast_rows(p['nat_k_norm'][l], 8), _bcast_rows(p['win_k_norm'][l], 4)], axis=0)
    wkv = p['mla_wkv_b'][l].reshape(MLA_KV_LORA, MLA_HEADS, MLA_NOPE + MLA_V)
    wk_t = wkv[:, :, :MLA_NOPE].reshape(MLA_KV_LORA, -1).T.astype(BF)
    wv_m = wkv[:, :, MLA_NOPE:].reshape(MLA_KV_LORA, -1).astype(BF)
    lam_f = p['dif_lambda'][l].astype(F32)
    lam_init = 0.8 - 0.6 * math.exp(-0.3 * l)
    lam = jnp.exp(jnp.sum(lam_f[0] * lam_f[1])) - jnp.exp(jnp.sum(lam_f[2] * lam_f[3])) + lam_init
    return dict(
        w_a=w_a, w_bt=w_bt, gain_b=gain_b,
        g1=p['norm1_g'][l].reshape(1, d), g2=p['norm2_g'][l].reshape(1, d),
        sink=p['win_sink'][l].astype(F32).reshape(1, WIN_HEADS),
        gqa=p['mla_q_a_norm'][l].reshape(1, -1), gkva=p['mla_kv_a_norm'][l].reshape(1, -1),
        wq_t=p['mla_wq_b'][l].T.astype(BF), wk_t=wk_t, wv_m=wv_m,
        gq=_bcast_rows(p['mla_q_norm'][l] * (MLA_QK ** -0.5)), gk=_bcast_rows(p['mla_k_norm'][l]),
        lam=lam.reshape(1, 1).astype(F32), lam_scale=1.0 - lam_init,
        subln=p['dif_subln'][l].astype(F32).reshape(1, DIF_V_DIM),
        wb=p['w_branch'][l].astype(BF), wo=p['w_out'][l].astype(BF),
        w1=p['moe_w1'][l].astype(BF), w3=p['moe_w3'][l].astype(BF), w2=p['moe_w2'][l].astype(BF),
    )


def kernel(x, c, ctx, c_ctx, ada_w, ada_b, norm1_g, norm2_g, w_in, win_q_norm, win_k_norm, win_sink,
           dif_q_norm, dif_k_norm, dif_lambda, dif_subln, nat_q_norm, nat_k_norm, nat_rpb,
           mla_q_a_norm, mla_wq_b, mla_kv_a_norm, mla_wkv_b, mla_q_norm, mla_k_norm,
           w_branch, w_out, router_w, router_b, moe_w1, moe_w3, moe_w2):
    p = dict(norm1_g=norm1_g, norm2_g=norm2_g, w_in=w_in, win_q_norm=win_q_norm, win_k_norm=win_k_norm,
             win_sink=win_sink, dif_q_norm=dif_q_norm, dif_k_norm=dif_k_norm, dif_lambda=dif_lambda,
             dif_subln=dif_subln, nat_q_norm=nat_q_norm, nat_k_norm=nat_k_norm,
             mla_q_a_norm=mla_q_a_norm, mla_wq_b=mla_wq_b, mla_kv_a_norm=mla_kv_a_norm,
             mla_wkv_b=mla_wkv_b, mla_q_norm=mla_q_norm, mla_k_norm=mla_k_norm,
             w_branch=w_branch, w_out=w_out, moe_w1=moe_w1, moe_w3=moe_w3, moe_w2=moe_w2)
    n_batch, seq, d = x.shape
    n_ctx = ctx.shape[1]
    depth = ada_w.shape[0]
    n_lat = n_batch * seq
    n_all = n_lat + n_batch * n_ctx
    assert seq % DENSE_TK == 0 and (n_batch * n_ctx) == TM and seq % TM == 0
    tiles_per_batch = seq // TM
    group_of_tile = lambda i: jnp.minimum(i // tiles_per_batch, n_batch)
    pos_of_tile = lambda i: jnp.where(i < n_batch * tiles_per_batch, i % tiles_per_batch, tiles_per_batch)

    cc = jnp.concatenate([c, c_ctx[None, :], jnp.zeros((8 - n_batch - 1, d), F32)], axis=0)
    mod = _modulation(cc, ada_w, ada_b)
    mod = mod[:, :n_batch + 1].reshape(depth, n_batch + 1, 6, d)
    mod = jnp.pad(mod, ((0, 0), (0, 0), (0, 2), (0, 0)))

    cos_h, sin_h = _rope_tables(seq, HEAD_DIM, TM)
    cos_m, sin_m = _rope_tables(seq, MLA_ROPE, TM)
    win_mask = _win_mask_table(seq)
    rw = jnp.pad(router_w.astype(F32), ((0, 0), (0, LANES - N_EXPERTS)))

    xa = jnp.concatenate([x.reshape(n_lat, d), ctx.reshape(n_batch * n_ctx, d)], axis=0)
    dense = functools.partial(_dense_attention, n_batch=n_batch, seq=seq, n_ctx=n_ctx)
    for l in range(depth):
        lp = _layer_params(l, p)
        want_ctx = l < depth - 1
        mod3 = mod[l]
        tok_a = _in_proj_a(xa, mod3, lp['g1'], lp['w_a'], group_of_tile)
        q_tok, kt = _in_proj_b(xa, mod3, lp['g1'], lp['w_bt'], lp['gain_b'], cos_h, sin_h,
                               group_of_tile, pos_of_tile)
        mq, mkt, mv = _mla_prep(tok_a, lp['gqa'], lp['gkva'], lp['wq_t'], lp['wk_t'], lp['wv_m'],
                                lp['gq'], lp['gk'], cos_m, sin_m, pos_of_tile)
        nat_bias = _nat_bias_table(nat_rpb[l], seq)

        def branches(latent):
            y_win = _win_attention(lp['sink'], q_tok, kt, tok_a, win_mask, n_batch, seq, n_ctx, latent)
            y_dif = dense(lp['lam'], q_tok, 1, 512, kt, KT_DK // 512, 512, tok_a, A_DV // 512, 512,
                          lp['subln'], n_heads=2 * DIF_HEADS, packed=True, diff=True, latent=latent,
                          lam_scale=lp['lam_scale'], name="dif_attn" if latent else "dif_attn_ctx")
            if latent:
                y_nat = _nat_attention(q_tok, kt, tok_a, nat_bias, n_batch, seq, n_ctx)
            else:
                y_nat = dense(lp['lam'], q_tok, 2, 512, kt, KT_NK // 512, 512, tok_a, A_NV // 512, 512,
                              None, n_heads=NAT_HEADS, packed=True, diff=False, latent=False,
                              lam_scale=1.0, name="nat_attn_ctx")
            y_mla = dense(lp['lam'], mq, 0, 1024, mkt, 0, 1024, mv, 0, 512, None,
                          n_heads=MLA_HEADS, packed=False, diff=False, latent=latent, lam_scale=1.0,
                          name="mla_attn" if latent else "mla_attn_ctx")
            return [y_win, y_dif, y_nat, y_mla]

        ys = branches(True)
        ys_c = branches(False) if want_ctx else None
        n_rows = n_all if want_ctx else n_lat
        xn, h2, scores = _merge(xa, mod3, lp['g2'], ys, ys_c, tok_a, lp['wb'], lp['wo'], rw, n_rows,
                                group_of_tile)
        blk_e, src, dst, wcol = _route(scores[:, :N_EXPERTS], router_b, n_rows)
        f = _experts(blk_e, src, dst, h2, wcol, lp['w1'], lp['w3'], lp['w2'])
        xa = _combine(xn, mod3, f, n_rows, group_of_tile)
    return xa[:n_lat].reshape(n_batch, seq, d)
```

```python
import functools
import math

import jax
import jax.numpy as jnp
import numpy as np
from jax import lax
from jax.experimental import pallas as pl
from jax.experimental.pallas import tpu as pltpu

F32 = jnp.float32
BF = jnp.bfloat16

GRID_W = 64
HEAD_DIM = 64
N_BRANCH = 4
BRANCH_W = 512
ROPE_BASE = 10000.0
EPS = 1e-6
NEG = -1e30
WIN_HEADS, WIN_KV_HEADS, WIN_RADIUS = 8, 2, 128
DIF_HEADS, DIF_QK_DIM, DIF_V_DIM = 4, 64, 128
NAT_HEADS, NAT_WIN_ROWS, NAT_WIN_COLS = 8, 8, 16
MLA_HEADS, MLA_NOPE, MLA_ROPE, MLA_V, MLA_Q_LORA, MLA_KV_LORA = 8, 64, 32, 64, 256, 128
MLA_QK = MLA_NOPE + MLA_ROPE
N_EXPERTS, N_GROUPS, TOP_K, D_EXPERT, MOE_BLOCK = 16, 4, 2, 512, 256

LANES = 128
TM = 512
WIN_TQ = 256
NAT_ROWS_PER_STEP = 4
NAT_KEY_ROWS = 12
DENSE_TQ = 512
DENSE_TK = 1024
VMEM_LIMIT = 48 * 1024 * 1024

A_GATES, A_DV, A_NV, A_MQA, A_MKVA, A_WV = 0, 4096, 4608, 5120, 5376, 5632
A_COLS = 5888
Q_COLS = 1536
KT_DK, KT_NK, KT_WK = 0, 512, 1024
KT_ROWS = 1280
B_ROWS = Q_COLS + KT_ROWS


def _cparams(sem, vmem=VMEM_LIMIT):
    return pltpu.CompilerParams(dimension_semantics=sem, vmem_limit_bytes=vmem)


def _nt_dot(a, b):
    return lax.dot_general(a, b, (((1,), (1,)), ((), ())), preferred_element_type=F32)


def _norm_mod(x, g, sc, sh):
    ms = jnp.mean(x * x, axis=-1, keepdims=True)
    return (x * lax.rsqrt(ms + EPS) * g) * (1.0 + sc) + sh


def _lane_tile(a, n):
    reps = n // a.shape[1]
    return a if reps == 1 else jnp.concatenate([a] * reps, axis=1)


def _mod_kernel(c_ref, w_ref, b_ref, o_ref):
    cc = c_ref[...]
    a = cc * jax.nn.sigmoid(cc)
    o_ref[0] = jnp.dot(a, w_ref[0], preferred_element_type=F32,
                       precision=lax.Precision.HIGHEST) + b_ref[0]


def _modulation(cc, ada_w, ada_b):
    n_layers, d, d6 = ada_w.shape
    tn = 1536
    return pl.pallas_call(
        _mod_kernel,
        out_shape=jax.ShapeDtypeStruct((n_layers, 8, d6), F32),
        grid=(n_layers, d6 // tn),
        in_specs=[pl.BlockSpec((8, d), lambda l, j: (0, 0)),
                  pl.BlockSpec((1, d, tn), lambda l, j: (l, 0, j)),
                  pl.BlockSpec((1, 1, tn), lambda l, j: (l, 0, j))],
        out_specs=pl.BlockSpec((1, 8, tn), lambda l, j: (l, 0, j)),
        compiler_params=_cparams(("parallel", "parallel")),
        name="adaln_mod",
    )(cc, ada_w, ada_b.reshape(n_layers, 1, d6))


def _in_a_kernel(x_ref, mod_ref, g_ref, w_ref, o_ref, h_sc):
    @pl.when(pl.program_id(1) == 0)
    def _():
        m = mod_ref[0]
        h_sc[...] = _norm_mod(x_ref[...], g_ref[...], m[1:2], m[0:1]).astype(BF)

    o_ref[...] = jnp.dot(h_sc[...], w_ref[...], preferred_element_type=F32).astype(BF)


def _in_proj_a(xa, mod3, g1, w_a, group_of_tile):
    n, d = xa.shape
    tn = A_COLS // 2
    return pl.pallas_call(
        _in_a_kernel,
        out_shape=jax.ShapeDtypeStruct((n, A_COLS), BF),
        grid=(n // TM, A_COLS // tn),
        in_specs=[pl.BlockSpec((TM, d), lambda i, j: (i, 0)),
                  pl.BlockSpec((1, 8, d), lambda i, j: (group_of_tile(i), 0, 0)),
                  pl.BlockSpec((1, d), lambda i, j: (0, 0)),
                  pl.BlockSpec((d, tn), lambda i, j: (0, j))],
        out_specs=pl.BlockSpec((TM, tn), lambda i, j: (i, j)),
        scratch_shapes=[pltpu.VMEM((TM, d), BF)],
        compiler_params=_cparams(("parallel", "arbitrary")),
        name="in_proj_tok",
    )(xa, mod3, g1, w_a)


def _head_norm_rope(x, g, cos, sin, rope):
    ss = jnp.sum(x * x, axis=0, keepdims=True)
    y = x * lax.rsqrt(ss * (1.0 / HEAD_DIM) + EPS) * g
    if not rope:
        return y
    half = HEAD_DIM // 2
    y1, y2 = y[:half], y[half:]
    return jnp.concatenate([y1 * cos - y2 * sin, y1 * sin + y2 * cos], axis=0)


def _in_b_kernel(x_ref, mod_ref, g_ref, wt_ref, gain_ref, cos_ref, sin_ref, q_ref, kt_ref, acc_sc):
    m = mod_ref[0]
    h = _norm_mod(x_ref[...], g_ref[...], m[1:2], m[0:1]).astype(BF)
    acc_sc[...] = _nt_dot(wt_ref[...], h)
    tm = h.shape[0]
    cos = cos_ref[...]
    sin = sin_ref[...]

    def pair(r0, rope):
        hs = []
        for e in range(2):
            r = r0 + e * HEAD_DIM
            g = _lane_tile(gain_ref[r:r + HEAD_DIM, :], tm)
            hs.append(_head_norm_rope(acc_sc[r:r + HEAD_DIM, :], g, cos, sin, rope))
        return jnp.concatenate(hs, axis=0)

    for p in range(Q_COLS // LANES):
        y = pair(p * LANES, rope=p < 8)
        q_ref[:, p * LANES:(p + 1) * LANES] = y.T.astype(BF)
    for p in range(KT_ROWS // LANES):
        y = pair(Q_COLS + p * LANES, rope=not (4 <= p < 8))
        kt_ref[p * LANES:(p + 1) * LANES, :] = y.astype(BF)


def _in_proj_b(xa, mod3, g1, w_bt, gain_b, cos_t, sin_t, group_of_tile, pos_of_tile):
    n, d = xa.shape
    return pl.pallas_call(
        _in_b_kernel,
        out_shape=(jax.ShapeDtypeStruct((n, Q_COLS), BF),
                   jax.ShapeDtypeStruct((KT_ROWS, n), BF)),
        grid=(n // TM,),
        in_specs=[pl.BlockSpec((TM, d), lambda i: (i, 0)),
                  pl.BlockSpec((1, 8, d), lambda i: (group_of_tile(i), 0, 0)),
                  pl.BlockSpec((1, d), lambda i: (0, 0)),
                  pl.BlockSpec((B_ROWS, d), lambda i: (0, 0)),
                  pl.BlockSpec((B_ROWS, LANES), lambda i: (0, 0)),
                  pl.BlockSpec((HEAD_DIM // 2, TM), lambda i: (0, pos_of_tile(i))),
                  pl.BlockSpec((HEAD_DIM // 2, TM), lambda i: (0, pos_of_tile(i)))],
        out_specs=(pl.BlockSpec((TM, Q_COLS), lambda i: (i, 0)),
                   pl.BlockSpec((KT_ROWS, TM), lambda i: (0, i))),
        scratch_shapes=[pltpu.VMEM((B_ROWS, TM), F32)],
        compiler_params=_cparams(("parallel",)),
        name="in_proj_heads",
    )(xa, mod3, g1, w_bt, gain_b, cos_t, sin_t)


def _mla_kernel(qa_ref, kva_ref, gqa_ref, gkva_ref, wqt_ref, wkt_ref, wv_ref, gq_ref, gk_ref,
                cos_ref, sin_ref, mq_ref, mkt_ref, mv_ref):
    tm = qa_ref.shape[0]
    cos = cos_ref[...]
    sin = sin_ref[...]
    rh = MLA_ROPE // 2

    def rms_rows(x, g):
        ms = jnp.mean(x * x, axis=-1, keepdims=True)
        return x * lax.rsqrt(ms + EPS) * g

    def rope_rows(x):
        x1, x2 = x[:rh], x[rh:]
        return jnp.concatenate([x1 * cos - x2 * sin, x1 * sin + x2 * cos], axis=0)

    qa = rms_rows(qa_ref[...].astype(F32), gqa_ref[...]).astype(BF)
    qt = _nt_dot(wqt_ref[...], qa)
    kva = kva_ref[...].astype(F32)
    cn = rms_rows(kva[:, :MLA_KV_LORA], gkva_ref[...]).astype(BF)
    knt = _nt_dot(wkt_ref[...], cn)
    mv_ref[...] = jnp.dot(cn, wv_ref[...], preferred_element_type=F32).astype(BF)
    krope = kva[:, MLA_KV_LORA:].T[:MLA_ROPE]
    kr_ss = jnp.sum(krope * krope, axis=0, keepdims=True)
    gq = _lane_tile(gq_ref[...], tm)
    gk = _lane_tile(gk_ref[...], tm)
    zpad = jnp.zeros((LANES - MLA_QK, tm), F32)
    for hd in range(MLA_HEADS):
        x = qt[hd * MLA_QK:(hd + 1) * MLA_QK]
        ss = jnp.sum(x * x, axis=0, keepdims=True)
        y = x * lax.rsqrt(ss * (1.0 / MLA_QK) + EPS) * gq
        y = jnp.concatenate([y[:MLA_NOPE], rope_rows(y[MLA_NOPE:]), zpad], axis=0)
        mq_ref[:, hd * LANES:(hd + 1) * LANES] = y.T.astype(BF)
        kn = knt[hd * MLA_NOPE:(hd + 1) * MLA_NOPE]
        ss = jnp.sum(kn * kn, axis=0, keepdims=True) + kr_ss
        r = lax.rsqrt(ss * (1.0 / MLA_QK) + EPS)
        yk = jnp.concatenate([kn * r * gk[:MLA_NOPE], rope_rows(krope * r * gk[MLA_NOPE:]), zpad], axis=0)
        mkt_ref[hd * LANES:(hd + 1) * LANES, :] = yk.astype(BF)


def _mla_prep(tok_a, gqa, gkva, wq_t, wk_t, wv, gq, gk, cos_t, sin_t, pos_of_tile):
    n = tok_a.shape[0]
    hw = MLA_HEADS * LANES
    return pl.pallas_call(
        _mla_kernel,
        out_shape=(jax.ShapeDtypeStruct((n, hw), BF),
                   jax.ShapeDtypeStruct((hw, n), BF),
                   jax.ShapeDtypeStruct((n, MLA_HEADS * MLA_V), BF)),
        grid=(n // TM,),
        in_specs=[pl.BlockSpec((TM, 256), lambda i: (i, A_MQA // 256)),
                  pl.BlockSpec((TM, 256), lambda i: (i, A_MKVA // 256)),
                  pl.BlockSpec((1, MLA_Q_LORA), lambda i: (0, 0)),
                  pl.BlockSpec((1, MLA_KV_LORA), lambda i: (0, 0)),
                  pl.BlockSpec(wq_t.shape, lambda i: (0, 0)),
                  pl.BlockSpec(wk_t.shape, lambda i: (0, 0)),
                  pl.BlockSpec(wv.shape, lambda i: (0, 0)),
                  pl.BlockSpec((MLA_QK, LANES), lambda i: (0, 0)),
                  pl.BlockSpec((MLA_QK, LANES), lambda i: (0, 0)),
                  pl.BlockSpec((MLA_ROPE // 2, TM), lambda i: (0, pos_of_tile(i))),
                  pl.BlockSpec((MLA_ROPE // 2, TM), lambda i: (0, pos_of_tile(i)))],
        out_specs=(pl.BlockSpec((TM, hw), lambda i: (i, 0)),
                   pl.BlockSpec((hw, TM), lambda i: (0, i)),
                   pl.BlockSpec((TM, MLA_HEADS * MLA_V), lambda i: (i, 0))),
        compiler_params=_cparams(("parallel",)),
        name="mla_prep",
    )(tok_a, tok_a, gqa, gkva, wq_t, wk_t, wv, gq, gk, cos_t, sin_t)


def _half_mask(shape):
    return lax.broadcasted_iota(jnp.int32, shape, 1) < (LANES // 2)


def _select_half(q, e, lo_mask):
    zero = jnp.zeros_like(q)
    return jnp.where(lo_mask, q, zero) if e == 0 else jnp.where(lo_mask, zero, q)


def _local_softmax_out(parts, extra_logit):
    m = parts[0][0].max(axis=-1, keepdims=True)
    for s, _ in parts[1:]:
        m = jnp.maximum(m, s.max(axis=-1, keepdims=True))
    if extra_logit is not None:
        m = jnp.maximum(m, extra_logit)
    z = None
    o = None
    for s, v in parts:
        p = jnp.exp(s - m)
        zs = p.sum(axis=-1, keepdims=True)
        os_ = jnp.dot(p.astype(BF), v, preferred_element_type=F32)
        z = zs if z is None else z + zs
        o = os_ if o is None else o + os_
    if extra_logit is not None:
        z = z + jnp.exp(extra_logit - m)
    return o / z


def _win_kernel(sink_ref, q_ref, *refs, band):
    if band:
        k0, k1, k2, k3, v0, v1, v2, v3, kc_ref, vc_ref, mask_ref, o_ref = refs
        kb = jnp.concatenate([k0[...], k1[...], k2[...], k3[...]], axis=1)
        vb = jnp.concatenate([v0[...], v1[...], v2[...], v3[...]], axis=0)
        mask = mask_ref[0]
    else:
        kc_ref, vc_ref, o_ref = refs
    q = q_ref[...]
    lo = _half_mask((q.shape[0], LANES))
    group = WIN_HEADS // WIN_KV_HEADS
    for j in range(WIN_HEADS // 2):
        qp = q[:, j * LANES:(j + 1) * LANES]
        g = (2 * j) // group
        kc = kc_ref[g * LANES:(g + 1) * LANES, :]
        vc = vc_ref[:, g * LANES:(g + 1) * LANES]
        outs = []
        for e in range(2):
            qm = _select_half(qp, e, lo)
            parts = []
            if band:
                s = jnp.dot(qm, kb[g * LANES:(g + 1) * LANES, :], preferred_element_type=F32) + mask
                parts.append((s, vb[:, g * LANES:(g + 1) * LANES]))
            parts.append((jnp.dot(qm, kc, preferred_element_type=F32), vc))
            outs.append(_local_softmax_out(parts, sink_ref[0, 2 * j + e]))
        o_ref[:, j * LANES:(j + 1) * LANES] = jnp.where(lo, outs[0], outs[1]).astype(BF)


def _win_attention(sink, q_tok, kt, tok_a, mask_tbl, n_batch, seq, n_ctx, latent):
    ctx_blk = (n_batch * seq) // n_ctx
    kc_spec = lambda f: pl.BlockSpec((2 * LANES, n_ctx), f)
    vc_spec = lambda f: pl.BlockSpec((n_ctx, 2 * LANES), f)
    smem = pl.BlockSpec(memory_space=pltpu.SMEM)
    n_out = n_batch * (seq if latent else n_ctx)
    out_shape = jax.ShapeDtypeStruct((n_out, WIN_HEADS * HEAD_DIM), BF)
    if not latent:
        return pl.pallas_call(
            functools.partial(_win_kernel, band=False),
            out_shape=out_shape,
            grid=(n_batch,),
            in_specs=[smem,
                      pl.BlockSpec((n_ctx, 512), lambda b: (ctx_blk + b, 0)),
                      kc_spec(lambda b: (KT_WK // 256, ctx_blk + b)),
                      vc_spec(lambda b: (ctx_blk + b, A_WV // 256))],
            out_specs=pl.BlockSpec((n_ctx, 512), lambda b: (b, 0)),
            compiler_params=_cparams(("parallel",)),
            name="win_attn_ctx",
        )(sink, q_tok, kt, tok_a)
    nq = seq // WIN_TQ
    nkb = seq // LANES

    def kidx(j):
        return lambda b, i: (KT_WK // 256, b * nkb + jnp.clip(2 * i - 1 + j, 0, nkb - 1))

    def vidx(j):
        return lambda b, i: (b * nkb + jnp.clip(2 * i - 1 + j, 0, nkb - 1), A_WV // 256)

    def variant(b, i):
        return (jnp.where(i == 0, 0, jnp.where(i == nq - 1, 2, 1)), 0, 0)

    in_specs = ([smem, pl.BlockSpec((WIN_TQ, 512), lambda b, i: (b * nq + i, 0))]
                + [pl.BlockSpec((2 * LANES, LANES), kidx(j)) for j in range(4)]
                + [pl.BlockSpec((LANES, 2 * LANES), vidx(j)) for j in range(4)]
                + [kc_spec(lambda b, i: (KT_WK // 256, ctx_blk + b)),
                   vc_spec(lambda b, i: (ctx_blk + b, A_WV // 256)),
                   pl.BlockSpec((1, WIN_TQ, 4 * LANES), variant)])
    return pl.pallas_call(
        functools.partial(_win_kernel, band=True),
        out_shape=out_shape,
        grid=(n_batch, nq),
        in_specs=in_specs,
        out_specs=pl.BlockSpec((WIN_TQ, 512), lambda b, i: (b * nq + i, 0)),
        compiler_params=_cparams(("parallel", "parallel")),
        name="win_attn",
    )(sink, q_tok, kt, kt, kt, kt, tok_a, tok_a, tok_a, tok_a, kt, tok_a, mask_tbl)


def _nat_kernel(q_ref, k0, k1, k2, v0, v1, v2, kc_ref, vc_ref, bias_ref, o_ref):
    q = q_ref[...]
    kb = jnp.concatenate([k0[...], k1[...], k2[...]], axis=1)
    vb = jnp.concatenate([v0[...], v1[...], v2[...]], axis=0)
    lo = _half_mask((q.shape[0], LANES))
    for j in range(NAT_HEADS // 2):
        sl = slice(j * LANES, (j + 1) * LANES)
        qp = q[:, sl]
        outs = []
        for e in range(2):
            qm = _select_half(qp, e, lo)
            s = jnp.dot(qm, kb[sl, :], preferred_element_type=F32) + bias_ref[0, 2 * j + e]
            sc = jnp.dot(qm, kc_ref[sl, :], preferred_element_type=F32)
            outs.append(_local_softmax_out([(s, vb[:, sl]), (sc, vc_ref[:, sl])], None))
        o_ref[:, sl] = jnp.where(lo, outs[0], outs[1]).astype(BF)


def _nat_attention(q_tok, kt, tok_a, bias_tbl, n_batch, seq, n_ctx):
    tq =NAT_ROWS_PER_STEP * GRID_W
    nq = seq // tq
    rows = seq // GRID_W
    ctx_blk = (n_batch * seq) // n_ctx
    q_col = 2
    k_row = KT_NK // 512
    v_col = A_NV // 512

    def wstart(i):
        return jnp.clip(NAT_ROWS_PER_STEP * i - NAT_WIN_ROWS // 2, 0, rows - NAT_KEY_ROWS) // NAT_ROWS_PER_STEP

    def kidx(j):
        return lambda b, i: (k_row, b * nq + wstart(i) + j)

    def vidx(j):
        return lambda b, i: (b * nq + wstart(i) + j, v_col)

    def variant(b, i):
        return (jnp.where(i == 0, 0, jnp.where(i == nq - 1, 2, 1)), 0, 0, 0)

    nk = NAT_KEY_ROWS * GRID_W
    in_specs = ([pl.BlockSpec((tq, 512), lambda b, i: (b * nq + i, q_col))]
                + [pl.BlockSpec((512, tq), kidx(j)) for j in range(3)]
                + [pl.BlockSpec((tq, 512), vidx(j)) for j in range(3)]
                + [pl.BlockSpec((512, n_ctx), lambda b, i: (k_row, ctx_blk + b)),
                   pl.BlockSpec((n_ctx, 512), lambda b, i: (ctx_blk + b, v_col)),
                   pl.BlockSpec((1, NAT_HEADS, tq, nk), variant)])
    return pl.pallas_call(
        _nat_kernel,
        out_shape=jax.ShapeDtypeStruct((n_batch * seq, NAT_HEADS * HEAD_DIM), BF),
        grid=(n_batch, nq),
        in_specs=in_specs,
        out_specs=pl.BlockSpec((tq, 512), lambda b, i: (b * nq + i, 0)),
        compiler_params=_cparams(("parallel", "parallel")),
        name="nat_attn",
    )(q_tok, kt, kt, kt, tok_a, tok_a, tok_a, kt, tok_a, bias_tbl)


def _dense_kernel(lam_ref, q_ref, kc_ref, vc_ref, *refs, n_heads, packed, diff, latent, lam_scale):
    if latent:
        k_ref, v_ref = refs[0], refs[1]
        refs = refs[2:]
    if diff:
        subln_ref, o_ref, qm_sc, m_sc, l_sc, acc_sc = refs
    else:
        o_ref, qm_sc, m_sc, l_sc, acc_sc = refs
    kt_step = pl.program_id(2) if latent else 0
    tq = q_ref.shape[0]

    def kv_slices(h):
        blk = h // 2 if packed else h
        ks = slice(blk * LANES, (blk + 1) * LANES)
        if diff:
            vs = ks
        elif packed:
            vs = ks
        else:
            vs = slice((h // 2) * LANES, (h // 2 + 1) * LANES)
        return ks, vs

    def step(h, k_blk, v_blk, first):
        ks, vs = kv_slices(h)
        s = jnp.dot(qm_sc[h], k_blk[ks, :], preferred_element_type=F32)
        smax = s.max(axis=-1, keepdims=True)
        if first:
            m_new = smax
        else:
            m_old = m_sc[h][:, :1]
            m_new = jnp.maximum(m_old, smax)
            alpha = jnp.exp(m_old - m_new)
        p = jnp.exp(s - m_new)
        ps = p.sum(axis=-1, keepdims=True)
        pv = jnp.dot(p.astype(BF), v_blk[:, vs], preferred_element_type=F32)
        if first:
            l_new, acc_new = ps, pv
        else:
            l_new = alpha * l_sc[h][:, :1] + ps
            acc_new = alpha * acc_sc[h] + pv
        m_sc[h] = jnp.broadcast_to(m_new, (tq, LANES))
        l_sc[h] = jnp.broadcast_to(l_new, (tq, LANES))
        acc_sc[h] = acc_new

    @pl.when(kt_step == 0)
    def _():
        q = q_ref[...]
        lo = _half_mask((tq, LANES))
        for h in range(n_heads):
            if packed:
                qp = q[:, (h // 2) * LANES:(h // 2 + 1) * LANES]
                qm_sc[h] = _select_half(qp, h % 2, lo)
            else:
                qm_sc[h] = q[:, h * LANES:(h + 1) * LANES]
        kc = kc_ref[...]
        vc = vc_ref[...]
        for h in range(n_heads):
            step(h, kc, vc, True)

    if latent:
        k_blk = k_ref[...]
        v_blk = v_ref[...]
        for h in range(n_heads):
            step(h, k_blk, v_blk, False)
        last = kt_step == pl.num_programs(2) - 1
    else:
        last = True

    def finish():
        lo = _half_mask((tq, LANES))
        if diff:
            lam = lam_ref[0, 0]
            for hv in range(n_heads // 2):
                y = (acc_sc[2 * hv] / l_sc[2 * hv][:, :1]
                     - lam * (acc_sc[2 * hv + 1] / l_sc[2 * hv + 1][:, :1]))
                ms = jnp.mean(y * y, axis=-1, keepdims=True)
                y = y * lax.rsqrt(ms + EPS) * subln_ref[...] * lam_scale
                o_ref[:, hv * LANES:(hv + 1) * LANES] = y.astype(BF)
        else:
            for hp in range(n_heads // 2):
                o0 = acc_sc[2 * hp] / l_sc[2 * hp][:, :1]
                o1 = acc_sc[2 * hp + 1] / l_sc[2 * hp + 1][:, :1]
                o_ref[:, hp * LANES:(hp + 1) * LANES] = jnp.where(lo, o0, o1).astype(BF)

    if latent:
        pl.when(last)(finish)
    else:
        finish()


def _dense_attention(lam, q_arr, q_col, q_w, k_arr, k_row, k_w, v_arr, v_col, v_w, subln,
                     n_batch, seq, n_ctx, *, n_heads, packed, diff, latent, lam_scale, name):
    n = n_batch * (seq if latent else n_ctx)
    ctx_blk = (n_batch * seq) // n_ctx
    out_w = v_w
    smem = pl.BlockSpec(memory_space=pltpu.SMEM)
    kern = functools.partial(_dense_kernel, n_heads=n_heads, packed=packed, diff=diff,
                             latent=latent, lam_scale=lam_scale)
    tq = DENSE_TQ if latent else n_ctx
    scratch = [pltpu.VMEM((n_heads, tq, LANES), BF),
               pltpu.VMEM((n_heads, tq, LANES), F32),
               pltpu.VMEM((n_heads, tq, LANES), F32),
               pltpu.VMEM((n_heads, tq, LANES), F32)]
    out_shape = jax.ShapeDtypeStruct((n, out_w), BF)
    if latent:
        nq = seq // tq
        nk = seq // DENSE_TK
        grid = (n_batch, nq, nk)
        in_specs = [smem,
                    pl.BlockSpec((tq, q_w), lambda b, i, k: (b * nq + i, q_col)),
                    pl.BlockSpec((k_w, n_ctx), lambda b, i, k: (k_row, ctx_blk + b)),
                    pl.BlockSpec((n_ctx, v_w), lambda b, i, k: (ctx_blk + b, v_col)),
                    pl.BlockSpec((k_w, DENSE_TK), lambda b, i, k: (k_row, b * nk + k)),
                    pl.BlockSpec((DENSE_TK, v_w), lambda b, i, k: (b * nk + k, v_col))]
        args = [lam, q_arr, k_arr, v_arr, k_arr, v_arr]
        if diff:
            in_specs.append(pl.BlockSpec((1, LANES), lambda b, i, k: (0, 0)))
            args.append(subln)
        out_specs = pl.BlockSpec((tq, out_w), lambda b, i, k: (b * nq + i, 0))
        sem = ("parallel", "parallel", "arbitrary")
    else:
        grid = (n_batch,)
        in_specs = [smem,
                    pl.BlockSpec((tq, q_w), lambda b: (ctx_blk + b, q_col)),
                    pl.BlockSpec((k_w, n_ctx), lambda b: (k_row, ctx_blk + b)),
                    pl.BlockSpec((n_ctx, v_w), lambda b: (ctx_blk + b, v_col))]
        args = [lam, q_arr, k_arr, v_arr]
        if diff:
            in_specs.append(pl.BlockSpec((1, LANES), lambda b: (0, 0)))
            args.append(subln)
        out_specs = pl.BlockSpec((tq, out_w), lambda b: (b, 0))
        sem = ("parallel",)
    return pl.pallas_call(
        kern, out_shape=out_shape, grid=grid, in_specs=in_specs, out_specs=out_specs,
        scratch_shapes=scratch, compiler_params=_cparams(sem), name=name,
    )(*args)


def _merge_kernel(x_ref, mod_ref, g2_ref, *refs, n_lat_tiles, has_ctx):
    ys = refs[:N_BRANCH]
    refs = refs[N_BRANCH:]
    if has_ctx:
        ycs = refs[:N_BRANCH]
        refs = refs[N_BRANCH:]
        is_ctx = pl.program_id(0) >= n_lat_tiles
    gts = refs[:N_BRANCH]
    wb_ref, wo_ref, rw_ref, xo_ref, h2_ref, sc_ref = refs[N_BRANCH:]
    m = mod_ref[0]
    mix = None
    for n_ in range(N_BRANCH):
        y = ys[n_][...]
        if has_ctx:
            y = jnp.where(is_ctx, ycs[n_][...], y)
        yb = jnp.dot(y, wb_ref[n_], preferred_element_type=F32)
        t = jax.nn.sigmoid(gts[n_][...].astype(F32)) * yb
        mix = t if mix is None else mix + t
    att = jnp.dot(mix.astype(BF), wo_ref[...], preferred_element_type=F32)
    xn = x_ref[...] + m[2:3] * att
    xo_ref[...] = xn
    h2 = _norm_mod(xn, g2_ref[...], m[4:5], m[3:4])
    h2_ref[...] = h2
    logits = jnp.dot(h2, rw_ref[...], preferred_element_type=F32, precision=lax.Precision.HIGHEST)
    sc_ref[...] = jax.nn.sigmoid(logits)


def _merge(xa, mod3, g2, ys, ys_ctx, tok_a, wb, wo, rw, n_rows, group_of_tile):
    d = xa.shape[1]
    n_lat_tiles = ys[0].shape[0] // TM
    has_ctx = ys_ctx is not None
    row = lambda w, c: pl.BlockSpec((TM, w), lambda i, c=c: (i, c))
    lat_row = pl.BlockSpec((TM, BRANCH_W), lambda i: (jnp.minimum(i, n_lat_tiles - 1), 0))
    in_specs = ([row(d, 0),
                 pl.BlockSpec((1, 8, d), lambda i: (group_of_tile(i), 0, 0)),
                 pl.BlockSpec((1, d), lambda i: (0, 0))]
                + [lat_row for _ in range(N_BRANCH)]
                + ([pl.BlockSpec((TM, BRANCH_W), lambda i: (0, 0)) for _ in range(N_BRANCH)] if has_ctx else [])
                + [row(d, c) for c in range(N_BRANCH)]
                + [pl.BlockSpec(wb.shape, lambda i: (0, 0, 0)),
                   pl.BlockSpec(wo.shape, lambda i: (0, 0)),
                   pl.BlockSpec(rw.shape, lambda i: (0, 0))])
    ys = list(ys) + (list(ys_ctx) if has_ctx else [])
    return pl.pallas_call(
        functools.partial(_merge_kernel, n_lat_tiles=n_lat_tiles, has_ctx=has_ctx),
        out_shape=(jax.ShapeDtypeStruct((n_rows, d), F32),
                   jax.ShapeDtypeStruct((n_rows, d), F32),
                   jax.ShapeDtypeStruct((n_rows, LANES), F32)),
        grid=(n_rows // TM,),
        in_specs=in_specs,
        out_specs=(row(d, 0), row(d, 0), row(LANES, 0)),
        compiler_params=_cparams(("parallel",)),
        name="merge",
    )(xa, mod3, g2, *ys, tok_a, tok_a, tok_a, tok_a, wb, wo, rw)


def _expert_kernel(blk_e_ref, src_ref, dst_ref, h2_hbm, w_ref, w1_ref, w3_ref, w2_ref, out_hbm,
                   xbuf, ybuf, sem_in, sem_out):
    i = pl.program_id(0)
    base = i * MOE_BLOCK

    def gather(r, c):
        t = src_ref[base + r]
        pltpu.make_async_copy(h2_hbm.at[pl.ds(t, 1)], xbuf.at[pl.ds(r, 1)], sem_in).start()
        return c

    lax.fori_loop(0, MOE_BLOCK, gather, 0)
    pltpu.make_async_copy(h2_hbm.at[pl.ds(0, MOE_BLOCK)], xbuf, sem_in).wait()
    xb = xbuf[...].astype(BF)
    a = jnp.dot(xb, w1_ref[0], preferred_element_type=F32)
    b = jnp.dot(xb, w3_ref[0], preferred_element_type=F32)
    hmid = (a * jax.nn.sigmoid(a) * b).astype(BF)
    ybuf[...] = jnp.dot(hmid, w2_ref[0], preferred_element_type=F32) * w_ref[...]

    def scatter(r, c):
        t = dst_ref[base + r]
        pltpu.make_async_copy(ybuf.at[pl.ds(r, 1)], out_hbm.at[pl.ds(t, 1)], sem_out).start()
        return c

    lax.fori_loop(0, MOE_BLOCK, scatter, 0)
    pltpu.make_async_copy(ybuf, out_hbm.at[pl.ds(0, MOE_BLOCK)], sem_out).wait()


def _experts(blk_e, src, dst, h2, wcol, w1, w3, w2):
    cap = src.shape[0]
    d = h2.shape[1]
    n_blk = cap // MOE_BLOCK
    wspec = lambda shp: pl.BlockSpec((1,) + shp, lambda i, be, s, t: (be[i], 0, 0))
    grid_spec = pltpu.PrefetchScalarGridSpec(
        num_scalar_prefetch=3,
        grid=(n_blk,),
        in_specs=[pl.BlockSpec(memory_space=pl.ANY),
                  pl.BlockSpec((MOE_BLOCK, 1), lambda i, be, s, t: (i, 0)),
                  wspec((d, D_EXPERT)), wspec((d, D_EXPERT)), wspec((D_EXPERT, d))],
        out_specs=pl.BlockSpec(memory_space=pl.ANY),
        scratch_shapes=[pltpu.VMEM((MOE_BLOCK, d), F32),
                        pltpu.VMEM((MOE_BLOCK, d), F32),
                        pltpu.SemaphoreType.DMA,
                        pltpu.SemaphoreType.DMA],
    )
    return pl.pallas_call(
        _expert_kernel,
        out_shape=jax.ShapeDtypeStruct((cap, d), F32),
        grid_spec=grid_spec,
        compiler_params=_cparams(("arbitrary",)),
        name="moe_experts",
    )(blk_e, src, dst, h2, wcol, w1, w3, w2)


def _route(scores, router_b, n):
    per_group = N_EXPERTS // N_GROUPS
    biased = scores + router_b.astype(F32)
    group_score = lax.top_k(biased.reshape(n, N_GROUPS, per_group), 2)[0].sum(-1)
    group = jnp.argmax(group_score, axis=-1)
    in_group = (jnp.arange(N_EXPERTS) // per_group)[None, :] == group[:, None]
    _, idx = lax.top_k(jnp.where(in_group, biased, -jnp.inf), TOP_K)
    wts = jnp.take_along_axis(scores, idx, axis=-1)
    wts = wts / wts.sum(-1, keepdims=True)
    flat_e = idx.reshape(-1).astype(jnp.int32)
    n_asg = n * TOP_K
    onehot = (flat_e[:, None] == jnp.arange(N_EXPERTS, dtype=jnp.int32)[None, :]).astype(F32)
    oh = onehot.reshape(n_asg // MOE_BLOCK, MOE_BLOCK, N_EXPERTS)
    tri = jnp.tril(jnp.ones((MOE_BLOCK, MOE_BLOCK), F32), -1)
    within = jnp.einsum('ij,bjk->bik', tri, oh)
    blk_tot = oh.sum(axis=1)
    blk_off = jnp.cumsum(blk_tot, axis=0) - blk_tot
    rank = ((within + blk_off[:, None, :]) * oh).sum(-1).reshape(n_asg).astype(jnp.int32)
    counts = blk_tot.sum(axis=0).astype(jnp.int32)
    padded = (counts + MOE_BLOCK - 1) // MOE_BLOCK * MOE_BLOCK
    pad_end = jnp.cumsum(padded)
    pad_start = pad_end - padded
    dest = (onehot * pad_start.astype(F32)[None, :]).sum(-1).astype(jnp.int32) + rank
    n_blk = (n_asg + N_EXPERTS * (MOE_BLOCK - 1) + MOE_BLOCK - 1) // MOE_BLOCK
    cap = n_blk * MOE_BLOCK
    slot_a = jnp.full((cap,), -1, jnp.int32).at[dest].set(jnp.arange(n_asg, dtype=jnp.int32))
    real = slot_a >= 0
    a = jnp.maximum(slot_a, 0)
    src = jnp.where(real, a // TOP_K, 0).astype(jnp.int32)
    pad_rank = jnp.cumsum(jnp.logical_not(real).astype(jnp.int32)) - 1
    dst = jnp.where(real, (a % TOP_K) * n + a // TOP_K, n * TOP_K + pad_rank).astype(jnp.int32)
    wcol = jnp.where(real, wts.reshape(-1)[a], 0.0).astype(F32).reshape(cap, 1)
    blk_e = jnp.minimum(jnp.searchsorted(pad_end, jnp.arange(n_blk) * MOE_BLOCK, side='right'),
                        N_EXPERTS - 1).astype(jnp.int32)
    return blk_e, src, dst, wcol


def _combine_kernel(x_ref, mod_ref, f0_ref, f1_ref, o_ref):
    o_ref[...] = x_ref[...] + mod_ref[0][5:6] * (f0_ref[...] + f1_ref[...])


def _combine(xn, mod3, f, n_rows, group_of_tile):
    d = xn.shape[1]
    nt = n_rows // TM
    return pl.pallas_call(
        _combine_kernel,
        out_shape=jax.ShapeDtypeStruct((n_rows, d), F32),
        grid=(nt,),
        in_specs=[pl.BlockSpec((TM, d), lambda i: (i, 0)),
                  pl.BlockSpec((1, 8, d), lambda i: (group_of_tile(i), 0, 0)),
                  pl.BlockSpec((TM, d), lambda i: (i, 0)),
                  pl.BlockSpec((TM, d), lambda i: (nt + i, 0))],
        out_specs=pl.BlockSpec((TM, d), lambda i: (i, 0)),
        compiler_params=_cparams(("parallel",)),
        name="moe_combine",
    )(xn, mod3, f, f)


def _rope_tables(seq, dim, pad):
    t = jnp.arange(seq)
    rows = (t // GRID_W).astype(F32)
    cols = (t % GRID_W).astype(F32)
    quarter = dim // 4
    inv_freq = jnp.exp(-math.log(ROPE_BASE) * jnp.arange(quarter, dtype=F32) / quarter)
    ang = jnp.concatenate([inv_freq[:, None] * rows[None, :], inv_freq[:, None] * cols[None, :]], axis=0)
    cos = jnp.concatenate([jnp.cos(ang), jnp.ones((dim // 2, pad), F32)], axis=1)
    sin = jnp.concatenate([jnp.sin(ang), jnp.zeros((dim // 2, pad), F32)], axis=1)
    return cos, sin


def _win_mask_table(seq):
    nkb = seq // LANES
    nq = seq // WIN_TQ
    tabs = []
    for i in (0, 1, nq - 1):
        t = i * WIN_TQ + np.arange(WIN_TQ)[:, None]
        blk = np.clip(2 * i - 1 + np.arange(4), 0, nkb - 1)
        want = 2 * i - 1 + np.arange(4)
        s = (blk[:, None] * LANES + np.arange(LANES)[None, :]).reshape(-1)[None, :]
        ok = (np.abs(t - s) <= WIN_RADIUS) & np.repeat(blk == want, LANES)[None, :]
        tabs.append(np.where(ok, 0.0, NEG))
    return jnp.asarray(np.stack(tabs), F32)


def _nat_bias_table(rpb, seq):
    rows = seq // GRID_W
    nq = rows // NAT_ROWS_PER_STEP
    wc = NAT_WIN_COLS
    col = np.arange(GRID_W)
    col_start = np.clip(col - wc // 2, 0, GRID_W - wc)
    col_ok = (col[None, :] >= col_start[:, None]) & (col[None, :] < col_start[:, None] + wc)
    d_col = np.clip(col[None, :] - col[:, None] + (wc - 1), 0, 2 * wc - 2)
    tabs = []
    for i in (0, 1, nq - 1):
        r0 = NAT_ROWS_PER_STEP * i
        ws = np.clip(r0 - NAT_WIN_ROWS // 2, 0, rows - NAT_KEY_ROWS)
        r = r0 + np.arange(NAT_ROWS_PER_STEP)
        rs = np.clip(r - NAT_WIN_ROWS // 2, 0, rows - NAT_WIN_ROWS)
        krow = ws + np.arange(NAT_KEY_ROWS)
        row_ok = (krow[None, :] >= rs[:, None]) & (krow[None, :] < rs[:, None] + NAT_WIN_ROWS)
        d_row = np.clip(krow[None, :] - r[:, None] + (NAT_WIN_ROWS - 1), 0, 2 * NAT_WIN_ROWS - 2)
        ok = row_ok[:, None, :, None] & col_ok[None, :, None, :]
        sel_r = jnp.asarray(d_row.reshape(-1)[:, None] == np.arange(2 * NAT_WIN_ROWS - 1)[None, :], F32)
        sel_c = jnp.asarray(d_col.reshape(-1)[:, None] == np.arange(2 * wc - 1)[None, :], F32)
        bias = jnp.einsum('pr,hrc,qc->hpq', sel_r, rpb.astype(F32), sel_c, precision=lax.Precision.HIGHEST)
        bias = bias.reshape(rpb.shape[0], NAT_ROWS_PER_STEP, NAT_KEY_ROWS, GRID_W, GRID_W)
        bias = bias.transpose(0, 1, 3, 2, 4)
        bias = jnp.where(jnp.asarray(ok)[None], bias, NEG)
        tabs.append(bias.reshape(rpb.shape[0], NAT_ROWS_PER_STEP * GRID_W, NAT_KEY_ROWS * GRID_W))
    return jnp.stack(tabs)


def _bcast_rows(v, reps=1):
    return jnp.tile(jnp.broadcast_to(v.astype(F32)[:, None], (v.shape[0], LANES)), (reps, 1))


def _layer_params(l, p):
    w = p['w_in'][l]
    sizes = (512, 128, 128, 512, 512, 512, 512, 512, 512, 256, 160, 4096)
    offs = np.concatenate([[0], np.cumsum(sizes)])
    seg = lambda k: w[:, offs[k]:offs[k + 1]]
    wq, wk, wv, dq, dk, dv, nq, nk, nv, mqa, mkva, gates = [seg(k) for k in range(12)]
    d = w.shape[0]
    dup = lambda m: jnp.concatenate([m[:, :64], m[:, :64], m[:, 64:], m[:, 64:]], axis=1)
    mkva_p = jnp.concatenate([mkva, jnp.zeros((d, 256 - mkva.shape[1]), F32)], axis=1)
    w_a = jnp.concatenate([gates, dv, nv, mqa, mkva_p, dup(wv)], axis=1).astype(BF)
    w_bt = jnp.concatenate([wq, dq, nq, dk, nk, dup(wk)], axis=1).T.astype(BF)
    scale = HEAD_DIM ** -0.5
    gain_b = jnp.concatenate([
        _bcast_rows(p['win_q_norm'][l] * scale, 8), _bcast_rows(p['dif_q_norm'][l] * scale, 8),
        _bcast_rows(p['nat_q_norm'][l] * scale, 8), _bcast_rows(p['dif_k_norm'][l], 8),
        _bcast_rows(p['nat_k_norm'][l], 8), _bcast_rows(p['win_k_norm'][l], 4)], axis=0)
    wkv = p['mla_wkv_b'][l].reshape(MLA_KV_LORA, MLA_HEADS, MLA_NOPE + MLA_V)
    wk_t = wkv[:, :, :MLA_NOPE].reshape(MLA_KV_LORA, -1).T.astype(BF)
    wv_m = wkv[:, :, MLA_NOPE:].reshape(MLA_KV_LORA, -1).astype(BF)
    lam_f = p['dif_lambda'][l].astype(F32)
    lam_init = 0.8 - 0.6 * math.exp(-0.3 * l)
    lam = jnp.exp(jnp.sum(lam_f[0] * lam_f[1])) - jnp.exp(jnp.sum(lam_f[2] * lam_f[3])) + lam_init
    return dict(
        w_a=w_a, w_bt=w_bt, gain_b=gain_b,
        g1=p['norm1_g'][l].reshape(1, d), g2=p['norm2_g'][l].reshape(1, d),
        sink=p['win_sink'][l].astype(F32).reshape(1, WIN_HEADS),
        gqa=p['mla_q_a_norm'][l].reshape(1, -1), gkva=p['mla_kv_a_norm'][l].reshape(1, -1),
        wq_t=p['mla_wq_b'][l].T.astype(BF), wk_t=wk_t, wv_m=wv_m,
        gq=_bcast_rows(p['mla_q_norm'][l] * (MLA_QK ** -0.5)), gk=_bcast_rows(p['mla_k_norm'][l]),
        lam=lam.reshape(1, 1).astype(F32), lam_scale=1.0 - lam_init,
        subln=p['dif_subln'][l].astype(F32).reshape(1, DIF_V_DIM),
        wb=p['w_branch'][l].astype(BF), wo=p['w_out'][l].astype(BF),
        w1=p['moe_w1'][l].astype(BF), w3=p['moe_w3'][l].astype(BF), w2=p['moe_w2'][l].astype(BF),
    )


def kernel(x, c, ctx, c_ctx, ada_w, ada_b, norm1_g, norm2_g, w_in, win_q_norm, win_k_norm, win_sink,
           dif_q_norm, dif_k_norm, dif_lambda, dif_subln, nat_q_norm, nat_k_norm, nat_rpb,
           mla_q_a_norm, mla_wq_b, mla_kv_a_norm, mla_wkv_b, mla_q_norm, mla_k_norm,
           w_branch, w_out, router_w, router_b, moe_w1, moe_w3, moe_w2):
    p = dict(norm1_g=norm1_g, norm2_g=norm2_g, w_in=w_in, win_q_norm=win_q_norm, win_k_norm=win_k_norm,
             win_sink=win_sink, dif_q_norm=dif_q_norm, dif_k_norm=dif_k_norm, dif_lambda=dif_lambda,
             dif_subln=dif_subln, nat_q_norm=nat_q_norm, nat_k_norm=nat_k_norm,
             mla_q_a_norm=mla_q_a_norm, mla_wq_b=mla_wq_b, mla_kv_a_norm=mla_kv_a_norm,
             mla_wkv_b=mla_wkv_b, mla_q_norm=mla_q_norm, mla_k_norm=mla_k_norm,
             w_branch=w_branch, w_out=w_out, moe_w1=moe_w1, moe_w3=moe_w3, moe_w2=moe_w2)
    n_batch, seq, d = x.shape
    n_ctx = ctx.shape[1]
    depth = ada_w.shape[0]
    n_lat = n_batch * seq
    n_all = n_lat + n_batch * n_ctx
    assert seq % DENSE_TK == 0 and (n_batch * n_ctx) == TM and seq % TM == 0
    tiles_per_batch = seq // TM
    group_of_tile = lambda i: jnp.minimum(i // tiles_per_batch, n_batch)
    pos_of_tile = lambda i: jnp.where(i < n_batch * tiles_per_batch, i % tiles_per_batch, tiles_per_batch)

    cc = jnp.concatenate([c, c_ctx[None, :], jnp.zeros((8 - n_batch - 1, d), F32)], axis=0)
    mod = _modulation(cc, ada_w, ada_b)
    mod = mod[:, :n_batch + 1].reshape(depth, n_batch + 1, 6, d)
    mod = jnp.pad(mod, ((0, 0), (0, 0), (0, 2), (0, 0)))

    cos_h, sin_h = _rope_tables(seq, HEAD_DIM, TM)
    cos_m, sin_m = _rope_tables(seq, MLA_ROPE, TM)
    win_mask = _win_mask_table(seq)
    rw = jnp.pad(router_w.astype(F32), ((0, 0), (0, LANES - N_EXPERTS)))

    xa = jnp.concatenate([x.reshape(n_lat, d), ctx.reshape(n_batch * n_ctx, d)], axis=0)
    dense = functools.partial(_dense_attention, n_batch=n_batch, seq=seq, n_ctx=n_ctx)
    for l in range(depth):
        lp = _layer_params(l, p)
        want_ctx = l < depth - 1
        mod3 = mod[l]
        tok_a = _in_proj_a(xa, mod3, lp['g1'], lp['w_a'], group_of_tile)
        q_tok, kt = _in_proj_b(xa, mod3, lp['g1'], lp['w_bt'], lp['gain_b'], cos_h, sin_h,
                               group_of_tile, pos_of_tile)
        mq, mkt, mv = _mla_prep(tok_a, lp['gqa'], lp['gkva'], lp['wq_t'], lp['wk_t'], lp['wv_m'],
                                lp['gq'], lp['gk'], cos_m, sin_m, pos_of_tile)
        nat_bias = _nat_bias_table(nat_rpb[l], seq)

        def branches(latent):
            y_win = _win_attention(lp['sink'], q_tok, kt, tok_a, win_mask, n_batch, seq, n_ctx, latent)
            y_dif = dense(lp['lam'], q_tok, 1, 512, kt, KT_DK // 512, 512, tok_a, A_DV // 512, 512,
                          lp['subln'], n_heads=2 * DIF_HEADS, packed=True, diff=True, latent=latent,
                          lam_scale=lp['lam_scale'], name="dif_attn" if latent else "dif_attn_ctx")
            if latent:
                y_nat = _nat_attention(q_tok, kt, tok_a, nat_bias, n_batch, seq, n_ctx)
            else:
                y_nat = dense(lp['lam'], q_tok, 2, 512, kt, KT_NK // 512, 512, tok_a, A_NV // 512, 512,
                              None, n_heads=NAT_HEADS, packed=True, diff=False, latent=False,
                              lam_scale=1.0, name="nat_attn_ctx")
            y_mla = dense(lp['lam'], mq, 0, 1024, mkt, 0, 1024, mv, 0, 512, None,
                          n_heads=MLA_HEADS, packed=False, diff=False, latent=latent, lam_scale=1.0,
                          name="mla_attn" if latent else "mla_attn_ctx")
            return [y_win, y_dif, y_nat, y_mla]

        ys = branches(True)
        ys_c = branches(False) if want_ctx else None
        n_rows = n_all if want_ctx else n_lat
        xn, h2, scores = _merge(xa, mod3, lp['g2'], ys, ys_c, tok_a, lp['wb'], lp['wo'], rw, n_rows,
                                group_of_tile)
        blk_e, src, dst, wcol = _route(scores[:, :N_EXPERTS], router_b, n_rows)
        f = _experts(blk_e, src, dst, h2, wcol, lp['w1'], lp['w3'], lp['w2'])
        xa = _combine(xn, mod3, f, n_rows, group_of_tile)
    return xa[:n_lat].reshape(n_batch, seq, d)
```

```python
import functools
import math

import jax
import jax.numpy as jnp
import numpy as np
from jax import lax
from jax.experimental import pallas as pl
from jax.experimental.pallas import tpu as pltpu

F32 = jnp.float32
BF = jnp.bfloat16

GRID_W = 64
HEAD_DIM = 64
N_BRANCH = 4
BRANCH_W = 512
ROPE_BASE = 10000.0
EPS = 1e-6
NEG = -1e30
LOG2E = math.log2(math.e)
WIN_HEADS, WIN_KV_HEADS, WIN_RADIUS = 8, 2, 128
DIF_HEADS, DIF_QK_DIM, DIF_V_DIM = 4, 64, 128
NAT_HEADS, NAT_WIN_ROWS, NAT_WIN_COLS = 8, 8, 16
MLA_HEADS, MLA_NOPE, MLA_ROPE, MLA_V, MLA_Q_LORA, MLA_KV_LORA = 8, 64, 32, 64, 256, 128
MLA_QK = MLA_NOPE + MLA_ROPE
N_EXPERTS, N_GROUPS, TOP_K, D_EXPERT, MOE_BLOCK = 16, 4, 2, 512, 256

LANES = 128
TM = 512
WIN_TQ = 256
NAT_ROWS_PER_STEP = 4
NAT_KEY_ROWS = 12
DENSE_TQ = 512
DENSE_TK = 1024
DENSE_RB = 64
VMEM_LIMIT = 48 * 1024 * 1024

A_GATES, A_DV, A_NV, A_MQA, A_MKVA, A_WV = 0, 4096, 4608, 5120, 5376, 5632
A_COLS = 5888
Q_COLS = 1536
KT_DK, KT_NK, KT_WK = 0, 512, 1024
KT_ROWS = 1280
B_ROWS = Q_COLS + KT_ROWS


def _cparams(sem, vmem=VMEM_LIMIT):
    return pltpu.CompilerParams(dimension_semantics=sem, vmem_limit_bytes=vmem)


def _nt_dot(a, b):
    return lax.dot_general(a, b, (((1,), (1,)), ((), ())), preferred_element_type=F32)


def _norm_mod(x, g, sc, sh):
    ms = jnp.mean(x * x, axis=-1, keepdims=True)
    return (x * lax.rsqrt(ms + EPS) * g) * (1.0 + sc) + sh


def _lane_tile(a, n):
    reps = n // a.shape[1]
    return a if reps == 1 else jnp.concatenate([a] * reps, axis=1)


def _mod_kernel(c_ref, w_ref, b_ref, o_ref):
    cc = c_ref[...]
    a = cc * jax.nn.sigmoid(cc)
    o_ref[0] = jnp.dot(a, w_ref[0], preferred_element_type=F32,
                       precision=lax.Precision.HIGHEST) + b_ref[0]


def _modulation(cc, ada_w, ada_b):
    n_layers, d, d6 = ada_w.shape
    tn = 1536
    return pl.pallas_call(
        _mod_kernel,
        out_shape=jax.ShapeDtypeStruct((n_layers, 8, d6), F32),
        grid=(n_layers, d6 // tn),
        in_specs=[pl.BlockSpec((8, d), lambda l, j: (0, 0)),
                  pl.BlockSpec((1, d, tn), lambda l, j: (l, 0, j)),
                  pl.BlockSpec((1, 1, tn), lambda l, j: (l, 0, j))],
        out_specs=pl.BlockSpec((1, 8, tn), lambda l, j: (l, 0, j)),
        compiler_params=_cparams(("parallel", "parallel")),
        name="adaln_mod",
    )(cc, ada_w, ada_b.reshape(n_layers, 1, d6))


def _in_a_kernel(x_ref, mod_ref, g_ref, w_ref, o_ref, h_sc):
    @pl.when(pl.program_id(1) == 0)
    def _():
        m = mod_ref[0]
        h_sc[...] = _norm_mod(x_ref[...], g_ref[...], m[1:2], m[0:1]).astype(BF)

    o_ref[...] = jnp.dot(h_sc[...], w_ref[...], preferred_element_type=F32).astype(BF)


def _in_proj_a(xa, mod3, g1, w_a, group_of_tile):
    n, d = xa.shape
    tn = A_COLS // 2
    return pl.pallas_call(
        _in_a_kernel,
        out_shape=jax.ShapeDtypeStruct((n, A_COLS), BF),
        grid=(n // TM, A_COLS // tn),
        in_specs=[pl.BlockSpec((TM, d), lambda i, j: (i, 0)),
                  pl.BlockSpec((1, 8, d), lambda i, j: (group_of_tile(i), 0, 0)),
                  pl.BlockSpec((1, d), lambda i, j: (0, 0)),
                  pl.BlockSpec((d, tn), lambda i, j: (0, j))],
        out_specs=pl.BlockSpec((TM, tn), lambda i, j: (i, j)),
        scratch_shapes=[pltpu.VMEM((TM, d), BF)],
        compiler_params=_cparams(("parallel", "arbitrary")),
        name="in_proj_tok",
    )(xa, mod3, g1, w_a)


def _head_norm_rope(x, g, cos, sin, rope):
    ss = jnp.sum(x * x, axis=0, keepdims=True)
    y = x * lax.rsqrt(ss * (1.0 / HEAD_DIM) + EPS) * g
    if not rope:
        return y
    half = HEAD_DIM // 2
    y1, y2 = y[:half], y[half:]
    return jnp.concatenate([y1 * cos - y2 * sin, y1 * sin + y2 * cos], axis=0)


def _in_b_kernel(x_ref, mod_ref, g_ref, wt_ref, gain_ref, cos_ref, sin_ref, q_ref, kt_ref, acc_sc):
    m = mod_ref[0]
    h = _norm_mod(x_ref[...], g_ref[...], m[1:2], m[0:1]).astype(BF)
    acc_sc[...] = _nt_dot(wt_ref[...], h)
    tm = h.shape[0]
    cos = cos_ref[...]
    sin = sin_ref[...]

    def pair(r0, rope):
        hs = []
        for e in range(2):
            r = r0 + e * HEAD_DIM
            g = _lane_tile(gain_ref[r:r + HEAD_DIM, :], tm)
            hs.append(_head_norm_rope(acc_sc[r:r + HEAD_DIM, :], g, cos, sin, rope))
        return jnp.concatenate(hs, axis=0)

    for p in range(Q_COLS // LANES):
        y = pair(p * LANES, rope=p < 8)
        q_ref[:, p * LANES:(p + 1) * LANES] = y.T.astype(BF)
    for p in range(KT_ROWS // LANES):
        y = pair(Q_COLS + p * LANES, rope=not (4 <= p < 8))
        kt_ref[p * LANES:(p + 1) * LANES, :] = y.astype(BF)


def _in_proj_b(xa, mod3, g1, w_bt, gain_b, cos_t, sin_t, group_of_tile, pos_of_tile):
    n, d = xa.shape
    return pl.pallas_call(
        _in_b_kernel,
        out_shape=(jax.ShapeDtypeStruct((n, Q_COLS), BF),
                   jax.ShapeDtypeStruct((KT_ROWS, n), BF)),
        grid=(n // TM,),
        in_specs=[pl.BlockSpec((TM, d), lambda i: (i, 0)),
                  pl.BlockSpec((1, 8, d), lambda i: (group_of_tile(i), 0, 0)),
                  pl.BlockSpec((1, d), lambda i: (0, 0)),
                  pl.BlockSpec((B_ROWS, d), lambda i: (0, 0)),
                  pl.BlockSpec((B_ROWS, LANES), lambda i: (0, 0)),
                  pl.BlockSpec((HEAD_DIM // 2, TM), lambda i: (0, pos_of_tile(i))),
                  pl.BlockSpec((HEAD_DIM // 2, TM), lambda i: (0, pos_of_tile(i)))],
        out_specs=(pl.BlockSpec((TM, Q_COLS), lambda i: (i, 0)),
                   pl.BlockSpec((KT_ROWS, TM), lambda i: (0, i))),
        scratch_shapes=[pltpu.VMEM((B_ROWS, TM), F32)],
        compiler_params=_cparams(("parallel",)),
        name="in_proj_heads",
    )(xa, mod3, g1, w_bt, gain_b, cos_t, sin_t)


def _mla_kernel(qa_ref, kva_ref, gqa_ref, gkva_ref, wqt_ref, wkt_ref, wv_ref, gq_ref, gk_ref,
                cos_ref, sin_ref, mq_ref, mkt_ref, mv_ref):
    tm = qa_ref.shape[0]
    cos = cos_ref[...]
    sin = sin_ref[...]
    rh = MLA_ROPE // 2

    def rms_rows(x, g):
        ms = jnp.mean(x * x, axis=-1, keepdims=True)
        return x * lax.rsqrt(ms + EPS) * g

    def rope_rows(x):
        x1, x2 = x[:rh], x[rh:]
        return jnp.concatenate([x1 * cos - x2 * sin, x1 * sin + x2 * cos], axis=0)

    qa = rms_rows(qa_ref[...].astype(F32), gqa_ref[...]).astype(BF)
    qt = _nt_dot(wqt_ref[...], qa)
    kva = kva_ref[...].astype(F32)
    cn = rms_rows(kva[:, :MLA_KV_LORA], gkva_ref[...]).astype(BF)
    knt = _nt_dot(wkt_ref[...], cn)
    mv_ref[...] = jnp.dot(cn, wv_ref[...], preferred_element_type=F32).astype(BF)
    krope = kva[:, MLA_KV_LORA:].T[:MLA_ROPE]
    kr_ss = jnp.sum(krope * krope, axis=0, keepdims=True)
    gq = _lane_tile(gq_ref[...], tm)
    gk = _lane_tile(gk_ref[...], tm)
    zpad = jnp.zeros((LANES - MLA_QK, tm), F32)
    for hd in range(MLA_HEADS):
        x = qt[hd * MLA_QK:(hd + 1) * MLA_QK]
        ss = jnp.sum(x * x, axis=0, keepdims=True)
        y = x * lax.rsqrt(ss * (1.0 / MLA_QK) + EPS) * gq
        y = jnp.concatenate([y[:MLA_NOPE], rope_rows(y[MLA_NOPE:]), zpad], axis=0)
        mq_ref[:, hd * LANES:(hd + 1) * LANES] = y.T.astype(BF)
        kn = knt[hd * MLA_NOPE:(hd + 1) * MLA_NOPE]
        ss = jnp.sum(kn * kn, axis=0, keepdims=True) + kr_ss
        r = lax.rsqrt(ss * (1.0 / MLA_QK) + EPS)
        yk = jnp.concatenate([kn * r * gk[:MLA_NOPE], rope_rows(krope * r * gk[MLA_NOPE:]), zpad], axis=0)
        mkt_ref[hd * LANES:(hd + 1) * LANES, :] = yk.astype(BF)


def _mla_prep(tok_a, gqa, gkva, wq_t, wk_t, wv, gq, gk, cos_t, sin_t, pos_of_tile):
    n = tok_a.shape[0]
    hw = MLA_HEADS * LANES
    return pl.pallas_call(
        _mla_kernel,
        out_shape=(jax.ShapeDtypeStruct((n, hw), BF),
                   jax.ShapeDtypeStruct((hw, n), BF),
                   jax.ShapeDtypeStruct((n, MLA_HEADS * MLA_V), BF)),
        grid=(n // TM,),
        in_specs=[pl.BlockSpec((TM, 256), lambda i: (i, A_MQA // 256)),
                  pl.BlockSpec((TM, 256), lambda i: (i, A_MKVA // 256)),
                  pl.BlockSpec((1, MLA_Q_LORA), lambda i: (0, 0)),
                  pl.BlockSpec((1, MLA_KV_LORA), lambda i: (0, 0)),
                  pl.BlockSpec(wq_t.shape, lambda i: (0, 0)),
                  pl.BlockSpec(wk_t.shape, lambda i: (0, 0)),
                  pl.BlockSpec(wv.shape, lambda i: (0, 0)),
                  pl.BlockSpec((MLA_QK, LANES), lambda i: (0, 0)),
                  pl.BlockSpec((MLA_QK, LANES), lambda i: (0, 0)),
                  pl.BlockSpec((MLA_ROPE // 2, TM), lambda i: (0, pos_of_tile(i))),
                  pl.BlockSpec((MLA_ROPE // 2, TM), lambda i: (0, pos_of_tile(i)))],
        out_specs=(pl.BlockSpec((TM, hw), lambda i: (i, 0)),
                   pl.BlockSpec((hw, TM), lambda i: (0, i)),
                   pl.BlockSpec((TM, MLA_HEADS * MLA_V), lambda i: (i, 0))),
        compiler_params=_cparams(("parallel",)),
        name="mla_prep",
    )(tok_a, tok_a, gqa, gkva, wq_t, wk_t, wv, gq, gk, cos_t, sin_t)


def _half_mask(shape):
    return lax.broadcasted_iota(jnp.int32, shape, 1) < (LANES // 2)


def _select_half(q, e, lo_mask):
    zero = jnp.zeros_like(q)
    return jnp.where(lo_mask, q, zero) if e == 0 else jnp.where(lo_mask, zero, q)


def _local_softmax_out(parts, extra_logit):
    m = parts[0][0].max(axis=-1, keepdims=True)
    for s, _ in parts[1:]:
        m = jnp.maximum(m, s.max(axis=-1, keepdims=True))
    if extra_logit is not None:
        m = jnp.maximum(m, extra_logit)
    z = None
    o = None
    for s, v in parts:
        p = jnp.exp2(s - m)
        zs = p.sum(axis=-1, keepdims=True)
        os_ = jnp.dot(p.astype(BF), v, preferred_element_type=F32)
        z = zs if z is None else z + zs
        o = os_ if o is None else o + os_
    if extra_logit is not None:
        z = z + jnp.exp2(extra_logit - m)
    return o / z


def _win_kernel(sink_ref, q_ref, *refs, band):
    if band:
        k0, k1, k2, k3, v0, v1, v2, v3, kc_ref, vc_ref, mask_ref, o_ref = refs
        kb = jnp.concatenate([k0[...], k1[...], k2[...], k3[...]], axis=1)
        vb = jnp.concatenate([v0[...], v1[...], v2[...], v3[...]], axis=0)
        mask = mask_ref[0]
    else:
        kc_ref, vc_ref, o_ref = refs
    q = q_ref[...]
    lo = _half_mask((q.shape[0], LANES))
    group = WIN_HEADS // WIN_KV_HEADS
    for j in range(WIN_HEADS // 2):
        qp = q[:, j * LANES:(j + 1) * LANES]
        g = (2 * j) // group
        kc = kc_ref[g * LANES:(g + 1) * LANES, :]
        vc = vc_ref[:, g * LANES:(g + 1) * LANES]
        outs = []
        for e in range(2):
            qm = _select_half(qp, e, lo)
            parts = []
            if band:
                s = jnp.dot(qm, kb[g * LANES:(g + 1) * LANES, :], preferred_element_type=F32) + mask
                parts.append((s, vb[:, g * LANES:(g + 1) * LANES]))
            parts.append((jnp.dot(qm, kc, preferred_element_type=F32), vc))
            outs.append(_local_softmax_out(parts, sink_ref[0, 2 * j + e]))
        o_ref[:, j * LANES:(j + 1) * LANES] = jnp.where(lo, outs[0], outs[1]).astype(BF)


def _win_attention(sink, q_tok, kt, tok_a, mask_tbl, n_batch, seq, n_ctx, latent):
    ctx_blk = (n_batch * seq) // n_ctx
    kc_spec = lambda f: pl.BlockSpec((2 * LANES, n_ctx), f)
    vc_spec = lambda f: pl.BlockSpec((n_ctx, 2 * LANES), f)
    smem = pl.BlockSpec(memory_space=pltpu.SMEM)
    n_out = n_batch * (seq if latent else n_ctx)
    out_shape = jax.ShapeDtypeStruct((n_out, WIN_HEADS * HEAD_DIM), BF)
    if not latent:
        return pl.pallas_call(
            functools.partial(_win_kernel, band=False),
            out_shape=out_shape,
            grid=(n_batch,),
            in_specs=[smem,
                      pl.BlockSpec((n_ctx, 512), lambda b: (ctx_blk + b, 0)),
                      kc_spec(lambda b: (KT_WK // 256, ctx_blk + b)),
                      vc_spec(lambda b: (ctx_blk + b, A_WV // 256))],
            out_specs=pl.BlockSpec((n_ctx, 512), lambda b: (b, 0)),
            compiler_params=_cparams(("parallel",)),
            name="win_attn_ctx",
        )(sink, q_tok, kt, tok_a)
    nq = seq // WIN_TQ
    nkb = seq // LANES

    def kidx(j):
        return lambda b, i: (KT_WK // 256, b * nkb + jnp.clip(2 * i - 1 + j, 0, nkb - 1))

    def vidx(j):
        return lambda b, i: (b * nkb + jnp.clip(2 * i - 1 + j, 0, nkb - 1), A_WV // 256)

    def variant(b, i):
        return (jnp.where(i == 0, 0, jnp.where(i == nq - 1, 2, 1)), 0, 0)

    in_specs = ([smem, pl.BlockSpec((WIN_TQ, 512), lambda b, i: (b * nq + i, 0))]
                + [pl.BlockSpec((2 * LANES, LANES), kidx(j)) for j in range(4)]
                + [pl.BlockSpec((LANES, 2 * LANES), vidx(j)) for j in range(4)]
                + [kc_spec(lambda b, i: (KT_WK // 256, ctx_blk + b)),
                   vc_spec(lambda b, i: (ctx_blk + b, A_WV // 256)),
                   pl.BlockSpec((1, WIN_TQ, 4 * LANES), variant)])
    return pl.pallas_call(
        functools.partial(_win_kernel, band=True),
        out_shape=out_shape,
        grid=(n_batch, nq),
        in_specs=in_specs,
        out_specs=pl.BlockSpec((WIN_TQ, 512), lambda b, i: (b * nq + i, 0)),
        compiler_params=_cparams(("parallel", "parallel")),
        name="win_attn",
    )(sink, q_tok, kt, kt, kt, kt, tok_a, tok_a, tok_a, tok_a, kt, tok_a, mask_tbl)


def _nat_kernel(q_ref, k0, k1, k2, v0, v1, v2, kc_ref, vc_ref, bias_ref, o_ref):
    q = q_ref[...]
    kb = jnp.concatenate([k0[...], k1[...], k2[...]], axis=1)
    vb = jnp.concatenate([v0[...], v1[...], v2[...]], axis=0)
    lo = _half_mask((q.shape[0], LANES))
    for j in range(NAT_HEADS // 2):
        sl = slice(j * LANES, (j + 1) * LANES)
        qp = q[:, sl]
        outs = []
        for e in range(2):
            qm = _select_half(qp, e, lo)
            s = jnp.dot(qm, kb[sl, :], preferred_element_type=F32) + bias_ref[0, 2 * j + e]
            sc = jnp.dot(qm, kc_ref[sl, :], preferred_element_type=F32)
            outs.append(_local_softmax_out([(s, vb[:, sl]), (sc, vc_ref[:, sl])], None))
        o_ref[:, sl] = jnp.where(lo, outs[0], outs[1]).astype(BF)


def _nat_attention(q_tok, kt, tok_a, bias_tbl, n_batch, seq, n_ctx):
    tq =NAT_ROWS_PER_STEP * GRID_W
    nq = seq // tq
    rows = seq // GRID_W
    ctx_blk = (n_batch * seq) // n_ctx
    q_col = 2
    k_row = KT_NK // 512
    v_col = A_NV // 512

    def wstart(i):
        return jnp.clip(NAT_ROWS_PER_STEP * i - NAT_WIN_ROWS // 2, 0, rows - NAT_KEY_ROWS) // NAT_ROWS_PER_STEP

    def kidx(j):
        return lambda b, i: (k_row, b * nq + wstart(i) + j)

    def vidx(j):
        return lambda b, i: (b * nq + wstart(i) + j, v_col)

    def variant(b, i):
        return (jnp.where(i == 0, 0, jnp.where(i == nq - 1, 2, 1)), 0, 0, 0)

    nk = NAT_KEY_ROWS * GRID_W
    in_specs = ([pl.BlockSpec((tq, 512), lambda b, i: (b * nq + i, q_col))]
                + [pl.BlockSpec((512, tq), kidx(j)) for j in range(3)]
                + [pl.BlockSpec((tq, 512), vidx(j)) for j in range(3)]
                + [pl.BlockSpec((512, n_ctx), lambda b, i: (k_row, ctx_blk + b)),
                   pl.BlockSpec((n_ctx, 512), lambda b, i: (ctx_blk + b, v_col)),
                   pl.BlockSpec((1, NAT_HEADS, tq, nk), variant)])
    return pl.pallas_call(
        _nat_kernel,
        out_shape=jax.ShapeDtypeStruct((n_batch * seq, NAT_HEADS * HEAD_DIM), BF),
        grid=(n_batch, nq),
        in_specs=in_specs,
        out_specs=pl.BlockSpec((tq, 512), lambda b, i: (b * nq + i, 0)),
        compiler_params=_cparams(("parallel", "parallel")),
        name="nat_attn",
    )(q_tok, kt, kt, kt, tok_a, tok_a, tok_a, kt, tok_a, bias_tbl)


def _dense_kernel(lam_ref, q_ref, kc_ref, vc_ref, *refs, n_heads, packed, diff, latent, lam_scale):
    if latent:
        k_ref, v_ref = refs[0], refs[1]
        refs = refs[2:]
    if diff:
        subln_ref, o_ref, qm_sc, m_sc, l_sc, acc_sc, s_sc, p_sc = refs
    else:
        o_ref, qm_sc, m_sc, l_sc, acc_sc, s_sc, p_sc = refs
    kt_step = pl.program_id(2) if latent else 0
    tq = q_ref.shape[0]

    def kv_slices(h):
        blk = h // 2 if packed else h
        ks = slice(blk * LANES, (blk + 1) * LANES)
        vs = ks if packed else slice((h // 2) * LANES, (h // 2 + 1) * LANES)
        return ks, vs

    def step(h, k_ref_, v_ref_):
        ks, vs = kv_slices(h)
        nk = k_ref_.shape[1]
        slot = h % 2
        s_sc[slot, :, :nk] = jnp.dot(qm_sc[h], k_ref_[ks, :], preferred_element_type=F32)
        for r in range(tq // DENSE_RB):
            rows = slice(r * DENSE_RB, (r + 1) * DENSE_RB)
            mx = s_sc[slot, rows, 0:LANES]
            for c in range(1, nk // LANES):
                mx = jnp.maximum(mx, s_sc[slot, rows, c * LANES:(c + 1) * LANES])
            m_old = m_sc[h, rows, :]
            m_new = jnp.maximum(m_old, jnp.max(mx, axis=-1, keepdims=True))
            alpha = jnp.exp2(m_old - m_new)
            lsum = None
            for c in range(nk // LANES):
                cols = slice(c * LANES, (c + 1) * LANES)
                p = jnp.exp2(s_sc[slot, rows, cols] - m_new)
                lsum = p if lsum is None else lsum + p
                p_sc[slot, rows, cols] = p.astype(BF)
            m_sc[h, rows, :] = m_new
            l_sc[h, rows, :] = alpha * l_sc[h, rows, :] + jnp.sum(lsum, axis=-1, keepdims=True)
            acc_sc[h, rows, :] = alpha * acc_sc[h, rows, :]
        acc_sc[h] += jnp.dot(p_sc[slot, :, :nk], v_ref_[:, vs], preferred_element_type=F32)

    @pl.when(kt_step == 0)
    def _():
        q = q_ref[...]
        lo = _half_mask((tq, LANES))
        for h in range(n_heads):
            if packed:
                qp = q[:, (h // 2) * LANES:(h // 2 + 1) * LANES]
                qm_sc[h] = _select_half(qp, h % 2, lo)
            else:
                qm_sc[h] = q[:, h * LANES:(h + 1) * LANES]
        m_sc[...] = jnp.full(m_sc.shape, NEG, F32)
        l_sc[...] = jnp.zeros(l_sc.shape, F32)
        acc_sc[...] = jnp.zeros(acc_sc.shape, F32)
        for h in range(n_heads):
            step(h, kc_ref, vc_ref)

    if latent:
        for h in range(n_heads):
            step(h, k_ref, v_ref)
        last = kt_step == pl.num_programs(2) - 1
    else:
        last = True

    def finish():
        lo = _half_mask((tq, LANES))
        if diff:
            lam = lam_ref[0, 0]
            for hv in range(n_heads // 2):
                y = (acc_sc[2 * hv] / l_sc[2 * hv]
                     - lam * (acc_sc[2 * hv + 1] / l_sc[2 * hv + 1]))
                ms = jnp.mean(y * y, axis=-1, keepdims=True)
                y = y * lax.rsqrt(ms + EPS) * subln_ref[...] * lam_scale
                o_ref[:, hv * LANES:(hv + 1) * LANES] = y.astype(BF)
        else:
            for hp in range(n_heads // 2):
                o0 = acc_sc[2 * hp] / l_sc[2 * hp]
                o1 = acc_sc[2 * hp + 1] / l_sc[2 * hp + 1]
                o_ref[:, hp * LANES:(hp + 1) * LANES] = jnp.where(lo, o0, o1).astype(BF)

    if latent:
        pl.when(last)(finish)
    else:
        finish()


def _dense_attention(lam, q_arr, q_col, q_w, k_arr, k_row, k_w, v_arr, v_col, v_w, subln,
                     n_batch, seq, n_ctx, *, n_heads, packed, diff, latent, lam_scale, name):
    n = n_batch * (seq if latent else n_ctx)
    ctx_blk = (n_batch * seq) // n_ctx
    out_w = v_w
    smem = pl.BlockSpec(memory_space=pltpu.SMEM)
    kern = functools.partial(_dense_kernel, n_heads=n_heads, packed=packed, diff=diff,
                             latent=latent, lam_scale=lam_scale)
    tq = DENSE_TQ if latent else n_ctx
    max_nk = DENSE_TK if latent else n_ctx
    scratch = [pltpu.VMEM((n_heads, tq, LANES), BF),
               pltpu.VMEM((n_heads, tq, LANES), F32),
               pltpu.VMEM((n_heads, tq, LANES), F32),
               pltpu.VMEM((n_heads, tq, LANES), F32),
               pltpu.VMEM((2, tq, max_nk), F32),
               pltpu.VMEM((2, tq, max_nk), BF)]
    out_shape = jax.ShapeDtypeStruct((n, out_w), BF)
    if latent:
        nq = seq // tq
        nk = seq // DENSE_TK
        grid = (n_batch, nq, nk)
        in_specs = [smem,
                    pl.BlockSpec((tq, q_w), lambda b, i, k: (b * nq + i, q_col)),
                    pl.BlockSpec((k_w, n_ctx), lambda b, i, k: (k_row, ctx_blk + b)),
                    pl.BlockSpec((n_ctx, v_w), lambda b, i, k: (ctx_blk + b, v_col)),
                    pl.BlockSpec((k_w, DENSE_TK), lambda b, i, k: (k_row, b * nk + k)),
                    pl.BlockSpec((DENSE_TK, v_w), lambda b, i, k: (b * nk + k, v_col))]
        args = [lam, q_arr, k_arr, v_arr, k_arr, v_arr]
        if diff:
            in_specs.append(pl.BlockSpec((1, LANES), lambda b, i, k: (0, 0)))
            args.append(subln)
        out_specs = pl.BlockSpec((tq, out_w), lambda b, i, k: (b * nq + i, 0))
        sem = ("parallel", "parallel", "arbitrary")
    else:
        grid = (n_batch,)
        in_specs = [smem,
                    pl.BlockSpec((tq, q_w), lambda b: (ctx_blk + b, q_col)),
                    pl.BlockSpec((k_w, n_ctx), lambda b: (k_row, ctx_blk + b)),
                    pl.BlockSpec((n_ctx, v_w), lambda b: (ctx_blk + b, v_col))]
        args = [lam, q_arr, k_arr, v_arr]
        if diff:
            in_specs.append(pl.BlockSpec((1, LANES), lambda b: (0, 0)))
            args.append(subln)
        out_specs = pl.BlockSpec((tq, out_w), lambda b: (b, 0))
        sem = ("parallel",)
    return pl.pallas_call(
        kern, out_shape=out_shape, grid=grid, in_specs=in_specs, out_specs=out_specs,
        scratch_shapes=scratch, compiler_params=_cparams(sem), name=name,
    )(*args)


def _merge_kernel(x_ref, mod_ref, g2_ref, *refs, n_lat_tiles, has_ctx):
    ys = refs[:N_BRANCH]
    refs = refs[N_BRANCH:]
    if has_ctx:
        ycs = refs[:N_BRANCH]
        refs = refs[N_BRANCH:]
        is_ctx = pl.program_id(0) >= n_lat_tiles
    gts = refs[:N_BRANCH]
    wb_ref, wo_ref, rw_ref, xo_ref, h2_ref, sc_ref = refs[N_BRANCH:]
    m = mod_ref[0]
    mix = None
    for n_ in range(N_BRANCH):
        y = ys[n_][...]
        if has_ctx:
            y = jnp.where(is_ctx, ycs[n_][...], y)
        yb = jnp.dot(y, wb_ref[n_], preferred_element_type=F32)
        t = jax.nn.sigmoid(gts[n_][...].astype(F32)) * yb
        mix = t if mix is None else mix + t
    att = jnp.dot(mix.astype(BF), wo_ref[...], preferred_element_type=F32)
    xn = x_ref[...] + m[2:3] * att
    xo_ref[...] = xn
    h2 = _norm_mod(xn, g2_ref[...], m[4:5], m[3:4])
    h2_ref[...] = h2
    logits = jnp.dot(h2, rw_ref[...], preferred_element_type=F32, precision=lax.Precision.HIGHEST)
    sc_ref[...] = jax.nn.sigmoid(logits)


def _merge(xa, mod3, g2, ys, ys_ctx, tok_a, wb, wo, rw, n_rows, group_of_tile):
    d = xa.shape[1]
    n_lat_tiles = ys[0].shape[0] // TM
    has_ctx = ys_ctx is not None
    row = lambda w, c: pl.BlockSpec((TM, w), lambda i, c=c: (i, c))
    lat_row = pl.BlockSpec((TM, BRANCH_W), lambda i: (jnp.minimum(i, n_lat_tiles - 1), 0))
    in_specs = ([row(d, 0),
                 pl.BlockSpec((1, 8, d), lambda i: (group_of_tile(i), 0, 0)),
                 pl.BlockSpec((1, d), lambda i: (0, 0))]
                + [lat_row for _ in range(N_BRANCH)]
                + ([pl.BlockSpec((TM, BRANCH_W), lambda i: (0, 0)) for _ in range(N_BRANCH)] if has_ctx else [])
                + [row(d, c) for c in range(N_BRANCH)]
                + [pl.BlockSpec(wb.shape, lambda i: (0, 0, 0)),
                   pl.BlockSpec(wo.shape, lambda i: (0, 0)),
                   pl.BlockSpec(rw.shape, lambda i: (0, 0))])
    ys = list(ys) + (list(ys_ctx) if has_ctx else [])
    return pl.pallas_call(
        functools.partial(_merge_kernel, n_lat_tiles=n_lat_tiles, has_ctx=has_ctx),
        out_shape=(jax.ShapeDtypeStruct((n_rows, d), F32),
                   jax.ShapeDtypeStruct((n_rows, d), F32),
                   jax.ShapeDtypeStruct((n_rows, LANES), F32)),
        grid=(n_rows // TM,),
        in_specs=in_specs,
        out_specs=(row(d, 0), row(d, 0), row(LANES, 0)),
        compiler_params=_cparams(("parallel",)),
        name="merge",
    )(xa, mod3, g2, *ys, tok_a, tok_a, tok_a, tok_a, wb, wo, rw)


def _expert_kernel(blk_e_ref, src_ref, dst_ref, h2_hbm, w_ref, w1_ref, w3_ref, w2_ref, out_hbm,
                   xbuf, ybuf, sem_in, sem_out):
    i = pl.program_id(0)
    base = i * MOE_BLOCK

    def gather(r, c):
        t = src_ref[base + r]
        pltpu.make_async_copy(h2_hbm.at[pl.ds(t, 1)], xbuf.at[pl.ds(r, 1)], sem_in).start()
        return c

    lax.fori_loop(0, MOE_BLOCK, gather, 0)
    pltpu.make_async_copy(h2_hbm.at[pl.ds(0, MOE_BLOCK)], xbuf, sem_in).wait()
    xb = xbuf[...].astype(BF)
    a = jnp.dot(xb, w1_ref[0], preferred_element_type=F32)
    b = jnp.dot(xb, w3_ref[0], preferred_element_type=F32)
    hmid = (a * jax.nn.sigmoid(a) * b).astype(BF)
    ybuf[...] = jnp.dot(hmid, w2_ref[0], preferred_element_type=F32) * w_ref[...]

    def scatter(r, c):
        t = dst_ref[base + r]
        pltpu.make_async_copy(ybuf.at[pl.ds(r, 1)], out_hbm.at[pl.ds(t, 1)], sem_out).start()
        return c

    lax.fori_loop(0, MOE_BLOCK, scatter, 0)
    pltpu.make_async_copy(ybuf, out_hbm.at[pl.ds(0, MOE_BLOCK)], sem_out).wait()


def _experts(blk_e, src, dst, h2, wcol, w1, w3, w2):
    cap = src.shape[0]
    d = h2.shape[1]
    n_blk = cap // MOE_BLOCK
    wspec = lambda shp: pl.BlockSpec((1,) + shp, lambda i, be, s, t: (be[i], 0, 0))
    grid_spec = pltpu.PrefetchScalarGridSpec(
        num_scalar_prefetch=3,
        grid=(n_blk,),
        in_specs=[pl.BlockSpec(memory_space=pl.ANY),
                  pl.BlockSpec((MOE_BLOCK, 1), lambda i, be, s, t: (i, 0)),
                  wspec((d, D_EXPERT)), wspec((d, D_EXPERT)), wspec((D_EXPERT, d))],
        out_specs=pl.BlockSpec(memory_space=pl.ANY),
        scratch_shapes=[pltpu.VMEM((MOE_BLOCK, d), F32),
                        pltpu.VMEM((MOE_BLOCK, d), F32),
                        pltpu.SemaphoreType.DMA,
                        pltpu.SemaphoreType.DMA],
    )
    return pl.pallas_call(
        _expert_kernel,
        out_shape=jax.ShapeDtypeStruct((cap, d), F32),
        grid_spec=grid_spec,
        compiler_params=_cparams(("arbitrary",)),
        name="moe_experts",
    )(blk_e, src, dst, h2, wcol, w1, w3, w2)


def _route(scores, router_b, n):
    per_group = N_EXPERTS // N_GROUPS
    biased = scores + router_b.astype(F32)

    def top2(v):
        lane = jnp.arange(v.shape[-1], dtype=jnp.int32)
        i0 = jnp.argmax(v, axis=-1).astype(jnp.int32)
        v0 = jnp.max(v, axis=-1)
        rest = jnp.where(lane == i0[..., None], -jnp.inf, v)
        i1 = jnp.argmax(rest, axis=-1).astype(jnp.int32)
        v1 = jnp.max(rest, axis=-1)
        return (v0, v1), (i0, i1)

    (g0, g1), _ = top2(biased.reshape(n, N_GROUPS, per_group))
    group = jnp.argmax(g0 + g1, axis=-1)
    in_group = (jnp.arange(N_EXPERTS) // per_group)[None, :] == group[:, None]
    _, (e0, e1) = top2(jnp.where(in_group, biased, -jnp.inf))
    idx = jnp.stack([e0, e1], axis=-1)
    wts = jnp.take_along_axis(scores, idx, axis=-1)
    wts = wts / wts.sum(-1, keepdims=True)
    flat_e = idx.reshape(-1).astype(jnp.int32)
    n_asg = n * TOP_K
    onehot = (flat_e[:, None] == jnp.arange(N_EXPERTS, dtype=jnp.int32)[None, :]).astype(F32)
    oh = onehot.reshape(n_asg // MOE_BLOCK, MOE_BLOCK, N_EXPERTS)
    tri = jnp.tril(jnp.ones((MOE_BLOCK, MOE_BLOCK), F32), -1)
    within = jnp.einsum('ij,bjk->bik', tri, oh)
    blk_tot = oh.sum(axis=1)
    blk_off = jnp.cumsum(blk_tot, axis=0) - blk_tot
    rank = ((within + blk_off[:, None, :]) * oh).sum(-1).reshape(n_asg).astype(jnp.int32)
    counts = blk_tot.sum(axis=0).astype(jnp.int32)
    padded = (counts + MOE_BLOCK - 1) // MOE_BLOCK * MOE_BLOCK
    pad_end = jnp.cumsum(padded)
    pad_start = pad_end - padded
    dest = (onehot * pad_start.astype(F32)[None, :]).sum(-1).astype(jnp.int32) + rank
    n_blk = (n_asg + N_EXPERTS * (MOE_BLOCK - 1) + MOE_BLOCK - 1) // MOE_BLOCK
    cap = n_blk * MOE_BLOCK
    slot_a = jnp.full((cap,), -1, jnp.int32).at[dest].set(jnp.arange(n_asg, dtype=jnp.int32))
    real = slot_a >= 0
    a = jnp.maximum(slot_a, 0)
    src = jnp.where(real, a // TOP_K, 0).astype(jnp.int32)
    pad_rank = jnp.cumsum(jnp.logical_not(real).astype(jnp.int32)) - 1
    dst = jnp.where(real, (a % TOP_K) * n + a // TOP_K, n * TOP_K + pad_rank).astype(jnp.int32)
    wcol = jnp.where(real, wts.reshape(-1)[a], 0.0).astype(F32).reshape(cap, 1)
    blk_e = jnp.minimum(jnp.searchsorted(pad_end, jnp.arange(n_blk) * MOE_BLOCK, side='right'),
                        N_EXPERTS - 1).astype(jnp.int32)
    return blk_e, src, dst, wcol


def _combine_kernel(x_ref, mod_ref, f0_ref, f1_ref, o_ref):
    o_ref[...] = x_ref[...] + mod_ref[0][5:6] * (f0_ref[...] + f1_ref[...])


def _combine(xn, mod3, f, n_rows, group_of_tile):
    d = xn.shape[1]
    nt = n_rows // TM
    return pl.pallas_call(
        _combine_kernel,
        out_shape=jax.ShapeDtypeStruct((n_rows, d), F32),
        grid=(nt,),
        in_specs=[pl.BlockSpec((TM, d), lambda i: (i, 0)),
                  pl.BlockSpec((1, 8, d), lambda i: (group_of_tile(i), 0, 0)),
                  pl.BlockSpec((TM, d), lambda i: (i, 0)),
                  pl.BlockSpec((TM, d), lambda i: (nt + i, 0))],
        out_specs=pl.BlockSpec((TM, d), lambda i: (i, 0)),
        compiler_params=_cparams(("parallel",)),
        name="moe_combine",
    )(xn, mod3, f, f)


def _rope_tables(seq, dim, pad):
    t = jnp.arange(seq)
    rows = (t // GRID_W).astype(F32)
    cols = (t % GRID_W).astype(F32)
    quarter = dim // 4
    inv_freq = jnp.exp(-math.log(ROPE_BASE) * jnp.arange(quarter, dtype=F32) / quarter)
    ang = jnp.concatenate([inv_freq[:, None] * rows[None, :], inv_freq[:, None] * cols[None, :]], axis=0)
    cos = jnp.concatenate([jnp.cos(ang), jnp.ones((dim // 2, pad), F32)], axis=1)
    sin = jnp.concatenate([jnp.sin(ang), jnp.zeros((dim // 2, pad), F32)], axis=1)
    return cos, sin


def _win_mask_table(seq):
    nkb = seq // LANES
    nq = seq // WIN_TQ
    tabs = []
    for i in (0, 1, nq - 1):
        t = i * WIN_TQ + np.arange(WIN_TQ)[:, None]
        blk = np.clip(2 * i - 1 + np.arange(4), 0, nkb - 1)
        want = 2 * i - 1 + np.arange(4)
        s = (blk[:, None] * LANES + np.arange(LANES)[None, :]).reshape(-1)[None, :]
        ok = (np.abs(t - s) <= WIN_RADIUS) & np.repeat(blk == want, LANES)[None, :]
        tabs.append(np.where(ok, 0.0, NEG))
    return jnp.asarray(np.stack(tabs), F32)


def _nat_bias_table(rpb, seq):
    rows = seq // GRID_W
    nq = rows // NAT_ROWS_PER_STEP
    wc = NAT_WIN_COLS
    col = np.arange(GRID_W)
    col_start = np.clip(col - wc // 2, 0, GRID_W - wc)
    col_ok = (col[None, :] >= col_start[:, None]) & (col[None, :] < col_start[:, None] + wc)
    d_col = np.clip(col[None, :] - col[:, None] + (wc - 1), 0, 2 * wc - 2)
    tabs = []
    for i in (0, 1, nq - 1):
        r0 = NAT_ROWS_PER_STEP * i
        ws = np.clip(r0 - NAT_WIN_ROWS // 2, 0, rows - NAT_KEY_ROWS)
        r = r0 + np.arange(NAT_ROWS_PER_STEP)
        rs = np.clip(r - NAT_WIN_ROWS // 2, 0, rows - NAT_WIN_ROWS)
        krow = ws + np.arange(NAT_KEY_ROWS)
        row_ok = (krow[None, :] >= rs[:, None]) & (krow[None, :] < rs[:, None] + NAT_WIN_ROWS)
        d_row = np.clip(krow[None, :] - r[:, None] + (NAT_WIN_ROWS - 1), 0, 2 * NAT_WIN_ROWS - 2)
        ok = row_ok[:, None, :, None] & col_ok[None, :, None, :]
        sel_r = jnp.asarray(d_row.reshape(-1)[:, None] == np.arange(2 * NAT_WIN_ROWS - 1)[None, :], F32)
        sel_c = jnp.asarray(d_col.reshape(-1)[:, None] == np.arange(2 * wc - 1)[None, :], F32)
        bias = jnp.einsum('pr,hrc,qc->hpq', sel_r, rpb.astype(F32), sel_c, precision=lax.Precision.HIGHEST)
        bias = bias.reshape(rpb.shape[0], NAT_ROWS_PER_STEP, NAT_KEY_ROWS, GRID_W, GRID_W)
        bias = bias.transpose(0, 1, 3, 2, 4)
        bias = jnp.where(jnp.asarray(ok)[None], bias, NEG)
        tabs.append(bias.reshape(rpb.shape[0], NAT_ROWS_PER_STEP * GRID_W, NAT_KEY_ROWS * GRID_W))
    return jnp.stack(tabs)


def _bcast_rows(v, reps=1):
    return jnp.tile(jnp.broadcast_to(v.astype(F32)[:, None], (v.shape[0], LANES)), (reps, 1))


def _layer_params(l, p):
    w = p['w_in'][l]
    sizes = (512, 128, 128, 512, 512, 512, 512, 512, 512, 256, 160, 4096)
    offs = np.concatenate([[0], np.cumsum(sizes)])
    seg = lambda k: w[:, offs[k]:offs[k + 1]]
    wq, wk, wv, dq, dk, dv, nq, nk, nv, mqa, mkva, gates = [seg(k) for k in range(12)]
    d = w.shape[0]
    dup = lambda m: jnp.concatenate([m[:, :64], m[:, :64], m[:, 64:], m[:, 64:]], axis=1)
    mkva_p = jnp.concatenate([mkva, jnp.zeros((d, 256 - mkva.shape[1]), F32)], axis=1)
    w_a = jnp.concatenate([gates, dv, nv, mqa, mkva_p, dup(wv)], axis=1).astype(BF)
    w_bt = jnp.concatenate([wq, dq, nq, dk, nk, dup(wk)], axis=1).T.astype(BF)
    scale = HEAD_DIM ** -0.5 * LOG2E
    gain_b = jnp.concatenate([
        _bcast_rows(p['win_q_norm'][l] * scale, 8), _bcast_rows(p['dif_q_norm'][l] * scale, 8),
        _bcast_rows(p['nat_q_norm'][l] * scale, 8), _bcast_rows(p['dif_k_norm'][l], 8),
        _bcast_rows(p['nat_k_norm'][l], 8), _bcast_rows(p['win_k_norm'][l], 4)], axis=0)
    wkv = p['mla_wkv_b'][l].reshape(MLA_KV_LORA, MLA_HEADS, MLA_NOPE + MLA_V)
    wk_t = wkv[:, :, :MLA_NOPE].reshape(MLA_KV_LORA, -1).T.astype(BF)
    wv_m = wkv[:, :, MLA_NOPE:].reshape(MLA_KV_LORA, -1).astype(BF)
    lam_f = p['dif_lambda'][l].astype(F32)
    lam_init = 0.8 - 0.6 * math.exp(-0.3 * l)
    lam = jnp.exp(jnp.sum(lam_f[0] * lam_f[1])) - jnp.exp(jnp.sum(lam_f[2] * lam_f[3])) + lam_init
    return dict(
        w_a=w_a, w_bt=w_bt, gain_b=gain_b,
        g1=p['norm1_g'][l].reshape(1, d), g2=p['norm2_g'][l].reshape(1, d),
        sink=(p['win_sink'][l].astype(F32) * LOG2E).reshape(1, WIN_HEADS),
        gqa=p['mla_q_a_norm'][l].reshape(1, -1), gkva=p['mla_kv_a_norm'][l].reshape(1, -1),
        wq_t=p['mla_wq_b'][l].T.astype(BF), wk_t=wk_t, wv_m=wv_m,
        gq=_bcast_rows(p['mla_q_norm'][l] * (MLA_QK ** -0.5 * LOG2E)), gk=_bcast_rows(p['mla_k_norm'][l]),
        lam=lam.reshape(1, 1).astype(F32), lam_scale=1.0 - lam_init,
        subln=p['dif_subln'][l].astype(F32).reshape(1, DIF_V_DIM),
        wb=p['w_branch'][l].astype(BF), wo=p['w_out'][l].astype(BF),
        w1=p['moe_w1'][l].astype(BF), w3=p['moe_w3'][l].astype(BF), w2=p['moe_w2'][l].astype(BF),
    )


def kernel(x, c, ctx, c_ctx, ada_w, ada_b, norm1_g, norm2_g, w_in, win_q_norm, win_k_norm, win_sink,
           dif_q_norm, dif_k_norm, dif_lambda, dif_subln, nat_q_norm, nat_k_norm, nat_rpb,
           mla_q_a_norm, mla_wq_b, mla_kv_a_norm, mla_wkv_b, mla_q_norm, mla_k_norm,
           w_branch, w_out, router_w, router_b, moe_w1, moe_w3, moe_w2):
    p = dict(norm1_g=norm1_g, norm2_g=norm2_g, w_in=w_in, win_q_norm=win_q_norm, win_k_norm=win_k_norm,
             win_sink=win_sink, dif_q_norm=dif_q_norm, dif_k_norm=dif_k_norm, dif_lambda=dif_lambda,
             dif_subln=dif_subln, nat_q_norm=nat_q_norm, nat_k_norm=nat_k_norm,
             mla_q_a_norm=mla_q_a_norm, mla_wq_b=mla_wq_b, mla_kv_a_norm=mla_kv_a_norm,
             mla_wkv_b=mla_wkv_b, mla_q_norm=mla_q_norm, mla_k_norm=mla_k_norm,
             w_branch=w_branch, w_out=w_out, moe_w1=moe_w1, moe_w3=moe_w3, moe_w2=moe_w2)
    n_batch, seq, d = x.shape
    n_ctx = ctx.shape[1]
    depth = ada_w.shape[0]
    n_lat = n_batch * seq
    n_all = n_lat + n_batch * n_ctx
    assert seq % DENSE_TK == 0 and (n_batch * n_ctx) == TM and seq % TM == 0
    tiles_per_batch = seq // TM
    group_of_tile = lambda i: jnp.minimum(i // tiles_per_batch, n_batch)
    pos_of_tile = lambda i: jnp.where(i < n_batch * tiles_per_batch, i % tiles_per_batch, tiles_per_batch)

    cc = jnp.concatenate([c, c_ctx[None, :], jnp.zeros((8 - n_batch - 1, d), F32)], axis=0)
    mod = _modulation(cc, ada_w, ada_b)
    mod = mod[:, :n_batch + 1].reshape(depth, n_batch + 1, 6, d)
    mod = jnp.pad(mod, ((0, 0), (0, 0), (0, 2), (0, 0)))

    cos_h, sin_h = _rope_tables(seq, HEAD_DIM, TM)
    cos_m, sin_m = _rope_tables(seq, MLA_ROPE, TM)
    win_mask = _win_mask_table(seq)
    rw = jnp.pad(router_w.astype(F32), ((0, 0), (0, LANES - N_EXPERTS)))

    xa = jnp.concatenate([x.reshape(n_lat, d), ctx.reshape(n_batch * n_ctx, d)], axis=0)
    dense = functools.partial(_dense_attention, n_batch=n_batch, seq=seq, n_ctx=n_ctx)
    for l in range(depth):
        lp = _layer_params(l, p)
        want_ctx = l < depth - 1
        mod3 = mod[l]
        tok_a = _in_proj_a(xa, mod3, lp['g1'], lp['w_a'], group_of_tile)
        q_tok, kt = _in_proj_b(xa, mod3, lp['g1'], lp['w_bt'], lp['gain_b'], cos_h, sin_h,
                               group_of_tile, pos_of_tile)
        mq, mkt, mv = _mla_prep(tok_a, lp['gqa'], lp['gkva'], lp['wq_t'], lp['wk_t'], lp['wv_m'],
                                lp['gq'], lp['gk'], cos_m, sin_m, pos_of_tile)
        nat_bias = _nat_bias_table(nat_rpb[l].astype(F32) * LOG2E, seq)

        def branches(latent):
            y_win = _win_attention(lp['sink'], q_tok, kt, tok_a, win_mask, n_batch, seq, n_ctx, latent)
            y_dif = dense(lp['lam'], q_tok, 1, 512, kt, KT_DK // 512, 512, tok_a, A_DV // 512, 512,
                          lp['subln'], n_heads=2 * DIF_HEADS, packed=True, diff=True, latent=latent,
                          lam_scale=lp['lam_scale'], name="dif_attn" if latent else "dif_attn_ctx")
            if latent:
                y_nat = _nat_attention(q_tok, kt, tok_a, nat_bias, n_batch, seq, n_ctx)
            else:
                y_nat = dense(lp['lam'], q_tok, 2, 512, kt, KT_NK // 512, 512, tok_a, A_NV // 512, 512,
                              None, n_heads=NAT_HEADS, packed=True, diff=False, latent=False,
                              lam_scale=1.0, name="nat_attn_ctx")
            y_mla = dense(lp['lam'], mq, 0, 1024, mkt, 0, 1024, mv, 0, 512, None,
                          n_heads=MLA_HEADS, packed=False, diff=False, latent=latent, lam_scale=1.0,
                          name="mla_attn" if latent else "mla_attn_ctx")
            return [y_win, y_dif, y_nat, y_mla]

        ys = branches(True)
        ys_c = branches(False) if want_ctx else None
        n_rows = n_all if want_ctx else n_lat
        xn, h2, scores = _merge(xa, mod3, lp['g2'], ys, ys_c, tok_a, lp['wb'], lp['wo'], rw, n_rows,
                                group_of_tile)
        blk_e, src, dst, wcol = _route(scores[:, :N_EXPERTS], router_b, n_rows)
        f = _experts(blk_e, src, dst, h2, wcol, lp['w1'], lp['w3'], lp['w2'])
        xa = _combine(xn, mod3, f, n_rows, group_of_tile)
    return xa[:n_lat].reshape(n_batch, seq, d)
```

```python
import functools
import math

import jax
import jax.numpy as jnp
import numpy as np
from jax import lax
from jax.experimental import pallas as pl
from jax.experimental.pallas import tpu as pltpu

F32 = jnp.float32
BF = jnp.bfloat16

GRID_W = 64
HEAD_DIM = 64
N_BRANCH = 4
BRANCH_W = 512
ROPE_BASE = 10000.0
EPS = 1e-6
NEG = -1e30
LOG2E = math.log2(math.e)
WIN_HEADS, WIN_KV_HEADS, WIN_RADIUS = 8, 2, 128
DIF_HEADS, DIF_QK_DIM, DIF_V_DIM = 4, 64, 128
NAT_HEADS, NAT_WIN_ROWS, NAT_WIN_COLS = 8, 8, 16
MLA_HEADS, MLA_NOPE, MLA_ROPE, MLA_V, MLA_Q_LORA, MLA_KV_LORA = 8, 64, 32, 64, 256, 128
MLA_QK = MLA_NOPE + MLA_ROPE
N_EXPERTS, N_GROUPS, TOP_K, D_EXPERT, MOE_BLOCK = 16, 4, 2, 512, 256

LANES = 128
TM = 512
WIN_TQ = 256
NAT_ROWS_PER_STEP = 4
NAT_KEY_ROWS = 12
DENSE_TQ = 512
DENSE_TK = 1024
DENSE_RB = 64
DENSE_KC = 256
VMEM_LIMIT = 48 * 1024 * 1024

A_GATES, A_DV, A_NV, A_MQA, A_MKVA, A_WV = 0, 4096, 4608, 5120, 5376, 5632
A_COLS = 5888
Q_COLS = 1536
KT_DK, KT_NK, KT_WK = 0, 512, 1024
KT_ROWS = 1280
B_ROWS = Q_COLS + KT_ROWS


def _cparams(sem, vmem=VMEM_LIMIT):
    return pltpu.CompilerParams(dimension_semantics=sem, vmem_limit_bytes=vmem)


def _nt_dot(a, b):
    return lax.dot_general(a, b, (((1,), (1,)), ((), ())), preferred_element_type=F32)


def _norm_mod(x, g, sc, sh):
    ms = jnp.mean(x * x, axis=-1, keepdims=True)
    return (x * lax.rsqrt(ms + EPS) * g) * (1.0 + sc) + sh


def _lane_tile(a, n):
    reps = n // a.shape[1]
    return a if reps == 1 else jnp.concatenate([a] * reps, axis=1)


def _mod_kernel(c_ref, w_ref, b_ref, o_ref):
    cc = c_ref[...]
    a = cc * jax.nn.sigmoid(cc)
    o_ref[0] = jnp.dot(a, w_ref[0], preferred_element_type=F32,
                       precision=lax.Precision.HIGHEST) + b_ref[0]


def _modulation(cc, ada_w, ada_b):
    n_layers, d, d6 = ada_w.shape
    tn = 1536
    return pl.pallas_call(
        _mod_kernel,
        out_shape=jax.ShapeDtypeStruct((n_layers, 8, d6), F32),
        grid=(n_layers, d6 // tn),
        in_specs=[pl.BlockSpec((8, d), lambda l, j: (0, 0)),
                  pl.BlockSpec((1, d, tn), lambda l, j: (l, 0, j)),
                  pl.BlockSpec((1, 1, tn), lambda l, j: (l, 0, j))],
        out_specs=pl.BlockSpec((1, 8, tn), lambda l, j: (l, 0, j)),
        compiler_params=_cparams(("parallel", "parallel")),
        name="adaln_mod",
    )(cc, ada_w, ada_b.reshape(n_layers, 1, d6))


def _in_a_kernel(x_ref, mod_ref, g_ref, w_ref, o_ref, h_sc):
    @pl.when(pl.program_id(1) == 0)
    def _():
        m = mod_ref[0]
        h_sc[...] = _norm_mod(x_ref[...], g_ref[...], m[1:2], m[0:1]).astype(BF)

    o_ref[...] = jnp.dot(h_sc[...], w_ref[...], preferred_element_type=F32).astype(BF)


def _in_proj_a(xa, mod3, g1, w_a, group_of_tile):
    n, d = xa.shape
    tn = A_COLS // 2
    return pl.pallas_call(
        _in_a_kernel,
        out_shape=jax.ShapeDtypeStruct((n, A_COLS), BF),
        grid=(n // TM, A_COLS // tn),
        in_specs=[pl.BlockSpec((TM, d), lambda i, j: (i, 0)),
                  pl.BlockSpec((1, 8, d), lambda i, j: (group_of_tile(i), 0, 0)),
                  pl.BlockSpec((1, d), lambda i, j: (0, 0)),
                  pl.BlockSpec((d, tn), lambda i, j: (0, j))],
        out_specs=pl.BlockSpec((TM, tn), lambda i, j: (i, j)),
        scratch_shapes=[pltpu.VMEM((TM, d), BF)],
        compiler_params=_cparams(("parallel", "arbitrary")),
        name="in_proj_tok",
    )(xa, mod3, g1, w_a)


def _head_norm_rope(x, g, cos, sin, rope):
    ss = jnp.sum(x * x, axis=0, keepdims=True)
    y = x * lax.rsqrt(ss * (1.0 / HEAD_DIM) + EPS) * g
    if not rope:
        return y
    half = HEAD_DIM // 2
    y1, y2 = y[:half], y[half:]
    return jnp.concatenate([y1 * cos - y2 * sin, y1 * sin + y2 * cos], axis=0)


def _in_b_kernel(x_ref, mod_ref, g_ref, wt_ref, gain_ref, cos_ref, sin_ref, q_ref, kt_ref, acc_sc):
    m = mod_ref[0]
    h = _norm_mod(x_ref[...], g_ref[...], m[1:2], m[0:1]).astype(BF)
    acc_sc[...] = _nt_dot(wt_ref[...], h)
    tm = h.shape[0]
    cos = cos_ref[...]
    sin = sin_ref[...]

    def pair(r0, rope):
        hs = []
        for e in range(2):
            r = r0 + e * HEAD_DIM
            g = _lane_tile(gain_ref[r:r + HEAD_DIM, :], tm)
            hs.append(_head_norm_rope(acc_sc[r:r + HEAD_DIM, :], g, cos, sin, rope))
        return jnp.concatenate(hs, axis=0)

    for p in range(Q_COLS // LANES):
        y = pair(p * LANES, rope=p < 8)
        q_ref[:, p * LANES:(p + 1) * LANES] = y.T.astype(BF)
    for p in range(KT_ROWS // LANES):
        y = pair(Q_COLS + p * LANES, rope=not (4 <= p < 8))
        kt_ref[p * LANES:(p + 1) * LANES, :] = y.astype(BF)


def _in_proj_b(xa, mod3, g1, w_bt, gain_b, cos_t, sin_t, group_of_tile, pos_of_tile):
    n, d = xa.shape
    return pl.pallas_call(
        _in_b_kernel,
        out_shape=(jax.ShapeDtypeStruct((n, Q_COLS), BF),
                   jax.ShapeDtypeStruct((KT_ROWS, n), BF)),
        grid=(n // TM,),
        in_specs=[pl.BlockSpec((TM, d), lambda i: (i, 0)),
                  pl.BlockSpec((1, 8, d), lambda i: (group_of_tile(i), 0, 0)),
                  pl.BlockSpec((1, d), lambda i: (0, 0)),
                  pl.BlockSpec((B_ROWS, d), lambda i: (0, 0)),
                  pl.BlockSpec((B_ROWS, LANES), lambda i: (0, 0)),
                  pl.BlockSpec((HEAD_DIM // 2, TM), lambda i: (0, pos_of_tile(i))),
                  pl.BlockSpec((HEAD_DIM // 2, TM), lambda i: (0, pos_of_tile(i)))],
        out_specs=(pl.BlockSpec((TM, Q_COLS), lambda i: (i, 0)),
                   pl.BlockSpec((KT_ROWS, TM), lambda i: (0, i))),
        scratch_shapes=[pltpu.VMEM((B_ROWS, TM), F32)],
        compiler_params=_cparams(("parallel",)),
        name="in_proj_heads",
    )(xa, mod3, g1, w_bt, gain_b, cos_t, sin_t)


def _mla_kernel(qa_ref, kva_ref, gqa_ref, gkva_ref, wqt_ref, wkt_ref, wv_ref, gq_ref, gk_ref,
                cos_ref, sin_ref, mq_ref, mkt_ref, mv_ref):
    tm = qa_ref.shape[0]
    cos = cos_ref[...]
    sin = sin_ref[...]
    rh = MLA_ROPE // 2

    def rms_rows(x, g):
        ms = jnp.mean(x * x, axis=-1, keepdims=True)
        return x * lax.rsqrt(ms + EPS) * g

    def rope_rows(x):
        x1, x2 = x[:rh], x[rh:]
        return jnp.concatenate([x1 * cos - x2 * sin, x1 * sin + x2 * cos], axis=0)

    qa = rms_rows(qa_ref[...].astype(F32), gqa_ref[...]).astype(BF)
    qt = _nt_dot(wqt_ref[...], qa)
    kva = kva_ref[...].astype(F32)
    cn = rms_rows(kva[:, :MLA_KV_LORA], gkva_ref[...]).astype(BF)
    knt = _nt_dot(wkt_ref[...], cn)
    mv_ref[...] = jnp.dot(cn, wv_ref[...], preferred_element_type=F32).astype(BF)
    krope = kva[:, MLA_KV_LORA:].T[:MLA_ROPE]
    kr_ss = jnp.sum(krope * krope, axis=0, keepdims=True)
    gq = _lane_tile(gq_ref[...], tm)
    gk = _lane_tile(gk_ref[...], tm)
    zpad = jnp.zeros((LANES - MLA_QK, tm), F32)
    for hd in range(MLA_HEADS):
        x = qt[hd * MLA_QK:(hd + 1) * MLA_QK]
        ss = jnp.sum(x * x, axis=0, keepdims=True)
        y = x * lax.rsqrt(ss * (1.0 / MLA_QK) + EPS) * gq
        y = jnp.concatenate([y[:MLA_NOPE], rope_rows(y[MLA_NOPE:]), zpad], axis=0)
        mq_ref[:, hd * LANES:(hd + 1) * LANES] = y.T.astype(BF)
        kn = knt[hd * MLA_NOPE:(hd + 1) * MLA_NOPE]
        ss = jnp.sum(kn * kn, axis=0, keepdims=True) + kr_ss
        r = lax.rsqrt(ss * (1.0 / MLA_QK) + EPS)
        yk = jnp.concatenate([kn * r * gk[:MLA_NOPE], rope_rows(krope * r * gk[MLA_NOPE:]), zpad], axis=0)
        mkt_ref[hd * LANES:(hd + 1) * LANES, :] = yk.astype(BF)


def _mla_prep(tok_a, gqa, gkva, wq_t, wk_t, wv, gq, gk, cos_t, sin_t, pos_of_tile):
    n = tok_a.shape[0]
    hw = MLA_HEADS * LANES
    return pl.pallas_call(
        _mla_kernel,
        out_shape=(jax.ShapeDtypeStruct((n, hw), BF),
                   jax.ShapeDtypeStruct((hw, n), BF),
                   jax.ShapeDtypeStruct((n, MLA_HEADS * MLA_V), BF)),
        grid=(n // TM,),
        in_specs=[pl.BlockSpec((TM, 256), lambda i: (i, A_MQA // 256)),
                  pl.BlockSpec((TM, 256), lambda i: (i, A_MKVA // 256)),
                  pl.BlockSpec((1, MLA_Q_LORA), lambda i: (0, 0)),
                  pl.BlockSpec((1, MLA_KV_LORA), lambda i: (0, 0)),
                  pl.BlockSpec(wq_t.shape, lambda i: (0, 0)),
                  pl.BlockSpec(wk_t.shape, lambda i: (0, 0)),
                  pl.BlockSpec(wv.shape, lambda i: (0, 0)),
                  pl.BlockSpec((MLA_QK, LANES), lambda i: (0, 0)),
                  pl.BlockSpec((MLA_QK, LANES), lambda i: (0, 0)),
                  pl.BlockSpec((MLA_ROPE // 2, TM), lambda i: (0, pos_of_tile(i))),
                  pl.BlockSpec((MLA_ROPE // 2, TM), lambda i: (0, pos_of_tile(i)))],
        out_specs=(pl.BlockSpec((TM, hw), lambda i: (i, 0)),
                   pl.BlockSpec((hw, TM), lambda i: (0, i)),
                   pl.BlockSpec((TM, MLA_HEADS * MLA_V), lambda i: (i, 0))),
        compiler_params=_cparams(("parallel",)),
        name="mla_prep",
    )(tok_a, tok_a, gqa, gkva, wq_t, wk_t, wv, gq, gk, cos_t, sin_t)


def _half_mask(shape):
    return lax.broadcasted_iota(jnp.int32, shape, 1) < (LANES // 2)


def _select_half(q, e, lo_mask):
    zero = jnp.zeros_like(q)
    return jnp.where(lo_mask, q, zero) if e == 0 else jnp.where(lo_mask, zero, q)


def _local_softmax_out(parts, extra_logit):
    m = parts[0][0].max(axis=-1, keepdims=True)
    for s, _ in parts[1:]:
        m = jnp.maximum(m, s.max(axis=-1, keepdims=True))
    if extra_logit is not None:
        m = jnp.maximum(m, extra_logit)
    z = None
    o = None
    for s, v in parts:
        p = jnp.exp2(s - m)
        zs = p.sum(axis=-1, keepdims=True)
        os_ = jnp.dot(p.astype(BF), v, preferred_element_type=F32)
        z = zs if z is None else z + zs
        o = os_ if o is None else o + os_
    if extra_logit is not None:
        z = z + jnp.exp2(extra_logit - m)
    return o / z


def _win_kernel(sink_ref, q_ref, *refs, band):
    if band:
        k0, k1, k2, k3, v0, v1, v2, v3, kc_ref, vc_ref, mask_ref, o_ref = refs
        kb = jnp.concatenate([k0[...], k1[...], k2[...], k3[...]], axis=1)
        vb = jnp.concatenate([v0[...], v1[...], v2[...], v3[...]], axis=0)
        mask = mask_ref[0]
    else:
        kc_ref, vc_ref, o_ref = refs
    q = q_ref[...]
    lo = _half_mask((q.shape[0], LANES))
    group = WIN_HEADS // WIN_KV_HEADS
    for j in range(WIN_HEADS // 2):
        qp = q[:, j * LANES:(j + 1) * LANES]
        g = (2 * j) // group
        kc = kc_ref[g * LANES:(g + 1) * LANES, :]
        vc = vc_ref[:, g * LANES:(g + 1) * LANES]
        outs = []
        for e in range(2):
            qm = _select_half(qp, e, lo)
            parts = []
            if band:
                s = jnp.dot(qm, kb[g * LANES:(g + 1) * LANES, :], preferred_element_type=F32) + mask
                parts.append((s, vb[:, g * LANES:(g + 1) * LANES]))
            parts.append((jnp.dot(qm, kc, preferred_element_type=F32), vc))
            outs.append(_local_softmax_out(parts, sink_ref[0, 2 * j + e]))
        o_ref[:, j * LANES:(j + 1) * LANES] = jnp.where(lo, outs[0], outs[1]).astype(BF)


def _win_attention(sink, q_tok, kt, tok_a, mask_tbl, n_batch, seq, n_ctx, latent):
    ctx_blk = (n_batch * seq) // n_ctx
    kc_spec = lambda f: pl.BlockSpec((2 * LANES, n_ctx), f)
    vc_spec = lambda f: pl.BlockSpec((n_ctx, 2 * LANES), f)
    smem = pl.BlockSpec(memory_space=pltpu.SMEM)
    n_out = n_batch * (seq if latent else n_ctx)
    out_shape = jax.ShapeDtypeStruct((n_out, WIN_HEADS * HEAD_DIM), BF)
    if not latent:
        return pl.pallas_call(
            functools.partial(_win_kernel, band=False),
            out_shape=out_shape,
            grid=(n_batch,),
            in_specs=[smem,
                      pl.BlockSpec((n_ctx, 512), lambda b: (ctx_blk + b, 0)),
                      kc_spec(lambda b: (KT_WK // 256, ctx_blk + b)),
                      vc_spec(lambda b: (ctx_blk + b, A_WV // 256))],
            out_specs=pl.BlockSpec((n_ctx, 512), lambda b: (b, 0)),
            compiler_params=_cparams(("parallel",)),
            name="win_attn_ctx",
        )(sink, q_tok, kt, tok_a)
    nq = seq // WIN_TQ
    nkb = seq // LANES

    def kidx(j):
        return lambda b, i: (KT_WK // 256, b * nkb + jnp.clip(2 * i - 1 + j, 0, nkb - 1))

    def vidx(j):
        return lambda b, i: (b * nkb + jnp.clip(2 * i - 1 + j, 0, nkb - 1), A_WV // 256)

    def variant(b, i):
        return (jnp.where(i == 0, 0, jnp.where(i == nq - 1, 2, 1)), 0, 0)

    in_specs = ([smem, pl.BlockSpec((WIN_TQ, 512), lambda b, i: (b * nq + i, 0))]
                + [pl.BlockSpec((2 * LANES, LANES), kidx(j)) for j in range(4)]
                + [pl.BlockSpec((LANES, 2 * LANES), vidx(j)) for j in range(4)]
                + [kc_spec(lambda b, i: (KT_WK // 256, ctx_blk + b)),
                   vc_spec(lambda b, i: (ctx_blk + b, A_WV // 256)),
                   pl.BlockSpec((1, WIN_TQ, 4 * LANES), variant)])
    return pl.pallas_call(
        functools.partial(_win_kernel, band=True),
        out_shape=out_shape,
        grid=(n_batch, nq),
        in_specs=in_specs,
        out_specs=pl.BlockSpec((WIN_TQ, 512), lambda b, i: (b * nq + i, 0)),
        compiler_params=_cparams(("parallel", "parallel")),
        name="win_attn",
    )(sink, q_tok, kt, kt, kt, kt, tok_a, tok_a, tok_a, tok_a, kt, tok_a, mask_tbl)


def _nat_kernel(q_ref, k0, k1, k2, v0, v1, v2, kc_ref, vc_ref, bias_ref, o_ref):
    q = q_ref[...]
    kb = jnp.concatenate([k0[...], k1[...], k2[...]], axis=1)
    vb = jnp.concatenate([v0[...], v1[...], v2[...]], axis=0)
    lo = _half_mask((q.shape[0], LANES))
    for j in range(NAT_HEADS // 2):
        sl = slice(j * LANES, (j + 1) * LANES)
        qp = q[:, sl]
        outs = []
        for e in range(2):
            qm = _select_half(qp, e, lo)
            s = jnp.dot(qm, kb[sl, :], preferred_element_type=F32) + bias_ref[0, 2 * j + e]
            sc = jnp.dot(qm, kc_ref[sl, :], preferred_element_type=F32)
            outs.append(_local_softmax_out([(s, vb[:, sl]), (sc, vc_ref[:, sl])], None))
        o_ref[:, sl] = jnp.where(lo, outs[0], outs[1]).astype(BF)


def _nat_attention(q_tok, kt, tok_a, bias_tbl, n_batch, seq, n_ctx):
    tq =NAT_ROWS_PER_STEP * GRID_W
    nq = seq // tq
    rows = seq // GRID_W
    ctx_blk = (n_batch * seq) // n_ctx
    q_col = 2
    k_row = KT_NK // 512
    v_col = A_NV // 512

    def wstart(i):
        return jnp.clip(NAT_ROWS_PER_STEP * i - NAT_WIN_ROWS // 2, 0, rows - NAT_KEY_ROWS) // NAT_ROWS_PER_STEP

    def kidx(j):
        return lambda b, i: (k_row, b * nq + wstart(i) + j)

    def vidx(j):
        return lambda b, i: (b * nq + wstart(i) + j, v_col)

    def variant(b, i):
        return (jnp.where(i == 0, 0, jnp.where(i == nq - 1, 2, 1)), 0, 0, 0)

    nk = NAT_KEY_ROWS * GRID_W
    in_specs = ([pl.BlockSpec((tq, 512), lambda b, i: (b * nq + i, q_col))]
                + [pl.BlockSpec((512, tq), kidx(j)) for j in range(3)]
                + [pl.BlockSpec((tq, 512), vidx(j)) for j in range(3)]
                + [pl.BlockSpec((512, n_ctx), lambda b, i: (k_row, ctx_blk + b)),
                   pl.BlockSpec((n_ctx, 512), lambda b, i: (ctx_blk + b, v_col)),
                   pl.BlockSpec((1, NAT_HEADS, tq, nk), variant)])
    return pl.pallas_call(
        _nat_kernel,
        out_shape=jax.ShapeDtypeStruct((n_batch * seq, NAT_HEADS * HEAD_DIM), BF),
        grid=(n_batch, nq),
        in_specs=in_specs,
        out_specs=pl.BlockSpec((tq, 512), lambda b, i: (b * nq + i, 0)),
        compiler_params=_cparams(("parallel", "parallel")),
        name="nat_attn",
    )(q_tok, kt, kt, kt, tok_a, tok_a, tok_a, kt, tok_a, bias_tbl)


def _dense_kernel(lam_ref, q_ref, kc_ref, vc_ref, *refs, n_heads, packed, diff, latent, lam_scale):
    if latent:
        k_ref, v_ref = refs[0], refs[1]
        refs = refs[2:]
    if diff:
        subln_ref, o_ref, qm_sc, m_sc, l_sc, acc_sc, s_sc, p_sc = refs
    else:
        o_ref, qm_sc, m_sc, l_sc, acc_sc, s_sc, p_sc = refs
    kt_step = pl.program_id(2) if latent else 0
    tq = q_ref.shape[0]

    def kv_slices(h):
        blk = h // 2 if packed else h
        ks = slice(blk * LANES, (blk + 1) * LANES)
        vs = ks if packed else slice((h // 2) * LANES, (h // 2 + 1) * LANES)
        return ks, vs

    def step(h, k_ref_, v_ref_):
        ks, vs = kv_slices(h)
        nk = k_ref_.shape[1]
        slot = h % 2
        s_sc[slot, :, :nk] = jnp.dot(qm_sc[h], k_ref_[ks, :], preferred_element_type=F32)
        for r in range(tq // DENSE_RB):
            rows = slice(r * DENSE_RB, (r + 1) * DENSE_RB)
            mx = s_sc[slot, rows, 0:LANES]
            for c in range(1, nk // LANES):
                mx = jnp.maximum(mx, s_sc[slot, rows, c * LANES:(c + 1) * LANES])
            m_old = m_sc[h, rows, :]
            m_new = jnp.maximum(m_old, jnp.max(mx, axis=-1, keepdims=True))
            alpha = jnp.exp2(m_old - m_new)
            lsum = None
            for c in range(nk // LANES):
                cols = slice(c * LANES, (c + 1) * LANES)
                p = jnp.exp2(s_sc[slot, rows, cols] - m_new)
                lsum = p if lsum is None else lsum + p
                p_sc[slot, rows, cols] = p.astype(BF)
            m_sc[h, rows, :] = m_new
            l_sc[h, rows, :] = alpha * l_sc[h, rows, :] + jnp.sum(lsum, axis=-1, keepdims=True)
            acc_sc[h, rows, :] = alpha * acc_sc[h, rows, :]
        acc_sc[h] += jnp.dot(p_sc[slot, :, :nk], v_ref_[:, vs], preferred_element_type=F32)

    @pl.when(kt_step == 0)
    def _():
        q = q_ref[...]
        lo = _half_mask((tq, LANES))
        for h in range(n_heads):
            if packed:
                qp = q[:, (h // 2) * LANES:(h // 2 + 1) * LANES]
                qm_sc[h] = _select_half(qp, h % 2, lo)
            else:
                qm_sc[h] = q[:, h * LANES:(h + 1) * LANES]
        m_sc[...] = jnp.full(m_sc.shape, NEG, F32)
        l_sc[...] = jnp.zeros(l_sc.shape, F32)
        acc_sc[...] = jnp.zeros(acc_sc.shape, F32)
        for h in range(n_heads):
            step(h, kc_ref, vc_ref)

    if latent:
        for h in range(n_heads):
            step(h, k_ref, v_ref)
        last = kt_step == pl.num_programs(2) - 1
    else:
        last = True

    def finish():
        lo = _half_mask((tq, LANES))
        if diff:
            lam = lam_ref[0, 0]
            for hv in range(n_heads // 2):
                y = (acc_sc[2 * hv] / l_sc[2 * hv]
                     - lam * (acc_sc[2 * hv + 1] / l_sc[2 * hv + 1]))
                ms = jnp.mean(y * y, axis=-1, keepdims=True)
                y = y * lax.rsqrt(ms + EPS) * subln_ref[...] * lam_scale
                o_ref[:, hv * LANES:(hv + 1) * LANES] = y.astype(BF)
        else:
            for hp in range(n_heads // 2):
                o0 = acc_sc[2 * hp] / l_sc[2 * hp]
                o1 = acc_sc[2 * hp + 1] / l_sc[2 * hp + 1]
                o_ref[:, hp * LANES:(hp + 1) * LANES] = jnp.where(lo, o0, o1).astype(BF)

    if latent:
        pl.when(last)(finish)
    else:
        finish()


def _dense_attention(lam, q_arr, q_col, q_w, k_arr, k_row, k_w, v_arr, v_col, v_w, subln,
                     n_batch, seq, n_ctx, *, n_heads, packed, diff, latent, lam_scale, name):
    n = n_batch * (seq if latent else n_ctx)
    ctx_blk = (n_batch * seq) // n_ctx
    out_w = v_w
    smem = pl.BlockSpec(memory_space=pltpu.SMEM)
    kern = functools.partial(_dense_kernel, n_heads=n_heads, packed=packed, diff=diff,
                             latent=latent, lam_scale=lam_scale)
    tq = DENSE_TQ if latent else n_ctx
    max_nk = DENSE_TK if latent else n_ctx
    scratch = [pltpu.VMEM((n_heads, tq, LANES), BF),
               pltpu.VMEM((n_heads, tq, LANES), F32),
               pltpu.VMEM((n_heads, tq, LANES), F32),
               pltpu.VMEM((n_heads, tq, LANES), F32),
               pltpu.VMEM((2, tq, max_nk), F32),
               pltpu.VMEM((2, tq, max_nk), BF)]
    out_shape = jax.ShapeDtypeStruct((n, out_w), BF)
    if latent:
        nq = seq // tq
        nk = seq // DENSE_TK
        grid = (n_batch, nq, nk)
        in_specs = [smem,
                    pl.BlockSpec((tq, q_w), lambda b, i, k: (b * nq + i, q_col)),
                    pl.BlockSpec((k_w, n_ctx), lambda b, i, k: (k_row, ctx_blk + b)),
                    pl.BlockSpec((n_ctx, v_w), lambda b, i, k: (ctx_blk + b, v_col)),
                    pl.BlockSpec((k_w, DENSE_TK), lambda b, i, k: (k_row, b * nk + k)),
                    pl.BlockSpec((DENSE_TK, v_w), lambda b, i, k: (b * nk + k, v_col))]
        args = [lam, q_arr, k_arr, v_arr, k_arr, v_arr]
        if diff:
            in_specs.append(pl.BlockSpec((1, LANES), lambda b, i, k: (0, 0)))
            args.append(subln)
        out_specs = pl.BlockSpec((tq, out_w), lambda b, i, k: (b * nq + i, 0))
        sem = ("parallel", "parallel", "arbitrary")
    else:
        grid = (n_batch,)
        in_specs = [smem,
                    pl.BlockSpec((tq, q_w), lambda b: (ctx_blk + b, q_col)),
                    pl.BlockSpec((k_w, n_ctx), lambda b: (k_row, ctx_blk + b)),
                    pl.BlockSpec((n_ctx, v_w), lambda b: (ctx_blk + b, v_col))]
        args = [lam, q_arr, k_arr, v_arr]
        if diff:
            in_specs.append(pl.BlockSpec((1, LANES), lambda b: (0, 0)))
            args.append(subln)
        out_specs = pl.BlockSpec((tq, out_w), lambda b: (b, 0))
        sem = ("parallel",)
    return pl.pallas_call(
        kern, out_shape=out_shape, grid=grid, in_specs=in_specs, out_specs=out_specs,
        scratch_shapes=scratch, compiler_params=_cparams(sem), name=name,
    )(*args)


SAFE_SUM_LOG2 = 100.0


def _dense_fixed_kernel(par_ref, q_ref, kc_ref, vc_ref, *refs, n_heads, packed, diff, latent, lam_scale):
    if latent:
        k_ref, v_ref = refs[0], refs[1]
        refs = refs[2:]
    ones_ref = refs[0]
    refs = refs[1:]
    if diff:
        subln_ref, o_ref, flag_ref, qm_sc, acc_sc, p_sc = refs
    else:
        o_ref, flag_ref, qm_sc, acc_sc, p_sc = refs
    kt_step = pl.program_id(2) if latent else 0
    tq = q_ref.shape[0]
    ref_logit = par_ref[0, 1]

    def step(h, k_ref_, v_ref_, first):
        blk = h // 2 if packed else h
        ks = slice(blk * LANES, (blk + 1) * LANES)
        vs = ks if packed else slice((h // 2) * LANES, (h // 2 + 1) * LANES)
        nk = k_ref_.shape[1]
        slot = h % 2
        for c in range(nk // DENSE_KC):
            cols = slice(c * DENSE_KC, (c + 1) * DENSE_KC)
            s = jnp.dot(qm_sc[h], k_ref_[ks, cols], preferred_element_type=F32)
            p_sc[slot, :, cols] = jnp.exp2(s - ref_logit).astype(BF)
        v_aug = jnp.concatenate([v_ref_[:, vs], ones_ref[:nk, :]], axis=1)
        pv = jnp.dot(p_sc[slot, :, :nk], v_aug, preferred_element_type=F32)
        if first:
            acc_sc[h] = pv
        else:
            acc_sc[h] += pv

    @pl.when(kt_step == 0)
    def _():
        q = q_ref[...]
        lo = _half_mask((tq, LANES))
        for h in range(n_heads):
            if packed:
                qp = q[:, (h // 2) * LANES:(h // 2 + 1) * LANES]
                qm_sc[h] = _select_half(qp, h % 2, lo)
            else:
                qm_sc[h] = q[:, h * LANES:(h + 1) * LANES]
        for h in range(n_heads):
            step(h, kc_ref, vc_ref, True)

    if latent:
        for h in range(n_heads):
            step(h, k_ref, v_ref, False)
        last = kt_step == pl.num_programs(2) - 1

    def finish():
        lo = _half_mask((tq, LANES))
        outs, bad = [], []
        for h in range(n_heads):
            a = acc_sc[h]
            l = a[:, LANES:]
            unsafe = jnp.logical_not((l > 2.0 ** -SAFE_SUM_LOG2) & (l < 2.0 ** SAFE_SUM_LOG2))
            bad.append(jnp.max(jnp.where(unsafe, 1.0, 0.0), axis=0, keepdims=True))
            outs.append(a[:, :LANES] / l)
        flag_ref[0] = jnp.concatenate(bad, axis=0)
        if diff:
            lam = par_ref[0, 0]
            for hv in range(n_heads // 2):
                y = outs[2 * hv] - lam * outs[2 * hv + 1]
                ms = jnp.mean(y * y, axis=-1, keepdims=True)
                y = y * lax.rsqrt(ms + EPS) * subln_ref[...] * lam_scale
                o_ref[:, hv * LANES:(hv + 1) * LANES] = y.astype(BF)
        else:
            for hp in range(n_heads // 2):
                o_ref[:, hp * LANES:(hp + 1) * LANES] = jnp.where(lo, outs[2 * hp], outs[2 * hp + 1]).astype(BF)

    if latent:
        pl.when(last)(finish)
    else:
        finish()


def _dense_fixed_attention(par, q_arr, q_col, q_w, k_arr, k_row, k_w, v_arr, v_col, v_w, subln, ones,
                           n_batch, seq, n_ctx, *, n_heads, packed, diff, latent, lam_scale, name):
    n = n_batch * (seq if latent else n_ctx)
    ctx_blk = (n_batch * seq) // n_ctx
    smem = pl.BlockSpec(memory_space=pltpu.SMEM)
    kern = functools.partial(_dense_fixed_kernel, n_heads=n_heads, packed=packed, diff=diff,
                             latent=latent, lam_scale=lam_scale)
    tq = DENSE_TQ if latent else n_ctx
    max_nk = DENSE_TK if latent else n_ctx
    scratch = [pltpu.VMEM((n_heads, tq, LANES), BF),
               pltpu.VMEM((n_heads, tq, 2 * LANES), F32),
               pltpu.VMEM((2, tq, max_nk), BF)]
    if latent:
        nq = seq // tq
        nk = seq // DENSE_TK
        grid = (n_batch, nq, nk)
        ix = lambda f: (lambda b, i, k: f(b, i, k))
        q_ix = ix(lambda b, i, k: (b * nq + i, q_col))
        in_specs = [smem,
                    pl.BlockSpec((tq, q_w), q_ix),
                    pl.BlockSpec((k_w, n_ctx), ix(lambda b, i, k: (k_row, ctx_blk + b))),
                    pl.BlockSpec((n_ctx, v_w), ix(lambda b, i, k: (ctx_blk + b, v_col))),
                    pl.BlockSpec((k_w, DENSE_TK), ix(lambda b, i, k: (k_row, b * nk + k))),
                    pl.BlockSpec((DENSE_TK, v_w), ix(lambda b, i, k: (b * nk + k, v_col))),
                    pl.BlockSpec((DENSE_TK, LANES), ix(lambda b, i, k: (0, 0)))]
        args = [par, q_arr, k_arr, v_arr, k_arr, v_arr, ones]
        const_ix = ix(lambda b, i, k: (0, 0))
        out_specs = (pl.BlockSpec((tq, v_w), ix(lambda b, i, k: (b * nq + i, 0))),
                     pl.BlockSpec((1, n_heads, LANES), ix(lambda b, i, k: (b * nq + i, 0, 0))))
        n_flag = n_batch * nq
        sem = ("parallel", "parallel", "arbitrary")
    else:
        grid = (n_batch,)
        in_specs = [smem,
                    pl.BlockSpec((tq, q_w), lambda b: (ctx_blk + b, q_col)),
                    pl.BlockSpec((k_w, n_ctx), lambda b: (k_row, ctx_blk + b)),
                    pl.BlockSpec((n_ctx, v_w), lambda b: (ctx_blk + b, v_col)),
                    pl.BlockSpec((DENSE_TK, LANES), lambda b: (0, 0))]
        args = [par, q_arr, k_arr, v_arr, ones]
        const_ix = lambda b: (0, 0)
        out_specs = (pl.BlockSpec((tq, v_w), lambda b: (b, 0)),
                     pl.BlockSpec((1, n_heads, LANES), lambda b: (b, 0, 0)))
        n_flag = n_batch
        sem = ("parallel",)
    if diff:
        in_specs.append(pl.BlockSpec((1, LANES), const_ix))
        args.append(subln)
    return pl.pallas_call(
        kern,
        out_shape=(jax.ShapeDtypeStruct((n, v_w), BF), jax.ShapeDtypeStruct((n_flag, n_heads, LANES), F32)),
        grid=grid, in_specs=in_specs, out_specs=out_specs,
        scratch_shapes=scratch, compiler_params=_cparams(sem), name=name + "_fixed",
    )(*args)


def _merge_kernel(x_ref, mod_ref, g2_ref, *refs, n_lat_tiles, has_ctx):
    ys = refs[:N_BRANCH]
    refs = refs[N_BRANCH:]
    if has_ctx:
        ycs = refs[:N_BRANCH]
        refs = refs[N_BRANCH:]
        is_ctx = pl.program_id(0) >= n_lat_tiles
    gts = refs[:N_BRANCH]
    wb_ref, wo_ref, rw_ref, xo_ref, h2_ref, sc_ref = refs[N_BRANCH:]
    m = mod_ref[0]
    mix = None
    for n_ in range(N_BRANCH):
        y = ys[n_][...]
        if has_ctx:
            y = jnp.where(is_ctx, ycs[n_][...], y)
        yb = jnp.dot(y, wb_ref[n_], preferred_element_type=F32)
        t = jax.nn.sigmoid(gts[n_][...].astype(F32)) * yb
        mix = t if mix is None else mix + t
    att = jnp.dot(mix.astype(BF), wo_ref[...], preferred_element_type=F32)
    xn = x_ref[...] + m[2:3] * att
    xo_ref[...] = xn
    h2 = _norm_mod(xn, g2_ref[...], m[4:5], m[3:4])
    h2_ref[...] = h2
    logits = jnp.dot(h2, rw_ref[...], preferred_element_type=F32, precision=lax.Precision.HIGHEST)
    sc_ref[...] = jax.nn.sigmoid(logits)


def _merge(xa, mod3, g2, ys, ys_ctx, tok_a, wb, wo, rw, n_rows, group_of_tile):
    d = xa.shape[1]
    n_lat_tiles = ys[0].shape[0] // TM
    has_ctx = ys_ctx is not None
    row = lambda w, c: pl.BlockSpec((TM, w), lambda i, c=c: (i, c))
    lat_row = pl.BlockSpec((TM, BRANCH_W), lambda i: (jnp.minimum(i, n_lat_tiles - 1), 0))
    in_specs = ([row(d, 0),
                 pl.BlockSpec((1, 8, d), lambda i: (group_of_tile(i), 0, 0)),
                 pl.BlockSpec((1, d), lambda i: (0, 0))]
                + [lat_row for _ in range(N_BRANCH)]
                + ([pl.BlockSpec((TM, BRANCH_W), lambda i: (0, 0)) for _ in range(N_BRANCH)] if has_ctx else [])
                + [row(d, c) for c in range(N_BRANCH)]
                + [pl.BlockSpec(wb.shape, lambda i: (0, 0, 0)),
                   pl.BlockSpec(wo.shape, lambda i: (0, 0)),
                   pl.BlockSpec(rw.shape, lambda i: (0, 0))])
    ys = list(ys) + (list(ys_ctx) if has_ctx else [])
    return pl.pallas_call(
        functools.partial(_merge_kernel, n_lat_tiles=n_lat_tiles, has_ctx=has_ctx),
        out_shape=(jax.ShapeDtypeStruct((n_rows, d), F32),
                   jax.ShapeDtypeStruct((n_rows, d), F32),
                   jax.ShapeDtypeStruct((n_rows, LANES), F32)),
        grid=(n_rows // TM,),
        in_specs=in_specs,
        out_specs=(row(d, 0), row(d, 0), row(LANES, 0)),
        compiler_params=_cparams(("parallel",)),
        name="merge",
    )(xa, mod3, g2, *ys, tok_a, tok_a, tok_a, tok_a, wb, wo, rw)


def _expert_kernel(blk_e_ref, src_ref, dst_ref, h2_hbm, w_ref, w1_ref, w3_ref, w2_ref, out_hbm,
                   xbuf, ybuf, sem_in, sem_out):
    i = pl.program_id(0)
    n_steps = pl.num_programs(0)
    slot = i % 2

    def start_gather(blk, s):
        def body(r, c):
            t = src_ref[blk * MOE_BLOCK + r]
            pltpu.make_async_copy(h2_hbm.at[pl.ds(t, 1)], xbuf.at[s, pl.ds(r, 1)], sem_in.at[s]).start()
            return c
        lax.fori_loop(0, MOE_BLOCK, body, 0, unroll=8)

    def wait_gather(s):
        pltpu.make_async_copy(h2_hbm.at[pl.ds(0, MOE_BLOCK)], xbuf.at[s], sem_in.at[s]).wait()

    def start_scatter(blk, s):
        def body(r, c):
            t = dst_ref[blk * MOE_BLOCK + r]
            pltpu.make_async_copy(ybuf.at[s, pl.ds(r, 1)], out_hbm.at[pl.ds(t, 1)], sem_out.at[s]).start()
            return c
        lax.fori_loop(0, MOE_BLOCK, body, 0, unroll=8)

    def wait_scatter(s):
        pltpu.make_async_copy(ybuf.at[s], out_hbm.at[pl.ds(0, MOE_BLOCK)], sem_out.at[s]).wait()

    @pl.when(i == 0)
    def _():
        start_gather(0, 0)

    @pl.when(i + 1 < n_steps)
    def _():
        start_gather(i + 1, 1 - slot)

    wait_gather(slot)

    @pl.when(i >= 2)
    def _():
        wait_scatter(slot)

    xb = xbuf[slot].astype(BF)
    a = jnp.dot(xb, w1_ref[0], preferred_element_type=F32)
    b = jnp.dot(xb, w3_ref[0], preferred_element_type=F32)
    hmid = (a * jax.nn.sigmoid(a) * b).astype(BF)
    ybuf[slot] = jnp.dot(hmid, w2_ref[0], preferred_element_type=F32) * w_ref[...]
    start_scatter(i, slot)

    @pl.when(i == n_steps - 1)
    def _():
        wait_scatter(slot)

        @pl.when(n_steps >= 2)
        def _():
            wait_scatter(1 - slot)


def _experts(blk_e, src, dst, h2, wcol, w1, w3, w2):
    cap = src.shape[0]
    d = h2.shape[1]
    n_blk = cap // MOE_BLOCK
    wspec = lambda shp: pl.BlockSpec((1,) + shp, lambda i, be, s, t: (be[i], 0, 0))
    grid_spec = pltpu.PrefetchScalarGridSpec(
        num_scalar_prefetch=3,
        grid=(n_blk,),
        in_specs=[pl.BlockSpec(memory_space=pl.ANY),
                  pl.BlockSpec((MOE_BLOCK, 1), lambda i, be, s, t: (i, 0)),
                  wspec((d, D_EXPERT)), wspec((d, D_EXPERT)), wspec((D_EXPERT, d))],
        out_specs=pl.BlockSpec(memory_space=pl.ANY),
        scratch_shapes=[pltpu.VMEM((2, MOE_BLOCK, d), F32),
                        pltpu.VMEM((2, MOE_BLOCK, d), F32),
                        pltpu.SemaphoreType.DMA((2,)),
                        pltpu.SemaphoreType.DMA((2,))],
    )
    return pl.pallas_call(
        _expert_kernel,
        out_shape=jax.ShapeDtypeStruct((cap, d), F32),
        grid_spec=grid_spec,
        compiler_params=_cparams(("arbitrary",)),
        name="moe_experts",
    )(blk_e, src, dst, h2, wcol, w1, w3, w2)


def _route(scores, router_b, n):
    per_group = N_EXPERTS // N_GROUPS
    biased = scores + router_b.astype(F32)

    def top2(v):
        lane = jnp.arange(v.shape[-1], dtype=jnp.int32)
        i0 = jnp.argmax(v, axis=-1).astype(jnp.int32)
        v0 = jnp.max(v, axis=-1)
        rest = jnp.where(lane == i0[..., None], -jnp.inf, v)
        i1 = jnp.argmax(rest, axis=-1).astype(jnp.int32)
        v1 = jnp.max(rest, axis=-1)
        return (v0, v1), (i0, i1)

    (g0, g1), _ = top2(biased.reshape(n, N_GROUPS, per_group))
    group = jnp.argmax(g0 + g1, axis=-1)
    in_group = (jnp.arange(N_EXPERTS) // per_group)[None, :] == group[:, None]
    _, (e0, e1) = top2(jnp.where(in_group, biased, -jnp.inf))
    idx = jnp.stack([e0, e1], axis=-1)
    wts = jnp.take_along_axis(scores, idx, axis=-1)
    wts = wts / wts.sum(-1, keepdims=True)
    flat_e = idx.reshape(-1).astype(jnp.int32)
    n_asg = n * TOP_K
    onehot = (flat_e[:, None] == jnp.arange(N_EXPERTS, dtype=jnp.int32)[None, :]).astype(F32)
    oh = onehot.reshape(n_asg // MOE_BLOCK, MOE_BLOCK, N_EXPERTS)
    tri = jnp.tril(jnp.ones((MOE_BLOCK, MOE_BLOCK), F32), -1)
    within = jnp.einsum('ij,bjk->bik', tri, oh)
    blk_tot = oh.sum(axis=1)
    blk_off = jnp.cumsum(blk_tot, axis=0) - blk_tot
    rank = ((within + blk_off[:, None, :]) * oh).sum(-1).reshape(n_asg).astype(jnp.int32)
    counts = blk_tot.sum(axis=0).astype(jnp.int32)
    padded = (counts + MOE_BLOCK - 1) // MOE_BLOCK * MOE_BLOCK
    pad_end = jnp.cumsum(padded)
    pad_start = pad_end - padded
    dest = (onehot * pad_start.astype(F32)[None, :]).sum(-1).astype(jnp.int32) + rank
    n_blk = (n_asg + N_EXPERTS * (MOE_BLOCK - 1) + MOE_BLOCK - 1) // MOE_BLOCK
    cap = n_blk * MOE_BLOCK
    slot_a = jnp.full((cap,), -1, jnp.int32).at[dest].set(jnp.arange(n_asg, dtype=jnp.int32))
    real = slot_a >= 0
    a = jnp.maximum(slot_a, 0)
    src = jnp.where(real, a // TOP_K, 0).astype(jnp.int32)
    pad_rank = jnp.cumsum(jnp.logical_not(real).astype(jnp.int32)) - 1
    dst = jnp.where(real, (a % TOP_K) * n + a // TOP_K, n * TOP_K + pad_rank).astype(jnp.int32)
    wcol = jnp.where(real, wts.reshape(-1)[a], 0.0).astype(F32).reshape(cap, 1)
    blk_e = jnp.minimum(jnp.searchsorted(pad_end, jnp.arange(n_blk) * MOE_BLOCK, side='right'),
                        N_EXPERTS - 1).astype(jnp.int32)
    return blk_e, src, dst, wcol


def _combine_kernel(x_ref, mod_ref, f0_ref, f1_ref, o_ref):
    o_ref[...] = x_ref[...] + mod_ref[0][5:6] * (f0_ref[...] + f1_ref[...])


def _combine(xn, mod3, f, n_rows, group_of_tile):
    d = xn.shape[1]
    nt = n_rows // TM
    return pl.pallas_call(
        _combine_kernel,
        out_shape=jax.ShapeDtypeStruct((n_rows, d), F32),
        grid=(nt,),
        in_specs=[pl.BlockSpec((TM, d), lambda i: (i, 0)),
                  pl.BlockSpec((1, 8, d), lambda i: (group_of_tile(i), 0, 0)),
                  pl.BlockSpec((TM, d), lambda i: (i, 0)),
                  pl.BlockSpec((TM, d), lambda i: (nt + i, 0))],
        out_specs=pl.BlockSpec((TM, d), lambda i: (i, 0)),
        compiler_params=_cparams(("parallel",)),
        name="moe_combine",
    )(xn, mod3, f, f)


def _rope_tables(seq, dim, pad):
    t = jnp.arange(seq)
    rows = (t // GRID_W).astype(F32)
    cols = (t % GRID_W).astype(F32)
    quarter = dim // 4
    inv_freq = jnp.exp(-math.log(ROPE_BASE) * jnp.arange(quarter, dtype=F32) / quarter)
    ang = jnp.concatenate([inv_freq[:, None] * rows[None, :], inv_freq[:, None] * cols[None, :]], axis=0)
    cos = jnp.concatenate([jnp.cos(ang), jnp.ones((dim // 2, pad), F32)], axis=1)
    sin = jnp.concatenate([jnp.sin(ang), jnp.zeros((dim // 2, pad), F32)], axis=1)
    return cos, sin


def _win_mask_table(seq):
    nkb = seq // LANES
    nq = seq // WIN_TQ
    tabs = []
    for i in (0, 1, nq - 1):
        t = i * WIN_TQ + np.arange(WIN_TQ)[:, None]
        blk = np.clip(2 * i - 1 + np.arange(4), 0, nkb - 1)
        want = 2 * i - 1 + np.arange(4)
        s = (blk[:, None] * LANES + np.arange(LANES)[None, :]).reshape(-1)[None, :]
        ok = (np.abs(t - s) <= WIN_RADIUS) & np.repeat(blk == want, LANES)[None, :]
        tabs.append(np.where(ok, 0.0, NEG))
    return jnp.asarray(np.stack(tabs), F32)


def _nat_bias_table(rpb, seq):
    rows = seq // GRID_W
    nq = rows // NAT_ROWS_PER_STEP
    wc = NAT_WIN_COLS
    col = np.arange(GRID_W)
    col_start = np.clip(col - wc // 2, 0, GRID_W - wc)
    col_ok = (col[None, :] >= col_start[:, None]) & (col[None, :] < col_start[:, None] + wc)
    d_col = np.clip(col[None, :] - col[:, None] + (wc - 1), 0, 2 * wc - 2)
    tabs = []
    for i in (0, 1, nq - 1):
        r0 = NAT_ROWS_PER_STEP * i
        ws = np.clip(r0 - NAT_WIN_ROWS // 2, 0, rows - NAT_KEY_ROWS)
        r = r0 + np.arange(NAT_ROWS_PER_STEP)
        rs = np.clip(r - NAT_WIN_ROWS // 2, 0, rows - NAT_WIN_ROWS)
        krow = ws + np.arange(NAT_KEY_ROWS)
        row_ok = (krow[None, :] >= rs[:, None]) & (krow[None, :] < rs[:, None] + NAT_WIN_ROWS)
        d_row = np.clip(krow[None, :] - r[:, None] + (NAT_WIN_ROWS - 1), 0, 2 * NAT_WIN_ROWS - 2)
        ok = row_ok[:, None, :, None] & col_ok[None, :, None, :]
        sel_r = jnp.asarray(d_row.reshape(-1)[:, None] == np.arange(2 * NAT_WIN_ROWS - 1)[None, :], F32)
        sel_c = jnp.asarray(d_col.reshape(-1)[:, None] == np.arange(2 * wc - 1)[None, :], F32)
        bias = jnp.einsum('pr,hrc,qc->hpq', sel_r, rpb.astype(F32), sel_c, precision=lax.Precision.HIGHEST)
        bias = bias.reshape(rpb.shape[0], NAT_ROWS_PER_STEP, NAT_KEY_ROWS, GRID_W, GRID_W)
        bias = bias.transpose(0, 1, 3, 2, 4)
        bias = jnp.where(jnp.asarray(ok)[None], bias, NEG)
        tabs.append(bias.reshape(rpb.shape[0], NAT_ROWS_PER_STEP * GRID_W, NAT_KEY_ROWS * GRID_W))
    return jnp.stack(tabs)


def _bcast_rows(v, reps=1):
    return jnp.tile(jnp.broadcast_to(v.astype(F32)[:, None], (v.shape[0], LANES)), (reps, 1))


def _layer_params(l, p):
    w = p['w_in'][l]
    sizes = (512, 128, 128, 512, 512, 512, 512, 512, 512, 256, 160, 4096)
    offs = np.concatenate([[0], np.cumsum(sizes)])
    seg = lambda k: w[:, offs[k]:offs[k + 1]]
    wq, wk, wv, dq, dk, dv, nq, nk, nv, mqa, mkva, gates = [seg(k) for k in range(12)]
    d = w.shape[0]
    dup = lambda m: jnp.concatenate([m[:, :64], m[:, :64], m[:, 64:], m[:, 64:]], axis=1)
    mkva_p = jnp.concatenate([mkva, jnp.zeros((d, 256 - mkva.shape[1]), F32)], axis=1)
    w_a = jnp.concatenate([gates, dv, nv, mqa, mkva_p, dup(wv)], axis=1).astype(BF)
    w_bt = jnp.concatenate([wq, dq, nq, dk, nk, dup(wk)], axis=1).T.astype(BF)
    scale = HEAD_DIM ** -0.5 * LOG2E
    gain_b = jnp.concatenate([
        _bcast_rows(p['win_q_norm'][l] * scale, 8), _bcast_rows(p['dif_q_norm'][l] * scale, 8),
        _bcast_rows(p['nat_q_norm'][l] * scale, 8), _bcast_rows(p['dif_k_norm'][l], 8),
        _bcast_rows(p['nat_k_norm'][l], 8), _bcast_rows(p['win_k_norm'][l], 4)], axis=0)
    wkv = p['mla_wkv_b'][l].reshape(MLA_KV_LORA, MLA_HEADS, MLA_NOPE + MLA_V)
    wk_t = wkv[:, :, :MLA_NOPE].reshape(MLA_KV_LORA, -1).T.astype(BF)
    wv_m = wkv[:, :, MLA_NOPE:].reshape(MLA_KV_LORA, -1).astype(BF)
    def logit_bound(gq, gk, dim):
        return dim * jnp.max(jnp.abs(gq)) * jnp.max(jnp.abs(gk)) * (1.0 + 2.0 ** -7)

    lam_f = p['dif_lambda'][l].astype(F32)
    lam_init = 0.8 - 0.6 * math.exp(-0.3 * l)
    lam = jnp.exp(jnp.sum(lam_f[0] * lam_f[1])) - jnp.exp(jnp.sum(lam_f[2] * lam_f[3])) + lam_init
    return dict(
        w_a=w_a, w_bt=w_bt, gain_b=gain_b,
        g1=p['norm1_g'][l].reshape(1, d), g2=p['norm2_g'][l].reshape(1, d),
        sink=(p['win_sink'][l].astype(F32) * LOG2E).reshape(1, WIN_HEADS),
        gqa=p['mla_q_a_norm'][l].reshape(1, -1), gkva=p['mla_kv_a_norm'][l].reshape(1, -1),
        wq_t=p['mla_wq_b'][l].T.astype(BF), wk_t=wk_t, wv_m=wv_m,
        gq=_bcast_rows(p['mla_q_norm'][l] * (MLA_QK ** -0.5 * LOG2E)), gk=_bcast_rows(p['mla_k_norm'][l]),
        lam=lam.reshape(1, 1).astype(F32), lam_scale=1.0 - lam_init,
        m_dif=logit_bound(p['dif_q_norm'][l] * scale, p['dif_k_norm'][l], HEAD_DIM),
        m_nat=logit_bound(p['nat_q_norm'][l] * scale, p['nat_k_norm'][l], HEAD_DIM),
        m_mla=logit_bound(p['mla_q_norm'][l] * (MLA_QK ** -0.5 * LOG2E), p['mla_k_norm'][l], MLA_QK),
        subln=p['dif_subln'][l].astype(F32).reshape(1, DIF_V_DIM),
        wb=p['w_branch'][l].astype(BF), wo=p['w_out'][l].astype(BF),
        w1=p['moe_w1'][l].astype(BF), w3=p['moe_w3'][l].astype(BF), w2=p['moe_w2'][l].astype(BF),
    )


def kernel(x, c, ctx, c_ctx, ada_w, ada_b, norm1_g, norm2_g, w_in, win_q_norm, win_k_norm, win_sink,
           dif_q_norm, dif_k_norm, dif_lambda, dif_subln, nat_q_norm, nat_k_norm, nat_rpb,
           mla_q_a_norm, mla_wq_b, mla_kv_a_norm, mla_wkv_b, mla_q_norm, mla_k_norm,
           w_branch, w_out, router_w, router_b, moe_w1, moe_w3, moe_w2):
    p = dict(norm1_g=norm1_g, norm2_g=norm2_g, w_in=w_in, win_q_norm=win_q_norm, win_k_norm=win_k_norm,
             win_sink=win_sink, dif_q_norm=dif_q_norm, dif_k_norm=dif_k_norm, dif_lambda=dif_lambda,
             dif_subln=dif_subln, nat_q_norm=nat_q_norm, nat_k_norm=nat_k_norm,
             mla_q_a_norm=mla_q_a_norm, mla_wq_b=mla_wq_b, mla_kv_a_norm=mla_kv_a_norm,
             mla_wkv_b=mla_wkv_b, mla_q_norm=mla_q_norm, mla_k_norm=mla_k_norm,
             w_branch=w_branch, w_out=w_out, moe_w1=moe_w1, moe_w3=moe_w3, moe_w2=moe_w2)
    n_batch, seq, d = x.shape
    n_ctx = ctx.shape[1]
    depth = ada_w.shape[0]
    n_lat = n_batch * seq
    n_all = n_lat + n_batch * n_ctx
    assert seq % DENSE_TK == 0 and (n_batch * n_ctx) == TM and seq % TM == 0
    tiles_per_batch = seq // TM
    group_of_tile = lambda i: jnp.minimum(i // tiles_per_batch, n_batch)
    pos_of_tile = lambda i: jnp.where(i < n_batch * tiles_per_batch, i % tiles_per_batch, tiles_per_batch)

    cc = jnp.concatenate([c, c_ctx[None, :], jnp.zeros((8 - n_batch - 1, d), F32)], axis=0)
    mod = _modulation(cc, ada_w, ada_b)
    mod = mod[:, :n_batch + 1].reshape(depth, n_batch + 1, 6, d)
    mod = jnp.pad(mod, ((0, 0), (0, 0), (0, 2), (0, 0)))

    cos_h, sin_h = _rope_tables(seq, HEAD_DIM, TM)
    cos_m, sin_m = _rope_tables(seq, MLA_ROPE, TM)
    win_mask = _win_mask_table(seq)
    rw = jnp.pad(router_w.astype(F32), ((0, 0), (0, LANES - N_EXPERTS)))

    xa = jnp.concatenate([x.reshape(n_lat, d), ctx.reshape(n_batch * n_ctx, d)], axis=0)
    ones = jnp.ones((DENSE_TK, LANES), BF)

    def dense(lam, logit_bound, q_arr, q_col, q_w, k_arr, k_row, k_w, v_arr, v_col, v_w, subln, **kw):
        operands = (q_arr, q_col, q_w, k_arr, k_row, k_w, v_arr, v_col, v_w, subln)
        par = jnp.concatenate([lam, logit_bound.reshape(1, 1).astype(F32)], axis=1)
        y, flag = _dense_fixed_attention(par, *operands, ones, n_batch, seq, n_ctx, **kw)
        return lax.cond(jnp.max(flag) > 0.0,
                        lambda: _dense_attention(lam, *operands, n_batch, seq, n_ctx, **kw),
                        lambda: y)
    for l in range(depth):
        lp = _layer_params(l, p)
        want_ctx = l < depth - 1
        mod3 = mod[l]
        tok_a = _in_proj_a(xa, mod3, lp['g1'], lp['w_a'], group_of_tile)
        q_tok, kt = _in_proj_b(xa, mod3, lp['g1'], lp['w_bt'], lp['gain_b'], cos_h, sin_h,
                               group_of_tile, pos_of_tile)
        mq, mkt, mv = _mla_prep(tok_a, lp['gqa'], lp['gkva'], lp['wq_t'], lp['wk_t'], lp['wv_m'],
                                lp['gq'], lp['gk'], cos_m, sin_m, pos_of_tile)
        nat_bias = _nat_bias_table(nat_rpb[l].astype(F32) * LOG2E, seq)

        def branches(latent):
            y_win = _win_attention(lp['sink'], q_tok, kt, tok_a, win_mask, n_batch, seq, n_ctx, latent)
            y_dif = dense(lp['lam'], lp['m_dif'], q_tok, 1, 512, kt, KT_DK // 512, 512, tok_a, A_DV // 512, 512,
                          lp['subln'], n_heads=2 * DIF_HEADS, packed=True, diff=True, latent=latent,
                          lam_scale=lp['lam_scale'], name="dif_attn" if latent else "dif_attn_ctx")
            if latent:
                y_nat = _nat_attention(q_tok, kt, tok_a, nat_bias, n_batch, seq, n_ctx)
            else:
                y_nat = dense(lp['lam'], lp['m_nat'], q_tok, 2, 512, kt, KT_NK // 512, 512, tok_a, A_NV // 512, 512,
                              None, n_heads=NAT_HEADS, packed=True, diff=False, latent=False,
                              lam_scale=1.0, name="nat_attn_ctx")
            y_mla = dense(lp['lam'], lp['m_mla'], mq, 0, 1024, mkt, 0, 1024, mv, 0, 512, None,
                          n_heads=MLA_HEADS, packed=False, diff=False, latent=latent, lam_scale=1.0,
                          name="mla_attn" if latent else "mla_attn_ctx")
            return [y_win, y_dif, y_nat, y_mla]

        ys = branches(True)
        ys_c = branches(False) if want_ctx else None
        n_rows = n_all if want_ctx else n_lat
        xn, h2, scores = _merge(xa, mod3, lp['g2'], ys, ys_c, tok_a, lp['wb'], lp['wo'], rw, n_rows,
                                group_of_tile)
        blk_e, src, dst, wcol = _route(scores[:, :N_EXPERTS], router_b, n_rows)
        f = _experts(blk_e, src, dst, h2, wcol, lp['w1'], lp['w3'], lp['w2'])
        xa = _combine(xn, mod3, f, n_rows, group_of_tile)
    return xa[:n_lat].reshape(n_batch, seq, d)
```

```python
import functools
import math

import jax
import jax.numpy as jnp
import numpy as np
from jax import lax
from jax.experimental import pallas as pl
from jax.experimental.pallas import tpu as pltpu

F32 = jnp.float32
BF = jnp.bfloat16

GRID_W = 64
HEAD_DIM = 64
N_BRANCH = 4
BRANCH_W = 512
ROPE_BASE = 10000.0
EPS = 1e-6
NEG = -1e30
LOG2E = math.log2(math.e)
WIN_HEADS, WIN_KV_HEADS, WIN_RADIUS = 8, 2, 128
DIF_HEADS, DIF_QK_DIM, DIF_V_DIM = 4, 64, 128
NAT_HEADS, NAT_WIN_ROWS, NAT_WIN_COLS = 8, 8, 16
MLA_HEADS, MLA_NOPE, MLA_ROPE, MLA_V, MLA_Q_LORA, MLA_KV_LORA = 8, 64, 32, 64, 256, 128
MLA_QK = MLA_NOPE + MLA_ROPE
N_EXPERTS, N_GROUPS, TOP_K, D_EXPERT, MOE_BLOCK = 16, 4, 2, 512, 256

LANES = 128
TM = 512
WIN_TQ = 256
NAT_ROWS_PER_STEP = 4
NAT_KEY_ROWS = 12
DENSE_TQ = 1024
DENSE_TQ_ONLINE = 512
DENSE_TK = 1024
DENSE_RB = 64
DENSE_KC = 256
VMEM_LIMIT = 48 * 1024 * 1024

A_GATES, A_DV, A_NV, A_MQA, A_MKVA, A_WV = 0, 4096, 4608, 5120, 5376, 5632
A_COLS = 5888
Q_COLS = 1536
KT_DK, KT_NK, KT_WK = 0, 512, 1024
KT_ROWS = 1280
B_ROWS = Q_COLS + KT_ROWS


def _cparams(sem, vmem=VMEM_LIMIT):
    return pltpu.CompilerParams(dimension_semantics=sem, vmem_limit_bytes=vmem)


def _nt_dot(a, b):
    return lax.dot_general(a, b, (((1,), (1,)), ((), ())), preferred_element_type=F32)


def _norm_mod(x, g, sc, sh):
    ms = jnp.mean(x * x, axis=-1, keepdims=True)
    return (x * lax.rsqrt(ms + EPS) * g) * (1.0 + sc) + sh


def _lane_tile(a, n):
    reps = n // a.shape[1]
    return a if reps == 1 else jnp.concatenate([a] * reps, axis=1)


def _mod_kernel(c_ref, w_ref, b_ref, o_ref):
    cc = c_ref[...]
    a = cc * jax.nn.sigmoid(cc)
    o_ref[0] = jnp.dot(a, w_ref[0], preferred_element_type=F32,
                       precision=lax.Precision.HIGHEST) + b_ref[0]


def _modulation(cc, ada_w, ada_b):
    n_layers, d, d6 = ada_w.shape
    tn = 1536
    return pl.pallas_call(
        _mod_kernel,
        out_shape=jax.ShapeDtypeStruct((n_layers, 8, d6), F32),
        grid=(n_layers, d6 // tn),
        in_specs=[pl.BlockSpec((8, d), lambda l, j: (0, 0)),
                  pl.BlockSpec((1, d, tn), lambda l, j: (l, 0, j)),
                  pl.BlockSpec((1, 1, tn), lambda l, j: (l, 0, j))],
        out_specs=pl.BlockSpec((1, 8, tn), lambda l, j: (l, 0, j)),
        compiler_params=_cparams(("parallel", "parallel")),
        name="adaln_mod",
    )(cc, ada_w, ada_b.reshape(n_layers, 1, d6))


def _in_a_kernel(x_ref, mod_ref, g_ref, w_ref, o_ref):
    m = mod_ref[0]
    h = _norm_mod(x_ref[...], g_ref[...], m[1:2], m[0:1]).astype(BF)
    o_ref[...] = jnp.dot(h, w_ref[...], preferred_element_type=F32).astype(BF)


def _in_proj_a(xa, mod3, g1, w_a, group_of_tile):
    n, d = xa.shape
    tn = A_COLS // 2
    return pl.pallas_call(
        _in_a_kernel,
        out_shape=jax.ShapeDtypeStruct((n, A_COLS), BF),
        grid=(A_COLS // tn, n // TM),
        in_specs=[pl.BlockSpec((TM, d), lambda j, i: (i, 0)),
                  pl.BlockSpec((1, 8, d), lambda j, i: (group_of_tile(i), 0, 0)),
                  pl.BlockSpec((1, d), lambda j, i: (0, 0)),
                  pl.BlockSpec((d, tn), lambda j, i: (0, j))],
        out_specs=pl.BlockSpec((TM, tn), lambda j, i: (i, j)),
        compiler_params=_cparams(("parallel", "parallel")),
        name="in_proj_tok",
    )(xa, mod3, g1, w_a)


def _head_norm_rope(x, g, cos, sin, rope):
    ss = jnp.sum(x * x, axis=0, keepdims=True)
    y = x * lax.rsqrt(ss * (1.0 / HEAD_DIM) + EPS) * g
    if not rope:
        return y
    half = HEAD_DIM // 2
    y1, y2 = y[:half], y[half:]
    return jnp.concatenate([y1 * cos - y2 * sin, y1 * sin + y2 * cos], axis=0)


def _in_b_kernel(x_ref, mod_ref, g_ref, wt_ref, gain_ref, cos_ref, sin_ref, q_ref, kt_ref, acc_sc):
    m = mod_ref[0]
    h = _norm_mod(x_ref[...], g_ref[...], m[1:2], m[0:1]).astype(BF)
    acc_sc[...] = _nt_dot(wt_ref[...], h)
    tm = h.shape[0]
    cos = cos_ref[...]
    sin = sin_ref[...]

    def pair(r0, rope):
        hs = []
        for e in range(2):
            r = r0 + e * HEAD_DIM
            g = _lane_tile(gain_ref[r:r + HEAD_DIM, :], tm)
            hs.append(_head_norm_rope(acc_sc[r:r + HEAD_DIM, :], g, cos, sin, rope))
        return jnp.concatenate(hs, axis=0)

    for p in range(Q_COLS // LANES):
        y = pair(p * LANES, rope=p < 8)
        q_ref[:, p * LANES:(p + 1) * LANES] = y.T.astype(BF)
    for p in range(KT_ROWS // LANES):
        y = pair(Q_COLS + p * LANES, rope=not (4 <= p < 8))
        kt_ref[p * LANES:(p + 1) * LANES, :] = y.astype(BF)


def _in_proj_b(xa, mod3, g1, w_bt, gain_b, cos_t, sin_t, group_of_tile, pos_of_tile):
    n, d = xa.shape
    return pl.pallas_call(
        _in_b_kernel,
        out_shape=(jax.ShapeDtypeStruct((n, Q_COLS), BF),
                   jax.ShapeDtypeStruct((KT_ROWS, n), BF)),
        grid=(n // TM,),
        in_specs=[pl.BlockSpec((TM, d), lambda i: (i, 0)),
                  pl.BlockSpec((1, 8, d), lambda i: (group_of_tile(i), 0, 0)),
                  pl.BlockSpec((1, d), lambda i: (0, 0)),
                  pl.BlockSpec((B_ROWS, d), lambda i: (0, 0)),
                  pl.BlockSpec((B_ROWS, LANES), lambda i: (0, 0)),
                  pl.BlockSpec((HEAD_DIM // 2, TM), lambda i: (0, pos_of_tile(i))),
                  pl.BlockSpec((HEAD_DIM // 2, TM), lambda i: (0, pos_of_tile(i)))],
        out_specs=(pl.BlockSpec((TM, Q_COLS), lambda i: (i, 0)),
                   pl.BlockSpec((KT_ROWS, TM), lambda i: (0, i))),
        scratch_shapes=[pltpu.VMEM((B_ROWS, TM), F32)],
        compiler_params=_cparams(("parallel",)),
        name="in_proj_heads",
    )(xa, mod3, g1, w_bt, gain_b, cos_t, sin_t)


def _mla_kernel(qa_ref, kva_ref, gqa_ref, gkva_ref, wqt_ref, wkt_ref, wv_ref, gq_ref, gk_ref,
                cos_ref, sin_ref, mq_ref, mkt_ref, mv_ref):
    tm = qa_ref.shape[0]
    cos = cos_ref[...]
    sin = sin_ref[...]
    rh = MLA_ROPE // 2

    def rms_rows(x, g):
        ms = jnp.mean(x * x, axis=-1, keepdims=True)
        return x * lax.rsqrt(ms + EPS) * g

    def rope_rows(x):
        x1, x2 = x[:rh], x[rh:]
        return jnp.concatenate([x1 * cos - x2 * sin, x1 * sin + x2 * cos], axis=0)

    qa = rms_rows(qa_ref[...].astype(F32), gqa_ref[...]).astype(BF)
    qt = _nt_dot(wqt_ref[...], qa)
    kva = kva_ref[...].astype(F32)
    cn = rms_rows(kva[:, :MLA_KV_LORA], gkva_ref[...]).astype(BF)
    knt = _nt_dot(wkt_ref[...], cn)
    mv_ref[...] = jnp.dot(cn, wv_ref[...], preferred_element_type=F32).astype(BF)
    krope = kva[:, MLA_KV_LORA:].T[:MLA_ROPE]
    kr_ss = jnp.sum(krope * krope, axis=0, keepdims=True)
    gq = _lane_tile(gq_ref[...], tm)
    gk = _lane_tile(gk_ref[...], tm)
    zpad = jnp.zeros((LANES - MLA_QK, tm), F32)
    for hd in range(MLA_HEADS):
        x = qt[hd * MLA_QK:(hd + 1) * MLA_QK]
        ss = jnp.sum(x * x, axis=0, keepdims=True)
        y = x * lax.rsqrt(ss * (1.0 / MLA_QK) + EPS) * gq
        y = jnp.concatenate([y[:MLA_NOPE], rope_rows(y[MLA_NOPE:]), zpad], axis=0)
        mq_ref[:, hd * LANES:(hd + 1) * LANES] = y.T.astype(BF)
        kn = knt[hd * MLA_NOPE:(hd + 1) * MLA_NOPE]
        ss = jnp.sum(kn * kn, axis=0, keepdims=True) + kr_ss
        r = lax.rsqrt(ss * (1.0 / MLA_QK) + EPS)
        yk = jnp.concatenate([kn * r * gk[:MLA_NOPE], rope_rows(krope * r * gk[MLA_NOPE:]), zpad], axis=0)
        mkt_ref[hd * LANES:(hd + 1) * LANES, :] = yk.astype(BF)


def _mla_prep(tok_a, gqa, gkva, wq_t, wk_t, wv, gq, gk, cos_t, sin_t, pos_of_tile):
    n = tok_a.shape[0]
    hw = MLA_HEADS * LANES
    return pl.pallas_call(
        _mla_kernel,
        out_shape=(jax.ShapeDtypeStruct((n, hw), BF),
                   jax.ShapeDtypeStruct((hw, n), BF),
                   jax.ShapeDtypeStruct((n, MLA_HEADS * MLA_V), BF)),
        grid=(n // TM,),
        in_specs=[pl.BlockSpec((TM, 256), lambda i: (i, A_MQA // 256)),
                  pl.BlockSpec((TM, 256), lambda i: (i, A_MKVA // 256)),
                  pl.BlockSpec((1, MLA_Q_LORA), lambda i: (0, 0)),
                  pl.BlockSpec((1, MLA_KV_LORA), lambda i: (0, 0)),
                  pl.BlockSpec(wq_t.shape, lambda i: (0, 0)),
                  pl.BlockSpec(wk_t.shape, lambda i: (0, 0)),
                  pl.BlockSpec(wv.shape, lambda i: (0, 0)),
                  pl.BlockSpec((MLA_QK, LANES), lambda i: (0, 0)),
                  pl.BlockSpec((MLA_QK, LANES), lambda i: (0, 0)),
                  pl.BlockSpec((MLA_ROPE // 2, TM), lambda i: (0, pos_of_tile(i))),
                  pl.BlockSpec((MLA_ROPE // 2, TM), lambda i: (0, pos_of_tile(i)))],
        out_specs=(pl.BlockSpec((TM, hw), lambda i: (i, 0)),
                   pl.BlockSpec((hw, TM), lambda i: (0, i)),
                   pl.BlockSpec((TM, MLA_HEADS * MLA_V), lambda i: (i, 0))),
        compiler_params=_cparams(("parallel",)),
        name="mla_prep",
    )(tok_a, tok_a, gqa, gkva, wq_t, wk_t, wv, gq, gk, cos_t, sin_t)


def _half_mask(shape):
    return lax.broadcasted_iota(jnp.int32, shape, 1) < (LANES // 2)


def _select_half(q, e, lo_mask):
    zero = jnp.zeros_like(q)
    return jnp.where(lo_mask, q, zero) if e == 0 else jnp.where(lo_mask, zero, q)


def _local_softmax_out(parts, extra_logit):
    m = parts[0][0].max(axis=-1, keepdims=True)
    for s, _ in parts[1:]:
        m = jnp.maximum(m, s.max(axis=-1, keepdims=True))
    if extra_logit is not None:
        m = jnp.maximum(m, extra_logit)
    z = None
    o = None
    for s, v in parts:
        p = jnp.exp2(s - m)
        zs = p.sum(axis=-1, keepdims=True)
        os_ = jnp.dot(p.astype(BF), v, preferred_element_type=F32)
        z = zs if z is None else z + zs
        o = os_ if o is None else o + os_
    if extra_logit is not None:
        z = z + jnp.exp2(extra_logit - m)
    return o / z


def _fixed_softmax_out(parts, ones_ref, extra_logit):
    acc = None
    for s, v in parts:
        aug = jnp.concatenate([v, ones_ref[:s.shape[1], :]], axis=1)
        t = jnp.dot(jnp.exp2(s).astype(BF), aug, preferred_element_type=F32)
        acc = t if acc is None else acc + t
    l = acc[:, LANES:]
    if extra_logit is not None:
        l = l + jnp.exp2(jnp.zeros_like(l) + extra_logit)
    unsafe = jnp.logical_not((l > 2.0 ** -SAFE_SUM_LOG2) & (l < 2.0 ** SAFE_SUM_LOG2))
    return acc[:, :LANES] / l, jnp.max(jnp.where(unsafe, 1.0, 0.0), axis=0, keepdims=True)


def _win_kernel(sink_ref, q_ref, *refs, band, fixed=False):
    if band:
        k0, k1, k2, k3, v0, v1, v2, v3, kc_ref, vc_ref, mask_ref = refs[:11]
        refs = refs[11:]
        kb = jnp.concatenate([k0[...], k1[...], k2[...], k3[...]], axis=1)
        vb = jnp.concatenate([v0[...], v1[...], v2[...], v3[...]], axis=0)
        mask = mask_ref[0]
    else:
        kc_ref, vc_ref = refs[:2]
        refs = refs[2:]
    if fixed:
        ones_ref, o_ref, flag_ref = refs
        ref_logit = sink_ref[0, WIN_HEADS]
    else:
        (o_ref,) = refs
    q = q_ref[...]
    lo = _half_mask((q.shape[0], LANES))
    group = WIN_HEADS // WIN_KV_HEADS
    bad = []
    for j in range(WIN_HEADS // 2):
        qp = q[:, j * LANES:(j + 1) * LANES]
        g = (2 * j) // group
        kc = kc_ref[g * LANES:(g + 1) * LANES, :]
        vc = vc_ref[:, g * LANES:(g + 1) * LANES]
        outs = []
        for e in range(2):
            qm = _select_half(qp, e, lo)
            parts = []
            if band:
                s = jnp.dot(qm, kb[g * LANES:(g + 1) * LANES, :], preferred_element_type=F32) + mask
                parts.append((s, vb[:, g * LANES:(g + 1) * LANES]))
            sc = jnp.dot(qm, kc, preferred_element_type=F32)
            if fixed:
                parts.append((sc - ref_logit, vc))
                o, u = _fixed_softmax_out(parts, ones_ref, sink_ref[0, 2 * j + e])
                bad.append(u)
                outs.append(o)
            else:
                parts.append((sc, vc))
                outs.append(_local_softmax_out(parts, sink_ref[0, 2 * j + e]))
        o_ref[:, j * LANES:(j + 1) * LANES] = jnp.where(lo, outs[0], outs[1]).astype(BF)
    if fixed:
        flag_ref[0] = jnp.concatenate(bad, axis=0)


def _win_attention(sink, q_tok, kt, tok_a, mask_tbl, n_batch, seq, n_ctx, latent, ones=None):
    ctx_blk = (n_batch * seq) // n_ctx
    kc_spec = lambda f: pl.BlockSpec((2 * LANES, n_ctx), f)
    vc_spec = lambda f: pl.BlockSpec((n_ctx, 2 * LANES), f)
    smem = pl.BlockSpec(memory_space=pltpu.SMEM)
    n_out = n_batch * (seq if latent else n_ctx)
    out_shape = jax.ShapeDtypeStruct((n_out, WIN_HEADS * HEAD_DIM), BF)
    if not latent:
        return pl.pallas_call(
            functools.partial(_win_kernel, band=False),
            out_shape=out_shape,
            grid=(n_batch,),
            in_specs=[smem,
                      pl.BlockSpec((n_ctx, 512), lambda b: (ctx_blk + b, 0)),
                      kc_spec(lambda b: (KT_WK // 256, ctx_blk + b)),
                      vc_spec(lambda b: (ctx_blk + b, A_WV // 256))],
            out_specs=pl.BlockSpec((n_ctx, 512), lambda b: (b, 0)),
            compiler_params=_cparams(("parallel",)),
            name="win_attn_ctx",
        )(sink, q_tok, kt, tok_a)
    nq = seq // WIN_TQ
    nkb = seq // LANES

    def kidx(j):
        return lambda b, i: (KT_WK // 256, b * nkb + jnp.clip(2 * i - 1 + j, 0, nkb - 1))

    def vidx(j):
        return lambda b, i: (b * nkb + jnp.clip(2 * i - 1 + j, 0, nkb - 1), A_WV // 256)

    def variant(b, i):
        return (jnp.where(i == 0, 0, jnp.where(i == nq - 1, 2, 1)), 0, 0)

    in_specs = ([smem, pl.BlockSpec((WIN_TQ, 512), lambda b, i: (b * nq + i, 0))]
                + [pl.BlockSpec((2 * LANES, LANES), kidx(j)) for j in range(4)]
                + [pl.BlockSpec((LANES, 2 * LANES), vidx(j)) for j in range(4)]
                + [kc_spec(lambda b, i: (KT_WK // 256, ctx_blk + b)),
                   vc_spec(lambda b, i: (ctx_blk + b, A_WV // 256)),
                   pl.BlockSpec((1, WIN_TQ, 4 * LANES), variant)])
    args = [sink, q_tok, kt, kt, kt, kt, tok_a, tok_a, tok_a, tok_a, kt, tok_a, mask_tbl]
    out_specs = pl.BlockSpec((WIN_TQ, 512), lambda b, i: (b * nq + i, 0))
    fixed = ones is not None
    if fixed:
        in_specs.append(pl.BlockSpec(ones.shape, lambda b, i: (0, 0)))
        args.append(ones)
        out_shape = (out_shape, jax.ShapeDtypeStruct((n_batch * nq, WIN_HEADS, LANES), F32))
        out_specs = (out_specs, pl.BlockSpec((1, WIN_HEADS, LANES), lambda b, i: (b * nq + i, 0, 0)))
    return pl.pallas_call(
        functools.partial(_win_kernel, band=True, fixed=fixed),
        out_shape=out_shape,
        grid=(n_batch, nq),
        in_specs=in_specs,
        out_specs=out_specs,
        compiler_params=_cparams(("parallel", "parallel")),
        name="win_attn_fixed" if fixed else "win_attn",
    )(*args)


def _nat_kernel(q_ref, k0, k1, k2, v0, v1, v2, kc_ref, vc_ref, bias_ref, *refs, fixed=False):
    if fixed:
        par_ref, ones_ref, o_ref, flag_ref = refs
        ref_logit = par_ref[0, 0]
    else:
        (o_ref,) = refs
    q = q_ref[...]
    kb = jnp.concatenate([k0[...], k1[...], k2[...]], axis=1)
    vb = jnp.concatenate([v0[...], v1[...], v2[...]], axis=0)
    lo = _half_mask((q.shape[0], LANES))
    bad = []
    for j in range(NAT_HEADS // 2):
        sl = slice(j * LANES, (j + 1) * LANES)
        qp = q[:, sl]
        outs = []
        for e in range(2):
            qm = _select_half(qp, e, lo)
            s = jnp.dot(qm, kb[sl, :], preferred_element_type=F32) + bias_ref[0, 2 * j + e]
            sc = jnp.dot(qm, kc_ref[sl, :], preferred_element_type=F32)
            if fixed:
                o, u = _fixed_softmax_out([(s, vb[:, sl]), (sc - ref_logit, vc_ref[:, sl])], ones_ref, None)
                bad.append(u)
                outs.append(o)
            else:
                outs.append(_local_softmax_out([(s, vb[:, sl]), (sc, vc_ref[:, sl])], None))
        o_ref[:, sl] = jnp.where(lo, outs[0], outs[1]).astype(BF)
    if fixed:
        flag_ref[0] = jnp.concatenate(bad, axis=0)


def _nat_attention(q_tok, kt, tok_a, bias_tbl, n_batch, seq, n_ctx, par=None, ones=None):
    tq = NAT_ROWS_PER_STEP * GRID_W
    nq = seq // tq
    rows = seq // GRID_W
    ctx_blk = (n_batch * seq) // n_ctx
    q_col = 2
    k_row = KT_NK // 512
    v_col = A_NV // 512

    def wstart(i):
        return jnp.clip(NAT_ROWS_PER_STEP * i - NAT_WIN_ROWS // 2, 0, rows - NAT_KEY_ROWS) // NAT_ROWS_PER_STEP

    def kidx(j):
        return lambda b, i: (k_row, b * nq + wstart(i) + j)

    def vidx(j):
        return lambda b, i: (b * nq + wstart(i) + j, v_col)

    def variant(b, i):
        return (jnp.where(i == 0, 0, jnp.where(i == nq - 1, 2, 1)), 0, 0, 0)

    nk = NAT_KEY_ROWS * GRID_W
    in_specs = ([pl.BlockSpec((tq, 512), lambda b, i: (b * nq + i, q_col))]
                + [pl.BlockSpec((512, tq), kidx(j)) for j in range(3)]
                + [pl.BlockSpec((tq, 512), vidx(j)) for j in range(3)]
                + [pl.BlockSpec((512, n_ctx), lambda b, i: (k_row, ctx_blk + b)),
                   pl.BlockSpec((n_ctx, 512), lambda b, i: (ctx_blk + b, v_col)),
                   pl.BlockSpec((1, NAT_HEADS, tq, nk), variant)])
    args = [q_tok, kt, kt, kt, tok_a, tok_a, tok_a, kt, tok_a, bias_tbl]
    out_shape = jax.ShapeDtypeStruct((n_batch * seq, NAT_HEADS * HEAD_DIM), BF)
    out_specs = pl.BlockSpec((tq, 512), lambda b, i: (b * nq + i, 0))
    fixed = ones is not None
    if fixed:
        in_specs += [pl.BlockSpec(memory_space=pltpu.SMEM), pl.BlockSpec(ones.shape, lambda b, i: (0, 0))]
        args += [par, ones]
        out_shape = (out_shape, jax.ShapeDtypeStruct((n_batch * nq, NAT_HEADS, LANES), F32))
        out_specs = (out_specs, pl.BlockSpec((1, NAT_HEADS, LANES), lambda b, i: (b * nq + i, 0, 0)))
    return pl.pallas_call(
        functools.partial(_nat_kernel, fixed=fixed),
        out_shape=out_shape,
        grid=(n_batch, nq),
        in_specs=in_specs,
        out_specs=out_specs,
        compiler_params=_cparams(("parallel", "parallel")),
        name="nat_attn_fixed" if fixed else "nat_attn",
    )(*args)


def _dense_kernel(lam_ref, q_ref, kc_ref, vc_ref, *refs, n_heads, packed, diff, latent, lam_scale):
    if latent:
        k_ref, v_ref = refs[0], refs[1]
        refs = refs[2:]
    if diff:
        subln_ref, o_ref, qm_sc, m_sc, l_sc, acc_sc, s_sc, p_sc = refs
    else:
        o_ref, qm_sc, m_sc, l_sc, acc_sc, s_sc, p_sc = refs
    kt_step = pl.program_id(2) if latent else 0
    tq = q_ref.shape[0]

    def kv_slices(h):
        blk = h // 2 if packed else h
        ks = slice(blk * LANES, (blk + 1) * LANES)
        vs = ks if packed else slice((h // 2) * LANES, (h // 2 + 1) * LANES)
        return ks, vs

    def step(h, k_ref_, v_ref_):
        ks, vs = kv_slices(h)
        nk = k_ref_.shape[1]
        slot = h % 2
        s_sc[slot, :, :nk] = jnp.dot(qm_sc[h], k_ref_[ks, :], preferred_element_type=F32)
        for r in range(tq // DENSE_RB):
            rows = slice(r * DENSE_RB, (r + 1) * DENSE_RB)
            mx = s_sc[slot, rows, 0:LANES]
            for c in range(1, nk // LANES):
                mx = jnp.maximum(mx, s_sc[slot, rows, c * LANES:(c + 1) * LANES])
            m_old = m_sc[h, rows, :]
            m_new = jnp.maximum(m_old, jnp.max(mx, axis=-1, keepdims=True))
            alpha = jnp.exp2(m_old - m_new)
            lsum = None
            for c in range(nk // LANES):
                cols = slice(c * LANES, (c + 1) * LANES)
                p = jnp.exp2(s_sc[slot, rows, cols] - m_new)
                lsum = p if lsum is None else lsum + p
                p_sc[slot, rows, cols] = p.astype(BF)
            m_sc[h, rows, :] = m_new
            l_sc[h, rows, :] = alpha * l_sc[h, rows, :] + jnp.sum(lsum, axis=-1, keepdims=True)
            acc_sc[h, rows, :] = alpha * acc_sc[h, rows, :]
        acc_sc[h] += jnp.dot(p_sc[slot, :, :nk], v_ref_[:, vs], preferred_element_type=F32)

    @pl.when(kt_step == 0)
    def _():
        q = q_ref[...]
        lo = _half_mask((tq, LANES))
        for h in range(n_heads):
            if packed:
                qp = q[:, (h // 2) * LANES:(h // 2 + 1) * LANES]
                qm_sc[h] = _select_half(qp, h % 2, lo)
            else:
                qm_sc[h] = q[:, h * LANES:(h + 1) * LANES]
        m_sc[...] = jnp.full(m_sc.shape, NEG, F32)
        l_sc[...] = jnp.zeros(l_sc.shape, F32)
        acc_sc[...] = jnp.zeros(acc_sc.shape, F32)
        for h in range(n_heads):
            step(h, kc_ref, vc_ref)

    if latent:
        for h in range(n_heads):
            step(h, k_ref, v_ref)
        last = kt_step == pl.num_programs(2) - 1
    else:
        last = True

    def finish():
        lo = _half_mask((tq, LANES))
        if diff:
            lam = lam_ref[0, 0]
            for hv in range(n_heads // 2):
                y = (acc_sc[2 * hv] / l_sc[2 * hv]
                     - lam * (acc_sc[2 * hv + 1] / l_sc[2 * hv + 1]))
                ms = jnp.mean(y * y, axis=-1, keepdims=True)
                y = y * lax.rsqrt(ms + EPS) * subln_ref[...] * lam_scale
                o_ref[:, hv * LANES:(hv + 1) * LANES] = y.astype(BF)
        else:
            for hp in range(n_heads // 2):
                o0 = acc_sc[2 * hp] / l_sc[2 * hp]
                o1 = acc_sc[2 * hp + 1] / l_sc[2 * hp + 1]
                o_ref[:, hp * LANES:(hp + 1) * LANES] = jnp.where(lo, o0, o1).astype(BF)

    if latent:
        pl.when(last)(finish)
    else:
        finish()


def _dense_attention(lam, q_arr, q_col, q_w, k_arr, k_row, k_w, v_arr, v_col, v_w, subln,
                     n_batch, seq, n_ctx, *, n_heads, packed, diff, latent, lam_scale, name):
    n = n_batch * (seq if latent else n_ctx)
    ctx_blk = (n_batch * seq) // n_ctx
    out_w = v_w
    smem = pl.BlockSpec(memory_space=pltpu.SMEM)
    kern = functools.partial(_dense_kernel, n_heads=n_heads, packed=packed, diff=diff,
                             latent=latent, lam_scale=lam_scale)
    tq = DENSE_TQ_ONLINE if latent else n_ctx
    max_nk = DENSE_TK if latent else n_ctx
    scratch = [pltpu.VMEM((n_heads, tq, LANES), BF),
               pltpu.VMEM((n_heads, tq, LANES), F32),
               pltpu.VMEM((n_heads, tq, LANES), F32),
               pltpu.VMEM((n_heads, tq, LANES), F32),
               pltpu.VMEM((2, tq, max_nk), F32),
               pltpu.VMEM((2, tq, max_nk), BF)]
    out_shape = jax.ShapeDtypeStruct((n, out_w), BF)
    if latent:
        nq = seq // tq
        nk = seq // DENSE_TK
        grid = (n_batch, nq, nk)
        in_specs = [smem,
                    pl.BlockSpec((tq, q_w), lambda b, i, k: (b * nq + i, q_col)),
                    pl.BlockSpec((k_w, n_ctx), lambda b, i, k: (k_row, ctx_blk + b)),
                    pl.BlockSpec((n_ctx, v_w), lambda b, i, k: (ctx_blk + b, v_col)),
                    pl.BlockSpec((k_w, DENSE_TK), lambda b, i, k: (k_row, b * nk + k)),
                    pl.BlockSpec((DENSE_TK, v_w), lambda b, i, k: (b * nk + k, v_col))]
        args = [lam, q_arr, k_arr, v_arr, k_arr, v_arr]
        if diff:
            in_specs.append(pl.BlockSpec((1, LANES), lambda b, i, k: (0, 0)))
            args.append(subln)
        out_specs = pl.BlockSpec((tq, out_w), lambda b, i, k: (b * nq + i, 0))
        sem = ("parallel", "parallel", "arbitrary")
    else:
        grid = (n_batch,)
        in_specs = [smem,
                    pl.BlockSpec((tq, q_w), lambda b: (ctx_blk + b, q_col)),
                    pl.BlockSpec((k_w, n_ctx), lambda b: (k_row, ctx_blk + b)),
                    pl.BlockSpec((n_ctx, v_w), lambda b: (ctx_blk + b, v_col))]
        args = [lam, q_arr, k_arr, v_arr]
        if diff:
            in_specs.append(pl.BlockSpec((1, LANES), lambda b: (0, 0)))
            args.append(subln)
        out_specs = pl.BlockSpec((tq, out_w), lambda b: (b, 0))
        sem = ("parallel",)
    return pl.pallas_call(
        kern, out_shape=out_shape, grid=grid, in_specs=in_specs, out_specs=out_specs,
        scratch_shapes=scratch, compiler_params=_cparams(sem), name=name,
    )(*args)


SAFE_SUM_LOG2 = 100.0


def _dense_fixed_kernel(par_ref, q_ref, kc_ref, vc_ref, *refs, n_heads, packed, diff, latent, lam_scale):
    if latent:
        k_ref, v_ref = refs[0], refs[1]
        refs = refs[2:]
    ones_ref = refs[0]
    refs = refs[1:]
    if diff:
        subln_ref, o_ref, flag_ref, qm_sc, acc_sc, p_sc = refs
    else:
        o_ref, flag_ref, qm_sc, acc_sc, p_sc = refs
    kt_step = pl.program_id(2) if latent else 0
    tq = q_ref.shape[0]
    ref_logit = par_ref[0, 1]

    def step(h, k_ref_, v_ref_, first):
        blk = h // 2 if packed else h
        ks = slice(blk * LANES, (blk + 1) * LANES)
        vs = ks if packed else slice((h // 2) * LANES, (h // 2 + 1) * LANES)
        nk = k_ref_.shape[1]
        slot = h % 2
        for c in range(nk // DENSE_KC):
            cols = slice(c * DENSE_KC, (c + 1) * DENSE_KC)
            s = jnp.dot(qm_sc[h], k_ref_[ks, cols], preferred_element_type=F32)
            p_sc[slot, :, cols] = jnp.exp2(s - ref_logit).astype(BF)
        v_aug = jnp.concatenate([v_ref_[:, vs], ones_ref[:nk, :]], axis=1)
        pv = jnp.dot(p_sc[slot, :, :nk], v_aug, preferred_element_type=F32)
        if first:
            acc_sc[h] = pv
        else:
            acc_sc[h] += pv

    @pl.when(kt_step == 0)
    def _():
        q = q_ref[...]
        lo = _half_mask((tq, LANES))
        for h in range(n_heads):
            if packed:
                qp = q[:, (h // 2) * LANES:(h // 2 + 1) * LANES]
                qm_sc[h] = _select_half(qp, h % 2, lo)
            else:
                qm_sc[h] = q[:, h * LANES:(h + 1) * LANES]
        for h in range(n_heads):
            step(h, kc_ref, vc_ref, True)

    if latent:
        for h in range(n_heads):
            step(h, k_ref, v_ref, False)
        last = kt_step == pl.num_programs(2) - 1

    def finish():
        lo = _half_mask((tq, LANES))
        outs, bad = [], []
        for h in range(n_heads):
            a = acc_sc[h]
            l = a[:, LANES:]
            unsafe = jnp.logical_not((l > 2.0 ** -SAFE_SUM_LOG2) & (l < 2.0 ** SAFE_SUM_LOG2))
            bad.append(jnp.max(jnp.where(unsafe, 1.0, 0.0), axis=0, keepdims=True))
            outs.append(a[:, :LANES] / l)
        flag_ref[0] = jnp.concatenate(bad, axis=0)
        if diff:
            lam = par_ref[0, 0]
            for hv in range(n_heads // 2):
                y = outs[2 * hv] - lam * outs[2 * hv + 1]
                ms = jnp.mean(y * y, axis=-1, keepdims=True)
                y = y * lax.rsqrt(ms + EPS) * subln_ref[...] * lam_scale
                o_ref[:, hv * LANES:(hv + 1) * LANES] = y.astype(BF)
        else:
            for hp in range(n_heads // 2):
                o_ref[:, hp * LANES:(hp + 1) * LANES] = jnp.where(lo, outs[2 * hp], outs[2 * hp + 1]).astype(BF)

    if latent:
        pl.when(last)(finish)
    else:
        finish()


def _dense_fixed_attention(par, q_arr, q_col, q_w, k_arr, k_row, k_w, v_arr, v_col, v_w, subln, ones,
                           n_batch, seq, n_ctx, *, n_heads, packed, diff, latent, lam_scale, name):
    n = n_batch * (seq if latent else n_ctx)
    ctx_blk = (n_batch * seq) // n_ctx
    smem = pl.BlockSpec(memory_space=pltpu.SMEM)
    kern = functools.partial(_dense_fixed_kernel, n_heads=n_heads, packed=packed, diff=diff,
                             latent=latent, lam_scale=lam_scale)
    tq = DENSE_TQ if latent else n_ctx
    max_nk = DENSE_TK if latent else n_ctx
    scratch = [pltpu.VMEM((n_heads, tq, LANES), BF),
               pltpu.VMEM((n_heads, tq, 2 * LANES), F32),
               pltpu.VMEM((2, tq, max_nk), BF)]
    if latent:
        nq = seq // tq
        nk = seq // DENSE_TK
        grid = (n_batch, nq, nk)
        ix = lambda f: (lambda b, i, k: f(b, i, k))
        q_ix = ix(lambda b, i, k: (b * nq + i, q_col))
        in_specs = [smem,
                    pl.BlockSpec((tq, q_w), q_ix),
                    pl.BlockSpec((k_w, n_ctx), ix(lambda b, i, k: (k_row, ctx_blk + b))),
                    pl.BlockSpec((n_ctx, v_w), ix(lambda b, i, k: (ctx_blk + b, v_col))),
                    pl.BlockSpec((k_w, DENSE_TK), ix(lambda b, i, k: (k_row, b * nk + k))),
                    pl.BlockSpec((DENSE_TK, v_w), ix(lambda b, i, k: (b * nk + k, v_col))),
                    pl.BlockSpec((DENSE_TK, LANES), ix(lambda b, i, k: (0, 0)))]
        args = [par, q_arr, k_arr, v_arr, k_arr, v_arr, ones]
        const_ix = ix(lambda b, i, k: (0, 0))
        out_specs = (pl.BlockSpec((tq, v_w), ix(lambda b, i, k: (b * nq + i, 0))),
                     pl.BlockSpec((1, n_heads, LANES), ix(lambda b, i, k: (b * nq + i, 0, 0))))
        n_flag = n_batch * nq
        sem = ("parallel", "parallel", "arbitrary")
    else:
        grid = (n_batch,)
        in_specs = [smem,
                    pl.BlockSpec((tq, q_w), lambda b: (ctx_blk + b, q_col)),
                    pl.BlockSpec((k_w, n_ctx), lambda b: (k_row, ctx_blk + b)),
                    pl.BlockSpec((n_ctx, v_w), lambda b: (ctx_blk + b, v_col)),
                    pl.BlockSpec((DENSE_TK, LANES), lambda b: (0, 0))]
        args = [par, q_arr, k_arr, v_arr, ones]
        const_ix = lambda b: (0, 0)
        out_specs = (pl.BlockSpec((tq, v_w), lambda b: (b, 0)),
                     pl.BlockSpec((1, n_heads, LANES), lambda b: (b, 0, 0)))
        n_flag = n_batch
        sem = ("parallel",)
    if diff:
        in_specs.append(pl.BlockSpec((1, LANES), const_ix))
        args.append(subln)
    return pl.pallas_call(
        kern,
        out_shape=(jax.ShapeDtypeStruct((n, v_w), BF), jax.ShapeDtypeStruct((n_flag, n_heads, LANES), F32)),
        grid=grid, in_specs=in_specs, out_specs=out_specs,
        scratch_shapes=scratch, compiler_params=_cparams(sem), name=name + "_fixed",
    )(*args)


def _merge_kernel(x_ref, mod_ref, g2_ref, *refs, n_lat_tiles, has_ctx):
    ys = refs[:N_BRANCH]
    refs = refs[N_BRANCH:]
    if has_ctx:
        ycs = refs[:N_BRANCH]
        refs = refs[N_BRANCH:]
        is_ctx = pl.program_id(0) >= n_lat_tiles
    gts = refs[:N_BRANCH]
    wb_ref, wo_ref, rw_ref, xo_ref, h2_ref, sc_ref = refs[N_BRANCH:]
    m = mod_ref[0]
    mix = None
    for n_ in range(N_BRANCH):
        y = ys[n_][...]
        if has_ctx:
            y = jnp.where(is_ctx, ycs[n_][...], y)
        yb = jnp.dot(y, wb_ref[n_], preferred_element_type=F32)
        t = jax.nn.sigmoid(gts[n_][...].astype(F32)) * yb
        mix = t if mix is None else mix + t
    att = jnp.dot(mix.astype(BF), wo_ref[...], preferred_element_type=F32)
    xn = x_ref[...] + m[2:3] * att
    xo_ref[...] = xn
    h2 = _norm_mod(xn, g2_ref[...], m[4:5], m[3:4])
    h2_ref[...] = h2
    logits = jnp.dot(h2, rw_ref[...], preferred_element_type=F32, precision=lax.Precision.HIGHEST)
    sc_ref[...] = jax.nn.sigmoid(logits)


def _merge(xa, mod3, g2, ys, ys_ctx, tok_a, wb, wo, rw, n_rows, group_of_tile):
    d = xa.shape[1]
    n_lat_tiles = ys[0].shape[0] // TM
    has_ctx = ys_ctx is not None
    row = lambda w, c: pl.BlockSpec((TM, w), lambda i, c=c: (i, c))
    lat_row = pl.BlockSpec((TM, BRANCH_W), lambda i: (jnp.minimum(i, n_lat_tiles - 1), 0))
    in_specs = ([row(d, 0),
                 pl.BlockSpec((1, 8, d), lambda i: (group_of_tile(i), 0, 0)),
                 pl.BlockSpec((1, d), lambda i: (0, 0))]
                + [lat_row for _ in range(N_BRANCH)]
                + ([pl.BlockSpec((TM, BRANCH_W), lambda i: (0, 0)) for _ in range(N_BRANCH)] if has_ctx else [])
                + [row(d, c) for c in range(N_BRANCH)]
                + [pl.BlockSpec(wb.shape, lambda i: (0, 0, 0)),
                   pl.BlockSpec(wo.shape, lambda i: (0, 0)),
                   pl.BlockSpec(rw.shape, lambda i: (0, 0))])
    ys = list(ys) + (list(ys_ctx) if has_ctx else [])
    return pl.pallas_call(
        functools.partial(_merge_kernel, n_lat_tiles=n_lat_tiles, has_ctx=has_ctx),
        out_shape=(jax.ShapeDtypeStruct((n_rows, d), F32),
                   jax.ShapeDtypeStruct((n_rows, d), F32),
                   jax.ShapeDtypeStruct((n_rows, LANES), F32)),
        grid=(n_rows // TM,),
        in_specs=in_specs,
        out_specs=(row(d, 0), row(d, 0), row(LANES, 0)),
        compiler_params=_cparams(("parallel",)),
        name="merge",
    )(xa, mod3, g2, *ys, tok_a, tok_a, tok_a, tok_a, wb, wo, rw)


def _expert_kernel(blk_e_ref, src_ref, dst_ref, h2_hbm, w_ref, w1_ref, w3_ref, w2_ref, out_hbm,
                   xbuf, ybuf, sem_in, sem_out):
    i = pl.program_id(0)
    n_steps = pl.num_programs(0)
    slot = i % 2

    def start_gather(blk, s):
        def body(r, c):
            t = src_ref[blk * MOE_BLOCK + r]
            pltpu.make_async_copy(h2_hbm.at[pl.ds(t, 1)], xbuf.at[s, pl.ds(r, 1)], sem_in.at[s]).start()
            return c
        lax.fori_loop(0, MOE_BLOCK, body, 0, unroll=8)

    def wait_gather(s):
        pltpu.make_async_copy(h2_hbm.at[pl.ds(0, MOE_BLOCK)], xbuf.at[s], sem_in.at[s]).wait()

    def start_scatter(blk, s):
        def body(r, c):
            t = dst_ref[blk * MOE_BLOCK + r]
            pltpu.make_async_copy(ybuf.at[s, pl.ds(r, 1)], out_hbm.at[pl.ds(t, 1)], sem_out.at[s]).start()
            return c
        lax.fori_loop(0, MOE_BLOCK, body, 0, unroll=8)

    def wait_scatter(s):
        pltpu.make_async_copy(ybuf.at[s], out_hbm.at[pl.ds(0, MOE_BLOCK)], sem_out.at[s]).wait()

    @pl.when(i == 0)
    def _():
        start_gather(0, 0)

    @pl.when(i + 1 < n_steps)
    def _():
        start_gather(i + 1, 1 - slot)

    wait_gather(slot)

    @pl.when(i >= 2)
    def _():
        wait_scatter(slot)

    xb = xbuf[slot].astype(BF)
    a = jnp.dot(xb, w1_ref[0], preferred_element_type=F32)
    b = jnp.dot(xb, w3_ref[0], preferred_element_type=F32)
    hmid = (a * jax.nn.sigmoid(a) * b).astype(BF)
    ybuf[slot] = jnp.dot(hmid, w2_ref[0], preferred_element_type=F32) * w_ref[...]
    start_scatter(i, slot)

    @pl.when(i == n_steps - 1)
    def _():
        wait_scatter(slot)

        @pl.when(n_steps >= 2)
        def _():
            wait_scatter(1 - slot)


def _experts(blk_e, src, dst, h2, wcol, w1, w3, w2):
    cap = src.shape[0]
    d = h2.shape[1]
    n_blk = cap // MOE_BLOCK
    wspec = lambda shp: pl.BlockSpec((1,) + shp, lambda i, be, s, t: (be[i], 0, 0))
    grid_spec = pltpu.PrefetchScalarGridSpec(
        num_scalar_prefetch=3,
        grid=(n_blk,),
        in_specs=[pl.BlockSpec(memory_space=pl.ANY),
                  pl.BlockSpec((MOE_BLOCK, 1), lambda i, be, s, t: (i, 0)),
                  wspec((d, D_EXPERT)), wspec((d, D_EXPERT)), wspec((D_EXPERT, d))],
        out_specs=pl.BlockSpec(memory_space=pl.ANY),
        scratch_shapes=[pltpu.VMEM((2, MOE_BLOCK, d), F32),
                        pltpu.VMEM((2, MOE_BLOCK, d), F32),
                        pltpu.SemaphoreType.DMA((2,)),
                        pltpu.SemaphoreType.DMA((2,))],
    )
    return pl.pallas_call(
        _expert_kernel,
        out_shape=jax.ShapeDtypeStruct((cap, d), F32),
        grid_spec=grid_spec,
        compiler_params=_cparams(("arbitrary",)),
        name="moe_experts",
    )(blk_e, src, dst, h2, wcol, w1, w3, w2)


def _route(scores, router_b, n):
    per_group = N_EXPERTS // N_GROUPS
    biased = scores + router_b.astype(F32)

    def top2(v):
        lane = jnp.arange(v.shape[-1], dtype=jnp.int32)
        i0 = jnp.argmax(v, axis=-1).astype(jnp.int32)
        v0 = jnp.max(v, axis=-1)
        rest = jnp.where(lane == i0[..., None], -jnp.inf, v)
        i1 = jnp.argmax(rest, axis=-1).astype(jnp.int32)
        v1 = jnp.max(rest, axis=-1)
        return (v0, v1), (i0, i1)

    (g0, g1), _ = top2(biased.reshape(n, N_GROUPS, per_group))
    group = jnp.argmax(g0 + g1, axis=-1)
    in_group = (jnp.arange(N_EXPERTS) // per_group)[None, :] == group[:, None]
    _, (e0, e1) = top2(jnp.where(in_group, biased, -jnp.inf))
    idx = jnp.stack([e0, e1], axis=-1)
    wts = jnp.take_along_axis(scores, idx, axis=-1)
    wts = wts / wts.sum(-1, keepdims=True)
    flat_e = idx.reshape(-1).astype(jnp.int32)
    n_asg = n * TOP_K
    onehot = (flat_e[:, None] == jnp.arange(N_EXPERTS, dtype=jnp.int32)[None, :]).astype(F32)
    oh = onehot.reshape(n_asg // MOE_BLOCK, MOE_BLOCK, N_EXPERTS)
    tri = jnp.tril(jnp.ones((MOE_BLOCK, MOE_BLOCK), F32), -1)
    within = jnp.einsum('ij,bjk->bik', tri, oh)
    blk_tot = oh.sum(axis=1)
    blk_off = jnp.cumsum(blk_tot, axis=0) - blk_tot
    rank = ((within + blk_off[:, None, :]) * oh).sum(-1).reshape(n_asg).astype(jnp.int32)
    counts = blk_tot.sum(axis=0).astype(jnp.int32)
    padded = (counts + MOE_BLOCK - 1) // MOE_BLOCK * MOE_BLOCK
    pad_end = jnp.cumsum(padded)
    pad_start = pad_end - padded
    dest = (onehot * pad_start.astype(F32)[None, :]).sum(-1).astype(jnp.int32) + rank
    n_blk = (n_asg + N_EXPERTS * (MOE_BLOCK - 1) + MOE_BLOCK - 1) // MOE_BLOCK
    cap = n_blk * MOE_BLOCK
    slot_a = jnp.full((cap,), -1, jnp.int32).at[dest].set(jnp.arange(n_asg, dtype=jnp.int32))
    real = slot_a >= 0
    a = jnp.maximum(slot_a, 0)
    src = jnp.where(real, a // TOP_K, 0).astype(jnp.int32)
    pad_rank = jnp.cumsum(jnp.logical_not(real).astype(jnp.int32)) - 1
    dst = jnp.where(real, (a % TOP_K) * n + a // TOP_K, n * TOP_K + pad_rank).astype(jnp.int32)
    wcol = jnp.where(real, wts.reshape(-1)[a], 0.0).astype(F32).reshape(cap, 1)
    blk_e = jnp.minimum(jnp.searchsorted(pad_end, jnp.arange(n_blk) * MOE_BLOCK, side='right'),
                        N_EXPERTS - 1).astype(jnp.int32)
    return blk_e, src, dst, wcol


def _combine_kernel(x_ref, mod_ref, f0_ref, f1_ref, o_ref):
    o_ref[...] = x_ref[...] + mod_ref[0][5:6] * (f0_ref[...] + f1_ref[...])


def _combine(xn, mod3, f, n_rows, group_of_tile):
    d = xn.shape[1]
    nt = n_rows // TM
    return pl.pallas_call(
        _combine_kernel,
        out_shape=jax.ShapeDtypeStruct((n_rows, d), F32),
        grid=(nt,),
        in_specs=[pl.BlockSpec((TM, d), lambda i: (i, 0)),
                  pl.BlockSpec((1, 8, d), lambda i: (group_of_tile(i), 0, 0)),
                  pl.BlockSpec((TM, d), lambda i: (i, 0)),
                  pl.BlockSpec((TM, d), lambda i: (nt + i, 0))],
        out_specs=pl.BlockSpec((TM, d), lambda i: (i, 0)),
        compiler_params=_cparams(("parallel",)),
        name="moe_combine",
    )(xn, mod3, f, f)


def _rope_tables(seq, dim, pad):
    t = jnp.arange(seq)
    rows = (t // GRID_W).astype(F32)
    cols = (t % GRID_W).astype(F32)
    quarter = dim // 4
    inv_freq = jnp.exp(-math.log(ROPE_BASE) * jnp.arange(quarter, dtype=F32) / quarter)
    ang = jnp.concatenate([inv_freq[:, None] * rows[None, :], inv_freq[:, None] * cols[None, :]], axis=0)
    cos = jnp.concatenate([jnp.cos(ang), jnp.ones((dim // 2, pad), F32)], axis=1)
    sin = jnp.concatenate([jnp.sin(ang), jnp.zeros((dim // 2, pad), F32)], axis=1)
    return cos, sin


def _win_mask_table(seq):
    nkb = seq // LANES
    nq = seq // WIN_TQ
    tabs = []
    for i in (0, 1, nq - 1):
        t = i * WIN_TQ + np.arange(WIN_TQ)[:, None]
        blk = np.clip(2 * i - 1 + np.arange(4), 0, nkb - 1)
        want = 2 * i - 1 + np.arange(4)
        s = (blk[:, None] * LANES + np.arange(LANES)[None, :]).reshape(-1)[None, :]
        ok = (np.abs(t - s) <= WIN_RADIUS) & np.repeat(blk == want, LANES)[None, :]
        tabs.append(np.where(ok, 0.0, NEG))
    return jnp.asarray(np.stack(tabs), F32)


def _nat_bias_table(rpb, seq):
    rows = seq // GRID_W
    nq = rows // NAT_ROWS_PER_STEP
    wc = NAT_WIN_COLS
    col = np.arange(GRID_W)
    col_start = np.clip(col - wc // 2, 0, GRID_W - wc)
    col_ok = (col[None, :] >= col_start[:, None]) & (col[None, :] < col_start[:, None] + wc)
    d_col = np.clip(col[None, :] - col[:, None] + (wc - 1), 0, 2 * wc - 2)
    tabs = []
    for i in (0, 1, nq - 1):
        r0 = NAT_ROWS_PER_STEP * i
        ws = np.clip(r0 - NAT_WIN_ROWS // 2, 0, rows - NAT_KEY_ROWS)
        r = r0 + np.arange(NAT_ROWS_PER_STEP)
        rs = np.clip(r - NAT_WIN_ROWS // 2, 0, rows - NAT_WIN_ROWS)
        krow = ws + np.arange(NAT_KEY_ROWS)
        row_ok = (krow[None, :] >= rs[:, None]) & (krow[None, :] < rs[:, None] + NAT_WIN_ROWS)
        d_row = np.clip(krow[None, :] - r[:, None] + (NAT_WIN_ROWS - 1), 0, 2 * NAT_WIN_ROWS - 2)
        ok = row_ok[:, None, :, None] & col_ok[None, :, None, :]
        sel_r = jnp.asarray(d_row.reshape(-1)[:, None] == np.arange(2 * NAT_WIN_ROWS - 1)[None, :], F32)
        sel_c = jnp.asarray(d_col.reshape(-1)[:, None] == np.arange(2 * wc - 1)[None, :], F32)
        bias = jnp.einsum('pr,hrc,qc->hpq', sel_r, rpb.astype(F32), sel_c, precision=lax.Precision.HIGHEST)
        bias = bias.reshape(rpb.shape[0], NAT_ROWS_PER_STEP, NAT_KEY_ROWS, GRID_W, GRID_W)
        bias = bias.transpose(0, 1, 3, 2, 4)
        bias = jnp.where(jnp.asarray(ok)[None], bias, NEG)
        tabs.append(bias.reshape(rpb.shape[0], NAT_ROWS_PER_STEP * GRID_W, NAT_KEY_ROWS * GRID_W))
    return jnp.stack(tabs)


def _bcast_rows(v, reps=1):
    return jnp.tile(jnp.broadcast_to(v.astype(F32)[:, None], (v.shape[0], LANES)), (reps, 1))


def _layer_params(l, p):
    w = p['w_in'][l]
    sizes = (512, 128, 128, 512, 512, 512, 512, 512, 512, 256, 160, 4096)
    offs = np.concatenate([[0], np.cumsum(sizes)])
    seg = lambda k: w[:, offs[k]:offs[k + 1]]
    wq, wk, wv, dq, dk, dv, nq, nk, nv, mqa, mkva, gates = [seg(k) for k in range(12)]
    d = w.shape[0]
    dup = lambda m: jnp.concatenate([m[:, :64], m[:, :64], m[:, 64:], m[:, 64:]], axis=1)
    mkva_p = jnp.concatenate([mkva, jnp.zeros((d, 256 - mkva.shape[1]), F32)], axis=1)
    w_a = jnp.concatenate([gates, dv, nv, mqa, mkva_p, dup(wv)], axis=1).astype(BF)
    w_bt = jnp.concatenate([wq, dq, nq, dk, nk, dup(wk)], axis=1).T.astype(BF)
    scale = HEAD_DIM ** -0.5 * LOG2E
    gain_b = jnp.concatenate([
        _bcast_rows(p['win_q_norm'][l] * scale, 8), _bcast_rows(p['dif_q_norm'][l] * scale, 8),
        _bcast_rows(p['nat_q_norm'][l] * scale, 8), _bcast_rows(p['dif_k_norm'][l], 8),
        _bcast_rows(p['nat_k_norm'][l], 8), _bcast_rows(p['win_k_norm'][l], 4)], axis=0)
    wkv = p['mla_wkv_b'][l].reshape(MLA_KV_LORA, MLA_HEADS, MLA_NOPE + MLA_V)
    wk_t = wkv[:, :, :MLA_NOPE].reshape(MLA_KV_LORA, -1).T.astype(BF)
    wv_m = wkv[:, :, MLA_NOPE:].reshape(MLA_KV_LORA, -1).astype(BF)
    def logit_bound(gq, gk, dim):
        return dim * jnp.max(jnp.abs(gq)) * jnp.max(jnp.abs(gk)) * (1.0 + 2.0 ** -7)

    lam_f = p['dif_lambda'][l].astype(F32)
    lam_init = 0.8 - 0.6 * math.exp(-0.3 * l)
    lam = jnp.exp(jnp.sum(lam_f[0] * lam_f[1])) - jnp.exp(jnp.sum(lam_f[2] * lam_f[3])) + lam_init
    return dict(
        w_a=w_a, w_bt=w_bt, gain_b=gain_b,
        g1=p['norm1_g'][l].reshape(1, d), g2=p['norm2_g'][l].reshape(1, d),
        sink=(p['win_sink'][l].astype(F32) * LOG2E).reshape(1, WIN_HEADS),
        gqa=p['mla_q_a_norm'][l].reshape(1, -1), gkva=p['mla_kv_a_norm'][l].reshape(1, -1),
        wq_t=p['mla_wq_b'][l].T.astype(BF), wk_t=wk_t, wv_m=wv_m,
        gq=_bcast_rows(p['mla_q_norm'][l] * (MLA_QK ** -0.5 * LOG2E)), gk=_bcast_rows(p['mla_k_norm'][l]),
        lam=lam.reshape(1, 1).astype(F32), lam_scale=1.0 - lam_init,
        m_win=logit_bound(p['win_q_norm'][l] * scale, p['win_k_norm'][l], HEAD_DIM),
        m_dif=logit_bound(p['dif_q_norm'][l] * scale, p['dif_k_norm'][l], HEAD_DIM),
        m_nat=logit_bound(p['nat_q_norm'][l] * scale, p['nat_k_norm'][l], HEAD_DIM),
        m_mla=logit_bound(p['mla_q_norm'][l] * (MLA_QK ** -0.5 * LOG2E), p['mla_k_norm'][l], MLA_QK),
        subln=p['dif_subln'][l].astype(F32).reshape(1, DIF_V_DIM),
        wb=p['w_branch'][l].astype(BF), wo=p['w_out'][l].astype(BF),
        w1=p['moe_w1'][l].astype(BF), w3=p['moe_w3'][l].astype(BF), w2=p['moe_w2'][l].astype(BF),
    )


def kernel(x, c, ctx, c_ctx, ada_w, ada_b, norm1_g, norm2_g, w_in, win_q_norm, win_k_norm, win_sink,
           dif_q_norm, dif_k_norm, dif_lambda, dif_subln, nat_q_norm, nat_k_norm, nat_rpb,
           mla_q_a_norm, mla_wq_b, mla_kv_a_norm, mla_wkv_b, mla_q_norm, mla_k_norm,
           w_branch, w_out, router_w, router_b, moe_w1, moe_w3, moe_w2):
    p = dict(norm1_g=norm1_g, norm2_g=norm2_g, w_in=w_in, win_q_norm=win_q_norm, win_k_norm=win_k_norm,
             win_sink=win_sink, dif_q_norm=dif_q_norm, dif_k_norm=dif_k_norm, dif_lambda=dif_lambda,
             dif_subln=dif_subln, nat_q_norm=nat_q_norm, nat_k_norm=nat_k_norm,
             mla_q_a_norm=mla_q_a_norm, mla_wq_b=mla_wq_b, mla_kv_a_norm=mla_kv_a_norm,
             mla_wkv_b=mla_wkv_b, mla_q_norm=mla_q_norm, mla_k_norm=mla_k_norm,
             w_branch=w_branch, w_out=w_out, moe_w1=moe_w1, moe_w3=moe_w3, moe_w2=moe_w2)
    n_batch, seq, d = x.shape
    n_ctx = ctx.shape[1]
    depth = ada_w.shape[0]
    n_lat = n_batch * seq
    n_all = n_lat + n_batch * n_ctx
    assert seq % DENSE_TK == 0 and seq % DENSE_TQ == 0 and (n_batch * n_ctx) == TM and seq % TM == 0
    tiles_per_batch = seq // TM
    group_of_tile = lambda i: jnp.minimum(i // tiles_per_batch, n_batch)
    pos_of_tile = lambda i: jnp.where(i < n_batch * tiles_per_batch, i % tiles_per_batch, tiles_per_batch)

    cc = jnp.concatenate([c, c_ctx[None, :], jnp.zeros((8 - n_batch - 1, d), F32)], axis=0)
    mod = _modulation(cc, ada_w, ada_b)
    mod = mod[:, :n_batch + 1].reshape(depth, n_batch + 1, 6, d)
    mod = jnp.pad(mod, ((0, 0), (0, 0), (0, 2), (0, 0)))

    cos_h, sin_h = _rope_tables(seq, HEAD_DIM, TM)
    cos_m, sin_m = _rope_tables(seq, MLA_ROPE, TM)
    win_mask = _win_mask_table(seq)
    rw = jnp.pad(router_w.astype(F32), ((0, 0), (0, LANES - N_EXPERTS)))

    xa = jnp.concatenate([x.reshape(n_lat, d), ctx.reshape(n_batch * n_ctx, d)], axis=0)
    ones = jnp.ones((DENSE_TK, LANES), BF)

    def dense(lam, logit_bound, q_arr, q_col, q_w, k_arr, k_row, k_w, v_arr, v_col, v_w, subln, **kw):
        operands = (q_arr, q_col, q_w, k_arr, k_row, k_w, v_arr, v_col, v_w, subln)
        par = jnp.concatenate([lam, logit_bound.reshape(1, 1).astype(F32)], axis=1)
        y, flag = _dense_fixed_attention(par, *operands, ones, n_batch, seq, n_ctx, **kw)
        return lax.cond(jnp.max(flag) > 0.0,
                        lambda: _dense_attention(lam, *operands, n_batch, seq, n_ctx, **kw),
                        lambda: y)
    for l in range(depth):
        lp = _layer_params(l, p)
        want_ctx = l < depth - 1
        mod3 = mod[l]
        tok_a = _in_proj_a(xa, mod3, lp['g1'], lp['w_a'], group_of_tile)
        q_tok, kt = _in_proj_b(xa, mod3, lp['g1'], lp['w_bt'], lp['gain_b'], cos_h, sin_h,
                               group_of_tile, pos_of_tile)
        mq, mkt, mv = _mla_prep(tok_a, lp['gqa'], lp['gkva'], lp['wq_t'], lp['wk_t'], lp['wv_m'],
                                lp['gq'], lp['gk'], cos_m, sin_m, pos_of_tile)
        nat_bias = _nat_bias_table(nat_rpb[l].astype(F32) * LOG2E, seq)

        def branches(latent):
            win_args = (q_tok, kt, tok_a)
            if latent:
                m_win = lp['m_win']
                sink_rel = jnp.concatenate([lp['sink'] - m_win, jnp.full((1, 8), m_win, F32)], axis=1)
                y_win, flag = _win_attention(sink_rel, *win_args, win_mask - m_win, n_batch, seq, n_ctx, True,
                                             ones=ones)
                y_win = lax.cond(jnp.max(flag) > 0.0,
                                 lambda: _win_attention(lp['sink'], *win_args, win_mask, n_batch, seq, n_ctx, True),
                                 lambda: y_win)
            else:
                y_win = _win_attention(lp['sink'], *win_args, win_mask, n_batch, seq, n_ctx, False)
            y_dif = dense(lp['lam'], lp['m_dif'], q_tok, 1, 512, kt, KT_DK // 512, 512, tok_a, A_DV // 512, 512,
                          lp['subln'], n_heads=2 * DIF_HEADS, packed=True, diff=True, latent=latent,
                          lam_scale=lp['lam_scale'], name="dif_attn" if latent else "dif_attn_ctx")
            if latent:
                m_nl = lp['m_nat'] + jnp.maximum(jnp.max(nat_rpb[l].astype(F32)) * LOG2E, 0.0)
                y_nat, flag = _nat_attention(q_tok, kt, tok_a, nat_bias - m_nl, n_batch, seq, n_ctx,
                                             par=m_nl.reshape(1, 1).astype(F32), ones=ones)
                y_nat = lax.cond(jnp.max(flag) > 0.0,
                                 lambda: _nat_attention(q_tok, kt, tok_a, nat_bias, n_batch, seq, n_ctx),
                                 lambda: y_nat)
            else:
                y_nat = dense(lp['lam'], lp['m_nat'], q_tok, 2, 512, kt, KT_NK // 512, 512, tok_a, A_NV // 512, 512,
                              None, n_heads=NAT_HEADS, packed=True, diff=False, latent=False,
                              lam_scale=1.0, name="nat_attn_ctx")
            y_mla = dense(lp['lam'], lp['m_mla'], mq, 0, 1024, mkt, 0, 1024, mv, 0, 512, None,
                          n_heads=MLA_HEADS, packed=False, diff=False, latent=latent, lam_scale=1.0,
                          name="mla_attn" if latent else "mla_attn_ctx")
            return [y_win, y_dif, y_nat, y_mla]

        ys = branches(True)
        ys_c = branches(False) if want_ctx else None
        n_rows = n_all if want_ctx else n_lat
        xn, h2, scores = _merge(xa, mod3, lp['g2'], ys, ys_c, tok_a, lp['wb'], lp['wo'], rw, n_rows,
                                group_of_tile)
        blk_e, src, dst, wcol = _route(scores[:, :N_EXPERTS], router_b, n_rows)
        f = _experts(blk_e, src, dst, h2, wcol, lp['w1'], lp['w3'], lp['w2'])
        xa = _combine(xn, mod3, f, n_rows, group_of_tile)
    return xa[:n_lat].reshape(n_batch, seq, d)
```

```python
import functools
import math

import jax
import jax.numpy as jnp
import numpy as np
from jax import lax
from jax.experimental import pallas as pl
from jax.experimental.pallas import tpu as pltpu

F32 = jnp.float32
BF = jnp.bfloat16

GRID_W = 64
HEAD_DIM = 64
N_BRANCH = 4
BRANCH_W = 512
ROPE_BASE = 10000.0
EPS = 1e-6
NEG = -1e30
LOG2E = math.log2(math.e)
WIN_HEADS, WIN_KV_HEADS, WIN_RADIUS = 8, 2, 128
DIF_HEADS, DIF_QK_DIM, DIF_V_DIM = 4, 64, 128
NAT_HEADS, NAT_WIN_ROWS, NAT_WIN_COLS = 8, 8, 16
MLA_HEADS, MLA_NOPE, MLA_ROPE, MLA_V, MLA_Q_LORA, MLA_KV_LORA = 8, 64, 32, 64, 256, 128
MLA_QK = MLA_NOPE + MLA_ROPE
N_EXPERTS, N_GROUPS, TOP_K, D_EXPERT, MOE_BLOCK = 16, 4, 2, 512, 256

LANES = 128
TM = 512
WIN_TQ = 256
NAT_ROWS_PER_STEP = 4
NAT_KEY_ROWS = 12
DENSE_TQ = 1024
DENSE_TQ_ONLINE = 512
DENSE_TK = 1024
DENSE_RB = 64
DENSE_KC = 256
VMEM_LIMIT = 48 * 1024 * 1024

A_GATES, A_DV, A_NV, A_MQA, A_MKVA, A_WV = 0, 4096, 4608, 5120, 5376, 5632
A_COLS = 5888
Q_COLS = 1536
KT_DK, KT_NK, KT_WK = 0, 512, 1024
KT_ROWS = 1280
B_ROWS = Q_COLS + KT_ROWS


def _cparams(sem, vmem=VMEM_LIMIT):
    return pltpu.CompilerParams(dimension_semantics=sem, vmem_limit_bytes=vmem)


def _nt_dot(a, b):
    return lax.dot_general(a, b, (((1,), (1,)), ((), ())), preferred_element_type=F32)


def _norm_mod(x, g, sc, sh):
    ms = jnp.mean(x * x, axis=-1, keepdims=True)
    return (x * lax.rsqrt(ms + EPS) * g) * (1.0 + sc) + sh


def _lane_tile(a, n):
    reps = n // a.shape[1]
    return a if reps == 1 else jnp.concatenate([a] * reps, axis=1)


def _mod_kernel(c_ref, w_ref, b_ref, o_ref):
    cc = c_ref[...]
    a = cc * jax.nn.sigmoid(cc)
    o_ref[0] = jnp.dot(a, w_ref[0], preferred_element_type=F32,
                       precision=lax.Precision.HIGHEST) + b_ref[0]


def _modulation(cc, ada_w, ada_b):
    n_layers, d, d6 = ada_w.shape
    tn = 1536
    return pl.pallas_call(
        _mod_kernel,
        out_shape=jax.ShapeDtypeStruct((n_layers, 8, d6), F32),
        grid=(n_layers, d6 // tn),
        in_specs=[pl.BlockSpec((8, d), lambda l, j: (0, 0)),
                  pl.BlockSpec((1, d, tn), lambda l, j: (l, 0, j)),
                  pl.BlockSpec((1, 1, tn), lambda l, j: (l, 0, j))],
        out_specs=pl.BlockSpec((1, 8, tn), lambda l, j: (l, 0, j)),
        compiler_params=_cparams(("parallel", "parallel")),
        name="adaln_mod",
    )(cc, ada_w, ada_b.reshape(n_layers, 1, d6))


def _in_a_kernel(x_ref, mod_ref, g_ref, w_ref, o_ref):
    m = mod_ref[0]
    h = _norm_mod(x_ref[...], g_ref[...], m[1:2], m[0:1]).astype(BF)
    o_ref[...] = jnp.dot(h, w_ref[...], preferred_element_type=F32).astype(BF)


def _in_proj_a(xa, mod3, g1, w_a, group_of_tile):
    n, d = xa.shape
    tn = A_COLS // 2
    return pl.pallas_call(
        _in_a_kernel,
        out_shape=jax.ShapeDtypeStruct((n, A_COLS), BF),
        grid=(A_COLS // tn, n // TM),
        in_specs=[pl.BlockSpec((TM, d), lambda j, i: (i, 0)),
                  pl.BlockSpec((1, 8, d), lambda j, i: (group_of_tile(i), 0, 0)),
                  pl.BlockSpec((1, d), lambda j, i: (0, 0)),
                  pl.BlockSpec((d, tn), lambda j, i: (0, j))],
        out_specs=pl.BlockSpec((TM, tn), lambda j, i: (i, j)),
        compiler_params=_cparams(("parallel", "parallel")),
        name="in_proj_tok",
    )(xa, mod3, g1, w_a)


def _head_norm_rope(x, g, cos, sin, rope):
    ss = jnp.sum(x * x, axis=0, keepdims=True)
    y = x * lax.rsqrt(ss * (1.0 / HEAD_DIM) + EPS) * g
    if not rope:
        return y
    half = HEAD_DIM // 2
    y1, y2 = y[:half], y[half:]
    return jnp.concatenate([y1 * cos - y2 * sin, y1 * sin + y2 * cos], axis=0)


def _in_b_kernel(x_ref, mod_ref, g_ref, wt_ref, gain_ref, cos_ref, sin_ref, q_ref, kt_ref, acc_sc):
    m = mod_ref[0]
    h = _norm_mod(x_ref[...], g_ref[...], m[1:2], m[0:1]).astype(BF)
    acc_sc[...] = _nt_dot(wt_ref[...], h)
    tm = h.shape[0]
    cos = cos_ref[...]
    sin = sin_ref[...]

    def pair(r0, rope):
        hs = []
        for e in range(2):
            r = r0 + e * HEAD_DIM
            g = _lane_tile(gain_ref[r:r + HEAD_DIM, :], tm)
            hs.append(_head_norm_rope(acc_sc[r:r + HEAD_DIM, :], g, cos, sin, rope))
        return jnp.concatenate(hs, axis=0)

    for p in range(Q_COLS // LANES):
        y = pair(p * LANES, rope=p < 8)
        q_ref[:, p * LANES:(p + 1) * LANES] = y.T.astype(BF)
    for p in range(KT_ROWS // LANES):
        y = pair(Q_COLS + p * LANES, rope=not (4 <= p < 8))
        kt_ref[p * LANES:(p + 1) * LANES, :] = y.astype(BF)


def _in_proj_b(xa, mod3, g1, w_bt, gain_b, cos_t, sin_t, group_of_tile, pos_of_tile):
    n, d = xa.shape
    return pl.pallas_call(
        _in_b_kernel,
        out_shape=(jax.ShapeDtypeStruct((n, Q_COLS), BF),
                   jax.ShapeDtypeStruct((KT_ROWS, n), BF)),
        grid=(n // TM,),
        in_specs=[pl.BlockSpec((TM, d), lambda i: (i, 0)),
                  pl.BlockSpec((1, 8, d), lambda i: (group_of_tile(i), 0, 0)),
                  pl.BlockSpec((1, d), lambda i: (0, 0)),
                  pl.BlockSpec((B_ROWS, d), lambda i: (0, 0)),
                  pl.BlockSpec((B_ROWS, LANES), lambda i: (0, 0)),
                  pl.BlockSpec((HEAD_DIM // 2, TM), lambda i: (0, pos_of_tile(i))),
                  pl.BlockSpec((HEAD_DIM // 2, TM), lambda i: (0, pos_of_tile(i)))],
        out_specs=(pl.BlockSpec((TM, Q_COLS), lambda i: (i, 0)),
                   pl.BlockSpec((KT_ROWS, TM), lambda i: (0, i))),
        scratch_shapes=[pltpu.VMEM((B_ROWS, TM), F32)],
        compiler_params=_cparams(("parallel",)),
        name="in_proj_heads",
    )(xa, mod3, g1, w_bt, gain_b, cos_t, sin_t)


def _mla_kernel(qa_ref, kva_ref, gqa_ref, gkva_ref, wqt_ref, wkt_ref, wv_ref, gq_ref, gk_ref,
                cos_ref, sin_ref, mq_ref, mkt_ref, mv_ref):
    tm = qa_ref.shape[0]
    cos = cos_ref[...]
    sin = sin_ref[...]
    rh = MLA_ROPE // 2

    def rms_rows(x, g):
        ms = jnp.mean(x * x, axis=-1, keepdims=True)
        return x * lax.rsqrt(ms + EPS) * g

    def rope_rows(x):
        x1, x2 = x[:rh], x[rh:]
        return jnp.concatenate([x1 * cos - x2 * sin, x1 * sin + x2 * cos], axis=0)

    qa = rms_rows(qa_ref[...].astype(F32), gqa_ref[...]).astype(BF)
    qt = _nt_dot(wqt_ref[...], qa)
    kva = kva_ref[...].astype(F32)
    cn = rms_rows(kva[:, :MLA_KV_LORA], gkva_ref[...]).astype(BF)
    knt = _nt_dot(wkt_ref[...], cn)
    mv_ref[...] = jnp.dot(cn, wv_ref[...], preferred_element_type=F32).astype(BF)
    krope = kva[:, MLA_KV_LORA:].T[:MLA_ROPE]
    kr_ss = jnp.sum(krope * krope, axis=0, keepdims=True)
    gq = _lane_tile(gq_ref[...], tm)
    gk = _lane_tile(gk_ref[...], tm)
    zpad = jnp.zeros((LANES - MLA_QK, tm), F32)
    for hd in range(MLA_HEADS):
        x = qt[hd * MLA_QK:(hd + 1) * MLA_QK]
        ss = jnp.sum(x * x, axis=0, keepdims=True)
        y = x * lax.rsqrt(ss * (1.0 / MLA_QK) + EPS) * gq
        y = jnp.concatenate([y[:MLA_NOPE], rope_rows(y[MLA_NOPE:]), zpad], axis=0)
        mq_ref[:, hd * LANES:(hd + 1) * LANES] = y.T.astype(BF)
        kn = knt[hd * MLA_NOPE:(hd + 1) * MLA_NOPE]
        ss = jnp.sum(kn * kn, axis=0, keepdims=True) + kr_ss
        r = lax.rsqrt(ss * (1.0 / MLA_QK) + EPS)
        yk = jnp.concatenate([kn * r * gk[:MLA_NOPE], rope_rows(krope * r * gk[MLA_NOPE:]), zpad], axis=0)
        mkt_ref[hd * LANES:(hd + 1) * LANES, :] = yk.astype(BF)


def _mla_prep(tok_a, gqa, gkva, wq_t, wk_t, wv, gq, gk, cos_t, sin_t, pos_of_tile):
    n = tok_a.shape[0]
    hw = MLA_HEADS * LANES
    return pl.pallas_call(
        _mla_kernel,
        out_shape=(jax.ShapeDtypeStruct((n, hw), BF),
                   jax.ShapeDtypeStruct((hw, n), BF),
                   jax.ShapeDtypeStruct((n, MLA_HEADS * MLA_V), BF)),
        grid=(n // TM,),
        in_specs=[pl.BlockSpec((TM, 256), lambda i: (i, A_MQA // 256)),
                  pl.BlockSpec((TM, 256), lambda i: (i, A_MKVA // 256)),
                  pl.BlockSpec((1, MLA_Q_LORA), lambda i: (0, 0)),
                  pl.BlockSpec((1, MLA_KV_LORA), lambda i: (0, 0)),
                  pl.BlockSpec(wq_t.shape, lambda i: (0, 0)),
                  pl.BlockSpec(wk_t.shape, lambda i: (0, 0)),
                  pl.BlockSpec(wv.shape, lambda i: (0, 0)),
                  pl.BlockSpec((MLA_QK, LANES), lambda i: (0, 0)),
                  pl.BlockSpec((MLA_QK, LANES), lambda i: (0, 0)),
                  pl.BlockSpec((MLA_ROPE // 2, TM), lambda i: (0, pos_of_tile(i))),
                  pl.BlockSpec((MLA_ROPE // 2, TM), lambda i: (0, pos_of_tile(i)))],
        out_specs=(pl.BlockSpec((TM, hw), lambda i: (i, 0)),
                   pl.BlockSpec((hw, TM), lambda i: (0, i)),
                   pl.BlockSpec((TM, MLA_HEADS * MLA_V), lambda i: (i, 0))),
        compiler_params=_cparams(("parallel",)),
        name="mla_prep",
    )(tok_a, tok_a, gqa, gkva, wq_t, wk_t, wv, gq, gk, cos_t, sin_t)


def _half_mask(shape):
    return lax.broadcasted_iota(jnp.int32, shape, 1) < (LANES // 2)


def _select_half(q, e, lo_mask):
    zero = jnp.zeros_like(q)
    return jnp.where(lo_mask, q, zero) if e == 0 else jnp.where(lo_mask, zero, q)


def _local_softmax_out(parts, extra_logit):
    m = parts[0][0].max(axis=-1, keepdims=True)
    for s, _ in parts[1:]:
        m = jnp.maximum(m, s.max(axis=-1, keepdims=True))
    if extra_logit is not None:
        m = jnp.maximum(m, extra_logit)
    z = None
    o = None
    for s, v in parts:
        p = jnp.exp2(s - m)
        zs = p.sum(axis=-1, keepdims=True)
        os_ = jnp.dot(p.astype(BF), v, preferred_element_type=F32)
        z = zs if z is None else z + zs
        o = os_ if o is None else o + os_
    if extra_logit is not None:
        z = z + jnp.exp2(extra_logit - m)
    return o / z


def _fixed_softmax_out(parts, ones_ref, extra_logit):
    acc = None
    for s, v in parts:
        aug = jnp.concatenate([v, ones_ref[:s.shape[1], :]], axis=1)
        t = jnp.dot(jnp.exp2(s).astype(BF), aug, preferred_element_type=F32)
        acc = t if acc is None else acc + t
    l = acc[:, LANES:]
    if extra_logit is not None:
        l = l + jnp.exp2(jnp.zeros_like(l) + extra_logit)
    unsafe = jnp.logical_not((l > 2.0 ** -SAFE_SUM_LOG2) & (l < 2.0 ** SAFE_SUM_LOG2))
    return acc[:, :LANES] / l, jnp.max(jnp.where(unsafe, 1.0, 0.0), axis=0, keepdims=True)


def _win_kernel(sink_ref, q_ref, *refs, band, fixed=False):
    if band:
        k0, k1, k2, k3, v0, v1, v2, v3, kc_ref, vc_ref, mask_ref = refs[:11]
        refs = refs[11:]
        kb = jnp.concatenate([k0[...], k1[...], k2[...], k3[...]], axis=1)
        vb = jnp.concatenate([v0[...], v1[...], v2[...], v3[...]], axis=0)
        mask = mask_ref[0]
    else:
        kc_ref, vc_ref = refs[:2]
        refs = refs[2:]
    if fixed:
        ones_ref, o_ref, flag_ref = refs
        ref_logit = sink_ref[0, WIN_HEADS]
    else:
        (o_ref,) = refs
    q = q_ref[...]
    lo = _half_mask((q.shape[0], LANES))
    group = WIN_HEADS // WIN_KV_HEADS
    bad = []
    for j in range(WIN_HEADS // 2):
        qp = q[:, j * LANES:(j + 1) * LANES]
        g = (2 * j) // group
        kc = kc_ref[g * LANES:(g + 1) * LANES, :]
        vc = vc_ref[:, g * LANES:(g + 1) * LANES]
        outs = []
        for e in range(2):
            qm = _select_half(qp, e, lo)
            parts = []
            if band:
                s = jnp.dot(qm, kb[g * LANES:(g + 1) * LANES, :], preferred_element_type=F32) + mask
                parts.append((s, vb[:, g * LANES:(g + 1) * LANES]))
            sc = jnp.dot(qm, kc, preferred_element_type=F32)
            if fixed:
                parts.append((sc - ref_logit, vc))
                o, u = _fixed_softmax_out(parts, ones_ref, sink_ref[0, 2 * j + e])
                bad.append(u)
                outs.append(o)
            else:
                parts.append((sc, vc))
                outs.append(_local_softmax_out(parts, sink_ref[0, 2 * j + e]))
        o_ref[:, j * LANES:(j + 1) * LANES] = jnp.where(lo, outs[0], outs[1]).astype(BF)
    if fixed:
        flag_ref[0] = jnp.concatenate(bad, axis=0)


def _win_attention(sink, q_tok, kt, tok_a, mask_tbl, n_batch, seq, n_ctx, latent, ones=None):
    ctx_blk = (n_batch * seq) // n_ctx
    kc_spec = lambda f: pl.BlockSpec((2 * LANES, n_ctx), f)
    vc_spec = lambda f: pl.BlockSpec((n_ctx, 2 * LANES), f)
    smem = pl.BlockSpec(memory_space=pltpu.SMEM)
    n_out = n_batch * (seq if latent else n_ctx)
    out_shape = jax.ShapeDtypeStruct((n_out, WIN_HEADS * HEAD_DIM), BF)
    if not latent:
        return pl.pallas_call(
            functools.partial(_win_kernel, band=False),
            out_shape=out_shape,
            grid=(n_batch,),
            in_specs=[smem,
                      pl.BlockSpec((n_ctx, 512), lambda b: (ctx_blk + b, 0)),
                      kc_spec(lambda b: (KT_WK // 256, ctx_blk + b)),
                      vc_spec(lambda b: (ctx_blk + b, A_WV // 256))],
            out_specs=pl.BlockSpec((n_ctx, 512), lambda b: (b, 0)),
            compiler_params=_cparams(("parallel",)),
            name="win_attn_ctx",
        )(sink, q_tok, kt, tok_a)
    nq = seq // WIN_TQ
    nkb = seq // LANES

    def kidx(j):
        return lambda b, i: (KT_WK // 256, b * nkb + jnp.clip(2 * i - 1 + j, 0, nkb - 1))

    def vidx(j):
        return lambda b, i: (b * nkb + jnp.clip(2 * i - 1 + j, 0, nkb - 1), A_WV // 256)

    def variant(b, i):
        return (jnp.where(i == 0, 0, jnp.where(i == nq - 1, 2, 1)), 0, 0)

    in_specs = ([smem, pl.BlockSpec((WIN_TQ, 512), lambda b, i: (b * nq + i, 0))]
                + [pl.BlockSpec((2 * LANES, LANES), kidx(j)) for j in range(4)]
                + [pl.BlockSpec((LANES, 2 * LANES), vidx(j)) for j in range(4)]
                + [kc_spec(lambda b, i: (KT_WK // 256, ctx_blk + b)),
                   vc_spec(lambda b, i: (ctx_blk + b, A_WV // 256)),
                   pl.BlockSpec((1, WIN_TQ, 4 * LANES), variant)])
    args = [sink, q_tok, kt, kt, kt, kt, tok_a, tok_a, tok_a, tok_a, kt, tok_a, mask_tbl]
    out_specs = pl.BlockSpec((WIN_TQ, 512), lambda b, i: (b * nq + i, 0))
    fixed = ones is not None
    if fixed:
        in_specs.append(pl.BlockSpec(ones.shape, lambda b, i: (0, 0)))
        args.append(ones)
        out_shape = (out_shape, jax.ShapeDtypeStruct((n_batch * nq, WIN_HEADS, LANES), F32))
        out_specs = (out_specs, pl.BlockSpec((1, WIN_HEADS, LANES), lambda b, i: (b * nq + i, 0, 0)))
    return pl.pallas_call(
        functools.partial(_win_kernel, band=True, fixed=fixed),
        out_shape=out_shape,
        grid=(n_batch, nq),
        in_specs=in_specs,
        out_specs=out_specs,
        compiler_params=_cparams(("parallel", "parallel")),
        name="win_attn_fixed" if fixed else "win_attn",
    )(*args)


def _nat_kernel(q_ref, k0, k1, k2, v0, v1, v2, kc_ref, vc_ref, bias_ref, *refs, fixed=False):
    if fixed:
        par_ref, ones_ref, o_ref, flag_ref = refs
        ref_logit = par_ref[0, 0]
    else:
        (o_ref,) = refs
    q = q_ref[...]
    kb = jnp.concatenate([k0[...], k1[...], k2[...]], axis=1)
    vb = jnp.concatenate([v0[...], v1[...], v2[...]], axis=0)
    lo = _half_mask((q.shape[0], LANES))
    bad = []
    for j in range(NAT_HEADS // 2):
        sl = slice(j * LANES, (j + 1) * LANES)
        qp = q[:, sl]
        outs = []
        for e in range(2):
            qm = _select_half(qp, e, lo)
            s = jnp.dot(qm, kb[sl, :], preferred_element_type=F32) + bias_ref[0, 2 * j + e]
            sc = jnp.dot(qm, kc_ref[sl, :], preferred_element_type=F32)
            if fixed:
                o, u = _fixed_softmax_out([(s, vb[:, sl]), (sc - ref_logit, vc_ref[:, sl])], ones_ref, None)
                bad.append(u)
                outs.append(o)
            else:
                outs.append(_local_softmax_out([(s, vb[:, sl]), (sc, vc_ref[:, sl])], None))
        o_ref[:, sl] = jnp.where(lo, outs[0], outs[1]).astype(BF)
    if fixed:
        flag_ref[0] = jnp.concatenate(bad, axis=0)


def _nat_attention(q_tok, kt, tok_a, bias_tbl, n_batch, seq, n_ctx, par=None, ones=None):
    tq = NAT_ROWS_PER_STEP * GRID_W
    nq = seq // tq
    rows = seq // GRID_W
    ctx_blk = (n_batch * seq) // n_ctx
    q_col = 2
    k_row = KT_NK // 512
    v_col = A_NV // 512

    def wstart(i):
        return jnp.clip(NAT_ROWS_PER_STEP * i - NAT_WIN_ROWS // 2, 0, rows - NAT_KEY_ROWS) // NAT_ROWS_PER_STEP

    def kidx(j):
        return lambda b, i: (k_row, b * nq + wstart(i) + j)

    def vidx(j):
        return lambda b, i: (b * nq + wstart(i) + j, v_col)

    def variant(b, i):
        return (jnp.where(i == 0, 0, jnp.where(i == nq - 1, 2, 1)), 0, 0, 0)

    nk = NAT_KEY_ROWS * GRID_W
    in_specs = ([pl.BlockSpec((tq, 512), lambda b, i: (b * nq + i, q_col))]
                + [pl.BlockSpec((512, tq), kidx(j)) for j in range(3)]
                + [pl.BlockSpec((tq, 512), vidx(j)) for j in range(3)]
                + [pl.BlockSpec((512, n_ctx), lambda b, i: (k_row, ctx_blk + b)),
                   pl.BlockSpec((n_ctx, 512), lambda b, i: (ctx_blk + b, v_col)),
                   pl.BlockSpec((1, NAT_HEADS, tq, nk), variant)])
    args = [q_tok, kt, kt, kt, tok_a, tok_a, tok_a, kt, tok_a, bias_tbl]
    out_shape = jax.ShapeDtypeStruct((n_batch * seq, NAT_HEADS * HEAD_DIM), BF)
    out_specs = pl.BlockSpec((tq, 512), lambda b, i: (b * nq + i, 0))
    fixed = ones is not None
    if fixed:
        in_specs += [pl.BlockSpec(memory_space=pltpu.SMEM), pl.BlockSpec(ones.shape, lambda b, i: (0, 0))]
        args += [par, ones]
        out_shape = (out_shape, jax.ShapeDtypeStruct((n_batch * nq, NAT_HEADS, LANES), F32))
        out_specs = (out_specs, pl.BlockSpec((1, NAT_HEADS, LANES), lambda b, i: (b * nq + i, 0, 0)))
    return pl.pallas_call(
        functools.partial(_nat_kernel, fixed=fixed),
        out_shape=out_shape,
        grid=(n_batch, nq),
        in_specs=in_specs,
        out_specs=out_specs,
        compiler_params=_cparams(("parallel", "parallel")),
        name="nat_attn_fixed" if fixed else "nat_attn",
    )(*args)


def _dense_kernel(lam_ref, q_ref, kc_ref, vc_ref, *refs, n_heads, packed, diff, latent, lam_scale):
    if latent:
        k_ref, v_ref = refs[0], refs[1]
        refs = refs[2:]
    if diff:
        subln_ref, o_ref, qm_sc, m_sc, l_sc, acc_sc, s_sc, p_sc = refs
    else:
        o_ref, qm_sc, m_sc, l_sc, acc_sc, s_sc, p_sc = refs
    kt_step = pl.program_id(2) if latent else 0
    tq = q_ref.shape[0]

    def kv_slices(h):
        blk = h // 2 if packed else h
        ks = slice(blk * LANES, (blk + 1) * LANES)
        vs = ks if packed else slice((h // 2) * LANES, (h // 2 + 1) * LANES)
        return ks, vs

    def step(h, k_ref_, v_ref_):
        ks, vs = kv_slices(h)
        nk = k_ref_.shape[1]
        slot = h % 2
        s_sc[slot, :, :nk] = jnp.dot(qm_sc[h], k_ref_[ks, :], preferred_element_type=F32)
        for r in range(tq // DENSE_RB):
            rows = slice(r * DENSE_RB, (r + 1) * DENSE_RB)
            mx = s_sc[slot, rows, 0:LANES]
            for c in range(1, nk // LANES):
                mx = jnp.maximum(mx, s_sc[slot, rows, c * LANES:(c + 1) * LANES])
            m_old = m_sc[h, rows, :]
            m_new = jnp.maximum(m_old, jnp.max(mx, axis=-1, keepdims=True))
            alpha = jnp.exp2(m_old - m_new)
            lsum = None
            for c in range(nk // LANES):
                cols = slice(c * LANES, (c + 1) * LANES)
                p = jnp.exp2(s_sc[slot, rows, cols] - m_new)
                lsum = p if lsum is None else lsum + p
                p_sc[slot, rows, cols] = p.astype(BF)
            m_sc[h, rows, :] = m_new
            l_sc[h, rows, :] = alpha * l_sc[h, rows, :] + jnp.sum(lsum, axis=-1, keepdims=True)
            acc_sc[h, rows, :] = alpha * acc_sc[h, rows, :]
        acc_sc[h] += jnp.dot(p_sc[slot, :, :nk], v_ref_[:, vs], preferred_element_type=F32)

    @pl.when(kt_step == 0)
    def _():
        q = q_ref[...]
        lo = _half_mask((tq, LANES))
        for h in range(n_heads):
            if packed:
                qp = q[:, (h // 2) * LANES:(h // 2 + 1) * LANES]
                qm_sc[h] = _select_half(qp, h % 2, lo)
            else:
                qm_sc[h] = q[:, h * LANES:(h + 1) * LANES]
        m_sc[...] = jnp.full(m_sc.shape, NEG, F32)
        l_sc[...] = jnp.zeros(l_sc.shape, F32)
        acc_sc[...] = jnp.zeros(acc_sc.shape, F32)
        for h in range(n_heads):
            step(h, kc_ref, vc_ref)

    if latent:
        for h in range(n_heads):
            step(h, k_ref, v_ref)
        last = kt_step == pl.num_programs(2) - 1
    else:
        last = True

    def finish():
        lo = _half_mask((tq, LANES))
        if diff:
            lam = lam_ref[0, 0]
            for hv in range(n_heads // 2):
                y = (acc_sc[2 * hv] / l_sc[2 * hv]
                     - lam * (acc_sc[2 * hv + 1] / l_sc[2 * hv + 1]))
                ms = jnp.mean(y * y, axis=-1, keepdims=True)
                y = y * lax.rsqrt(ms + EPS) * subln_ref[...] * lam_scale
                o_ref[:, hv * LANES:(hv + 1) * LANES] = y.astype(BF)
        else:
            for hp in range(n_heads // 2):
                o0 = acc_sc[2 * hp] / l_sc[2 * hp]
                o1 = acc_sc[2 * hp + 1] / l_sc[2 * hp + 1]
                o_ref[:, hp * LANES:(hp + 1) * LANES] = jnp.where(lo, o0, o1).astype(BF)

    if latent:
        pl.when(last)(finish)
    else:
        finish()


def _dense_attention(lam, q_arr, q_col, q_w, k_arr, k_row, k_w, v_arr, v_col, v_w, subln,
                     n_batch, seq, n_ctx, *, n_heads, packed, diff, latent, lam_scale, name):
    n = n_batch * (seq if latent else n_ctx)
    ctx_blk = (n_batch * seq) // n_ctx
    out_w = v_w
    smem = pl.BlockSpec(memory_space=pltpu.SMEM)
    kern = functools.partial(_dense_kernel, n_heads=n_heads, packed=packed, diff=diff,
                             latent=latent, lam_scale=lam_scale)
    tq = DENSE_TQ_ONLINE if latent else n_ctx
    max_nk = DENSE_TK if latent else n_ctx
    scratch = [pltpu.VMEM((n_heads, tq, LANES), BF),
               pltpu.VMEM((n_heads, tq, LANES), F32),
               pltpu.VMEM((n_heads, tq, LANES), F32),
               pltpu.VMEM((n_heads, tq, LANES), F32),
               pltpu.VMEM((2, tq, max_nk), F32),
               pltpu.VMEM((2, tq, max_nk), BF)]
    out_shape = jax.ShapeDtypeStruct((n, out_w), BF)
    if latent:
        nq = seq // tq
        nk = seq // DENSE_TK
        grid = (n_batch, nq, nk)
        in_specs = [smem,
                    pl.BlockSpec((tq, q_w), lambda b, i, k: (b * nq + i, q_col)),
                    pl.BlockSpec((k_w, n_ctx), lambda b, i, k: (k_row, ctx_blk + b)),
                    pl.BlockSpec((n_ctx, v_w), lambda b, i, k: (ctx_blk + b, v_col)),
                    pl.BlockSpec((k_w, DENSE_TK), lambda b, i, k: (k_row, b * nk + k)),
                    pl.BlockSpec((DENSE_TK, v_w), lambda b, i, k: (b * nk + k, v_col))]
        args = [lam, q_arr, k_arr, v_arr, k_arr, v_arr]
        if diff:
            in_specs.append(pl.BlockSpec((1, LANES), lambda b, i, k: (0, 0)))
            args.append(subln)
        out_specs = pl.BlockSpec((tq, out_w), lambda b, i, k: (b * nq + i, 0))
        sem = ("parallel", "parallel", "arbitrary")
    else:
        grid = (n_batch,)
        in_specs = [smem,
                    pl.BlockSpec((tq, q_w), lambda b: (ctx_blk + b, q_col)),
                    pl.BlockSpec((k_w, n_ctx), lambda b: (k_row, ctx_blk + b)),
                    pl.BlockSpec((n_ctx, v_w), lambda b: (ctx_blk + b, v_col))]
        args = [lam, q_arr, k_arr, v_arr]
        if diff:
            in_specs.append(pl.BlockSpec((1, LANES), lambda b: (0, 0)))
            args.append(subln)
        out_specs = pl.BlockSpec((tq, out_w), lambda b: (b, 0))
        sem = ("parallel",)
    return pl.pallas_call(
        kern, out_shape=out_shape, grid=grid, in_specs=in_specs, out_specs=out_specs,
        scratch_shapes=scratch, compiler_params=_cparams(sem), name=name,
    )(*args)


SAFE_SUM_LOG2 = 100.0


def _dense_fixed_kernel(par_ref, q_ref, kc_ref, vc_ref, *refs, n_heads, packed, diff, latent, lam_scale):
    if latent:
        k_ref, v_ref = refs[0], refs[1]
        refs = refs[2:]
    ones_ref = refs[0]
    refs = refs[1:]
    if diff:
        subln_ref, o_ref, flag_ref, qm_sc, acc_sc, p_sc = refs
    else:
        o_ref, flag_ref, qm_sc, acc_sc, p_sc = refs
    kt_step = pl.program_id(2) if latent else 0
    tq = q_ref.shape[0]
    ref_logit = par_ref[0, 1]

    def step(h, k_ref_, v_ref_, first):
        blk = h // 2 if packed else h
        ks = slice(blk * LANES, (blk + 1) * LANES)
        vs = ks if packed else slice((h // 2) * LANES, (h // 2 + 1) * LANES)
        nk = k_ref_.shape[1]
        slot = h % 2
        for c in range(nk // DENSE_KC):
            cols = slice(c * DENSE_KC, (c + 1) * DENSE_KC)
            s = jnp.dot(qm_sc[h], k_ref_[ks, cols], preferred_element_type=F32)
            p_sc[slot, :, cols] = jnp.exp2(s - ref_logit).astype(BF)
        v_aug = jnp.concatenate([v_ref_[:, vs], ones_ref[:nk, :]], axis=1)
        pv = jnp.dot(p_sc[slot, :, :nk], v_aug, preferred_element_type=F32)
        if first:
            acc_sc[h] = pv
        else:
            acc_sc[h] += pv

    @pl.when(kt_step == 0)
    def _():
        q = q_ref[...]
        lo = _half_mask((tq, LANES))
        for h in range(n_heads):
            if packed:
                qp = q[:, (h // 2) * LANES:(h // 2 + 1) * LANES]
                qm_sc[h] = _select_half(qp, h % 2, lo)
            else:
                qm_sc[h] = q[:, h * LANES:(h + 1) * LANES]
        for h in range(n_heads):
            step(h, kc_ref, vc_ref, True)

    if latent:
        for h in range(n_heads):
            step(h, k_ref, v_ref, False)
        last = kt_step == pl.num_programs(2) - 1

    def finish():
        lo = _half_mask((tq, LANES))
        outs, bad = [], []
        for h in range(n_heads):
            a = acc_sc[h]
            l = a[:, LANES:]
            unsafe = jnp.logical_not((l > 2.0 ** -SAFE_SUM_LOG2) & (l < 2.0 ** SAFE_SUM_LOG2))
            bad.append(jnp.max(jnp.where(unsafe, 1.0, 0.0), axis=0, keepdims=True))
            outs.append(a[:, :LANES] / l)
        flag_ref[0] = jnp.concatenate(bad, axis=0)
        if diff:
            lam = par_ref[0, 0]
            for hv in range(n_heads // 2):
                y = outs[2 * hv] - lam * outs[2 * hv + 1]
                ms = jnp.mean(y * y, axis=-1, keepdims=True)
                y = y * lax.rsqrt(ms + EPS) * subln_ref[...] * lam_scale
                o_ref[:, hv * LANES:(hv + 1) * LANES] = y.astype(BF)
        else:
            for hp in range(n_heads // 2):
                o_ref[:, hp * LANES:(hp + 1) * LANES] = jnp.where(lo, outs[2 * hp], outs[2 * hp + 1]).astype(BF)

    if latent:
        pl.when(last)(finish)
    else:
        finish()


def _dense_fixed_attention(par, q_arr, q_col, q_w, k_arr, k_row, k_w, v_arr, v_col, v_w, subln, ones,
                           n_batch, seq, n_ctx, *, n_heads, packed, diff, latent, lam_scale, name):
    n = n_batch * (seq if latent else n_ctx)
    ctx_blk = (n_batch * seq) // n_ctx
    smem = pl.BlockSpec(memory_space=pltpu.SMEM)
    kern = functools.partial(_dense_fixed_kernel, n_heads=n_heads, packed=packed, diff=diff,
                             latent=latent, lam_scale=lam_scale)
    tq = DENSE_TQ if latent else n_ctx
    max_nk = DENSE_TK if latent else n_ctx
    scratch = [pltpu.VMEM((n_heads, tq, LANES), BF),
               pltpu.VMEM((n_heads, tq, 2 * LANES), F32),
               pltpu.VMEM((2, tq, max_nk), BF)]
    if latent:
        nq = seq // tq
        nk = seq // DENSE_TK
        grid = (n_batch, nq, nk)
        ix = lambda f: (lambda b, i, k: f(b, i, k))
        q_ix = ix(lambda b, i, k: (b * nq + i, q_col))
        in_specs = [smem,
                    pl.BlockSpec((tq, q_w), q_ix),
                    pl.BlockSpec((k_w, n_ctx), ix(lambda b, i, k: (k_row, ctx_blk + b))),
                    pl.BlockSpec((n_ctx, v_w), ix(lambda b, i, k: (ctx_blk + b, v_col))),
                    pl.BlockSpec((k_w, DENSE_TK), ix(lambda b, i, k: (k_row, b * nk + k))),
                    pl.BlockSpec((DENSE_TK, v_w), ix(lambda b, i, k: (b * nk + k, v_col))),
                    pl.BlockSpec((DENSE_TK, LANES), ix(lambda b, i, k: (0, 0)))]
        args = [par, q_arr, k_arr, v_arr, k_arr, v_arr, ones]
        const_ix = ix(lambda b, i, k: (0, 0))
        out_specs = (pl.BlockSpec((tq, v_w), ix(lambda b, i, k: (b * nq + i, 0))),
                     pl.BlockSpec((1, n_heads, LANES), ix(lambda b, i, k: (b * nq + i, 0, 0))))
        n_flag = n_batch * nq
        sem = ("parallel", "parallel", "arbitrary")
    else:
        grid = (n_batch,)
        in_specs = [smem,
                    pl.BlockSpec((tq, q_w), lambda b: (ctx_blk + b, q_col)),
                    pl.BlockSpec((k_w, n_ctx), lambda b: (k_row, ctx_blk + b)),
                    pl.BlockSpec((n_ctx, v_w), lambda b: (ctx_blk + b, v_col)),
                    pl.BlockSpec((DENSE_TK, LANES), lambda b: (0, 0))]
        args = [par, q_arr, k_arr, v_arr, ones]
        const_ix = lambda b: (0, 0)
        out_specs = (pl.BlockSpec((tq, v_w), lambda b: (b, 0)),
                     pl.BlockSpec((1, n_heads, LANES), lambda b: (b, 0, 0)))
        n_flag = n_batch
        sem = ("parallel",)
    if diff:
        in_specs.append(pl.BlockSpec((1, LANES), const_ix))
        args.append(subln)
    return pl.pallas_call(
        kern,
        out_shape=(jax.ShapeDtypeStruct((n, v_w), BF), jax.ShapeDtypeStruct((n_flag, n_heads, LANES), F32)),
        grid=grid, in_specs=in_specs, out_specs=out_specs,
        scratch_shapes=scratch, compiler_params=_cparams(sem), name=name + "_fixed",
    )(*args)


def _merge_kernel(x_ref, mod_ref, g2_ref, *refs, n_lat_tiles, has_ctx):
    ys = refs[:N_BRANCH]
    refs = refs[N_BRANCH:]
    if has_ctx:
        ycs = refs[:N_BRANCH]
        refs = refs[N_BRANCH:]
        is_ctx = pl.program_id(0) >= n_lat_tiles
    gts = refs[:N_BRANCH]
    wb_ref, wo_ref, rw_ref, xo_ref, h2_ref, sc_ref = refs[N_BRANCH:]
    m = mod_ref[0]
    mix = None
    for n_ in range(N_BRANCH):
        y = ys[n_][...]
        if has_ctx:
            y = jnp.where(is_ctx, ycs[n_][...], y)
        yb = jnp.dot(y, wb_ref[n_], preferred_element_type=F32)
        t = jax.nn.sigmoid(gts[n_][...].astype(F32)) * yb
        mix = t if mix is None else mix + t
    att = jnp.dot(mix.astype(BF), wo_ref[...], preferred_element_type=F32)
    xn = x_ref[...] + m[2:3] * att
    xo_ref[...] = xn
    h2 = _norm_mod(xn, g2_ref[...], m[4:5], m[3:4])
    h2_ref[...] = h2
    logits = jnp.dot(h2, rw_ref[...], preferred_element_type=F32, precision=lax.Precision.HIGHEST)
    sc_ref[...] = jax.nn.sigmoid(logits)


def _merge(xa, mod3, g2, ys, ys_ctx, tok_a, wb, wo, rw, n_rows, group_of_tile):
    d = xa.shape[1]
    n_lat_tiles = ys[0].shape[0] // TM
    has_ctx = ys_ctx is not None
    row = lambda w, c: pl.BlockSpec((TM, w), lambda i, c=c: (i, c))
    lat_row = pl.BlockSpec((TM, BRANCH_W), lambda i: (jnp.minimum(i, n_lat_tiles - 1), 0))
    in_specs = ([row(d, 0),
                 pl.BlockSpec((1, 8, d), lambda i: (group_of_tile(i), 0, 0)),
                 pl.BlockSpec((1, d), lambda i: (0, 0))]
                + [lat_row for _ in range(N_BRANCH)]
                + ([pl.BlockSpec((TM, BRANCH_W), lambda i: (0, 0)) for _ in range(N_BRANCH)] if has_ctx else [])
                + [row(d, c) for c in range(N_BRANCH)]
                + [pl.BlockSpec(wb.shape, lambda i: (0, 0, 0)),
                   pl.BlockSpec(wo.shape, lambda i: (0, 0)),
                   pl.BlockSpec(rw.shape, lambda i: (0, 0))])
    ys = list(ys) + (list(ys_ctx) if has_ctx else [])
    return pl.pallas_call(
        functools.partial(_merge_kernel, n_lat_tiles=n_lat_tiles, has_ctx=has_ctx),
        out_shape=(jax.ShapeDtypeStruct((n_rows, d), F32),
                   jax.ShapeDtypeStruct((n_rows, d), F32),
                   jax.ShapeDtypeStruct((n_rows, LANES), F32)),
        grid=(n_rows // TM,),
        in_specs=in_specs,
        out_specs=(row(d, 0), row(d, 0), row(LANES, 0)),
        compiler_params=_cparams(("parallel",)),
        name="merge",
    )(xa, mod3, g2, *ys, tok_a, tok_a, tok_a, tok_a, wb, wo, rw)


def _expert_kernel(blk_e_ref, src_ref, dst_ref, h2_hbm, w_ref, w1_ref, w3_ref, w2_ref, out_hbm,
                   xbuf, ybuf, sem_in, sem_out):
    i = pl.program_id(0)
    n_steps = pl.num_programs(0)
    slot = i % 2

    def start_gather(blk, s):
        def body(r, c):
            t = src_ref[blk * MOE_BLOCK + r]
            pltpu.make_async_copy(h2_hbm.at[pl.ds(t, 1)], xbuf.at[s, pl.ds(r, 1)], sem_in.at[s]).start()
            return c
        lax.fori_loop(0, MOE_BLOCK, body, 0, unroll=8)

    def wait_gather(s):
        pltpu.make_async_copy(h2_hbm.at[pl.ds(0, MOE_BLOCK)], xbuf.at[s], sem_in.at[s]).wait()

    def start_scatter(blk, s):
        def body(r, c):
            t = dst_ref[blk * MOE_BLOCK + r]
            pltpu.make_async_copy(ybuf.at[s, pl.ds(r, 1)], out_hbm.at[pl.ds(t, 1)], sem_out.at[s]).start()
            return c
        lax.fori_loop(0, MOE_BLOCK, body, 0, unroll=8)

    def wait_scatter(s):
        pltpu.make_async_copy(ybuf.at[s], out_hbm.at[pl.ds(0, MOE_BLOCK)], sem_out.at[s]).wait()

    @pl.when(i == 0)
    def _():
        start_gather(0, 0)

    @pl.when(i + 1 < n_steps)
    def _():
        start_gather(i + 1, 1 - slot)

    wait_gather(slot)

    @pl.when(i >= 2)
    def _():
        wait_scatter(slot)

    xb = xbuf[slot].astype(BF)
    a = jnp.dot(xb, w1_ref[0], preferred_element_type=F32)
    b = jnp.dot(xb, w3_ref[0], preferred_element_type=F32)
    hmid = (a * jax.nn.sigmoid(a) * b).astype(BF)
    ybuf[slot] = jnp.dot(hmid, w2_ref[0], preferred_element_type=F32) * w_ref[...]
    start_scatter(i, slot)

    @pl.when(i == n_steps - 1)
    def _():
        wait_scatter(slot)

        @pl.when(n_steps >= 2)
        def _():
            wait_scatter(1 - slot)


def _experts(blk_e, src, dst, h2, wcol, w1, w3, w2):
    cap = src.shape[0]
    d = h2.shape[1]
    n_blk = cap // MOE_BLOCK
    wspec = lambda shp: pl.BlockSpec((1,) + shp, lambda i, be, s, t: (be[i], 0, 0))
    grid_spec = pltpu.PrefetchScalarGridSpec(
        num_scalar_prefetch=3,
        grid=(n_blk,),
        in_specs=[pl.BlockSpec(memory_space=pl.ANY),
                  pl.BlockSpec((MOE_BLOCK, 1), lambda i, be, s, t: (i, 0)),
                  wspec((d, D_EXPERT)), wspec((d, D_EXPERT)), wspec((D_EXPERT, d))],
        out_specs=pl.BlockSpec(memory_space=pl.ANY),
        scratch_shapes=[pltpu.VMEM((2, MOE_BLOCK, d), F32),
                        pltpu.VMEM((2, MOE_BLOCK, d), F32),
                        pltpu.SemaphoreType.DMA((2,)),
                        pltpu.SemaphoreType.DMA((2,))],
    )
    return pl.pallas_call(
        _expert_kernel,
        out_shape=jax.ShapeDtypeStruct((cap, d), F32),
        grid_spec=grid_spec,
        compiler_params=_cparams(("arbitrary",)),
        name="moe_experts",
    )(blk_e, src, dst, h2, wcol, w1, w3, w2)


def _route(scores, router_b, n):
    per_group = N_EXPERTS // N_GROUPS
    biased = scores + router_b.astype(F32)

    def top2(v):
        lane = jnp.arange(v.shape[-1], dtype=jnp.int32)
        i0 = jnp.argmax(v, axis=-1).astype(jnp.int32)
        v0 = jnp.max(v, axis=-1)
        rest = jnp.where(lane == i0[..., None], -jnp.inf, v)
        i1 = jnp.argmax(rest, axis=-1).astype(jnp.int32)
        v1 = jnp.max(rest, axis=-1)
        return (v0, v1), (i0, i1)

    (g0, g1), _ = top2(biased.reshape(n, N_GROUPS, per_group))
    group = jnp.argmax(g0 + g1, axis=-1)
    in_group = (jnp.arange(N_EXPERTS) // per_group)[None, :] == group[:, None]
    _, (e0, e1) = top2(jnp.where(in_group, biased, -jnp.inf))
    idx = jnp.stack([e0, e1], axis=-1)
    wts = jnp.take_along_axis(scores, idx, axis=-1)
    wts = wts / wts.sum(-1, keepdims=True)
    flat_e = idx.reshape(-1).astype(jnp.int32)
    n_asg = n * TOP_K
    onehot = (flat_e[:, None] == jnp.arange(N_EXPERTS, dtype=jnp.int32)[None, :]).astype(F32)
    oh = onehot.reshape(n_asg // MOE_BLOCK, MOE_BLOCK, N_EXPERTS)
    tri = jnp.tril(jnp.ones((MOE_BLOCK, MOE_BLOCK), F32), -1)
    within = jnp.einsum('ij,bjk->bik', tri, oh)
    blk_tot = oh.sum(axis=1)
    blk_off = jnp.cumsum(blk_tot, axis=0) - blk_tot
    rank = ((within + blk_off[:, None, :]) * oh).sum(-1).reshape(n_asg).astype(jnp.int32)
    counts = blk_tot.sum(axis=0).astype(jnp.int32)
    padded = (counts + MOE_BLOCK - 1) // MOE_BLOCK * MOE_BLOCK
    pad_end = jnp.cumsum(padded)
    pad_start = pad_end - padded
    dest = (onehot * pad_start.astype(F32)[None, :]).sum(-1).astype(jnp.int32) + rank
    n_blk = (n_asg + N_EXPERTS * (MOE_BLOCK - 1) + MOE_BLOCK - 1) // MOE_BLOCK
    cap = n_blk * MOE_BLOCK
    blk_start = jnp.arange(n_blk, dtype=jnp.int32) * MOE_BLOCK
    blk_e = jnp.minimum((blk_start[:, None] >= pad_end[None, :]).sum(-1), N_EXPERTS - 1).astype(jnp.int32)
    pad_lo = jnp.concatenate([pad_start + counts, pad_end[-1:]]).astype(jnp.int32)
    pad_hi = jnp.concatenate([pad_end, jnp.full((1,), cap)]).astype(jnp.int32)
    n_used = (pad_end[-1:] // MOE_BLOCK).astype(jnp.int32)
    return blk_e, dest.astype(jnp.int32), wts.astype(F32), pad_lo, pad_hi, n_used, cap


DISPATCH_CHUNK = 1024


def _dispatch_kernel(dest_ref, pad_lo_ref, pad_hi_ref, h2_hbm, xs_hbm, zrow, sem, zsem):
    i = pl.program_id(0)

    @pl.when(i == 0)
    def _():
        zrow[...] = jnp.zeros(zrow.shape, F32)
        for e in range(N_EXPERTS + 1):
            lo, hi = pad_lo_ref[e], pad_hi_ref[e]

            def zero_row(r, c):
                pltpu.make_async_copy(zrow.at[pl.ds(0, 1)], xs_hbm.at[pl.ds(r, 1)], zsem).start()
                return c

            def zero_wait(r, c):
                pltpu.make_async_copy(zrow.at[pl.ds(0, 1)], xs_hbm.at[pl.ds(r, 1)], zsem).wait()
                return c

            lax.fori_loop(lo, hi, zero_row, 0)
            lax.fori_loop(lo, hi, zero_wait, 0)

    def copy_row(r, c):
        a = i * DISPATCH_CHUNK + r
        t = lax.shift_right_logical(a, 1)
        pltpu.make_async_copy(h2_hbm.at[pl.ds(t, 1)], xs_hbm.at[pl.ds(dest_ref[a], 1)], sem).start()
        return c

    lax.fori_loop(0, DISPATCH_CHUNK, copy_row, 0, unroll=8)
    pltpu.make_async_copy(h2_hbm.at[pl.ds(0, DISPATCH_CHUNK)], xs_hbm.at[pl.ds(0, DISPATCH_CHUNK)], sem).wait()


def _dispatch(dest, pad_lo, pad_hi, h2, cap):
    n_asg = dest.shape[0]
    d = h2.shape[1]
    assert TOP_K == 2 and n_asg % DISPATCH_CHUNK == 0
    grid_spec = pltpu.PrefetchScalarGridSpec(
        num_scalar_prefetch=3,
        grid=(n_asg // DISPATCH_CHUNK,),
        in_specs=[pl.BlockSpec(memory_space=pl.ANY)],
        out_specs=pl.BlockSpec(memory_space=pl.ANY),
        scratch_shapes=[pltpu.VMEM((8, d), F32), pltpu.SemaphoreType.DMA, pltpu.SemaphoreType.DMA],
    )
    return pl.pallas_call(
        _dispatch_kernel,
        out_shape=jax.ShapeDtypeStruct((cap, d), F32),
        grid_spec=grid_spec,
        compiler_params=_cparams(("arbitrary",)),
        name="moe_dispatch",
    )(dest, pad_lo, pad_hi, h2)


def _grouped_ffn_kernel(blk_e_ref, n_used_ref, x_ref, w1_ref, w3_ref, w2_ref, o_ref):
    i = pl.program_id(0)

    @pl.when(i < n_used_ref[0])
    def _():
        xb = x_ref[...].astype(BF)
        a = jnp.dot(xb, w1_ref[0], preferred_element_type=F32)
        b = jnp.dot(xb, w3_ref[0], preferred_element_type=F32)
        hmid = (a * jax.nn.sigmoid(a) * b).astype(BF)
        o_ref[...] = jnp.dot(hmid, w2_ref[0], preferred_element_type=F32)

    @pl.when(i >= n_used_ref[0])
    def _():
        o_ref[...] = jnp.zeros(o_ref.shape, F32)


def _grouped_ffn(blk_e, n_used, xs, w1, w3, w2):
    cap, d = xs.shape
    wspec = lambda shp: pl.BlockSpec((1,) + shp, lambda i, be, nu: (be[i], 0, 0))
    grid_spec = pltpu.PrefetchScalarGridSpec(
        num_scalar_prefetch=2,
        grid=(cap // MOE_BLOCK,),
        in_specs=[pl.BlockSpec((MOE_BLOCK, d), lambda i, be, nu: (jnp.minimum(i, nu[0] - 1), 0)),
                  wspec((d, D_EXPERT)), wspec((d, D_EXPERT)), wspec((D_EXPERT, d))],
        out_specs=pl.BlockSpec((MOE_BLOCK, d), lambda i, be, nu: (i, 0)),
    )
    return pl.pallas_call(
        _grouped_ffn_kernel,
        out_shape=jax.ShapeDtypeStruct((cap, d), F32),
        grid_spec=grid_spec,
        compiler_params=_cparams(("arbitrary",)),
        name="moe_ffn",
    )(blk_e, n_used, xs, w1, w3, w2)


def _gather_combine_kernel(dest_ref, x_ref, mod_ref, w_ref, ys_hbm, o_ref, buf, sem):
    i = pl.program_id(0)
    n_steps = pl.num_programs(0)
    slot = i % 2
    tm = x_ref.shape[0]

    def start_gather(tile, s):
        def body(r, c):
            a = (tile * tm + r) * TOP_K
            for k in range(TOP_K):
                pltpu.make_async_copy(ys_hbm.at[pl.ds(dest_ref[a + k], 1)], buf.at[s, k, pl.ds(r, 1)],
                                      sem.at[s]).start()
            return c
        lax.fori_loop(0, tm, body, 0, unroll=4)

    @pl.when(i == 0)
    def _():
        start_gather(0, 0)

    @pl.when(i + 1 < n_steps)
    def _():
        start_gather(i + 1, 1 - slot)

    for k in range(TOP_K):
        pltpu.make_async_copy(ys_hbm.at[pl.ds(0, tm)], buf.at[slot, k], sem.at[slot]).wait()
    w = w_ref[...]
    f = w[:, 0:1] * buf[slot, 0] + w[:, 1:2] * buf[slot, 1]
    o_ref[...] = x_ref[...] + mod_ref[0][5:6] * f


def _gather_combine(dest, xn, mod3, wts, ys, n_rows, group_of_tile):
    d = xn.shape[1]
    grid_spec = pltpu.PrefetchScalarGridSpec(
        num_scalar_prefetch=1,
        grid=(n_rows // TM,),
        in_specs=[pl.BlockSpec((TM, d), lambda i, de: (i, 0)),
                  pl.BlockSpec((1, 8, d), lambda i, de: (group_of_tile(i), 0, 0)),
                  pl.BlockSpec((TM, TOP_K), lambda i, de: (i, 0)),
                  pl.BlockSpec(memory_space=pl.ANY)],
        out_specs=pl.BlockSpec((TM, d), lambda i, de: (i, 0)),
        scratch_shapes=[pltpu.VMEM((2, TOP_K, TM, d), F32), pltpu.SemaphoreType.DMA((2,))],
    )
    return pl.pallas_call(
        _gather_combine_kernel,
        out_shape=jax.ShapeDtypeStruct((n_rows, d), F32),
        grid_spec=grid_spec,
        compiler_params=_cparams(("arbitrary",)),
        name="moe_combine",
    )(dest, xn, mod3, wts, ys)


def _combine_kernel(x_ref, mod_ref, f0_ref, f1_ref, o_ref):
    o_ref[...] = x_ref[...] + mod_ref[0][5:6] * (f0_ref[...] + f1_ref[...])


def _combine(xn, mod3, f, n_rows, group_of_tile):
    d = xn.shape[1]
    nt = n_rows // TM
    return pl.pallas_call(
        _combine_kernel,
        out_shape=jax.ShapeDtypeStruct((n_rows, d), F32),
        grid=(nt,),
        in_specs=[pl.BlockSpec((TM, d), lambda i: (i, 0)),
                  pl.BlockSpec((1, 8, d), lambda i: (group_of_tile(i), 0, 0)),
                  pl.BlockSpec((TM, d), lambda i: (i, 0)),
                  pl.BlockSpec((TM, d), lambda i: (nt + i, 0))],
        out_specs=pl.BlockSpec((TM, d), lambda i: (i, 0)),
        compiler_params=_cparams(("parallel",)),
        name="moe_combine",
    )(xn, mod3, f, f)


def _rope_tables(seq, dim, pad):
    t = jnp.arange(seq)
    rows = (t // GRID_W).astype(F32)
    cols = (t % GRID_W).astype(F32)
    quarter = dim // 4
    inv_freq = jnp.exp(-math.log(ROPE_BASE) * jnp.arange(quarter, dtype=F32) / quarter)
    ang = jnp.concatenate([inv_freq[:, None] * rows[None, :], inv_freq[:, None] * cols[None, :]], axis=0)
    cos = jnp.concatenate([jnp.cos(ang), jnp.ones((dim // 2, pad), F32)], axis=1)
    sin = jnp.concatenate([jnp.sin(ang), jnp.zeros((dim // 2, pad), F32)], axis=1)
    return cos, sin


def _win_mask_table(seq):
    nkb = seq // LANES
    nq = seq // WIN_TQ
    tabs = []
    for i in (0, 1, nq - 1):
        t = i * WIN_TQ + np.arange(WIN_TQ)[:, None]
        blk = np.clip(2 * i - 1 + np.arange(4), 0, nkb - 1)
        want = 2 * i - 1 + np.arange(4)
        s = (blk[:, None] * LANES + np.arange(LANES)[None, :]).reshape(-1)[None, :]
        ok = (np.abs(t - s) <= WIN_RADIUS) & np.repeat(blk == want, LANES)[None, :]
        tabs.append(np.where(ok, 0.0, NEG))
    return jnp.asarray(np.stack(tabs), F32)


def _nat_bias_table(rpb, seq):
    rows = seq // GRID_W
    nq = rows // NAT_ROWS_PER_STEP
    wc = NAT_WIN_COLS
    col = np.arange(GRID_W)
    col_start = np.clip(col - wc // 2, 0, GRID_W - wc)
    col_ok = (col[None, :] >= col_start[:, None]) & (col[None, :] < col_start[:, None] + wc)
    d_col = np.clip(col[None, :] - col[:, None] + (wc - 1), 0, 2 * wc - 2)
    tabs = []
    for i in (0, 1, nq - 1):
        r0 = NAT_ROWS_PER_STEP * i
        ws = np.clip(r0 - NAT_WIN_ROWS // 2, 0, rows - NAT_KEY_ROWS)
        r = r0 + np.arange(NAT_ROWS_PER_STEP)
        rs = np.clip(r - NAT_WIN_ROWS // 2, 0, rows - NAT_WIN_ROWS)
        krow = ws + np.arange(NAT_KEY_ROWS)
        row_ok = (krow[None, :] >= rs[:, None]) & (krow[None, :] < rs[:, None] + NAT_WIN_ROWS)
        d_row = np.clip(krow[None, :] - r[:, None] + (NAT_WIN_ROWS - 1), 0, 2 * NAT_WIN_ROWS - 2)
        ok = row_ok[:, None, :, None] & col_ok[None, :, None, :]
        sel_r = jnp.asarray(d_row.reshape(-1)[:, None] == np.arange(2 * NAT_WIN_ROWS - 1)[None, :], F32)
        sel_c = jnp.asarray(d_col.reshape(-1)[:, None] == np.arange(2 * wc - 1)[None, :], F32)
        bias = jnp.einsum('pr,hrc,qc->hpq', sel_r, rpb.astype(F32), sel_c, precision=lax.Precision.HIGHEST)
        bias = bias.reshape(rpb.shape[0], NAT_ROWS_PER_STEP, NAT_KEY_ROWS, GRID_W, GRID_W)
        bias = bias.transpose(0, 1, 3, 2, 4)
        bias = jnp.where(jnp.asarray(ok)[None], bias, NEG)
        tabs.append(bias.reshape(rpb.shape[0], NAT_ROWS_PER_STEP * GRID_W, NAT_KEY_ROWS * GRID_W))
    return jnp.stack(tabs)


def _bcast_rows(v, reps=1):
    return jnp.tile(jnp.broadcast_to(v.astype(F32)[:, None], (v.shape[0], LANES)), (reps, 1))


def _layer_params(l, p):
    w = p['w_in'][l]
    sizes = (512, 128, 128, 512, 512, 512, 512, 512, 512, 256, 160, 4096)
    offs = np.concatenate([[0], np.cumsum(sizes)])
    seg = lambda k: w[:, offs[k]:offs[k + 1]]
    wq, wk, wv, dq, dk, dv, nq, nk, nv, mqa, mkva, gates = [seg(k) for k in range(12)]
    d = w.shape[0]
    dup = lambda m: jnp.concatenate([m[:, :64], m[:, :64], m[:, 64:], m[:, 64:]], axis=1)
    mkva_p = jnp.concatenate([mkva, jnp.zeros((d, 256 - mkva.shape[1]), F32)], axis=1)
    w_a = jnp.concatenate([gates, dv, nv, mqa, mkva_p, dup(wv)], axis=1).astype(BF)
    w_bt = jnp.concatenate([wq, dq, nq, dk, nk, dup(wk)], axis=1).T.astype(BF)
    scale = HEAD_DIM ** -0.5 * LOG2E
    gain_b = jnp.concatenate([
        _bcast_rows(p['win_q_norm'][l] * scale, 8), _bcast_rows(p['dif_q_norm'][l] * scale, 8),
        _bcast_rows(p['nat_q_norm'][l] * scale, 8), _bcast_rows(p['dif_k_norm'][l], 8),
        _bcast_rows(p['nat_k_norm'][l], 8), _bcast_rows(p['win_k_norm'][l], 4)], axis=0)
    wkv = p['mla_wkv_b'][l].reshape(MLA_KV_LORA, MLA_HEADS, MLA_NOPE + MLA_V)
    wk_t = wkv[:, :, :MLA_NOPE].reshape(MLA_KV_LORA, -1).T.astype(BF)
    wv_m = wkv[:, :, MLA_NOPE:].reshape(MLA_KV_LORA, -1).astype(BF)
    def logit_bound(gq, gk, dim):
        return dim * jnp.max(jnp.abs(gq)) * jnp.max(jnp.abs(gk)) * (1.0 + 2.0 ** -7)

    lam_f = p['dif_lambda'][l].astype(F32)
    lam_init = 0.8 - 0.6 * math.exp(-0.3 * l)
    lam = jnp.exp(jnp.sum(lam_f[0] * lam_f[1])) - jnp.exp(jnp.sum(lam_f[2] * lam_f[3])) + lam_init
    return dict(
        w_a=w_a, w_bt=w_bt, gain_b=gain_b,
        g1=p['norm1_g'][l].reshape(1, d), g2=p['norm2_g'][l].reshape(1, d),
        sink=(p['win_sink'][l].astype(F32) * LOG2E).reshape(1, WIN_HEADS),
        gqa=p['mla_q_a_norm'][l].reshape(1, -1), gkva=p['mla_kv_a_norm'][l].reshape(1, -1),
        wq_t=p['mla_wq_b'][l].T.astype(BF), wk_t=wk_t, wv_m=wv_m,
        gq=_bcast_rows(p['mla_q_norm'][l] * (MLA_QK ** -0.5 * LOG2E)), gk=_bcast_rows(p['mla_k_norm'][l]),
        lam=lam.reshape(1, 1).astype(F32), lam_scale=1.0 - lam_init,
        m_win=logit_bound(p['win_q_norm'][l] * scale, p['win_k_norm'][l], HEAD_DIM),
        m_dif=logit_bound(p['dif_q_norm'][l] * scale, p['dif_k_norm'][l], HEAD_DIM),
        m_nat=logit_bound(p['nat_q_norm'][l] * scale, p['nat_k_norm'][l], HEAD_DIM),
        m_mla=logit_bound(p['mla_q_norm'][l] * (MLA_QK ** -0.5 * LOG2E), p['mla_k_norm'][l], MLA_QK),
        subln=p['dif_subln'][l].astype(F32).reshape(1, DIF_V_DIM),
        wb=p['w_branch'][l].astype(BF), wo=p['w_out'][l].astype(BF),
        w1=p['moe_w1'][l].astype(BF), w3=p['moe_w3'][l].astype(BF), w2=p['moe_w2'][l].astype(BF),
    )


def kernel(x, c, ctx, c_ctx, ada_w, ada_b, norm1_g, norm2_g, w_in, win_q_norm, win_k_norm, win_sink,
           dif_q_norm, dif_k_norm, dif_lambda, dif_subln, nat_q_norm, nat_k_norm, nat_rpb,
           mla_q_a_norm, mla_wq_b, mla_kv_a_norm, mla_wkv_b, mla_q_norm, mla_k_norm,
           w_branch, w_out, router_w, router_b, moe_w1, moe_w3, moe_w2):
    p = dict(norm1_g=norm1_g, norm2_g=norm2_g, w_in=w_in, win_q_norm=win_q_norm, win_k_norm=win_k_norm,
             win_sink=win_sink, dif_q_norm=dif_q_norm, dif_k_norm=dif_k_norm, dif_lambda=dif_lambda,
             dif_subln=dif_subln, nat_q_norm=nat_q_norm, nat_k_norm=nat_k_norm,
             mla_q_a_norm=mla_q_a_norm, mla_wq_b=mla_wq_b, mla_kv_a_norm=mla_kv_a_norm,
             mla_wkv_b=mla_wkv_b, mla_q_norm=mla_q_norm, mla_k_norm=mla_k_norm,
             w_branch=w_branch, w_out=w_out, moe_w1=moe_w1, moe_w3=moe_w3, moe_w2=moe_w2)
    n_batch, seq, d = x.shape
    n_ctx = ctx.shape[1]
    depth = ada_w.shape[0]
    n_lat = n_batch * seq
    n_all = n_lat + n_batch * n_ctx
    assert seq % DENSE_TK == 0 and seq % DENSE_TQ == 0 and (n_batch * n_ctx) == TM and seq % TM == 0
    tiles_per_batch = seq // TM
    group_of_tile = lambda i: jnp.minimum(i // tiles_per_batch, n_batch)
    pos_of_tile = lambda i: jnp.where(i < n_batch * tiles_per_batch, i % tiles_per_batch, tiles_per_batch)

    cc = jnp.concatenate([c, c_ctx[None, :], jnp.zeros((8 - n_batch - 1, d), F32)], axis=0)
    mod = _modulation(cc, ada_w, ada_b)
    mod = mod[:, :n_batch + 1].reshape(depth, n_batch + 1, 6, d)
    mod = jnp.pad(mod, ((0, 0), (0, 0), (0, 2), (0, 0)))

    cos_h, sin_h = _rope_tables(seq, HEAD_DIM, TM)
    cos_m, sin_m = _rope_tables(seq, MLA_ROPE, TM)
    win_mask = _win_mask_table(seq)
    rw = jnp.pad(router_w.astype(F32), ((0, 0), (0, LANES - N_EXPERTS)))

    xa = jnp.concatenate([x.reshape(n_lat, d), ctx.reshape(n_batch * n_ctx, d)], axis=0)
    ones = jnp.ones((DENSE_TK, LANES), BF)

    def dense(lam, logit_bound, q_arr, q_col, q_w, k_arr, k_row, k_w, v_arr, v_col, v_w, subln, **kw):
        operands = (q_arr, q_col, q_w, k_arr, k_row, k_w, v_arr, v_col, v_w, subln)
        par = jnp.concatenate([lam, logit_bound.reshape(1, 1).astype(F32)], axis=1)
        y, flag = _dense_fixed_attention(par, *operands, ones, n_batch, seq, n_ctx, **kw)
        return lax.cond(jnp.max(flag) > 0.0,
                        lambda: _dense_attention(lam, *operands, n_batch, seq, n_ctx, **kw),
                        lambda: y)
    for l in range(depth):
        lp = _layer_params(l, p)
        want_ctx = l < depth - 1
        mod3 = mod[l]
        tok_a = _in_proj_a(xa, mod3, lp['g1'], lp['w_a'], group_of_tile)
        q_tok, kt = _in_proj_b(xa, mod3, lp['g1'], lp['w_bt'], lp['gain_b'], cos_h, sin_h,
                               group_of_tile, pos_of_tile)
        mq, mkt, mv = _mla_prep(tok_a, lp['gqa'], lp['gkva'], lp['wq_t'], lp['wk_t'], lp['wv_m'],
                                lp['gq'], lp['gk'], cos_m, sin_m, pos_of_tile)
        nat_bias = _nat_bias_table(nat_rpb[l].astype(F32) * LOG2E, seq)

        def branches(latent):
            win_args = (q_tok, kt, tok_a)
            if latent:
                m_win = lp['m_win']
                sink_rel = jnp.concatenate([lp['sink'] - m_win, jnp.full((1, 8), m_win, F32)], axis=1)
                y_win, flag = _win_attention(sink_rel, *win_args, win_mask - m_win, n_batch, seq, n_ctx, True,
                                             ones=ones)
                y_win = lax.cond(jnp.max(flag) > 0.0,
                                 lambda: _win_attention(lp['sink'], *win_args, win_mask, n_batch, seq, n_ctx, True),
                                 lambda: y_win)
            else:
                y_win = _win_attention(lp['sink'], *win_args, win_mask, n_batch, seq, n_ctx, False)
            y_dif = dense(lp['lam'], lp['m_dif'], q_tok, 1, 512, kt, KT_DK // 512, 512, tok_a, A_DV // 512, 512,
                          lp['subln'], n_heads=2 * DIF_HEADS, packed=True, diff=True, latent=latent,
                          lam_scale=lp['lam_scale'], name="dif_attn" if latent else "dif_attn_ctx")
            if latent:
                m_nl = lp['m_nat'] + jnp.maximum(jnp.max(nat_rpb[l].astype(F32)) * LOG2E, 0.0)
                y_nat, flag = _nat_attention(q_tok, kt, tok_a, nat_bias - m_nl, n_batch, seq, n_ctx,
                                             par=m_nl.reshape(1, 1).astype(F32), ones=ones)
                y_nat = lax.cond(jnp.max(flag) > 0.0,
                                 lambda: _nat_attention(q_tok, kt, tok_a, nat_bias, n_batch, seq, n_ctx),
                                 lambda: y_nat)
            else:
                y_nat = dense(lp['lam'], lp['m_nat'], q_tok, 2, 512, kt, KT_NK // 512, 512, tok_a, A_NV // 512, 512,
                              None, n_heads=NAT_HEADS, packed=True, diff=False, latent=False,
                              lam_scale=1.0, name="nat_attn_ctx")
            y_mla = dense(lp['lam'], lp['m_mla'], mq, 0, 1024, mkt, 0, 1024, mv, 0, 512, None,
                          n_heads=MLA_HEADS, packed=False, diff=False, latent=latent, lam_scale=1.0,
                          name="mla_attn" if latent else "mla_attn_ctx")
            return [y_win, y_dif, y_nat, y_mla]

        ys = branches(True)
        ys_c = branches(False) if want_ctx else None
        n_rows = n_all if want_ctx else n_lat
        xn, h2, scores = _merge(xa, mod3, lp['g2'], ys, ys_c, tok_a, lp['wb'], lp['wo'], rw, n_rows,
                                group_of_tile)
        blk_e, dest, wts, pad_lo, pad_hi, n_used, cap = _route(scores[:, :N_EXPERTS], router_b, n_rows)
        xs = _dispatch(dest, pad_lo, pad_hi, h2, cap)
        ys = _grouped_ffn(blk_e, n_used, xs, lp['w1'], lp['w3'], lp['w2'])
        xa = _gather_combine(dest, xn, mod3, wts, ys, n_rows, group_of_tile)
    return xa[:n_lat].reshape(n_batch, seq, d)
```

```python
import functools
import math

import jax
import jax.numpy as jnp
import numpy as np
from jax import lax
from jax.experimental import pallas as pl
from jax.experimental.pallas import tpu as pltpu

F32 = jnp.float32
BF = jnp.bfloat16

GRID_W = 64
HEAD_DIM = 64
N_BRANCH = 4
BRANCH_W = 512
ROPE_BASE = 10000.0
EPS = 1e-6
NEG = -1e30
LOG2E = math.log2(math.e)
WIN_HEADS, WIN_KV_HEADS, WIN_RADIUS = 8, 2, 128
DIF_HEADS, DIF_QK_DIM, DIF_V_DIM = 4, 64, 128
NAT_HEADS, NAT_WIN_ROWS, NAT_WIN_COLS = 8, 8, 16
MLA_HEADS, MLA_NOPE, MLA_ROPE, MLA_V, MLA_Q_LORA, MLA_KV_LORA = 8, 64, 32, 64, 256, 128
MLA_QK = MLA_NOPE + MLA_ROPE
N_EXPERTS, N_GROUPS, TOP_K, D_EXPERT, MOE_BLOCK = 16, 4, 2, 512, 256

LANES = 128
TM = 512
WIN_TQ = 256
NAT_ROWS_PER_STEP = 4
NAT_KEY_ROWS = 12
DENSE_TQ = 1024
DENSE_TQ_ONLINE = 512
DENSE_TK = 1024
DENSE_RB = 64
DENSE_KC = 256
VMEM_LIMIT = 48 * 1024 * 1024

A_GATES, A_DV, A_NV, A_MQA, A_MKVA, A_WV = 0, 4096, 4608, 5120, 5376, 5632
A_COLS = 5888
Q_COLS = 1536
KT_DK, KT_NK, KT_WK = 0, 512, 1024
KT_ROWS = 1280
B_ROWS = Q_COLS + KT_ROWS


def _cparams(sem, vmem=VMEM_LIMIT):
    return pltpu.CompilerParams(dimension_semantics=sem, vmem_limit_bytes=vmem)


def _nt_dot(a, b):
    return lax.dot_general(a, b, (((1,), (1,)), ((), ())), preferred_element_type=F32)


def _norm_mod(x, g, sc, sh):
    ms = jnp.mean(x * x, axis=-1, keepdims=True)
    return (x * lax.rsqrt(ms + EPS) * g) * (1.0 + sc) + sh


def _lane_tile(a, n):
    reps = n // a.shape[1]
    return a if reps == 1 else jnp.concatenate([a] * reps, axis=1)


def _mod_kernel(c_ref, w_ref, b_ref, o_ref):
    cc = c_ref[...]
    a = cc * jax.nn.sigmoid(cc)
    o_ref[0] = jnp.dot(a, w_ref[0], preferred_element_type=F32,
                       precision=lax.Precision.HIGHEST) + b_ref[0]


def _modulation(cc, ada_w, ada_b):
    n_layers, d, d6 = ada_w.shape
    tn = 1536
    return pl.pallas_call(
        _mod_kernel,
        out_shape=jax.ShapeDtypeStruct((n_layers, 8, d6), F32),
        grid=(n_layers, d6 // tn),
        in_specs=[pl.BlockSpec((8, d), lambda l, j: (0, 0)),
                  pl.BlockSpec((1, d, tn), lambda l, j: (l, 0, j)),
                  pl.BlockSpec((1, 1, tn), lambda l, j: (l, 0, j))],
        out_specs=pl.BlockSpec((1, 8, tn), lambda l, j: (l, 0, j)),
        compiler_params=_cparams(("parallel", "parallel")),
        name="adaln_mod",
    )(cc, ada_w, ada_b.reshape(n_layers, 1, d6))


def _in_a_kernel(x_ref, mod_ref, g_ref, w_ref, o_ref):
    m = mod_ref[0]
    h = _norm_mod(x_ref[...], g_ref[...], m[1:2], m[0:1]).astype(BF)
    o_ref[...] = jnp.dot(h, w_ref[...], preferred_element_type=F32).astype(BF)


def _in_proj_a(xa, mod3, g1, w_a, group_of_tile):
    n, d = xa.shape
    tn = A_COLS // 2
    return pl.pallas_call(
        _in_a_kernel,
        out_shape=jax.ShapeDtypeStruct((n, A_COLS), BF),
        grid=(A_COLS // tn, n // TM),
        in_specs=[pl.BlockSpec((TM, d), lambda j, i: (i, 0)),
                  pl.BlockSpec((1, 8, d), lambda j, i: (group_of_tile(i), 0, 0)),
                  pl.BlockSpec((1, d), lambda j, i: (0, 0)),
                  pl.BlockSpec((d, tn), lambda j, i: (0, j))],
        out_specs=pl.BlockSpec((TM, tn), lambda j, i: (i, j)),
        compiler_params=_cparams(("parallel", "parallel")),
        name="in_proj_tok",
    )(xa, mod3, g1, w_a)


def _head_norm_rope(x, g, cos, sin, rope):
    ss = jnp.sum(x * x, axis=0, keepdims=True)
    y = x * lax.rsqrt(ss * (1.0 / HEAD_DIM) + EPS) * g
    if not rope:
        return y
    half = HEAD_DIM // 2
    y1, y2 = y[:half], y[half:]
    return jnp.concatenate([y1 * cos - y2 * sin, y1 * sin + y2 * cos], axis=0)


def _in_b_kernel(x_ref, mod_ref, g_ref, wt_ref, gain_ref, cos_ref, sin_ref, q_ref, kt_ref, acc_sc):
    m = mod_ref[0]
    h = _norm_mod(x_ref[...], g_ref[...], m[1:2], m[0:1]).astype(BF)
    acc_sc[...] = _nt_dot(wt_ref[...], h)
    tm = h.shape[0]
    cos = cos_ref[...]
    sin = sin_ref[...]

    def pair(r0, rope):
        hs = []
        for e in range(2):
            r = r0 + e * HEAD_DIM
            g = _lane_tile(gain_ref[r:r + HEAD_DIM, :], tm)
            hs.append(_head_norm_rope(acc_sc[r:r + HEAD_DIM, :], g, cos, sin, rope))
        return jnp.concatenate(hs, axis=0)

    for p in range(Q_COLS // LANES):
        y = pair(p * LANES, rope=p < 8)
        q_ref[:, p * LANES:(p + 1) * LANES] = y.T.astype(BF)
    for p in range(KT_ROWS // LANES):
        y = pair(Q_COLS + p * LANES, rope=not (4 <= p < 8))
        kt_ref[p * LANES:(p + 1) * LANES, :] = y.astype(BF)


def _in_proj_b(xa, mod3, g1, w_bt, gain_b, cos_t, sin_t, group_of_tile, pos_of_tile):
    n, d = xa.shape
    return pl.pallas_call(
        _in_b_kernel,
        out_shape=(jax.ShapeDtypeStruct((n, Q_COLS), BF),
                   jax.ShapeDtypeStruct((KT_ROWS, n), BF)),
        grid=(n // TM,),
        in_specs=[pl.BlockSpec((TM, d), lambda i: (i, 0)),
                  pl.BlockSpec((1, 8, d), lambda i: (group_of_tile(i), 0, 0)),
                  pl.BlockSpec((1, d), lambda i: (0, 0)),
                  pl.BlockSpec((B_ROWS, d), lambda i: (0, 0)),
                  pl.BlockSpec((B_ROWS, LANES), lambda i: (0, 0)),
                  pl.BlockSpec((HEAD_DIM // 2, TM), lambda i: (0, pos_of_tile(i))),
                  pl.BlockSpec((HEAD_DIM // 2, TM), lambda i: (0, pos_of_tile(i)))],
        out_specs=(pl.BlockSpec((TM, Q_COLS), lambda i: (i, 0)),
                   pl.BlockSpec((KT_ROWS, TM), lambda i: (0, i))),
        scratch_shapes=[pltpu.VMEM((B_ROWS, TM), F32)],
        compiler_params=_cparams(("parallel",)),
        name="in_proj_heads",
    )(xa, mod3, g1, w_bt, gain_b, cos_t, sin_t)


def _mla_kernel(qa_ref, kva_ref, gqa_ref, gkva_ref, wqt_ref, wkt_ref, wv_ref, gq_ref, gk_ref,
                cos_ref, sin_ref, mq_ref, mkt_ref, mv_ref):
    tm = qa_ref.shape[0]
    cos = cos_ref[...]
    sin = sin_ref[...]
    rh = MLA_ROPE // 2

    def rms_rows(x, g):
        ms = jnp.mean(x * x, axis=-1, keepdims=True)
        return x * lax.rsqrt(ms + EPS) * g

    def rope_rows(x):
        x1, x2 = x[:rh], x[rh:]
        return jnp.concatenate([x1 * cos - x2 * sin, x1 * sin + x2 * cos], axis=0)

    qa = rms_rows(qa_ref[...].astype(F32), gqa_ref[...]).astype(BF)
    qt = _nt_dot(wqt_ref[...], qa)
    kva = kva_ref[...].astype(F32)
    cn = rms_rows(kva[:, :MLA_KV_LORA], gkva_ref[...]).astype(BF)
    knt = _nt_dot(wkt_ref[...], cn)
    mv_ref[...] = jnp.dot(cn, wv_ref[...], preferred_element_type=F32).astype(BF)
    krope = kva[:, MLA_KV_LORA:].T[:MLA_ROPE]
    kr_ss = jnp.sum(krope * krope, axis=0, keepdims=True)
    gq = _lane_tile(gq_ref[...], tm)
    gk = _lane_tile(gk_ref[...], tm)
    zpad = jnp.zeros((LANES - MLA_QK, tm), F32)
    for hd in range(MLA_HEADS):
        x = qt[hd * MLA_QK:(hd + 1) * MLA_QK]
        ss = jnp.sum(x * x, axis=0, keepdims=True)
        y = x * lax.rsqrt(ss * (1.0 / MLA_QK) + EPS) * gq
        y = jnp.concatenate([y[:MLA_NOPE], rope_rows(y[MLA_NOPE:]), zpad], axis=0)
        mq_ref[:, hd * LANES:(hd + 1) * LANES] = y.T.astype(BF)
        kn = knt[hd * MLA_NOPE:(hd + 1) * MLA_NOPE]
        ss = jnp.sum(kn * kn, axis=0, keepdims=True) + kr_ss
        r = lax.rsqrt(ss * (1.0 / MLA_QK) + EPS)
        yk = jnp.concatenate([kn * r * gk[:MLA_NOPE], rope_rows(krope * r * gk[MLA_NOPE:]), zpad], axis=0)
        mkt_ref[hd * LANES:(hd + 1) * LANES, :] = yk.astype(BF)


def _mla_prep(tok_a, gqa, gkva, wq_t, wk_t, wv, gq, gk, cos_t, sin_t, pos_of_tile):
    n = tok_a.shape[0]
    hw = MLA_HEADS * LANES
    return pl.pallas_call(
        _mla_kernel,
        out_shape=(jax.ShapeDtypeStruct((n, hw), BF),
                   jax.ShapeDtypeStruct((hw, n), BF),
                   jax.ShapeDtypeStruct((n, MLA_HEADS * MLA_V), BF)),
        grid=(n // TM,),
        in_specs=[pl.BlockSpec((TM, 256), lambda i: (i, A_MQA // 256)),
                  pl.BlockSpec((TM, 256), lambda i: (i, A_MKVA // 256)),
                  pl.BlockSpec((1, MLA_Q_LORA), lambda i: (0, 0)),
                  pl.BlockSpec((1, MLA_KV_LORA), lambda i: (0, 0)),
                  pl.BlockSpec(wq_t.shape, lambda i: (0, 0)),
                  pl.BlockSpec(wk_t.shape, lambda i: (0, 0)),
                  pl.BlockSpec(wv.shape, lambda i: (0, 0)),
                  pl.BlockSpec((MLA_QK, LANES), lambda i: (0, 0)),
                  pl.BlockSpec((MLA_QK, LANES), lambda i: (0, 0)),
                  pl.BlockSpec((MLA_ROPE // 2, TM), lambda i: (0, pos_of_tile(i))),
                  pl.BlockSpec((MLA_ROPE // 2, TM), lambda i: (0, pos_of_tile(i)))],
        out_specs=(pl.BlockSpec((TM, hw), lambda i: (i, 0)),
                   pl.BlockSpec((hw, TM), lambda i: (0, i)),
                   pl.BlockSpec((TM, MLA_HEADS * MLA_V), lambda i: (i, 0))),
        compiler_params=_cparams(("parallel",)),
        name="mla_prep",
    )(tok_a, tok_a, gqa, gkva, wq_t, wk_t, wv, gq, gk, cos_t, sin_t)


def _half_mask(shape):
    return lax.broadcasted_iota(jnp.int32, shape, 1) < (LANES // 2)


def _select_half(q, e, lo_mask):
    zero = jnp.zeros_like(q)
    return jnp.where(lo_mask, q, zero) if e == 0 else jnp.where(lo_mask, zero, q)


def _local_softmax_out(parts, extra_logit):
    m = parts[0][0].max(axis=-1, keepdims=True)
    for s, _ in parts[1:]:
        m = jnp.maximum(m, s.max(axis=-1, keepdims=True))
    if extra_logit is not None:
        m = jnp.maximum(m, extra_logit)
    z = None
    o = None
    for s, v in parts:
        p = jnp.exp2(s - m)
        zs = p.sum(axis=-1, keepdims=True)
        os_ = jnp.dot(p.astype(BF), v, preferred_element_type=F32)
        z = zs if z is None else z + zs
        o = os_ if o is None else o + os_
    if extra_logit is not None:
        z = z + jnp.exp2(extra_logit - m)
    return o / z


def _fixed_softmax_out(parts, ones_ref, extra_logit):
    acc = None
    for s, v in parts:
        aug = jnp.concatenate([v, ones_ref[:s.shape[1], :]], axis=1)
        t = jnp.dot(jnp.exp2(s).astype(BF), aug, preferred_element_type=F32)
        acc = t if acc is None else acc + t
    l = acc[:, LANES:]
    if extra_logit is not None:
        l = l + jnp.exp2(jnp.zeros_like(l) + extra_logit)
    unsafe = jnp.logical_not((l > 2.0 ** -SAFE_SUM_LOG2) & (l < 2.0 ** SAFE_SUM_LOG2))
    return acc[:, :LANES] / l, jnp.max(jnp.where(unsafe, 1.0, 0.0), axis=0, keepdims=True)


def _win_kernel(sink_ref, q_ref, *refs, band, fixed=False):
    if band:
        k0, k1, k2, k3, v0, v1, v2, v3, kc_ref, vc_ref, mask_ref = refs[:11]
        refs = refs[11:]
        kb = jnp.concatenate([k0[...], k1[...], k2[...], k3[...]], axis=1)
        vb = jnp.concatenate([v0[...], v1[...], v2[...], v3[...]], axis=0)
        mask = mask_ref[0]
    else:
        kc_ref, vc_ref = refs[:2]
        refs = refs[2:]
    if fixed:
        ones_ref, o_ref, flag_ref = refs
        ref_logit = sink_ref[0, WIN_HEADS]
    else:
        (o_ref,) = refs
    q = q_ref[...]
    lo = _half_mask((q.shape[0], LANES))
    group = WIN_HEADS // WIN_KV_HEADS
    bad = []
    for j in range(WIN_HEADS // 2):
        qp = q[:, j * LANES:(j + 1) * LANES]
        g = (2 * j) // group
        kc = kc_ref[g * LANES:(g + 1) * LANES, :]
        vc = vc_ref[:, g * LANES:(g + 1) * LANES]
        outs = []
        for e in range(2):
            qm = _select_half(qp, e, lo)
            parts = []
            if band:
                s = jnp.dot(qm, kb[g * LANES:(g + 1) * LANES, :], preferred_element_type=F32) + mask
                parts.append((s, vb[:, g * LANES:(g + 1) * LANES]))
            sc = jnp.dot(qm, kc, preferred_element_type=F32)
            if fixed:
                parts.append((sc - ref_logit, vc))
                o, u = _fixed_softmax_out(parts, ones_ref, sink_ref[0, 2 * j + e])
                bad.append(u)
                outs.append(o)
            else:
                parts.append((sc, vc))
                outs.append(_local_softmax_out(parts, sink_ref[0, 2 * j + e]))
        o_ref[:, j * LANES:(j + 1) * LANES] = jnp.where(lo, outs[0], outs[1]).astype(BF)
    if fixed:
        flag_ref[0] = jnp.concatenate(bad, axis=0)


def _win_attention(sink, q_tok, kt, tok_a, mask_tbl, n_batch, seq, n_ctx, latent, ones=None):
    ctx_blk = (n_batch * seq) // n_ctx
    kc_spec = lambda f: pl.BlockSpec((2 * LANES, n_ctx), f)
    vc_spec = lambda f: pl.BlockSpec((n_ctx, 2 * LANES), f)
    smem = pl.BlockSpec(memory_space=pltpu.SMEM)
    n_out = n_batch * (seq if latent else n_ctx)
    out_shape = jax.ShapeDtypeStruct((n_out, WIN_HEADS * HEAD_DIM), BF)
    if not latent:
        return pl.pallas_call(
            functools.partial(_win_kernel, band=False),
            out_shape=out_shape,
            grid=(n_batch,),
            in_specs=[smem,
                      pl.BlockSpec((n_ctx, 512), lambda b: (ctx_blk + b, 0)),
                      kc_spec(lambda b: (KT_WK // 256, ctx_blk + b)),
                      vc_spec(lambda b: (ctx_blk + b, A_WV // 256))],
            out_specs=pl.BlockSpec((n_ctx, 512), lambda b: (b, 0)),
            compiler_params=_cparams(("parallel",)),
            name="win_attn_ctx",
        )(sink, q_tok, kt, tok_a)
    nq = seq // WIN_TQ
    nkb = seq // LANES

    def kidx(j):
        return lambda b, i: (KT_WK // 256, b * nkb + jnp.clip(2 * i - 1 + j, 0, nkb - 1))

    def vidx(j):
        return lambda b, i: (b * nkb + jnp.clip(2 * i - 1 + j, 0, nkb - 1), A_WV // 256)

    def variant(b, i):
        return (jnp.where(i == 0, 0, jnp.where(i == nq - 1, 2, 1)), 0, 0)

    in_specs = ([smem, pl.BlockSpec((WIN_TQ, 512), lambda b, i: (b * nq + i, 0))]
                + [pl.BlockSpec((2 * LANES, LANES), kidx(j)) for j in range(4)]
                + [pl.BlockSpec((LANES, 2 * LANES), vidx(j)) for j in range(4)]
                + [kc_spec(lambda b, i: (KT_WK // 256, ctx_blk + b)),
                   vc_spec(lambda b, i: (ctx_blk + b, A_WV // 256)),
                   pl.BlockSpec((1, WIN_TQ, 4 * LANES), variant)])
    args = [sink, q_tok, kt, kt, kt, kt, tok_a, tok_a, tok_a, tok_a, kt, tok_a, mask_tbl]
    out_specs = pl.BlockSpec((WIN_TQ, 512), lambda b, i: (b * nq + i, 0))
    fixed = ones is not None
    if fixed:
        in_specs.append(pl.BlockSpec(ones.shape, lambda b, i: (0, 0)))
        args.append(ones)
        out_shape = (out_shape, jax.ShapeDtypeStruct((n_batch * nq, WIN_HEADS, LANES), F32))
        out_specs = (out_specs, pl.BlockSpec((1, WIN_HEADS, LANES), lambda b, i: (b * nq + i, 0, 0)))
    return pl.pallas_call(
        functools.partial(_win_kernel, band=True, fixed=fixed),
        out_shape=out_shape,
        grid=(n_batch, nq),
        in_specs=in_specs,
        out_specs=out_specs,
        compiler_params=_cparams(("parallel", "parallel")),
        name="win_attn_fixed" if fixed else "win_attn",
    )(*args)


def _nat_kernel(q_ref, k0, k1, k2, v0, v1, v2, kc_ref, vc_ref, bias_ref, *refs, fixed=False):
    if fixed:
        par_ref, ones_ref, o_ref, flag_ref = refs
        ref_logit = par_ref[0, 0]
    else:
        (o_ref,) = refs
    q = q_ref[...]
    kb = jnp.concatenate([k0[...], k1[...], k2[...]], axis=1)
    vb = jnp.concatenate([v0[...], v1[...], v2[...]], axis=0)
    lo = _half_mask((q.shape[0], LANES))
    bad = []
    for j in range(NAT_HEADS // 2):
        sl = slice(j * LANES, (j + 1) * LANES)
        qp = q[:, sl]
        outs = []
        for e in range(2):
            qm = _select_half(qp, e, lo)
            s = jnp.dot(qm, kb[sl, :], preferred_element_type=F32) + bias_ref[0, 2 * j + e]
            sc = jnp.dot(qm, kc_ref[sl, :], preferred_element_type=F32)
            if fixed:
                o, u = _fixed_softmax_out([(s, vb[:, sl]), (sc - ref_logit, vc_ref[:, sl])], ones_ref, None)
                bad.append(u)
                outs.append(o)
            else:
                outs.append(_local_softmax_out([(s, vb[:, sl]), (sc, vc_ref[:, sl])], None))
        o_ref[:, sl] = jnp.where(lo, outs[0], outs[1]).astype(BF)
    if fixed:
        flag_ref[0] = jnp.concatenate(bad, axis=0)


def _nat_attention(q_tok, kt, tok_a, bias_tbl, n_batch, seq, n_ctx, par=None, ones=None):
    tq = NAT_ROWS_PER_STEP * GRID_W
    nq = seq // tq
    rows = seq // GRID_W
    ctx_blk = (n_batch * seq) // n_ctx
    q_col = 2
    k_row = KT_NK // 512
    v_col = A_NV // 512

    def wstart(i):
        return jnp.clip(NAT_ROWS_PER_STEP * i - NAT_WIN_ROWS // 2, 0, rows - NAT_KEY_ROWS) // NAT_ROWS_PER_STEP

    def kidx(j):
        return lambda b, i: (k_row, b * nq + wstart(i) + j)

    def vidx(j):
        return lambda b, i: (b * nq + wstart(i) + j, v_col)

    def variant(b, i):
        return (jnp.where(i == 0, 0, jnp.where(i == nq - 1, 2, 1)), 0, 0, 0)

    nk = NAT_KEY_ROWS * GRID_W
    in_specs = ([pl.BlockSpec((tq, 512), lambda b, i: (b * nq + i, q_col))]
                + [pl.BlockSpec((512, tq), kidx(j)) for j in range(3)]
                + [pl.BlockSpec((tq, 512), vidx(j)) for j in range(3)]
                + [pl.BlockSpec((512, n_ctx), lambda b, i: (k_row, ctx_blk + b)),
                   pl.BlockSpec((n_ctx, 512), lambda b, i: (ctx_blk + b, v_col)),
                   pl.BlockSpec((1, NAT_HEADS, tq, nk), variant)])
    args = [q_tok, kt, kt, kt, tok_a, tok_a, tok_a, kt, tok_a, bias_tbl]
    out_shape = jax.ShapeDtypeStruct((n_batch * seq, NAT_HEADS * HEAD_DIM), BF)
    out_specs = pl.BlockSpec((tq, 512), lambda b, i: (b * nq + i, 0))
    fixed = ones is not None
    if fixed:
        in_specs += [pl.BlockSpec(memory_space=pltpu.SMEM), pl.BlockSpec(ones.shape, lambda b, i: (0, 0))]
        args += [par, ones]
        out_shape = (out_shape, jax.ShapeDtypeStruct((n_batch * nq, NAT_HEADS, LANES), F32))
        out_specs = (out_specs, pl.BlockSpec((1, NAT_HEADS, LANES), lambda b, i: (b * nq + i, 0, 0)))
    return pl.pallas_call(
        functools.partial(_nat_kernel, fixed=fixed),
        out_shape=out_shape,
        grid=(n_batch, nq),
        in_specs=in_specs,
        out_specs=out_specs,
        compiler_params=_cparams(("parallel", "parallel")),
        name="nat_attn_fixed" if fixed else "nat_attn",
    )(*args)


def _dense_kernel(lam_ref, q_ref, kc_ref, vc_ref, *refs, n_heads, packed, diff, latent, lam_scale):
    if latent:
        k_ref, v_ref = refs[0], refs[1]
        refs = refs[2:]
    if diff:
        subln_ref, o_ref, qm_sc, m_sc, l_sc, acc_sc, s_sc, p_sc = refs
    else:
        o_ref, qm_sc, m_sc, l_sc, acc_sc, s_sc, p_sc = refs
    kt_step = pl.program_id(2) if latent else 0
    tq = q_ref.shape[0]

    def kv_slices(h):
        blk = h // 2 if packed else h
        ks = slice(blk * LANES, (blk + 1) * LANES)
        vs = ks if packed else slice((h // 2) * LANES, (h // 2 + 1) * LANES)
        return ks, vs

    def step(h, k_ref_, v_ref_):
        ks, vs = kv_slices(h)
        nk = k_ref_.shape[1]
        slot = h % 2
        s_sc[slot, :, :nk] = jnp.dot(qm_sc[h], k_ref_[ks, :], preferred_element_type=F32)
        for r in range(tq // DENSE_RB):
            rows = slice(r * DENSE_RB, (r + 1) * DENSE_RB)
            mx = s_sc[slot, rows, 0:LANES]
            for c in range(1, nk // LANES):
                mx = jnp.maximum(mx, s_sc[slot, rows, c * LANES:(c + 1) * LANES])
            m_old = m_sc[h, rows, :]
            m_new = jnp.maximum(m_old, jnp.max(mx, axis=-1, keepdims=True))
            alpha = jnp.exp2(m_old - m_new)
            lsum = None
            for c in range(nk // LANES):
                cols = slice(c * LANES, (c + 1) * LANES)
                p = jnp.exp2(s_sc[slot, rows, cols] - m_new)
                lsum = p if lsum is None else lsum + p
                p_sc[slot, rows, cols] = p.astype(BF)
            m_sc[h, rows, :] = m_new
            l_sc[h, rows, :] = alpha * l_sc[h, rows, :] + jnp.sum(lsum, axis=-1, keepdims=True)
            acc_sc[h, rows, :] = alpha * acc_sc[h, rows, :]
        acc_sc[h] += jnp.dot(p_sc[slot, :, :nk], v_ref_[:, vs], preferred_element_type=F32)

    @pl.when(kt_step == 0)
    def _():
        q = q_ref[...]
        lo = _half_mask((tq, LANES))
        for h in range(n_heads):
            if packed:
                qp = q[:, (h // 2) * LANES:(h // 2 + 1) * LANES]
                qm_sc[h] = _select_half(qp, h % 2, lo)
            else:
                qm_sc[h] = q[:, h * LANES:(h + 1) * LANES]
        m_sc[...] = jnp.full(m_sc.shape, NEG, F32)
        l_sc[...] = jnp.zeros(l_sc.shape, F32)
        acc_sc[...] = jnp.zeros(acc_sc.shape, F32)
        for h in range(n_heads):
            step(h, kc_ref, vc_ref)

    if latent:
        for h in range(n_heads):
            step(h, k_ref, v_ref)
        last = kt_step == pl.num_programs(2) - 1
    else:
        last = True

    def finish():
        lo = _half_mask((tq, LANES))
        if diff:
            lam = lam_ref[0, 0]
            for hv in range(n_heads // 2):
                y = (acc_sc[2 * hv] / l_sc[2 * hv]
                     - lam * (acc_sc[2 * hv + 1] / l_sc[2 * hv + 1]))
                ms = jnp.mean(y * y, axis=-1, keepdims=True)
                y = y * lax.rsqrt(ms + EPS) * subln_ref[...] * lam_scale
                o_ref[:, hv * LANES:(hv + 1) * LANES] = y.astype(BF)
        else:
            for hp in range(n_heads // 2):
                o0 = acc_sc[2 * hp] / l_sc[2 * hp]
                o1 = acc_sc[2 * hp + 1] / l_sc[2 * hp + 1]
                o_ref[:, hp * LANES:(hp + 1) * LANES] = jnp.where(lo, o0, o1).astype(BF)

    if latent:
        pl.when(last)(finish)
    else:
        finish()


def _dense_attention(lam, q_arr, q_col, q_w, k_arr, k_row, k_w, v_arr, v_col, v_w, subln,
                     n_batch, seq, n_ctx, *, n_heads, packed, diff, latent, lam_scale, name):
    n = n_batch * (seq if latent else n_ctx)
    ctx_blk = (n_batch * seq) // n_ctx
    out_w = v_w
    smem = pl.BlockSpec(memory_space=pltpu.SMEM)
    kern = functools.partial(_dense_kernel, n_heads=n_heads, packed=packed, diff=diff,
                             latent=latent, lam_scale=lam_scale)
    tq = DENSE_TQ_ONLINE if latent else n_ctx
    max_nk = DENSE_TK if latent else n_ctx
    scratch = [pltpu.VMEM((n_heads, tq, LANES), BF),
               pltpu.VMEM((n_heads, tq, LANES), F32),
               pltpu.VMEM((n_heads, tq, LANES), F32),
               pltpu.VMEM((n_heads, tq, LANES), F32),
               pltpu.VMEM((2, tq, max_nk), F32),
               pltpu.VMEM((2, tq, max_nk), BF)]
    out_shape = jax.ShapeDtypeStruct((n, out_w), BF)
    if latent:
        nq = seq // tq
        nk = seq // DENSE_TK
        grid = (n_batch, nq, nk)
        in_specs = [smem,
                    pl.BlockSpec((tq, q_w), lambda b, i, k: (b * nq + i, q_col)),
                    pl.BlockSpec((k_w, n_ctx), lambda b, i, k: (k_row, ctx_blk + b)),
                    pl.BlockSpec((n_ctx, v_w), lambda b, i, k: (ctx_blk + b, v_col)),
                    pl.BlockSpec((k_w, DENSE_TK), lambda b, i, k: (k_row, b * nk + k)),
                    pl.BlockSpec((DENSE_TK, v_w), lambda b, i, k: (b * nk + k, v_col))]
        args = [lam, q_arr, k_arr, v_arr, k_arr, v_arr]
        if diff:
            in_specs.append(pl.BlockSpec((1, LANES), lambda b, i, k: (0, 0)))
            args.append(subln)
        out_specs = pl.BlockSpec((tq, out_w), lambda b, i, k: (b * nq + i, 0))
        sem = ("parallel", "parallel", "arbitrary")
    else:
        grid = (n_batch,)
        in_specs = [smem,
                    pl.BlockSpec((tq, q_w), lambda b: (ctx_blk + b, q_col)),
                    pl.BlockSpec((k_w, n_ctx), lambda b: (k_row, ctx_blk + b)),
                    pl.BlockSpec((n_ctx, v_w), lambda b: (ctx_blk + b, v_col))]
        args = [lam, q_arr, k_arr, v_arr]
        if diff:
            in_specs.append(pl.BlockSpec((1, LANES), lambda b: (0, 0)))
            args.append(subln)
        out_specs = pl.BlockSpec((tq, out_w), lambda b: (b, 0))
        sem = ("parallel",)
    return pl.pallas_call(
        kern, out_shape=out_shape, grid=grid, in_specs=in_specs, out_specs=out_specs,
        scratch_shapes=scratch, compiler_params=_cparams(sem), name=name,
    )(*args)


SAFE_SUM_LOG2 = 100.0


def _dense_fixed_kernel(par_ref, q_ref, kc_ref, vc_ref, *refs, n_heads, packed, diff, latent, lam_scale):
    if latent:
        k_ref, v_ref = refs[0], refs[1]
        refs = refs[2:]
    ones_ref = refs[0]
    refs = refs[1:]
    if diff:
        subln_ref, o_ref, flag_ref, qm_sc, acc_sc, p_sc = refs
    else:
        o_ref, flag_ref, qm_sc, acc_sc, p_sc = refs
    kt_step = pl.program_id(2) if latent else 0
    tq = q_ref.shape[0]
    ref_logit = par_ref[0, 1]

    def step(h, k_ref_, v_ref_, first):
        blk = h // 2 if packed else h
        ks = slice(blk * LANES, (blk + 1) * LANES)
        vs = ks if packed else slice((h // 2) * LANES, (h // 2 + 1) * LANES)
        nk = k_ref_.shape[1]
        slot = h % 2
        for c in range(nk // DENSE_KC):
            cols = slice(c * DENSE_KC, (c + 1) * DENSE_KC)
            s = jnp.dot(qm_sc[h], k_ref_[ks, cols], preferred_element_type=F32)
            p_sc[slot, :, cols] = jnp.exp2(s - ref_logit).astype(BF)
        v_aug = jnp.concatenate([v_ref_[:, vs], ones_ref[:nk, :]], axis=1)
        pv = jnp.dot(p_sc[slot, :, :nk], v_aug, preferred_element_type=F32)
        if first:
            acc_sc[h] = pv
        else:
            acc_sc[h] += pv

    @pl.when(kt_step == 0)
    def _():
        q = q_ref[...]
        lo = _half_mask((tq, LANES))
        for h in range(n_heads):
            if packed:
                qp = q[:, (h // 2) * LANES:(h // 2 + 1) * LANES]
                qm_sc[h] = _select_half(qp, h % 2, lo)
            else:
                qm_sc[h] = q[:, h * LANES:(h + 1) * LANES]
        for h in range(n_heads):
            step(h, kc_ref, vc_ref, True)

    if latent:
        for h in range(n_heads):
            step(h, k_ref, v_ref, False)
        last = kt_step == pl.num_programs(2) - 1

    def finish():
        lo = _half_mask((tq, LANES))
        outs, bad = [], []
        for h in range(n_heads):
            a = acc_sc[h]
            l = a[:, LANES:]
            unsafe = jnp.logical_not((l > 2.0 ** -SAFE_SUM_LOG2) & (l < 2.0 ** SAFE_SUM_LOG2))
            bad.append(jnp.max(jnp.where(unsafe, 1.0, 0.0), axis=0, keepdims=True))
            outs.append(a[:, :LANES] / l)
        flag_ref[0] = jnp.concatenate(bad, axis=0)
        if diff:
            lam = par_ref[0, 0]
            for hv in range(n_heads // 2):
                y = outs[2 * hv] - lam * outs[2 * hv + 1]
                ms = jnp.mean(y * y, axis=-1, keepdims=True)
                y = y * lax.rsqrt(ms + EPS) * subln_ref[...] * lam_scale
                o_ref[:, hv * LANES:(hv + 1) * LANES] = y.astype(BF)
        else:
            for hp in range(n_heads // 2):
                o_ref[:, hp * LANES:(hp + 1) * LANES] = jnp.where(lo, outs[2 * hp], outs[2 * hp + 1]).astype(BF)

    if latent:
        pl.when(last)(finish)
    else:
        finish()


def _dense_fixed_attention(par, q_arr, q_col, q_w, k_arr, k_row, k_w, v_arr, v_col, v_w, subln, ones,
                           n_batch, seq, n_ctx, *, n_heads, packed, diff, latent, lam_scale, name):
    n = n_batch * (seq if latent else n_ctx)
    ctx_blk = (n_batch * seq) // n_ctx
    smem = pl.BlockSpec(memory_space=pltpu.SMEM)
    kern = functools.partial(_dense_fixed_kernel, n_heads=n_heads, packed=packed, diff=diff,
                             latent=latent, lam_scale=lam_scale)
    tq = DENSE_TQ if latent else n_ctx
    max_nk = DENSE_TK if latent else n_ctx
    scratch = [pltpu.VMEM((n_heads, tq, LANES), BF),
               pltpu.VMEM((n_heads, tq, 2 * LANES), F32),
               pltpu.VMEM((2, tq, max_nk), BF)]
    if latent:
        nq = seq // tq
        nk = seq // DENSE_TK
        grid = (n_batch, nq, nk)
        ix = lambda f: (lambda b, i, k: f(b, i, k))
        q_ix = ix(lambda b, i, k: (b * nq + i, q_col))
        in_specs = [smem,
                    pl.BlockSpec((tq, q_w), q_ix),
                    pl.BlockSpec((k_w, n_ctx), ix(lambda b, i, k: (k_row, ctx_blk + b))),
                    pl.BlockSpec((n_ctx, v_w), ix(lambda b, i, k: (ctx_blk + b, v_col))),
                    pl.BlockSpec((k_w, DENSE_TK), ix(lambda b, i, k: (k_row, b * nk + k))),
                    pl.BlockSpec((DENSE_TK, v_w), ix(lambda b, i, k: (b * nk + k, v_col))),
                    pl.BlockSpec((DENSE_TK, LANES), ix(lambda b, i, k: (0, 0)))]
        args = [par, q_arr, k_arr, v_arr, k_arr, v_arr, ones]
        const_ix = ix(lambda b, i, k: (0, 0))
        out_specs = (pl.BlockSpec((tq, v_w), ix(lambda b, i, k: (b * nq + i, 0))),
                     pl.BlockSpec((1, n_heads, LANES), ix(lambda b, i, k: (b * nq + i, 0, 0))))
        n_flag = n_batch * nq
        sem = ("parallel", "parallel", "arbitrary")
    else:
        grid = (n_batch,)
        in_specs = [smem,
                    pl.BlockSpec((tq, q_w), lambda b: (ctx_blk + b, q_col)),
                    pl.BlockSpec((k_w, n_ctx), lambda b: (k_row, ctx_blk + b)),
                    pl.BlockSpec((n_ctx, v_w), lambda b: (ctx_blk + b, v_col)),
                    pl.BlockSpec((DENSE_TK, LANES), lambda b: (0, 0))]
        args = [par, q_arr, k_arr, v_arr, ones]
        const_ix = lambda b: (0, 0)
        out_specs = (pl.BlockSpec((tq, v_w), lambda b: (b, 0)),
                     pl.BlockSpec((1, n_heads, LANES), lambda b: (b, 0, 0)))
        n_flag = n_batch
        sem = ("parallel",)
    if diff:
        in_specs.append(pl.BlockSpec((1, LANES), const_ix))
        args.append(subln)
    return pl.pallas_call(
        kern,
        out_shape=(jax.ShapeDtypeStruct((n, v_w), BF), jax.ShapeDtypeStruct((n_flag, n_heads, LANES), F32)),
        grid=grid, in_specs=in_specs, out_specs=out_specs,
        scratch_shapes=scratch, compiler_params=_cparams(sem), name=name + "_fixed",
    )(*args)


def _merge_kernel(x_ref, mod_ref, g2_ref, *refs, n_lat_tiles, has_ctx):
    ys = refs[:N_BRANCH]
    refs = refs[N_BRANCH:]
    if has_ctx:
        ycs = refs[:N_BRANCH]
        refs = refs[N_BRANCH:]
        is_ctx = pl.program_id(0) >= n_lat_tiles
    gts = refs[:N_BRANCH]
    wb_ref, wo_ref, rw_ref, xo_ref, h2_ref, sc_ref = refs[N_BRANCH:]
    m = mod_ref[0]
    mix = None
    for n_ in range(N_BRANCH):
        y = ys[n_][...]
        if has_ctx:
            y = jnp.where(is_ctx, ycs[n_][...], y)
        yb = jnp.dot(y, wb_ref[n_], preferred_element_type=F32)
        t = jax.nn.sigmoid(gts[n_][...].astype(F32)) * yb
        mix = t if mix is None else mix + t
    att = jnp.dot(mix.astype(BF), wo_ref[...], preferred_element_type=F32)
    xn = x_ref[...] + m[2:3] * att
    xo_ref[...] = xn
    h2 = _norm_mod(xn, g2_ref[...], m[4:5], m[3:4])
    h2_ref[...] = h2
    logits = jnp.dot(h2, rw_ref[...], preferred_element_type=F32, precision=lax.Precision.HIGHEST)
    sc_ref[...] = jax.nn.sigmoid(logits)


def _merge(xa, mod3, g2, ys, ys_ctx, tok_a, wb, wo, rw, n_rows, group_of_tile):
    d = xa.shape[1]
    n_lat_tiles = ys[0].shape[0] // TM
    has_ctx = ys_ctx is not None
    row = lambda w, c: pl.BlockSpec((TM, w), lambda i, c=c: (i, c))
    lat_row = pl.BlockSpec((TM, BRANCH_W), lambda i: (jnp.minimum(i, n_lat_tiles - 1), 0))
    in_specs = ([row(d, 0),
                 pl.BlockSpec((1, 8, d), lambda i: (group_of_tile(i), 0, 0)),
                 pl.BlockSpec((1, d), lambda i: (0, 0))]
                + [lat_row for _ in range(N_BRANCH)]
                + ([pl.BlockSpec((TM, BRANCH_W), lambda i: (0, 0)) for _ in range(N_BRANCH)] if has_ctx else [])
                + [row(d, c) for c in range(N_BRANCH)]
                + [pl.BlockSpec(wb.shape, lambda i: (0, 0, 0)),
                   pl.BlockSpec(wo.shape, lambda i: (0, 0)),
                   pl.BlockSpec(rw.shape, lambda i: (0, 0))])
    ys = list(ys) + (list(ys_ctx) if has_ctx else [])
    return pl.pallas_call(
        functools.partial(_merge_kernel, n_lat_tiles=n_lat_tiles, has_ctx=has_ctx),
        out_shape=(jax.ShapeDtypeStruct((n_rows, d), F32),
                   jax.ShapeDtypeStruct((n_rows, d), F32),
                   jax.ShapeDtypeStruct((n_rows, LANES), F32)),
        grid=(n_rows // TM,),
        in_specs=in_specs,
        out_specs=(row(d, 0), row(d, 0), row(LANES, 0)),
        compiler_params=_cparams(("parallel",)),
        name="merge",
    )(xa, mod3, g2, *ys, tok_a, tok_a, tok_a, tok_a, wb, wo, rw)


def _expert_kernel(blk_e_ref, src_ref, dst_ref, h2_hbm, w_ref, w1_ref, w3_ref, w2_ref, out_hbm,
                   xbuf, ybuf, sem_in, sem_out):
    i = pl.program_id(0)
    n_steps = pl.num_programs(0)
    slot = i % 2

    def start_gather(blk, s):
        def body(r, c):
            t = src_ref[blk * MOE_BLOCK + r]
            pltpu.make_async_copy(h2_hbm.at[pl.ds(t, 1)], xbuf.at[s, pl.ds(r, 1)], sem_in.at[s]).start()
            return c
        lax.fori_loop(0, MOE_BLOCK, body, 0, unroll=8)

    def wait_gather(s):
        pltpu.make_async_copy(h2_hbm.at[pl.ds(0, MOE_BLOCK)], xbuf.at[s], sem_in.at[s]).wait()

    def start_scatter(blk, s):
        def body(r, c):
            t = dst_ref[blk * MOE_BLOCK + r]
            pltpu.make_async_copy(ybuf.at[s, pl.ds(r, 1)], out_hbm.at[pl.ds(t, 1)], sem_out.at[s]).start()
            return c
        lax.fori_loop(0, MOE_BLOCK, body, 0, unroll=8)

    def wait_scatter(s):
        pltpu.make_async_copy(ybuf.at[s], out_hbm.at[pl.ds(0, MOE_BLOCK)], sem_out.at[s]).wait()

    @pl.when(i == 0)
    def _():
        start_gather(0, 0)

    @pl.when(i + 1 < n_steps)
    def _():
        start_gather(i + 1, 1 - slot)

    wait_gather(slot)

    @pl.when(i >= 2)
    def _():
        wait_scatter(slot)

    xb = xbuf[slot].astype(BF)
    a = jnp.dot(xb, w1_ref[0], preferred_element_type=F32)
    b = jnp.dot(xb, w3_ref[0], preferred_element_type=F32)
    hmid = (a * jax.nn.sigmoid(a) * b).astype(BF)
    ybuf[slot] = jnp.dot(hmid, w2_ref[0], preferred_element_type=F32) * w_ref[...]
    start_scatter(i, slot)

    @pl.when(i == n_steps - 1)
    def _():
        wait_scatter(slot)

        @pl.when(n_steps >= 2)
        def _():
            wait_scatter(1 - slot)


def _experts(blk_e, src, dst, h2, wcol, w1, w3, w2):
    cap = src.shape[0]
    d = h2.shape[1]
    n_blk = cap // MOE_BLOCK
    wspec = lambda shp: pl.BlockSpec((1,) + shp, lambda i, be, s, t: (be[i], 0, 0))
    grid_spec = pltpu.PrefetchScalarGridSpec(
        num_scalar_prefetch=3,
        grid=(n_blk,),
        in_specs=[pl.BlockSpec(memory_space=pl.ANY),
                  pl.BlockSpec((MOE_BLOCK, 1), lambda i, be, s, t: (i, 0)),
                  wspec((d, D_EXPERT)), wspec((d, D_EXPERT)), wspec((D_EXPERT, d))],
        out_specs=pl.BlockSpec(memory_space=pl.ANY),
        scratch_shapes=[pltpu.VMEM((2, MOE_BLOCK, d), F32),
                        pltpu.VMEM((2, MOE_BLOCK, d), F32),
                        pltpu.SemaphoreType.DMA((2,)),
                        pltpu.SemaphoreType.DMA((2,))],
    )
    return pl.pallas_call(
        _expert_kernel,
        out_shape=jax.ShapeDtypeStruct((cap, d), F32),
        grid_spec=grid_spec,
        compiler_params=_cparams(("arbitrary",)),
        name="moe_experts",
    )(blk_e, src, dst, h2, wcol, w1, w3, w2)


def _route(scores, router_b, n):
    per_group = N_EXPERTS // N_GROUPS
    biased = scores + router_b.astype(F32)

    def top2(v):
        lane = jnp.arange(v.shape[-1], dtype=jnp.int32)
        i0 = jnp.argmax(v, axis=-1).astype(jnp.int32)
        v0 = jnp.max(v, axis=-1)
        rest = jnp.where(lane == i0[..., None], -jnp.inf, v)
        i1 = jnp.argmax(rest, axis=-1).astype(jnp.int32)
        v1 = jnp.max(rest, axis=-1)
        return (v0, v1), (i0, i1)

    (g0, g1), _ = top2(biased.reshape(n, N_GROUPS, per_group))
    group = jnp.argmax(g0 + g1, axis=-1)
    in_group = (jnp.arange(N_EXPERTS) // per_group)[None, :] == group[:, None]
    _, (e0, e1) = top2(jnp.where(in_group, biased, -jnp.inf))
    idx = jnp.stack([e0, e1], axis=-1)
    wts = jnp.take_along_axis(scores, idx, axis=-1)
    wts = wts / wts.sum(-1, keepdims=True)
    flat_e = idx.reshape(-1).astype(jnp.int32)
    n_asg = n * TOP_K
    onehot = (flat_e[:, None] == jnp.arange(N_EXPERTS, dtype=jnp.int32)[None, :]).astype(F32)
    oh = onehot.reshape(n_asg // MOE_BLOCK, MOE_BLOCK, N_EXPERTS)
    tri = jnp.tril(jnp.ones((MOE_BLOCK, MOE_BLOCK), F32), -1)
    within = jnp.einsum('ij,bjk->bik', tri, oh)
    blk_tot = oh.sum(axis=1)
    blk_off = jnp.cumsum(blk_tot, axis=0) - blk_tot
    rank = ((within + blk_off[:, None, :]) * oh).sum(-1).reshape(n_asg).astype(jnp.int32)
    counts = blk_tot.sum(axis=0).astype(jnp.int32)
    padded = (counts + MOE_BLOCK - 1) // MOE_BLOCK * MOE_BLOCK
    pad_end = jnp.cumsum(padded)
    pad_start = pad_end - padded
    dest = (onehot * pad_start.astype(F32)[None, :]).sum(-1).astype(jnp.int32) + rank
    n_blk = (n_asg + N_EXPERTS * (MOE_BLOCK - 1) + MOE_BLOCK - 1) // MOE_BLOCK
    cap = n_blk * MOE_BLOCK
    blk_start = jnp.arange(n_blk, dtype=jnp.int32) * MOE_BLOCK
    blk_e = jnp.minimum((blk_start[:, None] >= pad_end[None, :]).sum(-1), N_EXPERTS - 1).astype(jnp.int32)
    pad_lo = jnp.concatenate([pad_start + counts, pad_end[-1:]]).astype(jnp.int32)
    pad_hi = jnp.concatenate([pad_end, jnp.full((1,), cap)]).astype(jnp.int32)
    n_used = (pad_end[-1:] // MOE_BLOCK).astype(jnp.int32)
    return blk_e, dest.astype(jnp.int32), wts.astype(F32), pad_lo, pad_hi, n_used, cap


def _dispatch_kernel(dest_ref, pad_lo_ref, pad_hi_ref, h2_ref, xs_hbm, zrow, sem, zsem):
    i = pl.program_id(0)

    @pl.when(i == 0)
    def _():
        zrow[...] = jnp.zeros(zrow.shape, F32)
        for e in range(N_EXPERTS + 1):
            lo, hi = pad_lo_ref[e], pad_hi_ref[e]

            def zero_row(r, c):
                pltpu.make_async_copy(zrow.at[pl.ds(0, 1)], xs_hbm.at[pl.ds(r, 1)], zsem).start()
                return c

            def zero_wait(r, c):
                pltpu.make_async_copy(zrow.at[pl.ds(0, 1)], xs_hbm.at[pl.ds(r, 1)], zsem).wait()
                return c

            lax.fori_loop(lo, hi, zero_row, 0)
            lax.fori_loop(lo, hi, zero_wait, 0)

    tm = h2_ref.shape[0]

    def copy_row(r, c):
        a = (i * tm + r) * TOP_K
        for k in range(TOP_K):
            pltpu.make_async_copy(h2_ref.at[pl.ds(r, 1)], xs_hbm.at[pl.ds(dest_ref[a + k], 1)], sem).start()
        return c

    lax.fori_loop(0, tm, copy_row, 0, unroll=4)
    for k in range(TOP_K):
        pltpu.make_async_copy(h2_ref, xs_hbm.at[pl.ds(0, tm)], sem).wait()


def _dispatch(dest, pad_lo, pad_hi, h2, cap):
    n, d = h2.shape
    grid_spec = pltpu.PrefetchScalarGridSpec(
        num_scalar_prefetch=3,
        grid=(n // TM,),
        in_specs=[pl.BlockSpec((TM, d), lambda i, de, lo, hi: (i, 0))],
        out_specs=pl.BlockSpec(memory_space=pl.ANY),
        scratch_shapes=[pltpu.VMEM((8, d), F32), pltpu.SemaphoreType.DMA, pltpu.SemaphoreType.DMA],
    )
    return pl.pallas_call(
        _dispatch_kernel,
        out_shape=jax.ShapeDtypeStruct((cap, d), F32),
        grid_spec=grid_spec,
        compiler_params=_cparams(("arbitrary",)),
        name="moe_dispatch",
    )(dest, pad_lo, pad_hi, h2)


def _grouped_ffn_kernel(blk_e_ref, n_used_ref, x_ref, w1_ref, w3_ref, w2_ref, o_ref):
    i = pl.program_id(0)

    @pl.when(i < n_used_ref[0])
    def _():
        xb = x_ref[...].astype(BF)
        a = jnp.dot(xb, w1_ref[0], preferred_element_type=F32)
        b = jnp.dot(xb, w3_ref[0], preferred_element_type=F32)
        hmid = (a * jax.nn.sigmoid(a) * b).astype(BF)
        o_ref[...] = jnp.dot(hmid, w2_ref[0], preferred_element_type=F32)

    @pl.when(i >= n_used_ref[0])
    def _():
        o_ref[...] = jnp.zeros(o_ref.shape, F32)


def _grouped_ffn(blk_e, n_used, xs, w1, w3, w2):
    cap, d = xs.shape
    wspec = lambda shp: pl.BlockSpec((1,) + shp, lambda i, be, nu: (be[i], 0, 0))
    grid_spec = pltpu.PrefetchScalarGridSpec(
        num_scalar_prefetch=2,
        grid=(cap // MOE_BLOCK,),
        in_specs=[pl.BlockSpec((MOE_BLOCK, d), lambda i, be, nu: (jnp.minimum(i, nu[0] - 1), 0)),
                  wspec((d, D_EXPERT)), wspec((d, D_EXPERT)), wspec((D_EXPERT, d))],
        out_specs=pl.BlockSpec((MOE_BLOCK, d), lambda i, be, nu: (i, 0)),
    )
    return pl.pallas_call(
        _grouped_ffn_kernel,
        out_shape=jax.ShapeDtypeStruct((cap, d), F32),
        grid_spec=grid_spec,
        compiler_params=_cparams(("arbitrary",)),
        name="moe_ffn",
    )(blk_e, n_used, xs, w1, w3, w2)


def _gather_combine_kernel(dest_ref, x_ref, mod_ref, w_ref, ys_hbm, o_ref, buf, sem):
    i = pl.program_id(0)
    n_steps = pl.num_programs(0)
    slot = i % 2
    tm = x_ref.shape[0]

    def start_gather(tile, s):
        def body(r, c):
            a = (tile * tm + r) * TOP_K
            for k in range(TOP_K):
                pltpu.make_async_copy(ys_hbm.at[pl.ds(dest_ref[a + k], 1)], buf.at[s, k, pl.ds(r, 1)],
                                      sem.at[s]).start()
            return c
        lax.fori_loop(0, tm, body, 0, unroll=4)

    @pl.when(i == 0)
    def _():
        start_gather(0, 0)

    @pl.when(i + 1 < n_steps)
    def _():
        start_gather(i + 1, 1 - slot)

    for k in range(TOP_K):
        pltpu.make_async_copy(ys_hbm.at[pl.ds(0, tm)], buf.at[slot, k], sem.at[slot]).wait()
    w = w_ref[...]
    f = w[:, 0:1] * buf[slot, 0] + w[:, 1:2] * buf[slot, 1]
    o_ref[...] = x_ref[...] + mod_ref[0][5:6] * f


def _gather_combine(dest, xn, mod3, wts, ys, n_rows, group_of_tile):
    d = xn.shape[1]
    grid_spec = pltpu.PrefetchScalarGridSpec(
        num_scalar_prefetch=1,
        grid=(n_rows // TM,),
        in_specs=[pl.BlockSpec((TM, d), lambda i, de: (i, 0)),
                  pl.BlockSpec((1, 8, d), lambda i, de: (group_of_tile(i), 0, 0)),
                  pl.BlockSpec((TM, TOP_K), lambda i, de: (i, 0)),
                  pl.BlockSpec(memory_space=pl.ANY)],
        out_specs=pl.BlockSpec((TM, d), lambda i, de: (i, 0)),
        scratch_shapes=[pltpu.VMEM((2, TOP_K, TM, d), F32), pltpu.SemaphoreType.DMA((2,))],
    )
    return pl.pallas_call(
        _gather_combine_kernel,
        out_shape=jax.ShapeDtypeStruct((n_rows, d), F32),
        grid_spec=grid_spec,
        compiler_params=_cparams(("arbitrary",)),
        name="moe_combine",
    )(dest, xn, mod3, wts, ys)


def _combine_kernel(x_ref, mod_ref, f0_ref, f1_ref, o_ref):
    o_ref[...] = x_ref[...] + mod_ref[0][5:6] * (f0_ref[...] + f1_ref[...])


def _combine(xn, mod3, f, n_rows, group_of_tile):
    d = xn.shape[1]
    nt = n_rows // TM
    return pl.pallas_call(
        _combine_kernel,
        out_shape=jax.ShapeDtypeStruct((n_rows, d), F32),
        grid=(nt,),
        in_specs=[pl.BlockSpec((TM, d), lambda i: (i, 0)),
                  pl.BlockSpec((1, 8, d), lambda i: (group_of_tile(i), 0, 0)),
                  pl.BlockSpec((TM, d), lambda i: (i, 0)),
                  pl.BlockSpec((TM, d), lambda i: (nt + i, 0))],
        out_specs=pl.BlockSpec((TM, d), lambda i: (i, 0)),
        compiler_params=_cparams(("parallel",)),
        name="moe_combine",
    )(xn, mod3, f, f)


def _rope_tables(seq, dim, pad):
    t = jnp.arange(seq)
    rows = (t // GRID_W).astype(F32)
    cols = (t % GRID_W).astype(F32)
    quarter = dim // 4
    inv_freq = jnp.exp(-math.log(ROPE_BASE) * jnp.arange(quarter, dtype=F32) / quarter)
    ang = jnp.concatenate([inv_freq[:, None] * rows[None, :], inv_freq[:, None] * cols[None, :]], axis=0)
    cos = jnp.concatenate([jnp.cos(ang), jnp.ones((dim // 2, pad), F32)], axis=1)
    sin = jnp.concatenate([jnp.sin(ang), jnp.zeros((dim // 2, pad), F32)], axis=1)
    return cos, sin


def _win_mask_table(seq):
    nkb = seq // LANES
    nq = seq // WIN_TQ
    tabs = []
    for i in (0, 1, nq - 1):
        t = i * WIN_TQ + np.arange(WIN_TQ)[:, None]
        blk = np.clip(2 * i - 1 + np.arange(4), 0, nkb - 1)
        want = 2 * i - 1 + np.arange(4)
        s = (blk[:, None] * LANES + np.arange(LANES)[None, :]).reshape(-1)[None, :]
        ok = (np.abs(t - s) <= WIN_RADIUS) & np.repeat(blk == want, LANES)[None, :]
        tabs.append(np.where(ok, 0.0, NEG))
    return jnp.asarray(np.stack(tabs), F32)


def _nat_bias_table(rpb, seq):
    rows = seq // GRID_W
    nq = rows // NAT_ROWS_PER_STEP
    wc = NAT_WIN_COLS
    col = np.arange(GRID_W)
    col_start = np.clip(col - wc // 2, 0, GRID_W - wc)
    col_ok = (col[None, :] >= col_start[:, None]) & (col[None, :] < col_start[:, None] + wc)
    d_col = np.clip(col[None, :] - col[:, None] + (wc - 1), 0, 2 * wc - 2)
    tabs = []
    for i in (0, 1, nq - 1):
        r0 = NAT_ROWS_PER_STEP * i
        ws = np.clip(r0 - NAT_WIN_ROWS // 2, 0, rows - NAT_KEY_ROWS)
        r = r0 + np.arange(NAT_ROWS_PER_STEP)
        rs = np.clip(r - NAT_WIN_ROWS // 2, 0, rows - NAT_WIN_ROWS)
        krow = ws + np.arange(NAT_KEY_ROWS)
        row_ok = (krow[None, :] >= rs[:, None]) & (krow[None, :] < rs[:, None] + NAT_WIN_ROWS)
        d_row = np.clip(krow[None, :] - r[:, None] + (NAT_WIN_ROWS - 1), 0, 2 * NAT_WIN_ROWS - 2)
        ok = row_ok[:, None, :, None] & col_ok[None, :, None, :]
        sel_r = jnp.asarray(d_row.reshape(-1)[:, None] == np.arange(2 * NAT_WIN_ROWS - 1)[None, :], F32)
        sel_c = jnp.asarray(d_col.reshape(-1)[:, None] == np.arange(2 * wc - 1)[None, :], F32)
        bias = jnp.einsum('pr,hrc,qc->hpq', sel_r, rpb.astype(F32), sel_c, precision=lax.Precision.HIGHEST)
        bias = bias.reshape(rpb.shape[0], NAT_ROWS_PER_STEP, NAT_KEY_ROWS, GRID_W, GRID_W)
        bias = bias.transpose(0, 1, 3, 2, 4)
        bias = jnp.where(jnp.asarray(ok)[None], bias, NEG)
        tabs.append(bias.reshape(rpb.shape[0], NAT_ROWS_PER_STEP * GRID_W, NAT_KEY_ROWS * GRID_W))
    return jnp.stack(tabs)


def _bcast_rows(v, reps=1):
    return jnp.tile(jnp.broadcast_to(v.astype(F32)[:, None], (v.shape[0], LANES)), (reps, 1))


def _layer_params(l, p):
    w = p['w_in'][l]
    sizes = (512, 128, 128, 512, 512, 512, 512, 512, 512, 256, 160, 4096)
    offs = np.concatenate([[0], np.cumsum(sizes)])
    seg = lambda k: w[:, offs[k]:offs[k + 1]]
    wq, wk, wv, dq, dk, dv, nq, nk, nv, mqa, mkva, gates = [seg(k) for k in range(12)]
    d = w.shape[0]
    dup = lambda m: jnp.concatenate([m[:, :64], m[:, :64], m[:, 64:], m[:, 64:]], axis=1)
    mkva_p = jnp.concatenate([mkva, jnp.zeros((d, 256 - mkva.shape[1]), F32)], axis=1)
    w_a = jnp.concatenate([gates, dv, nv, mqa, mkva_p, dup(wv)], axis=1).astype(BF)
    w_bt = jnp.concatenate([wq, dq, nq, dk, nk, dup(wk)], axis=1).T.astype(BF)
    scale = HEAD_DIM ** -0.5 * LOG2E
    gain_b = jnp.concatenate([
        _bcast_rows(p['win_q_norm'][l] * scale, 8), _bcast_rows(p['dif_q_norm'][l] * scale, 8),
        _bcast_rows(p['nat_q_norm'][l] * scale, 8), _bcast_rows(p['dif_k_norm'][l], 8),
        _bcast_rows(p['nat_k_norm'][l], 8), _bcast_rows(p['win_k_norm'][l], 4)], axis=0)
    wkv = p['mla_wkv_b'][l].reshape(MLA_KV_LORA, MLA_HEADS, MLA_NOPE + MLA_V)
    wk_t = wkv[:, :, :MLA_NOPE].reshape(MLA_KV_LORA, -1).T.astype(BF)
    wv_m = wkv[:, :, MLA_NOPE:].reshape(MLA_KV_LORA, -1).astype(BF)
    def logit_bound(gq, gk, dim):
        return dim * jnp.max(jnp.abs(gq)) * jnp.max(jnp.abs(gk)) * (1.0 + 2.0 ** -7)

    lam_f = p['dif_lambda'][l].astype(F32)
    lam_init = 0.8 - 0.6 * math.exp(-0.3 * l)
    lam = jnp.exp(jnp.sum(lam_f[0] * lam_f[1])) - jnp.exp(jnp.sum(lam_f[2] * lam_f[3])) + lam_init
    return dict(
        w_a=w_a, w_bt=w_bt, gain_b=gain_b,
        g1=p['norm1_g'][l].reshape(1, d), g2=p['norm2_g'][l].reshape(1, d),
        sink=(p['win_sink'][l].astype(F32) * LOG2E).reshape(1, WIN_HEADS),
        gqa=p['mla_q_a_norm'][l].reshape(1, -1), gkva=p['mla_kv_a_norm'][l].reshape(1, -1),
        wq_t=p['mla_wq_b'][l].T.astype(BF), wk_t=wk_t, wv_m=wv_m,
        gq=_bcast_rows(p['mla_q_norm'][l] * (MLA_QK ** -0.5 * LOG2E)), gk=_bcast_rows(p['mla_k_norm'][l]),
        lam=lam.reshape(1, 1).astype(F32), lam_scale=1.0 - lam_init,
        m_win=logit_bound(p['win_q_norm'][l] * scale, p['win_k_norm'][l], HEAD_DIM),
        m_dif=logit_bound(p['dif_q_norm'][l] * scale, p['dif_k_norm'][l], HEAD_DIM),
        m_nat=logit_bound(p['nat_q_norm'][l] * scale, p['nat_k_norm'][l], HEAD_DIM),
        m_mla=logit_bound(p['mla_q_norm'][l] * (MLA_QK ** -0.5 * LOG2E), p['mla_k_norm'][l], MLA_QK),
        subln=p['dif_subln'][l].astype(F32).reshape(1, DIF_V_DIM),
        wb=p['w_branch'][l].astype(BF), wo=p['w_out'][l].astype(BF),
        w1=p['moe_w1'][l].astype(BF), w3=p['moe_w3'][l].astype(BF), w2=p['moe_w2'][l].astype(BF),
    )


def kernel(x, c, ctx, c_ctx, ada_w, ada_b, norm1_g, norm2_g, w_in, win_q_norm, win_k_norm, win_sink,
           dif_q_norm, dif_k_norm, dif_lambda, dif_subln, nat_q_norm, nat_k_norm, nat_rpb,
           mla_q_a_norm, mla_wq_b, mla_kv_a_norm, mla_wkv_b, mla_q_norm, mla_k_norm,
           w_branch, w_out, router_w, router_b, moe_w1, moe_w3, moe_w2):
    p = dict(norm1_g=norm1_g, norm2_g=norm2_g, w_in=w_in, win_q_norm=win_q_norm, win_k_norm=win_k_norm,
             win_sink=win_sink, dif_q_norm=dif_q_norm, dif_k_norm=dif_k_norm, dif_lambda=dif_lambda,
             dif_subln=dif_subln, nat_q_norm=nat_q_norm, nat_k_norm=nat_k_norm,
             mla_q_a_norm=mla_q_a_norm, mla_wq_b=mla_wq_b, mla_kv_a_norm=mla_kv_a_norm,
             mla_wkv_b=mla_wkv_b, mla_q_norm=mla_q_norm, mla_k_norm=mla_k_norm,
             w_branch=w_branch, w_out=w_out, moe_w1=moe_w1, moe_w3=moe_w3, moe_w2=moe_w2)
    n_batch, seq, d = x.shape
    n_ctx = ctx.shape[1]
    depth = ada_w.shape[0]
    n_lat = n_batch * seq
    n_all = n_lat + n_batch * n_ctx
    assert seq % DENSE_TK == 0 and seq % DENSE_TQ == 0 and (n_batch * n_ctx) == TM and seq % TM == 0
    tiles_per_batch = seq // TM
    group_of_tile = lambda i: jnp.minimum(i // tiles_per_batch, n_batch)
    pos_of_tile = lambda i: jnp.where(i < n_batch * tiles_per_batch, i % tiles_per_batch, tiles_per_batch)

    cc = jnp.concatenate([c, c_ctx[None, :], jnp.zeros((8 - n_batch - 1, d), F32)], axis=0)
    mod = _modulation(cc, ada_w, ada_b)
    mod = mod[:, :n_batch + 1].reshape(depth, n_batch + 1, 6, d)
    mod = jnp.pad(mod, ((0, 0), (0, 0), (0, 2), (0, 0)))

    cos_h, sin_h = _rope_tables(seq, HEAD_DIM, TM)
    cos_m, sin_m = _rope_tables(seq, MLA_ROPE, TM)
    win_mask = _win_mask_table(seq)
    rw = jnp.pad(router_w.astype(F32), ((0, 0), (0, LANES - N_EXPERTS)))

    xa = jnp.concatenate([x.reshape(n_lat, d), ctx.reshape(n_batch * n_ctx, d)], axis=0)
    ones = jnp.ones((DENSE_TK, LANES), BF)

    def dense(lam, logit_bound, q_arr, q_col, q_w, k_arr, k_row, k_w, v_arr, v_col, v_w, subln, **kw):
        operands = (q_arr, q_col, q_w, k_arr, k_row, k_w, v_arr, v_col, v_w, subln)
        par = jnp.concatenate([lam, logit_bound.reshape(1, 1).astype(F32)], axis=1)
        y, flag = _dense_fixed_attention(par, *operands, ones, n_batch, seq, n_ctx, **kw)
        return lax.cond(jnp.max(flag) > 0.0,
                        lambda: _dense_attention(lam, *operands, n_batch, seq, n_ctx, **kw),
                        lambda: y)
    for l in range(depth):
        lp = _layer_params(l, p)
        want_ctx = l < depth - 1
        mod3 = mod[l]
        tok_a = _in_proj_a(xa, mod3, lp['g1'], lp['w_a'], group_of_tile)
        q_tok, kt = _in_proj_b(xa, mod3, lp['g1'], lp['w_bt'], lp['gain_b'], cos_h, sin_h,
                               group_of_tile, pos_of_tile)
        mq, mkt, mv = _mla_prep(tok_a, lp['gqa'], lp['gkva'], lp['wq_t'], lp['wk_t'], lp['wv_m'],
                                lp['gq'], lp['gk'], cos_m, sin_m, pos_of_tile)
        nat_bias = _nat_bias_table(nat_rpb[l].astype(F32) * LOG2E, seq)

        def branches(latent):
            win_args = (q_tok, kt, tok_a)
            if latent:
                m_win = lp['m_win']
                sink_rel = jnp.concatenate([lp['sink'] - m_win, jnp.full((1, 8), m_win, F32)], axis=1)
                y_win, flag = _win_attention(sink_rel, *win_args, win_mask - m_win, n_batch, seq, n_ctx, True,
                                             ones=ones)
                y_win = lax.cond(jnp.max(flag) > 0.0,
                                 lambda: _win_attention(lp['sink'], *win_args, win_mask, n_batch, seq, n_ctx, True),
                                 lambda: y_win)
            else:
                y_win = _win_attention(lp['sink'], *win_args, win_mask, n_batch, seq, n_ctx, False)
            y_dif = dense(lp['lam'], lp['m_dif'], q_tok, 1, 512, kt, KT_DK // 512, 512, tok_a, A_DV // 512, 512,
                          lp['subln'], n_heads=2 * DIF_HEADS, packed=True, diff=True, latent=latent,
                          lam_scale=lp['lam_scale'], name="dif_attn" if latent else "dif_attn_ctx")
            if latent:
                m_nl = lp['m_nat'] + jnp.maximum(jnp.max(nat_rpb[l].astype(F32)) * LOG2E, 0.0)
                y_nat, flag = _nat_attention(q_tok, kt, tok_a, nat_bias - m_nl, n_batch, seq, n_ctx,
                                             par=m_nl.reshape(1, 1).astype(F32), ones=ones)
                y_nat = lax.cond(jnp.max(flag) > 0.0,
                                 lambda: _nat_attention(q_tok, kt, tok_a, nat_bias, n_batch, seq, n_ctx),
                                 lambda: y_nat)
            else:
                y_nat = dense(lp['lam'], lp['m_nat'], q_tok, 2, 512, kt, KT_NK // 512, 512, tok_a, A_NV // 512, 512,
                              None, n_heads=NAT_HEADS, packed=True, diff=False, latent=False,
                              lam_scale=1.0, name="nat_attn_ctx")
            y_mla = dense(lp['lam'], lp['m_mla'], mq, 0, 1024, mkt, 0, 1024, mv, 0, 512, None,
                          n_heads=MLA_HEADS, packed=False, diff=False, latent=latent, lam_scale=1.0,
                          name="mla_attn" if latent else "mla_attn_ctx")
            return [y_win, y_dif, y_nat, y_mla]

        ys = branches(True)
        ys_c = branches(False) if want_ctx else None
        n_rows = n_all if want_ctx else n_lat
        xn, h2, scores = _merge(xa, mod3, lp['g2'], ys, ys_c, tok_a, lp['wb'], lp['wo'], rw, n_rows,
                                group_of_tile)
        blk_e, dest, wts, pad_lo, pad_hi, n_used, cap = _route(scores[:, :N_EXPERTS], router_b, n_rows)
        xs = _dispatch(dest, pad_lo, pad_hi, h2, cap)
        ys = _grouped_ffn(blk_e, n_used, xs, lp['w1'], lp['w3'], lp['w2'])
        xa = _gather_combine(dest, xn, mod3, wts, ys, n_rows, group_of_tile)
    return xa[:n_lat].reshape(n_batch, seq, d)
```

```python
import functools
import math

import jax
import jax.numpy as jnp
import numpy as np
from jax import lax
from jax.experimental import pallas as pl
from jax.experimental.pallas import tpu as pltpu

F32 = jnp.float32
BF = jnp.bfloat16

GRID_W = 64
HEAD_DIM = 64
N_BRANCH = 4
BRANCH_W = 512
ROPE_BASE = 10000.0
EPS = 1e-6
NEG = -1e30
LOG2E = math.log2(math.e)
WIN_HEADS, WIN_KV_HEADS, WIN_RADIUS = 8, 2, 128
DIF_HEADS, DIF_QK_DIM, DIF_V_DIM = 4, 64, 128
NAT_HEADS, NAT_WIN_ROWS, NAT_WIN_COLS = 8, 8, 16
MLA_HEADS, MLA_NOPE, MLA_ROPE, MLA_V, MLA_Q_LORA, MLA_KV_LORA = 8, 64, 32, 64, 256, 128
MLA_QK = MLA_NOPE + MLA_ROPE
N_EXPERTS, N_GROUPS, TOP_K, D_EXPERT, MOE_BLOCK = 16, 4, 2, 512, 256

LANES = 128
TM = 512
WIN_TQ = 256
NAT_ROWS_PER_STEP = 4
NAT_KEY_ROWS = 12
DENSE_TQ = 1024
DENSE_TQ_ONLINE = 512
DENSE_TK = 1024
DENSE_RB = 64
DENSE_KC = 256
VMEM_LIMIT = 48 * 1024 * 1024

A_GATES, A_DV, A_NV, A_MQA, A_MKVA, A_WV = 0, 4096, 4608, 5120, 5376, 5632
A_COLS = 5888
Q_COLS = 1536
KT_DK, KT_NK, KT_WK = 0, 512, 1024
KT_ROWS = 1280
B_ROWS = Q_COLS + KT_ROWS


def _cparams(sem, vmem=VMEM_LIMIT):
    return pltpu.CompilerParams(dimension_semantics=sem, vmem_limit_bytes=vmem)


def _nt_dot(a, b):
    return lax.dot_general(a, b, (((1,), (1,)), ((), ())), preferred_element_type=F32)


def _norm_mod(x, g, sc, sh):
    ms = jnp.mean(x * x, axis=-1, keepdims=True)
    return (x * lax.rsqrt(ms + EPS) * g) * (1.0 + sc) + sh


def _lane_tile(a, n):
    reps = n // a.shape[1]
    return a if reps == 1 else jnp.concatenate([a] * reps, axis=1)


def _mod_kernel(c_ref, w_ref, b_ref, o_ref):
    cc = c_ref[...]
    a = cc * jax.nn.sigmoid(cc)
    o_ref[0] = jnp.dot(a, w_ref[0], preferred_element_type=F32,
                       precision=lax.Precision.HIGHEST) + b_ref[0]


def _modulation(cc, ada_w, ada_b):
    n_layers, d, d6 = ada_w.shape
    tn = 1536
    return pl.pallas_call(
        _mod_kernel,
        out_shape=jax.ShapeDtypeStruct((n_layers, 8, d6), F32),
        grid=(n_layers, d6 // tn),
        in_specs=[pl.BlockSpec((8, d), lambda l, j: (0, 0)),
                  pl.BlockSpec((1, d, tn), lambda l, j: (l, 0, j)),
                  pl.BlockSpec((1, 1, tn), lambda l, j: (l, 0, j))],
        out_specs=pl.BlockSpec((1, 8, tn), lambda l, j: (l, 0, j)),
        compiler_params=_cparams(("parallel", "parallel")),
        name="adaln_mod",
    )(cc, ada_w, ada_b.reshape(n_layers, 1, d6))


def _in_a_kernel(x_ref, mod_ref, g_ref, w_ref, o_ref):
    m = mod_ref[0]
    h = _norm_mod(x_ref[...], g_ref[...], m[1:2], m[0:1]).astype(BF)
    o_ref[...] = jnp.dot(h, w_ref[...], preferred_element_type=F32).astype(BF)


def _in_proj_a(xa, mod3, g1, w_a, group_of_tile):
    n, d = xa.shape
    tn = A_COLS // 2
    return pl.pallas_call(
        _in_a_kernel,
        out_shape=jax.ShapeDtypeStruct((n, A_COLS), BF),
        grid=(A_COLS // tn, n // TM),
        in_specs=[pl.BlockSpec((TM, d), lambda j, i: (i, 0)),
                  pl.BlockSpec((1, 8, d), lambda j, i: (group_of_tile(i), 0, 0)),
                  pl.BlockSpec((1, d), lambda j, i: (0, 0)),
                  pl.BlockSpec((d, tn), lambda j, i: (0, j))],
        out_specs=pl.BlockSpec((TM, tn), lambda j, i: (i, j)),
        compiler_params=_cparams(("parallel", "parallel")),
        name="in_proj_tok",
    )(xa, mod3, g1, w_a)


def _head_norm_rope(x, g, cos, sin, rope):
    ss = jnp.sum(x * x, axis=0, keepdims=True)
    y = x * lax.rsqrt(ss * (1.0 / HEAD_DIM) + EPS) * g
    if not rope:
        return y
    half = HEAD_DIM // 2
    y1, y2 = y[:half], y[half:]
    return jnp.concatenate([y1 * cos - y2 * sin, y1 * sin + y2 * cos], axis=0)


def _in_b_kernel(x_ref, mod_ref, g_ref, wt_ref, gain_ref, cos_ref, sin_ref, q_ref, kt_ref, acc_sc):
    m = mod_ref[0]
    h = _norm_mod(x_ref[...], g_ref[...], m[1:2], m[0:1]).astype(BF)
    acc_sc[...] = _nt_dot(wt_ref[...], h)
    tm = h.shape[0]
    cos = cos_ref[...]
    sin = sin_ref[...]

    def pair(r0, rope):
        hs = []
        for e in range(2):
            r = r0 + e * HEAD_DIM
            g = _lane_tile(gain_ref[r:r + HEAD_DIM, :], tm)
            hs.append(_head_norm_rope(acc_sc[r:r + HEAD_DIM, :], g, cos, sin, rope))
        return jnp.concatenate(hs, axis=0)

    for p in range(Q_COLS // LANES):
        y = pair(p * LANES, rope=p < 8)
        q_ref[:, p * LANES:(p + 1) * LANES] = y.T.astype(BF)
    for p in range(KT_ROWS // LANES):
        y = pair(Q_COLS + p * LANES, rope=not (4 <= p < 8))
        kt_ref[p * LANES:(p + 1) * LANES, :] = y.astype(BF)


def _in_proj_b(xa, mod3, g1, w_bt, gain_b, cos_t, sin_t, group_of_tile, pos_of_tile):
    n, d = xa.shape
    return pl.pallas_call(
        _in_b_kernel,
        out_shape=(jax.ShapeDtypeStruct((n, Q_COLS), BF),
                   jax.ShapeDtypeStruct((KT_ROWS, n), BF)),
        grid=(n // TM,),
        in_specs=[pl.BlockSpec((TM, d), lambda i: (i, 0)),
                  pl.BlockSpec((1, 8, d), lambda i: (group_of_tile(i), 0, 0)),
                  pl.BlockSpec((1, d), lambda i: (0, 0)),
                  pl.BlockSpec((B_ROWS, d), lambda i: (0, 0)),
                  pl.BlockSpec((B_ROWS, LANES), lambda i: (0, 0)),
                  pl.BlockSpec((HEAD_DIM // 2, TM), lambda i: (0, pos_of_tile(i))),
                  pl.BlockSpec((HEAD_DIM // 2, TM), lambda i: (0, pos_of_tile(i)))],
        out_specs=(pl.BlockSpec((TM, Q_COLS), lambda i: (i, 0)),
                   pl.BlockSpec((KT_ROWS, TM), lambda i: (0, i))),
        scratch_shapes=[pltpu.VMEM((B_ROWS, TM), F32)],
        compiler_params=_cparams(("parallel",)),
        name="in_proj_heads",
    )(xa, mod3, g1, w_bt, gain_b, cos_t, sin_t)


def _mla_kernel(qa_ref, kva_ref, gqa_ref, gkva_ref, wqt_ref, wkt_ref, wv_ref, gq_ref, gk_ref,
                cos_ref, sin_ref, mq_ref, mkt_ref, mv_ref):
    tm = qa_ref.shape[0]
    cos = cos_ref[...]
    sin = sin_ref[...]
    rh = MLA_ROPE // 2

    def rms_rows(x, g):
        ms = jnp.mean(x * x, axis=-1, keepdims=True)
        return x * lax.rsqrt(ms + EPS) * g

    def rope_rows(x):
        x1, x2 = x[:rh], x[rh:]
        return jnp.concatenate([x1 * cos - x2 * sin, x1 * sin + x2 * cos], axis=0)

    qa = rms_rows(qa_ref[...].astype(F32), gqa_ref[...]).astype(BF)
    qt = _nt_dot(wqt_ref[...], qa)
    kva = kva_ref[...].astype(F32)
    cn = rms_rows(kva[:, :MLA_KV_LORA], gkva_ref[...]).astype(BF)
    knt = _nt_dot(wkt_ref[...], cn)
    mv_ref[...] = jnp.dot(cn, wv_ref[...], preferred_element_type=F32).astype(BF)
    krope = kva[:, MLA_KV_LORA:].T[:MLA_ROPE]
    kr_ss = jnp.sum(krope * krope, axis=0, keepdims=True)
    gq = _lane_tile(gq_ref[...], tm)
    gk = _lane_tile(gk_ref[...], tm)
    zpad = jnp.zeros((LANES - MLA_QK, tm), F32)
    for hd in range(MLA_HEADS):
        x = qt[hd * MLA_QK:(hd + 1) * MLA_QK]
        ss = jnp.sum(x * x, axis=0, keepdims=True)
        y = x * lax.rsqrt(ss * (1.0 / MLA_QK) + EPS) * gq
        y = jnp.concatenate([y[:MLA_NOPE], rope_rows(y[MLA_NOPE:]), zpad], axis=0)
        mq_ref[:, hd * LANES:(hd + 1) * LANES] = y.T.astype(BF)
        kn = knt[hd * MLA_NOPE:(hd + 1) * MLA_NOPE]
        ss = jnp.sum(kn * kn, axis=0, keepdims=True) + kr_ss
        r = lax.rsqrt(ss * (1.0 / MLA_QK) + EPS)
        yk = jnp.concatenate([kn * r * gk[:MLA_NOPE], rope_rows(krope * r * gk[MLA_NOPE:]), zpad], axis=0)
        mkt_ref[hd * LANES:(hd + 1) * LANES, :] = yk.astype(BF)


def _mla_prep(tok_a, gqa, gkva, wq_t, wk_t, wv, gq, gk, cos_t, sin_t, pos_of_tile):
    n = tok_a.shape[0]
    hw = MLA_HEADS * LANES
    return pl.pallas_call(
        _mla_kernel,
        out_shape=(jax.ShapeDtypeStruct((n, hw), BF),
                   jax.ShapeDtypeStruct((hw, n), BF),
                   jax.ShapeDtypeStruct((n, MLA_HEADS * MLA_V), BF)),
        grid=(n // TM,),
        in_specs=[pl.BlockSpec((TM, 256), lambda i: (i, A_MQA // 256)),
                  pl.BlockSpec((TM, 256), lambda i: (i, A_MKVA // 256)),
                  pl.BlockSpec((1, MLA_Q_LORA), lambda i: (0, 0)),
                  pl.BlockSpec((1, MLA_KV_LORA), lambda i: (0, 0)),
                  pl.BlockSpec(wq_t.shape, lambda i: (0, 0)),
                  pl.BlockSpec(wk_t.shape, lambda i: (0, 0)),
                  pl.BlockSpec(wv.shape, lambda i: (0, 0)),
                  pl.BlockSpec((MLA_QK, LANES), lambda i: (0, 0)),
                  pl.BlockSpec((MLA_QK, LANES), lambda i: (0, 0)),
                  pl.BlockSpec((MLA_ROPE // 2, TM), lambda i: (0, pos_of_tile(i))),
                  pl.BlockSpec((MLA_ROPE // 2, TM), lambda i: (0, pos_of_tile(i)))],
        out_specs=(pl.BlockSpec((TM, hw), lambda i: (i, 0)),
                   pl.BlockSpec((hw, TM), lambda i: (0, i)),
                   pl.BlockSpec((TM, MLA_HEADS * MLA_V), lambda i: (i, 0))),
        compiler_params=_cparams(("parallel",)),
        name="mla_prep",
    )(tok_a, tok_a, gqa, gkva, wq_t, wk_t, wv, gq, gk, cos_t, sin_t)


def _half_mask(shape):
    return lax.broadcasted_iota(jnp.int32, shape, 1) < (LANES // 2)


def _select_half(q, e, lo_mask):
    zero = jnp.zeros_like(q)
    return jnp.where(lo_mask, q, zero) if e == 0 else jnp.where(lo_mask, zero, q)


def _local_softmax_out(parts, extra_logit):
    m = parts[0][0].max(axis=-1, keepdims=True)
    for s, _ in parts[1:]:
        m = jnp.maximum(m, s.max(axis=-1, keepdims=True))
    if extra_logit is not None:
        m = jnp.maximum(m, extra_logit)
    z = None
    o = None
    for s, v in parts:
        p = jnp.exp2(s - m)
        zs = p.sum(axis=-1, keepdims=True)
        os_ = jnp.dot(p.astype(BF), v, preferred_element_type=F32)
        z = zs if z is None else z + zs
        o = os_ if o is None else o + os_
    if extra_logit is not None:
        z = z + jnp.exp2(extra_logit - m)
    return o / z


def _fixed_softmax_out(parts, ones_ref, extra_logit):
    acc = None
    for s, v in parts:
        aug = jnp.concatenate([v, ones_ref[:s.shape[1], :]], axis=1)
        t = jnp.dot(jnp.exp2(s).astype(BF), aug, preferred_element_type=F32)
        acc = t if acc is None else acc + t
    l = acc[:, LANES:]
    if extra_logit is not None:
        l = l + jnp.exp2(jnp.zeros_like(l) + extra_logit)
    unsafe = jnp.logical_not((l > 2.0 ** -SAFE_SUM_LOG2) & (l < 2.0 ** SAFE_SUM_LOG2))
    return acc[:, :LANES] / l, jnp.max(jnp.where(unsafe, 1.0, 0.0), axis=0, keepdims=True)


def _win_kernel(sink_ref, q_ref, *refs, band, fixed=False):
    if band:
        k0, k1, k2, k3, v0, v1, v2, v3, kc_ref, vc_ref, mask_ref = refs[:11]
        refs = refs[11:]
        kb = jnp.concatenate([k0[...], k1[...], k2[...], k3[...]], axis=1)
        vb = jnp.concatenate([v0[...], v1[...], v2[...], v3[...]], axis=0)
        mask = mask_ref[0]
    else:
        kc_ref, vc_ref = refs[:2]
        refs = refs[2:]
    if fixed:
        ones_ref, o_ref, flag_ref = refs
        ref_logit = sink_ref[0, WIN_HEADS]
    else:
        (o_ref,) = refs
    q = q_ref[...]
    lo = _half_mask((q.shape[0], LANES))
    group = WIN_HEADS // WIN_KV_HEADS
    bad = []
    for j in range(WIN_HEADS // 2):
        qp = q[:, j * LANES:(j + 1) * LANES]
        g = (2 * j) // group
        kc = kc_ref[g * LANES:(g + 1) * LANES, :]
        vc = vc_ref[:, g * LANES:(g + 1) * LANES]
        outs = []
        for e in range(2):
            qm = _select_half(qp, e, lo)
            parts = []
            if band:
                s = jnp.dot(qm, kb[g * LANES:(g + 1) * LANES, :], preferred_element_type=F32) + mask
                parts.append((s, vb[:, g * LANES:(g + 1) * LANES]))
            sc = jnp.dot(qm, kc, preferred_element_type=F32)
            if fixed:
                parts.append((sc - ref_logit, vc))
                o, u = _fixed_softmax_out(parts, ones_ref, sink_ref[0, 2 * j + e])
                bad.append(u)
                outs.append(o)
            else:
                parts.append((sc, vc))
                outs.append(_local_softmax_out(parts, sink_ref[0, 2 * j + e]))
        o_ref[:, j * LANES:(j + 1) * LANES] = jnp.where(lo, outs[0], outs[1]).astype(BF)
    if fixed:
        flag_ref[0] = jnp.concatenate(bad, axis=0)


def _win_attention(sink, q_tok, kt, tok_a, mask_tbl, n_batch, seq, n_ctx, latent, ones=None):
    ctx_blk = (n_batch * seq) // n_ctx
    kc_spec = lambda f: pl.BlockSpec((2 * LANES, n_ctx), f)
    vc_spec = lambda f: pl.BlockSpec((n_ctx, 2 * LANES), f)
    smem = pl.BlockSpec(memory_space=pltpu.SMEM)
    n_out = n_batch * (seq if latent else n_ctx)
    out_shape = jax.ShapeDtypeStruct((n_out, WIN_HEADS * HEAD_DIM), BF)
    if not latent:
        return pl.pallas_call(
            functools.partial(_win_kernel, band=False),
            out_shape=out_shape,
            grid=(n_batch,),
            in_specs=[smem,
                      pl.BlockSpec((n_ctx, 512), lambda b: (ctx_blk + b, 0)),
                      kc_spec(lambda b: (KT_WK // 256, ctx_blk + b)),
                      vc_spec(lambda b: (ctx_blk + b, A_WV // 256))],
            out_specs=pl.BlockSpec((n_ctx, 512), lambda b: (b, 0)),
            compiler_params=_cparams(("parallel",)),
            name="win_attn_ctx",
        )(sink, q_tok, kt, tok_a)
    nq = seq // WIN_TQ
    nkb = seq // LANES

    def kidx(j):
        return lambda b, i: (KT_WK // 256, b * nkb + jnp.clip(2 * i - 1 + j, 0, nkb - 1))

    def vidx(j):
        return lambda b, i: (b * nkb + jnp.clip(2 * i - 1 + j, 0, nkb - 1), A_WV // 256)

    def variant(b, i):
        return (jnp.where(i == 0, 0, jnp.where(i == nq - 1, 2, 1)), 0, 0)

    in_specs = ([smem, pl.BlockSpec((WIN_TQ, 512), lambda b, i: (b * nq + i, 0))]
                + [pl.BlockSpec((2 * LANES, LANES), kidx(j)) for j in range(4)]
                + [pl.BlockSpec((LANES, 2 * LANES), vidx(j)) for j in range(4)]
                + [kc_spec(lambda b, i: (KT_WK // 256, ctx_blk + b)),
                   vc_spec(lambda b, i: (ctx_blk + b, A_WV // 256)),
                   pl.BlockSpec((1, WIN_TQ, 4 * LANES), variant)])
    args = [sink, q_tok, kt, kt, kt, kt, tok_a, tok_a, tok_a, tok_a, kt, tok_a, mask_tbl]
    out_specs = pl.BlockSpec((WIN_TQ, 512), lambda b, i: (b * nq + i, 0))
    fixed = ones is not None
    if fixed:
        in_specs.append(pl.BlockSpec(ones.shape, lambda b, i: (0, 0)))
        args.append(ones)
        out_shape = (out_shape, jax.ShapeDtypeStruct((n_batch * nq, WIN_HEADS, LANES), F32))
        out_specs = (out_specs, pl.BlockSpec((1, WIN_HEADS, LANES), lambda b, i: (b * nq + i, 0, 0)))
    return pl.pallas_call(
        functools.partial(_win_kernel, band=True, fixed=fixed),
        out_shape=out_shape,
        grid=(n_batch, nq),
        in_specs=in_specs,
        out_specs=out_specs,
        compiler_params=_cparams(("parallel", "parallel")),
        name="win_attn_fixed" if fixed else "win_attn",
    )(*args)


def _nat_kernel(q_ref, k0, k1, k2, v0, v1, v2, kc_ref, vc_ref, bias_ref, *refs, fixed=False):
    if fixed:
        par_ref, ones_ref, o_ref, flag_ref = refs
        ref_logit = par_ref[0, 0]
    else:
        (o_ref,) = refs
    q = q_ref[...]
    kb = jnp.concatenate([k0[...], k1[...], k2[...]], axis=1)
    vb = jnp.concatenate([v0[...], v1[...], v2[...]], axis=0)
    lo = _half_mask((q.shape[0], LANES))
    bad = []
    for j in range(NAT_HEADS // 2):
        sl = slice(j * LANES, (j + 1) * LANES)
        qp = q[:, sl]
        outs = []
        for e in range(2):
            qm = _select_half(qp, e, lo)
            s = jnp.dot(qm, kb[sl, :], preferred_element_type=F32) + bias_ref[0, 2 * j + e]
            sc = jnp.dot(qm, kc_ref[sl, :], preferred_element_type=F32)
            if fixed:
                o, u = _fixed_softmax_out([(s, vb[:, sl]), (sc - ref_logit, vc_ref[:, sl])], ones_ref, None)
                bad.append(u)
                outs.append(o)
            else:
                outs.append(_local_softmax_out([(s, vb[:, sl]), (sc, vc_ref[:, sl])], None))
        o_ref[:, sl] = jnp.where(lo, outs[0], outs[1]).astype(BF)
    if fixed:
        flag_ref[0] = jnp.concatenate(bad, axis=0)


def _nat_attention(q_tok, kt, tok_a, bias_tbl, n_batch, seq, n_ctx, par=None, ones=None):
    tq = NAT_ROWS_PER_STEP * GRID_W
    nq = seq // tq
    rows = seq // GRID_W
    ctx_blk = (n_batch * seq) // n_ctx
    q_col = 2
    k_row = KT_NK // 512
    v_col = A_NV // 512

    def wstart(i):
        return jnp.clip(NAT_ROWS_PER_STEP * i - NAT_WIN_ROWS // 2, 0, rows - NAT_KEY_ROWS) // NAT_ROWS_PER_STEP

    def kidx(j):
        return lambda b, i: (k_row, b * nq + wstart(i) + j)

    def vidx(j):
        return lambda b, i: (b * nq + wstart(i) + j, v_col)

    def variant(b, i):
        return (jnp.where(i == 0, 0, jnp.where(i == nq - 1, 2, 1)), 0, 0, 0)

    nk = NAT_KEY_ROWS * GRID_W
    in_specs = ([pl.BlockSpec((tq, 512), lambda b, i: (b * nq + i, q_col))]
                + [pl.BlockSpec((512, tq), kidx(j)) for j in range(3)]
                + [pl.BlockSpec((tq, 512), vidx(j)) for j in range(3)]
                + [pl.BlockSpec((512, n_ctx), lambda b, i: (k_row, ctx_blk + b)),
                   pl.BlockSpec((n_ctx, 512), lambda b, i: (ctx_blk + b, v_col)),
                   pl.BlockSpec((1, NAT_HEADS, tq, nk), variant)])
    args = [q_tok, kt, kt, kt, tok_a, tok_a, tok_a, kt, tok_a, bias_tbl]
    out_shape = jax.ShapeDtypeStruct((n_batch * seq, NAT_HEADS * HEAD_DIM), BF)
    out_specs = pl.BlockSpec((tq, 512), lambda b, i: (b * nq + i, 0))
    fixed = ones is not None
    if fixed:
        in_specs += [pl.BlockSpec(memory_space=pltpu.SMEM), pl.BlockSpec(ones.shape, lambda b, i: (0, 0))]
        args += [par, ones]
        out_shape = (out_shape, jax.ShapeDtypeStruct((n_batch * nq, NAT_HEADS, LANES), F32))
        out_specs = (out_specs, pl.BlockSpec((1, NAT_HEADS, LANES), lambda b, i: (b * nq + i, 0, 0)))
    return pl.pallas_call(
        functools.partial(_nat_kernel, fixed=fixed),
        out_shape=out_shape,
        grid=(n_batch, nq),
        in_specs=in_specs,
        out_specs=out_specs,
        compiler_params=_cparams(("parallel", "parallel")),
        name="nat_attn_fixed" if fixed else "nat_attn",
    )(*args)


def _dense_kernel(lam_ref, q_ref, kc_ref, vc_ref, *refs, n_heads, packed, diff, latent, lam_scale):
    if latent:
        k_ref, v_ref = refs[0], refs[1]
        refs = refs[2:]
    if diff:
        subln_ref, o_ref, qm_sc, m_sc, l_sc, acc_sc, s_sc, p_sc = refs
    else:
        o_ref, qm_sc, m_sc, l_sc, acc_sc, s_sc, p_sc = refs
    kt_step = pl.program_id(2) if latent else 0
    tq = q_ref.shape[0]

    def kv_slices(h):
        blk = h // 2 if packed else h
        ks = slice(blk * LANES, (blk + 1) * LANES)
        vs = ks if packed else slice((h // 2) * LANES, (h // 2 + 1) * LANES)
        return ks, vs

    def step(h, k_ref_, v_ref_):
        ks, vs = kv_slices(h)
        nk = k_ref_.shape[1]
        slot = h % 2
        s_sc[slot, :, :nk] = jnp.dot(qm_sc[h], k_ref_[ks, :], preferred_element_type=F32)
        for r in range(tq // DENSE_RB):
            rows = slice(r * DENSE_RB, (r + 1) * DENSE_RB)
            mx = s_sc[slot, rows, 0:LANES]
            for c in range(1, nk // LANES):
                mx = jnp.maximum(mx, s_sc[slot, rows, c * LANES:(c + 1) * LANES])
            m_old = m_sc[h, rows, :]
            m_new = jnp.maximum(m_old, jnp.max(mx, axis=-1, keepdims=True))
            alpha = jnp.exp2(m_old - m_new)
            lsum = None
            for c in range(nk // LANES):
                cols = slice(c * LANES, (c + 1) * LANES)
                p = jnp.exp2(s_sc[slot, rows, cols] - m_new)
                lsum = p if lsum is None else lsum + p
                p_sc[slot, rows, cols] = p.astype(BF)
            m_sc[h, rows, :] = m_new
            l_sc[h, rows, :] = alpha * l_sc[h, rows, :] + jnp.sum(lsum, axis=-1, keepdims=True)
            acc_sc[h, rows, :] = alpha * acc_sc[h, rows, :]
        acc_sc[h] += jnp.dot(p_sc[slot, :, :nk], v_ref_[:, vs], preferred_element_type=F32)

    @pl.when(kt_step == 0)
    def _():
        q = q_ref[...]
        lo = _half_mask((tq, LANES))
        for h in range(n_heads):
            if packed:
                qp = q[:, (h // 2) * LANES:(h // 2 + 1) * LANES]
                qm_sc[h] = _select_half(qp, h % 2, lo)
            else:
                qm_sc[h] = q[:, h * LANES:(h + 1) * LANES]
        m_sc[...] = jnp.full(m_sc.shape, NEG, F32)
        l_sc[...] = jnp.zeros(l_sc.shape, F32)
        acc_sc[...] = jnp.zeros(acc_sc.shape, F32)
        for h in range(n_heads):
            step(h, kc_ref, vc_ref)

    if latent:
        for h in range(n_heads):
            step(h, k_ref, v_ref)
        last = kt_step == pl.num_programs(2) - 1
    else:
        last = True

    def finish():
        lo = _half_mask((tq, LANES))
        if diff:
            lam = lam_ref[0, 0]
            for hv in range(n_heads // 2):
                y = (acc_sc[2 * hv] / l_sc[2 * hv]
                     - lam * (acc_sc[2 * hv + 1] / l_sc[2 * hv + 1]))
                ms = jnp.mean(y * y, axis=-1, keepdims=True)
                y = y * lax.rsqrt(ms + EPS) * subln_ref[...] * lam_scale
                o_ref[:, hv * LANES:(hv + 1) * LANES] = y.astype(BF)
        else:
            for hp in range(n_heads // 2):
                o0 = acc_sc[2 * hp] / l_sc[2 * hp]
                o1 = acc_sc[2 * hp + 1] / l_sc[2 * hp + 1]
                o_ref[:, hp * LANES:(hp + 1) * LANES] = jnp.where(lo, o0, o1).astype(BF)

    if latent:
        pl.when(last)(finish)
    else:
        finish()


def _dense_attention(lam, q_arr, q_col, q_w, k_arr, k_row, k_w, v_arr, v_col, v_w, subln,
                     n_batch, seq, n_ctx, *, n_heads, packed, diff, latent, lam_scale, name):
    n = n_batch * (seq if latent else n_ctx)
    ctx_blk = (n_batch * seq) // n_ctx
    out_w = v_w
    smem = pl.BlockSpec(memory_space=pltpu.SMEM)
    kern = functools.partial(_dense_kernel, n_heads=n_heads, packed=packed, diff=diff,
                             latent=latent, lam_scale=lam_scale)
    tq = DENSE_TQ_ONLINE if latent else n_ctx
    max_nk = DENSE_TK if latent else n_ctx
    scratch = [pltpu.VMEM((n_heads, tq, LANES), BF),
               pltpu.VMEM((n_heads, tq, LANES), F32),
               pltpu.VMEM((n_heads, tq, LANES), F32),
               pltpu.VMEM((n_heads, tq, LANES), F32),
               pltpu.VMEM((2, tq, max_nk), F32),
               pltpu.VMEM((2, tq, max_nk), BF)]
    out_shape = jax.ShapeDtypeStruct((n, out_w), BF)
    if latent:
        nq = seq // tq
        nk = seq // DENSE_TK
        grid = (n_batch, nq, nk)
        in_specs = [smem,
                    pl.BlockSpec((tq, q_w), lambda b, i, k: (b * nq + i, q_col)),
                    pl.BlockSpec((k_w, n_ctx), lambda b, i, k: (k_row, ctx_blk + b)),
                    pl.BlockSpec((n_ctx, v_w), lambda b, i, k: (ctx_blk + b, v_col)),
                    pl.BlockSpec((k_w, DENSE_TK), lambda b, i, k: (k_row, b * nk + k)),
                    pl.BlockSpec((DENSE_TK, v_w), lambda b, i, k: (b * nk + k, v_col))]
        args = [lam, q_arr, k_arr, v_arr, k_arr, v_arr]
        if diff:
            in_specs.append(pl.BlockSpec((1, LANES), lambda b, i, k: (0, 0)))
            args.append(subln)
        out_specs = pl.BlockSpec((tq, out_w), lambda b, i, k: (b * nq + i, 0))
        sem = ("parallel", "parallel", "arbitrary")
    else:
        grid = (n_batch,)
        in_specs = [smem,
                    pl.BlockSpec((tq, q_w), lambda b: (ctx_blk + b, q_col)),
                    pl.BlockSpec((k_w, n_ctx), lambda b: (k_row, ctx_blk + b)),
                    pl.BlockSpec((n_ctx, v_w), lambda b: (ctx_blk + b, v_col))]
        args = [lam, q_arr, k_arr, v_arr]
        if diff:
            in_specs.append(pl.BlockSpec((1, LANES), lambda b: (0, 0)))
            args.append(subln)
        out_specs = pl.BlockSpec((tq, out_w), lambda b: (b, 0))
        sem = ("parallel",)
    return pl.pallas_call(
        kern, out_shape=out_shape, grid=grid, in_specs=in_specs, out_specs=out_specs,
        scratch_shapes=scratch, compiler_params=_cparams(sem), name=name,
    )(*args)


SAFE_SUM_LOG2 = 100.0


def _dense_fixed_kernel(par_ref, q_ref, kc_ref, vc_ref, *refs, n_heads, packed, diff, latent, lam_scale):
    if latent:
        k_ref, v_ref = refs[0], refs[1]
        refs = refs[2:]
    ones_ref = refs[0]
    refs = refs[1:]
    if diff:
        subln_ref, o_ref, flag_ref, qm_sc, acc_sc, p_sc = refs
    else:
        o_ref, flag_ref, qm_sc, acc_sc, p_sc = refs
    kt_step = pl.program_id(2) if latent else 0
    tq = q_ref.shape[0]
    ref_logit = par_ref[0, 1]

    def step(h, k_ref_, v_ref_, first):
        blk = h // 2 if packed else h
        ks = slice(blk * LANES, (blk + 1) * LANES)
        vs = ks if packed else slice((h // 2) * LANES, (h // 2 + 1) * LANES)
        nk = k_ref_.shape[1]
        slot = h % 2
        for c in range(nk // DENSE_KC):
            cols = slice(c * DENSE_KC, (c + 1) * DENSE_KC)
            s = jnp.dot(qm_sc[h], k_ref_[ks, cols], preferred_element_type=F32)
            p_sc[slot, :, cols] = jnp.exp2(s - ref_logit).astype(BF)
        v_aug = jnp.concatenate([v_ref_[:, vs], ones_ref[:nk, :]], axis=1)
        pv = jnp.dot(p_sc[slot, :, :nk], v_aug, preferred_element_type=F32)
        if first:
            acc_sc[h] = pv
        else:
            acc_sc[h] += pv

    @pl.when(kt_step == 0)
    def _():
        q = q_ref[...]
        lo = _half_mask((tq, LANES))
        for h in range(n_heads):
            if packed:
                qp = q[:, (h // 2) * LANES:(h // 2 + 1) * LANES]
                qm_sc[h] = _select_half(qp, h % 2, lo)
            else:
                qm_sc[h] = q[:, h * LANES:(h + 1) * LANES]
        for h in range(n_heads):
            step(h, kc_ref, vc_ref, True)

    if latent:
        for h in range(n_heads):
            step(h, k_ref, v_ref, False)
        last = kt_step == pl.num_programs(2) - 1

    def finish():
        lo = _half_mask((tq, LANES))
        outs, bad = [], []
        for h in range(n_heads):
            a = acc_sc[h]
            l = a[:, LANES:]
            unsafe = jnp.logical_not((l > 2.0 ** -SAFE_SUM_LOG2) & (l < 2.0 ** SAFE_SUM_LOG2))
            bad.append(jnp.max(jnp.where(unsafe, 1.0, 0.0), axis=0, keepdims=True))
            outs.append(a[:, :LANES] / l)
        flag_ref[0] = jnp.concatenate(bad, axis=0)
        if diff:
            lam = par_ref[0, 0]
            for hv in range(n_heads // 2):
                y = outs[2 * hv] - lam * outs[2 * hv + 1]
                ms = jnp.mean(y * y, axis=-1, keepdims=True)
                y = y * lax.rsqrt(ms + EPS) * subln_ref[...] * lam_scale
                o_ref[:, hv * LANES:(hv + 1) * LANES] = y.astype(BF)
        else:
            for hp in range(n_heads // 2):
                o_ref[:, hp * LANES:(hp + 1) * LANES] = jnp.where(lo, outs[2 * hp], outs[2 * hp + 1]).astype(BF)

    if latent:
        pl.when(last)(finish)
    else:
        finish()


def _dense_fixed_attention(par, q_arr, q_col, q_w, k_arr, k_row, k_w, v_arr, v_col, v_w, subln, ones,
                           n_batch, seq, n_ctx, *, n_heads, packed, diff, latent, lam_scale, name):
    n = n_batch * (seq if latent else n_ctx)
    ctx_blk = (n_batch * seq) // n_ctx
    smem = pl.BlockSpec(memory_space=pltpu.SMEM)
    kern = functools.partial(_dense_fixed_kernel, n_heads=n_heads, packed=packed, diff=diff,
                             latent=latent, lam_scale=lam_scale)
    tq = DENSE_TQ if latent else n_ctx
    max_nk = DENSE_TK if latent else n_ctx
    scratch = [pltpu.VMEM((n_heads, tq, LANES), BF),
               pltpu.VMEM((n_heads, tq, 2 * LANES), F32),
               pltpu.VMEM((2, tq, max_nk), BF)]
    if latent:
        nq = seq // tq
        nk = seq // DENSE_TK
        grid = (n_batch, nq, nk)
        ix = lambda f: (lambda b, i, k: f(b, i, k))
        q_ix = ix(lambda b, i, k: (b * nq + i, q_col))
        in_specs = [smem,
                    pl.BlockSpec((tq, q_w), q_ix),
                    pl.BlockSpec((k_w, n_ctx), ix(lambda b, i, k: (k_row, ctx_blk + b))),
                    pl.BlockSpec((n_ctx, v_w), ix(lambda b, i, k: (ctx_blk + b, v_col))),
                    pl.BlockSpec((k_w, DENSE_TK), ix(lambda b, i, k: (k_row, b * nk + k))),
                    pl.BlockSpec((DENSE_TK, v_w), ix(lambda b, i, k: (b * nk + k, v_col))),
                    pl.BlockSpec((DENSE_TK, LANES), ix(lambda b, i, k: (0, 0)))]
        args = [par, q_arr, k_arr, v_arr, k_arr, v_arr, ones]
        const_ix = ix(lambda b, i, k: (0, 0))
        out_specs = (pl.BlockSpec((tq, v_w), ix(lambda b, i, k: (b * nq + i, 0))),
                     pl.BlockSpec((1, n_heads, LANES), ix(lambda b, i, k: (b * nq + i, 0, 0))))
        n_flag = n_batch * nq
        sem = ("parallel", "parallel", "arbitrary")
    else:
        grid = (n_batch,)
        in_specs = [smem,
                    pl.BlockSpec((tq, q_w), lambda b: (ctx_blk + b, q_col)),
                    pl.BlockSpec((k_w, n_ctx), lambda b: (k_row, ctx_blk + b)),
                    pl.BlockSpec((n_ctx, v_w), lambda b: (ctx_blk + b, v_col)),
                    pl.BlockSpec((DENSE_TK, LANES), lambda b: (0, 0))]
        args = [par, q_arr, k_arr, v_arr, ones]
        const_ix = lambda b: (0, 0)
        out_specs = (pl.BlockSpec((tq, v_w), lambda b: (b, 0)),
                     pl.BlockSpec((1, n_heads, LANES), lambda b: (b, 0, 0)))
        n_flag = n_batch
        sem = ("parallel",)
    if diff:
        in_specs.append(pl.BlockSpec((1, LANES), const_ix))
        args.append(subln)
    return pl.pallas_call(
        kern,
        out_shape=(jax.ShapeDtypeStruct((n, v_w), BF), jax.ShapeDtypeStruct((n_flag, n_heads, LANES), F32)),
        grid=grid, in_specs=in_specs, out_specs=out_specs,
        scratch_shapes=scratch, compiler_params=_cparams(sem), name=name + "_fixed",
    )(*args)


def _merge_kernel(x_ref, mod_ref, g2_ref, *refs, n_lat_tiles, has_ctx):
    ys = refs[:N_BRANCH]
    refs = refs[N_BRANCH:]
    if has_ctx:
        ycs = refs[:N_BRANCH]
        refs = refs[N_BRANCH:]
        is_ctx = pl.program_id(0) >= n_lat_tiles
    gts = refs[:N_BRANCH]
    wb_ref, wo_ref, rw_ref, xo_ref, h2_ref, sc_ref = refs[N_BRANCH:]
    m = mod_ref[0]
    mix = None
    for n_ in range(N_BRANCH):
        y = ys[n_][...]
        if has_ctx:
            y = jnp.where(is_ctx, ycs[n_][...], y)
        yb = jnp.dot(y, wb_ref[n_], preferred_element_type=F32)
        t = jax.nn.sigmoid(gts[n_][...].astype(F32)) * yb
        mix = t if mix is None else mix + t
    att = jnp.dot(mix.astype(BF), wo_ref[...], preferred_element_type=F32)
    xn = x_ref[...] + m[2:3] * att
    xo_ref[...] = xn
    h2 = _norm_mod(xn, g2_ref[...], m[4:5], m[3:4])
    h2_ref[...] = h2
    logits = jnp.dot(h2, rw_ref[...], preferred_element_type=F32, precision=lax.Precision.HIGHEST)
    sc_ref[...] = jax.nn.sigmoid(logits)


def _merge(xa, mod3, g2, ys, ys_ctx, tok_a, wb, wo, rw, n_rows, group_of_tile):
    d = xa.shape[1]
    n_lat_tiles = ys[0].shape[0] // TM
    has_ctx = ys_ctx is not None
    row = lambda w, c: pl.BlockSpec((TM, w), lambda i, c=c: (i, c))
    lat_row = pl.BlockSpec((TM, BRANCH_W), lambda i: (jnp.minimum(i, n_lat_tiles - 1), 0))
    in_specs = ([row(d, 0),
                 pl.BlockSpec((1, 8, d), lambda i: (group_of_tile(i), 0, 0)),
                 pl.BlockSpec((1, d), lambda i: (0, 0))]
                + [lat_row for _ in range(N_BRANCH)]
                + ([pl.BlockSpec((TM, BRANCH_W), lambda i: (0, 0)) for _ in range(N_BRANCH)] if has_ctx else [])
                + [row(d, c) for c in range(N_BRANCH)]
                + [pl.BlockSpec(wb.shape, lambda i: (0, 0, 0)),
                   pl.BlockSpec(wo.shape, lambda i: (0, 0)),
                   pl.BlockSpec(rw.shape, lambda i: (0, 0))])
    ys = list(ys) + (list(ys_ctx) if has_ctx else [])
    return pl.pallas_call(
        functools.partial(_merge_kernel, n_lat_tiles=n_lat_tiles, has_ctx=has_ctx),
        out_shape=(jax.ShapeDtypeStruct((n_rows, d), F32),
                   jax.ShapeDtypeStruct((n_rows, d), F32),
                   jax.ShapeDtypeStruct((n_rows, LANES), F32)),
        grid=(n_rows // TM,),
        in_specs=in_specs,
        out_specs=(row(d, 0), row(d, 0), row(LANES, 0)),
        compiler_params=_cparams(("parallel",)),
        name="merge",
    )(xa, mod3, g2, *ys, tok_a, tok_a, tok_a, tok_a, wb, wo, rw)


def _expert_kernel(blk_e_ref, src_ref, dst_ref, h2_hbm, w_ref, w1_ref, w3_ref, w2_ref, out_hbm,
                   xbuf, ybuf, sem_in, sem_out):
    i = pl.program_id(0)
    n_steps = pl.num_programs(0)
    slot = i % 2

    def start_gather(blk, s):
        def body(r, c):
            t = src_ref[blk * MOE_BLOCK + r]
            pltpu.make_async_copy(h2_hbm.at[pl.ds(t, 1)], xbuf.at[s, pl.ds(r, 1)], sem_in.at[s]).start()
            return c
        lax.fori_loop(0, MOE_BLOCK, body, 0, unroll=8)

    def wait_gather(s):
        pltpu.make_async_copy(h2_hbm.at[pl.ds(0, MOE_BLOCK)], xbuf.at[s], sem_in.at[s]).wait()

    def start_scatter(blk, s):
        def body(r, c):
            t = dst_ref[blk * MOE_BLOCK + r]
            pltpu.make_async_copy(ybuf.at[s, pl.ds(r, 1)], out_hbm.at[pl.ds(t, 1)], sem_out.at[s]).start()
            return c
        lax.fori_loop(0, MOE_BLOCK, body, 0, unroll=8)

    def wait_scatter(s):
        pltpu.make_async_copy(ybuf.at[s], out_hbm.at[pl.ds(0, MOE_BLOCK)], sem_out.at[s]).wait()

    @pl.when(i == 0)
    def _():
        start_gather(0, 0)

    @pl.when(i + 1 < n_steps)
    def _():
        start_gather(i + 1, 1 - slot)

    wait_gather(slot)

    @pl.when(i >= 2)
    def _():
        wait_scatter(slot)

    xb = xbuf[slot].astype(BF)
    a = jnp.dot(xb, w1_ref[0], preferred_element_type=F32)
    b = jnp.dot(xb, w3_ref[0], preferred_element_type=F32)
    hmid = (a * jax.nn.sigmoid(a) * b).astype(BF)
    ybuf[slot] = jnp.dot(hmid, w2_ref[0], preferred_element_type=F32) * w_ref[...]
    start_scatter(i, slot)

    @pl.when(i == n_steps - 1)
    def _():
        wait_scatter(slot)

        @pl.when(n_steps >= 2)
        def _():
            wait_scatter(1 - slot)


def _experts(blk_e, src, dst, h2, wcol, w1, w3, w2):
    cap = src.shape[0]
    d = h2.shape[1]
    n_blk = cap // MOE_BLOCK
    wspec = lambda shp: pl.BlockSpec((1,) + shp, lambda i, be, s, t: (be[i], 0, 0))
    grid_spec = pltpu.PrefetchScalarGridSpec(
        num_scalar_prefetch=3,
        grid=(n_blk,),
        in_specs=[pl.BlockSpec(memory_space=pl.ANY),
                  pl.BlockSpec((MOE_BLOCK, 1), lambda i, be, s, t: (i, 0)),
                  wspec((d, D_EXPERT)), wspec((d, D_EXPERT)), wspec((D_EXPERT, d))],
        out_specs=pl.BlockSpec(memory_space=pl.ANY),
        scratch_shapes=[pltpu.VMEM((2, MOE_BLOCK, d), F32),
                        pltpu.VMEM((2, MOE_BLOCK, d), F32),
                        pltpu.SemaphoreType.DMA((2,)),
                        pltpu.SemaphoreType.DMA((2,))],
    )
    return pl.pallas_call(
        _expert_kernel,
        out_shape=jax.ShapeDtypeStruct((cap, d), F32),
        grid_spec=grid_spec,
        compiler_params=_cparams(("arbitrary",)),
        name="moe_experts",
    )(blk_e, src, dst, h2, wcol, w1, w3, w2)


def _route(scores, router_b, n):
    per_group = N_EXPERTS // N_GROUPS
    st = scores.T
    biased = st + router_b.astype(F32)[:, None]

    def top2(v, axis):
        pos = lax.broadcasted_iota(jnp.int32, v.shape, axis)
        i0 = jnp.argmax(v, axis=axis).astype(jnp.int32)
        v0 = jnp.max(v, axis=axis)
        rest = jnp.where(pos == jnp.expand_dims(i0, axis), -jnp.inf, v)
        i1 = jnp.argmax(rest, axis=axis).astype(jnp.int32)
        v1 = jnp.max(rest, axis=axis)
        return (v0, v1), (i0, i1)

    (g0, g1), _ = top2(biased.reshape(N_GROUPS, per_group, n), 1)
    group = jnp.argmax(g0 + g1, axis=0)
    expert = jnp.arange(N_EXPERTS, dtype=jnp.int32)[:, None]
    in_group = (expert // per_group) == group[None, :]
    _, (e0, e1) = top2(jnp.where(in_group, biased, -jnp.inf), 0)
    oh0 = (expert == e0[None, :]).astype(F32)
    oh1 = (expert == e1[None, :]).astype(F32)
    w0 = (oh0 * st).sum(0)
    w1 = (oh1 * st).sum(0)
    wts = jnp.stack([w0, w1], axis=-1) / (w0 + w1)[:, None]
    n_asg = n * TOP_K
    cnt = (oh0 + oh1).reshape(N_EXPERTS, n // MOE_BLOCK, MOE_BLOCK)
    tri = jnp.triu(jnp.ones((MOE_BLOCK, MOE_BLOCK), F32), 1)
    within = jnp.einsum('ebj,ji->ebi', cnt, tri)
    blk_tot = cnt.sum(axis=-1)
    nb = blk_tot.shape[1]
    blk_off = jnp.einsum('eb,bc->ec', blk_tot, jnp.triu(jnp.ones((nb, nb), F32), 1),
                         precision=lax.Precision.HIGHEST)
    prefix = (within + blk_off[:, :, None]).reshape(N_EXPERTS, n)
    counts = blk_tot.sum(axis=-1).astype(jnp.int32)
    padded = (counts + MOE_BLOCK - 1) // MOE_BLOCK * MOE_BLOCK
    pad_end = jnp.cumsum(padded)
    pad_start = pad_end - padded
    slot = prefix + pad_start.astype(F32)[:, None]
    dest = jnp.stack([(oh0 * slot).sum(0), (oh1 * slot).sum(0)], axis=-1).reshape(n_asg)
    n_blk = (n_asg + N_EXPERTS * (MOE_BLOCK - 1) + MOE_BLOCK - 1) // MOE_BLOCK
    cap = n_blk * MOE_BLOCK
    blk_start = jnp.arange(n_blk, dtype=jnp.int32) * MOE_BLOCK
    blk_e = jnp.minimum((blk_start[:, None] >= pad_end[None, :]).sum(-1), N_EXPERTS - 1).astype(jnp.int32)
    pad_lo = jnp.concatenate([pad_start + counts, pad_end[-1:]]).astype(jnp.int32)
    pad_hi = jnp.concatenate([pad_end, jnp.full((1,), cap)]).astype(jnp.int32)
    n_used = (pad_end[-1:] // MOE_BLOCK).astype(jnp.int32)
    return blk_e, dest.astype(jnp.int32), wts.astype(F32), pad_lo, pad_hi, n_used, cap


def _dispatch_kernel(dest_ref, pad_lo_ref, pad_hi_ref, h2_ref, xs_hbm, zrow, sem, zsem):
    i = pl.program_id(0)

    @pl.when(i == 0)
    def _():
        zrow[...] = jnp.zeros(zrow.shape, F32)
        for e in range(N_EXPERTS + 1):
            lo, hi = pad_lo_ref[e], pad_hi_ref[e]

            def zero_row(r, c):
                pltpu.make_async_copy(zrow.at[pl.ds(0, 1)], xs_hbm.at[pl.ds(r, 1)], zsem).start()
                return c

            def zero_wait(r, c):
                pltpu.make_async_copy(zrow.at[pl.ds(0, 1)], xs_hbm.at[pl.ds(r, 1)], zsem).wait()
                return c

            lax.fori_loop(lo, hi, zero_row, 0)
            lax.fori_loop(lo, hi, zero_wait, 0)

    tm = h2_ref.shape[0]

    def copy_row(r, c):
        a = (i * tm + r) * TOP_K
        for k in range(TOP_K):
            pltpu.make_async_copy(h2_ref.at[pl.ds(r, 1)], xs_hbm.at[pl.ds(dest_ref[a + k], 1)], sem).start()
        return c

    lax.fori_loop(0, tm, copy_row, 0, unroll=4)
    for k in range(TOP_K):
        pltpu.make_async_copy(h2_ref, xs_hbm.at[pl.ds(0, tm)], sem).wait()


def _dispatch(dest, pad_lo, pad_hi, h2, cap):
    n, d = h2.shape
    grid_spec = pltpu.PrefetchScalarGridSpec(
        num_scalar_prefetch=3,
        grid=(n // TM,),
        in_specs=[pl.BlockSpec((TM, d), lambda i, de, lo, hi: (i, 0))],
        out_specs=pl.BlockSpec(memory_space=pl.ANY),
        scratch_shapes=[pltpu.VMEM((8, d), F32), pltpu.SemaphoreType.DMA, pltpu.SemaphoreType.DMA],
    )
    return pl.pallas_call(
        _dispatch_kernel,
        out_shape=jax.ShapeDtypeStruct((cap, d), F32),
        grid_spec=grid_spec,
        compiler_params=_cparams(("arbitrary",)),
        name="moe_dispatch",
    )(dest, pad_lo, pad_hi, h2)


def _grouped_ffn_kernel(blk_e_ref, n_used_ref, x_ref, w1_ref, w3_ref, w2_ref, o_ref):
    i = pl.program_id(0)

    @pl.when(i < n_used_ref[0])
    def _():
        xb = x_ref[...].astype(BF)
        a = jnp.dot(xb, w1_ref[0], preferred_element_type=F32)
        b = jnp.dot(xb, w3_ref[0], preferred_element_type=F32)
        hmid = (a * jax.nn.sigmoid(a) * b).astype(BF)
        o_ref[...] = jnp.dot(hmid, w2_ref[0], preferred_element_type=F32)

    @pl.when(i >= n_used_ref[0])
    def _():
        o_ref[...] = jnp.zeros(o_ref.shape, F32)


def _grouped_ffn(blk_e, n_used, xs, w1, w3, w2):
    cap, d = xs.shape
    wspec = lambda shp: pl.BlockSpec((1,) + shp, lambda i, be, nu: (be[i], 0, 0))
    grid_spec = pltpu.PrefetchScalarGridSpec(
        num_scalar_prefetch=2,
        grid=(cap // MOE_BLOCK,),
        in_specs=[pl.BlockSpec((MOE_BLOCK, d), lambda i, be, nu: (jnp.minimum(i, nu[0] - 1), 0)),
                  wspec((d, D_EXPERT)), wspec((d, D_EXPERT)), wspec((D_EXPERT, d))],
        out_specs=pl.BlockSpec((MOE_BLOCK, d), lambda i, be, nu: (i, 0)),
    )
    return pl.pallas_call(
        _grouped_ffn_kernel,
        out_shape=jax.ShapeDtypeStruct((cap, d), F32),
        grid_spec=grid_spec,
        compiler_params=_cparams(("arbitrary",)),
        name="moe_ffn",
    )(blk_e, n_used, xs, w1, w3, w2)


def _gather_combine_kernel(dest_ref, x_ref, mod_ref, w_ref, ys_hbm, o_ref, buf, sem):
    i = pl.program_id(0)
    n_steps = pl.num_programs(0)
    slot = i % 2
    tm = x_ref.shape[0]

    def start_gather(tile, s):
        def body(r, c):
            a = (tile * tm + r) * TOP_K
            for k in range(TOP_K):
                pltpu.make_async_copy(ys_hbm.at[pl.ds(dest_ref[a + k], 1)], buf.at[s, k, pl.ds(r, 1)],
                                      sem.at[s]).start()
            return c
        lax.fori_loop(0, tm, body, 0, unroll=4)

    @pl.when(i == 0)
    def _():
        start_gather(0, 0)

    @pl.when(i + 1 < n_steps)
    def _():
        start_gather(i + 1, 1 - slot)

    for k in range(TOP_K):
        pltpu.make_async_copy(ys_hbm.at[pl.ds(0, tm)], buf.at[slot, k], sem.at[slot]).wait()
    w = w_ref[...]
    f = w[:, 0:1] * buf[slot, 0] + w[:, 1:2] * buf[slot, 1]
    o_ref[...] = x_ref[...] + mod_ref[0][5:6] * f


def _gather_combine(dest, xn, mod3, wts, ys, n_rows, group_of_tile):
    d = xn.shape[1]
    grid_spec = pltpu.PrefetchScalarGridSpec(
        num_scalar_prefetch=1,
        grid=(n_rows // TM,),
        in_specs=[pl.BlockSpec((TM, d), lambda i, de: (i, 0)),
                  pl.BlockSpec((1, 8, d), lambda i, de: (group_of_tile(i), 0, 0)),
                  pl.BlockSpec((TM, TOP_K), lambda i, de: (i, 0)),
                  pl.BlockSpec(memory_space=pl.ANY)],
        out_specs=pl.BlockSpec((TM, d), lambda i, de: (i, 0)),
        scratch_shapes=[pltpu.VMEM((2, TOP_K, TM, d), F32), pltpu.SemaphoreType.DMA((2,))],
    )
    return pl.pallas_call(
        _gather_combine_kernel,
        out_shape=jax.ShapeDtypeStruct((n_rows, d), F32),
        grid_spec=grid_spec,
        compiler_params=_cparams(("arbitrary",)),
        name="moe_combine",
    )(dest, xn, mod3, wts, ys)


def _combine_kernel(x_ref, mod_ref, f0_ref, f1_ref, o_ref):
    o_ref[...] = x_ref[...] + mod_ref[0][5:6] * (f0_ref[...] + f1_ref[...])


def _combine(xn, mod3, f, n_rows, group_of_tile):
    d = xn.shape[1]
    nt = n_rows // TM
    return pl.pallas_call(
        _combine_kernel,
        out_shape=jax.ShapeDtypeStruct((n_rows, d), F32),
        grid=(nt,),
        in_specs=[pl.BlockSpec((TM, d), lambda i: (i, 0)),
                  pl.BlockSpec((1, 8, d), lambda i: (group_of_tile(i), 0, 0)),
                  pl.BlockSpec((TM, d), lambda i: (i, 0)),
                  pl.BlockSpec((TM, d), lambda i: (nt + i, 0))],
        out_specs=pl.BlockSpec((TM, d), lambda i: (i, 0)),
        compiler_params=_cparams(("parallel",)),
        name="moe_combine",
    )(xn, mod3, f, f)


def _rope_tables(seq, dim, pad):
    t = jnp.arange(seq)
    rows = (t // GRID_W).astype(F32)
    cols = (t % GRID_W).astype(F32)
    quarter = dim // 4
    inv_freq = jnp.exp(-math.log(ROPE_BASE) * jnp.arange(quarter, dtype=F32) / quarter)
    ang = jnp.concatenate([inv_freq[:, None] * rows[None, :], inv_freq[:, None] * cols[None, :]], axis=0)
    cos = jnp.concatenate([jnp.cos(ang), jnp.ones((dim // 2, pad), F32)], axis=1)
    sin = jnp.concatenate([jnp.sin(ang), jnp.zeros((dim // 2, pad), F32)], axis=1)
    return cos, sin


def _win_mask_table(seq):
    nkb = seq // LANES
    nq = seq // WIN_TQ
    tabs = []
    for i in (0, 1, nq - 1):
        t = i * WIN_TQ + np.arange(WIN_TQ)[:, None]
        blk = np.clip(2 * i - 1 + np.arange(4), 0, nkb - 1)
        want = 2 * i - 1 + np.arange(4)
        s = (blk[:, None] * LANES + np.arange(LANES)[None, :]).reshape(-1)[None, :]
        ok = (np.abs(t - s) <= WIN_RADIUS) & np.repeat(blk == want, LANES)[None, :]
        tabs.append(np.where(ok, 0.0, NEG))
    return jnp.asarray(np.stack(tabs), F32)


def _nat_bias_table(rpb, seq, shift):
    rows = seq // GRID_W
    nq = rows // NAT_ROWS_PER_STEP
    wc = NAT_WIN_COLS
    col = np.arange(GRID_W)
    col_start = np.clip(col - wc // 2, 0, GRID_W - wc)
    col_ok = (col[None, :] >= col_start[:, None]) & (col[None, :] < col_start[:, None] + wc)
    d_col = np.clip(col[None, :] - col[:, None] + (wc - 1), 0, 2 * wc - 2)
    sel_r, oks = [], []
    for i in (0, 1, nq - 1):
        r0 = NAT_ROWS_PER_STEP * i
        ws = np.clip(r0 - NAT_WIN_ROWS // 2, 0, rows - NAT_KEY_ROWS)
        r = r0 + np.arange(NAT_ROWS_PER_STEP)
        rs = np.clip(r - NAT_WIN_ROWS // 2, 0, rows - NAT_WIN_ROWS)
        krow = ws + np.arange(NAT_KEY_ROWS)
        row_ok = (krow[None, :] >= rs[:, None]) & (krow[None, :] < rs[:, None] + NAT_WIN_ROWS)
        d_row = np.clip(krow[None, :] - r[:, None] + (NAT_WIN_ROWS - 1), 0, 2 * NAT_WIN_ROWS - 2)
        oks.append(row_ok[:, None, :, None] & col_ok[None, :, None, :])
        sel_r.append(d_row[:, :, None] == np.arange(2 * NAT_WIN_ROWS - 1)[None, None, :])
    sel_r = jnp.asarray(np.stack(sel_r), F32)
    sel_c = jnp.asarray(d_col[:, :, None] == np.arange(2 * wc - 1)[None, None, :], F32)
    cols = jnp.einsum('hrc,vkc->hrvk', rpb.astype(F32), sel_c, precision=lax.Precision.HIGHEST)
    bias = jnp.einsum('tuar,hrvk->thuvak', sel_r, cols, precision=lax.Precision.HIGHEST)
    bias = jnp.where(jnp.asarray(np.stack(oks))[:, None], bias - shift, NEG)
    return bias.reshape(3, rpb.shape[0], NAT_ROWS_PER_STEP * GRID_W, NAT_KEY_ROWS * GRID_W)


def _bcast_rows(v, reps=1):
    return jnp.tile(jnp.broadcast_to(v.astype(F32)[:, None], (v.shape[0], LANES)), (reps, 1))


def _layer_params(l, p):
    w = p['w_in'][l]
    sizes = (512, 128, 128, 512, 512, 512, 512, 512, 512, 256, 160, 4096)
    offs = np.concatenate([[0], np.cumsum(sizes)])
    seg = lambda k: w[:, offs[k]:offs[k + 1]]
    wq, wk, wv, dq, dk, dv, nq, nk, nv, mqa, mkva, gates = [seg(k) for k in range(12)]
    d = w.shape[0]
    dup = lambda m: jnp.concatenate([m[:, :64], m[:, :64], m[:, 64:], m[:, 64:]], axis=1)
    mkva_p = jnp.concatenate([mkva, jnp.zeros((d, 256 - mkva.shape[1]), F32)], axis=1)
    w_a = jnp.concatenate([gates, dv, nv, mqa, mkva_p, dup(wv)], axis=1).astype(BF)
    w_bt = jnp.concatenate([wq, dq, nq, dk, nk, dup(wk)], axis=1).T.astype(BF)
    scale = HEAD_DIM ** -0.5 * LOG2E
    gain_b = jnp.concatenate([
        _bcast_rows(p['win_q_norm'][l] * scale, 8), _bcast_rows(p['dif_q_norm'][l] * scale, 8),
        _bcast_rows(p['nat_q_norm'][l] * scale, 8), _bcast_rows(p['dif_k_norm'][l], 8),
        _bcast_rows(p['nat_k_norm'][l], 8), _bcast_rows(p['win_k_norm'][l], 4)], axis=0)
    wkv = p['mla_wkv_b'][l].reshape(MLA_KV_LORA, MLA_HEADS, MLA_NOPE + MLA_V)
    wk_t = wkv[:, :, :MLA_NOPE].reshape(MLA_KV_LORA, -1).T.astype(BF)
    wv_m = wkv[:, :, MLA_NOPE:].reshape(MLA_KV_LORA, -1).astype(BF)
    def logit_bound(gq, gk, dim):
        return dim * jnp.max(jnp.abs(gq)) * jnp.max(jnp.abs(gk)) * (1.0 + 2.0 ** -7)

    lam_f = p['dif_lambda'][l].astype(F32)
    lam_init = 0.8 - 0.6 * math.exp(-0.3 * l)
    lam = jnp.exp(jnp.sum(lam_f[0] * lam_f[1])) - jnp.exp(jnp.sum(lam_f[2] * lam_f[3])) + lam_init
    return dict(
        w_a=w_a, w_bt=w_bt, gain_b=gain_b,
        g1=p['norm1_g'][l].reshape(1, d), g2=p['norm2_g'][l].reshape(1, d),
        sink=(p['win_sink'][l].astype(F32) * LOG2E).reshape(1, WIN_HEADS),
        gqa=p['mla_q_a_norm'][l].reshape(1, -1), gkva=p['mla_kv_a_norm'][l].reshape(1, -1),
        wq_t=p['mla_wq_b'][l].T.astype(BF), wk_t=wk_t, wv_m=wv_m,
        gq=_bcast_rows(p['mla_q_norm'][l] * (MLA_QK ** -0.5 * LOG2E)), gk=_bcast_rows(p['mla_k_norm'][l]),
        lam=lam.reshape(1, 1).astype(F32), lam_scale=1.0 - lam_init,
        m_win=logit_bound(p['win_q_norm'][l] * scale, p['win_k_norm'][l], HEAD_DIM),
        m_dif=logit_bound(p['dif_q_norm'][l] * scale, p['dif_k_norm'][l], HEAD_DIM),
        m_nat=logit_bound(p['nat_q_norm'][l] * scale, p['nat_k_norm'][l], HEAD_DIM),
        m_mla=logit_bound(p['mla_q_norm'][l] * (MLA_QK ** -0.5 * LOG2E), p['mla_k_norm'][l], MLA_QK),
        subln=p['dif_subln'][l].astype(F32).reshape(1, DIF_V_DIM),
        wb=p['w_branch'][l].astype(BF), wo=p['w_out'][l].astype(BF),
        w1=p['moe_w1'][l].astype(BF), w3=p['moe_w3'][l].astype(BF), w2=p['moe_w2'][l].astype(BF),
    )


def kernel(x, c, ctx, c_ctx, ada_w, ada_b, norm1_g, norm2_g, w_in, win_q_norm, win_k_norm, win_sink,
           dif_q_norm, dif_k_norm, dif_lambda, dif_subln, nat_q_norm, nat_k_norm, nat_rpb,
           mla_q_a_norm, mla_wq_b, mla_kv_a_norm, mla_wkv_b, mla_q_norm, mla_k_norm,
           w_branch, w_out, router_w, router_b, moe_w1, moe_w3, moe_w2):
    p = dict(norm1_g=norm1_g, norm2_g=norm2_g, w_in=w_in, win_q_norm=win_q_norm, win_k_norm=win_k_norm,
             win_sink=win_sink, dif_q_norm=dif_q_norm, dif_k_norm=dif_k_norm, dif_lambda=dif_lambda,
             dif_subln=dif_subln, nat_q_norm=nat_q_norm, nat_k_norm=nat_k_norm,
             mla_q_a_norm=mla_q_a_norm, mla_wq_b=mla_wq_b, mla_kv_a_norm=mla_kv_a_norm,
             mla_wkv_b=mla_wkv_b, mla_q_norm=mla_q_norm, mla_k_norm=mla_k_norm,
             w_branch=w_branch, w_out=w_out, moe_w1=moe_w1, moe_w3=moe_w3, moe_w2=moe_w2)
    n_batch, seq, d = x.shape
    n_ctx = ctx.shape[1]
    depth = ada_w.shape[0]
    n_lat = n_batch * seq
    n_all = n_lat + n_batch * n_ctx
    assert seq % DENSE_TK == 0 and seq % DENSE_TQ == 0 and (n_batch * n_ctx) == TM and seq % TM == 0
    tiles_per_batch = seq // TM
    group_of_tile = lambda i: jnp.minimum(i // tiles_per_batch, n_batch)
    pos_of_tile = lambda i: jnp.where(i < n_batch * tiles_per_batch, i % tiles_per_batch, tiles_per_batch)

    cc = jnp.concatenate([c, c_ctx[None, :], jnp.zeros((8 - n_batch - 1, d), F32)], axis=0)
    mod = _modulation(cc, ada_w, ada_b)
    mod = mod[:, :n_batch + 1].reshape(depth, n_batch + 1, 6, d)
    mod = jnp.pad(mod, ((0, 0), (0, 0), (0, 2), (0, 0)))

    cos_h, sin_h = _rope_tables(seq, HEAD_DIM, TM)
    cos_m, sin_m = _rope_tables(seq, MLA_ROPE, TM)
    win_mask = _win_mask_table(seq)
    rw = jnp.pad(router_w.astype(F32), ((0, 0), (0, LANES - N_EXPERTS)))

    xa = jnp.concatenate([x.reshape(n_lat, d), ctx.reshape(n_batch * n_ctx, d)], axis=0)
    ones = jnp.ones((DENSE_TK, LANES), BF)

    def dense(lam, logit_bound, q_arr, q_col, q_w, k_arr, k_row, k_w, v_arr, v_col, v_w, subln, **kw):
        operands = (q_arr, q_col, q_w, k_arr, k_row, k_w, v_arr, v_col, v_w, subln)
        par = jnp.concatenate([lam, logit_bound.reshape(1, 1).astype(F32)], axis=1)
        y, flag = _dense_fixed_attention(par, *operands, ones, n_batch, seq, n_ctx, **kw)
        return lax.cond(jnp.max(flag) > 0.0,
                        lambda: _dense_attention(lam, *operands, n_batch, seq, n_ctx, **kw),
                        lambda: y)
    for l in range(depth):
        lp = _layer_params(l, p)
        want_ctx = l < depth - 1
        mod3 = mod[l]
        tok_a = _in_proj_a(xa, mod3, lp['g1'], lp['w_a'], group_of_tile)
        q_tok, kt = _in_proj_b(xa, mod3, lp['g1'], lp['w_bt'], lp['gain_b'], cos_h, sin_h,
                               group_of_tile, pos_of_tile)
        mq, mkt, mv = _mla_prep(tok_a, lp['gqa'], lp['gkva'], lp['wq_t'], lp['wk_t'], lp['wv_m'],
                                lp['gq'], lp['gk'], cos_m, sin_m, pos_of_tile)
        rpb2 = nat_rpb[l].astype(F32) * LOG2E
        m_nl = lp['m_nat'] + jnp.maximum(jnp.max(rpb2), 0.0)
        nat_bias_rel = _nat_bias_table(rpb2, seq, m_nl)

        def branches(latent):
            win_args = (q_tok, kt, tok_a)
            if latent:
                m_win = lp['m_win']
                sink_rel = jnp.concatenate([lp['sink'] - m_win, jnp.full((1, 8), m_win, F32)], axis=1)
                y_win, flag = _win_attention(sink_rel, *win_args, win_mask - m_win, n_batch, seq, n_ctx, True,
                                             ones=ones)
                y_win = lax.cond(jnp.max(flag) > 0.0,
                                 lambda: _win_attention(lp['sink'], *win_args, win_mask, n_batch, seq, n_ctx, True),
                                 lambda: y_win)
            else:
                y_win = _win_attention(lp['sink'], *win_args, win_mask, n_batch, seq, n_ctx, False)
            y_dif = dense(lp['lam'], lp['m_dif'], q_tok, 1, 512, kt, KT_DK // 512, 512, tok_a, A_DV // 512, 512,
                          lp['subln'], n_heads=2 * DIF_HEADS, packed=True, diff=True, latent=latent,
                          lam_scale=lp['lam_scale'], name="dif_attn" if latent else "dif_attn_ctx")
            if latent:
                y_nat, flag = _nat_attention(q_tok, kt, tok_a, nat_bias_rel, n_batch, seq, n_ctx,
                                             par=m_nl.reshape(1, 1).astype(F32), ones=ones)
                y_nat = lax.cond(jnp.max(flag) > 0.0,
                                 lambda: _nat_attention(q_tok, kt, tok_a, _nat_bias_table(rpb2, seq, 0.0),
                                                        n_batch, seq, n_ctx),
                                 lambda: y_nat)
            else:
                y_nat = dense(lp['lam'], lp['m_nat'], q_tok, 2, 512, kt, KT_NK // 512, 512, tok_a, A_NV // 512, 512,
                              None, n_heads=NAT_HEADS, packed=True, diff=False, latent=False,
                              lam_scale=1.0, name="nat_attn_ctx")
            y_mla = dense(lp['lam'], lp['m_mla'], mq, 0, 1024, mkt, 0, 1024, mv, 0, 512, None,
                          n_heads=MLA_HEADS, packed=False, diff=False, latent=latent, lam_scale=1.0,
                          name="mla_attn" if latent else "mla_attn_ctx")
            return [y_win, y_dif, y_nat, y_mla]

        ys = branches(True)
        ys_c = branches(False) if want_ctx else None
        n_rows = n_all if want_ctx else n_lat
        xn, h2, scores = _merge(xa, mod3, lp['g2'], ys, ys_c, tok_a, lp['wb'], lp['wo'], rw, n_rows,
                                group_of_tile)
        blk_e, dest, wts, pad_lo, pad_hi, n_used, cap = _route(scores[:, :N_EXPERTS], router_b, n_rows)
        xs = _dispatch(dest, pad_lo, pad_hi, h2, cap)
        ys = _grouped_ffn(blk_e, n_used, xs, lp['w1'], lp['w3'], lp['w2'])
        xa = _gather_combine(dest, xn, mod3, wts, ys, n_rows, group_of_tile)
    return xa[:n_lat].reshape(n_batch, seq, d)
```

```python
import functools
import math

import jax
import jax.numpy as jnp
import numpy as np
from jax import lax
from jax.experimental import pallas as pl
from jax.experimental.pallas import tpu as pltpu

F32 = jnp.float32
BF = jnp.bfloat16

GRID_W = 64
HEAD_DIM = 64
N_BRANCH = 4
BRANCH_W = 512
ROPE_BASE = 10000.0
EPS = 1e-6
NEG = -1e30
LOG2E = math.log2(math.e)
WIN_HEADS, WIN_KV_HEADS, WIN_RADIUS = 8, 2, 128
DIF_HEADS, DIF_QK_DIM, DIF_V_DIM = 4, 64, 128
NAT_HEADS, NAT_WIN_ROWS, NAT_WIN_COLS = 8, 8, 16
MLA_HEADS, MLA_NOPE, MLA_ROPE, MLA_V, MLA_Q_LORA, MLA_KV_LORA = 8, 64, 32, 64, 256, 128
MLA_QK = MLA_NOPE + MLA_ROPE
N_EXPERTS, N_GROUPS, TOP_K, D_EXPERT, MOE_BLOCK = 16, 4, 2, 512, 256

LANES = 128
TM = 512
WIN_TQ = 256
NAT_ROWS_PER_STEP = 4
NAT_KEY_ROWS = 12
DENSE_TQ = 1024
DENSE_TQ_ONLINE = 512
DENSE_TK = 1024
DENSE_RB = 64
DENSE_KC = 256
VMEM_LIMIT = 48 * 1024 * 1024

A_GATES, A_DV, A_NV, A_MQA, A_MKVA, A_WV = 0, 4096, 4608, 5120, 5376, 5632
A_COLS = 5888
Q_COLS = 1536
KT_DK, KT_NK, KT_WK = 0, 512, 1024
KT_ROWS = 1280
B_ROWS = Q_COLS + KT_ROWS


def _cparams(sem, vmem=VMEM_LIMIT):
    return pltpu.CompilerParams(dimension_semantics=sem, vmem_limit_bytes=vmem)


def _nt_dot(a, b):
    return lax.dot_general(a, b, (((1,), (1,)), ((), ())), preferred_element_type=F32)


def _norm_mod(x, g, sc, sh):
    ms = jnp.mean(x * x, axis=-1, keepdims=True)
    return (x * lax.rsqrt(ms + EPS) * g) * (1.0 + sc) + sh


def _lane_tile(a, n):
    reps = n // a.shape[1]
    return a if reps == 1 else jnp.concatenate([a] * reps, axis=1)


def _mod_kernel(c_ref, w_ref, b_ref, o_ref):
    cc = c_ref[...]
    a = cc * jax.nn.sigmoid(cc)
    o_ref[0] = jnp.dot(a, w_ref[0], preferred_element_type=F32,
                       precision=lax.Precision.HIGHEST) + b_ref[0]


def _modulation(cc, ada_w, ada_b):
    n_layers, d, d6 = ada_w.shape
    tn = 1536
    return pl.pallas_call(
        _mod_kernel,
        out_shape=jax.ShapeDtypeStruct((n_layers, 8, d6), F32),
        grid=(n_layers, d6 // tn),
        in_specs=[pl.BlockSpec((8, d), lambda l, j: (0, 0)),
                  pl.BlockSpec((1, d, tn), lambda l, j: (l, 0, j)),
                  pl.BlockSpec((1, 1, tn), lambda l, j: (l, 0, j))],
        out_specs=pl.BlockSpec((1, 8, tn), lambda l, j: (l, 0, j)),
        compiler_params=_cparams(("parallel", "parallel")),
        name="adaln_mod",
    )(cc, ada_w, ada_b.reshape(n_layers, 1, d6))


def _in_a_kernel(x_ref, mod_ref, g_ref, w_ref, o_ref):
    m = mod_ref[0]
    h = _norm_mod(x_ref[...], g_ref[...], m[1:2], m[0:1]).astype(BF)
    acc = jnp.dot(h, w_ref[...], preferred_element_type=F32)
    tn = o_ref.shape[1]
    both = A_DV - tn
    o_ref[:, :both] = jax.nn.sigmoid(acc[:, :both]).astype(BF)
    rest = acc[:, both:]
    o_ref[:, both:] = jnp.where(pl.program_id(0) == 0, jax.nn.sigmoid(rest), rest).astype(BF)


def _in_proj_a(xa, mod3, g1, w_a, group_of_tile):
    n, d = xa.shape
    tn = A_COLS // 2
    assert A_GATES == 0 and tn < A_DV <= 2 * tn and (A_DV - tn) % LANES == 0
    return pl.pallas_call(
        _in_a_kernel,
        out_shape=jax.ShapeDtypeStruct((n, A_COLS), BF),
        grid=(A_COLS // tn, n // TM),
        in_specs=[pl.BlockSpec((TM, d), lambda j, i: (i, 0)),
                  pl.BlockSpec((1, 8, d), lambda j, i: (group_of_tile(i), 0, 0)),
                  pl.BlockSpec((1, d), lambda j, i: (0, 0)),
                  pl.BlockSpec((d, tn), lambda j, i: (0, j))],
        out_specs=pl.BlockSpec((TM, tn), lambda j, i: (i, j)),
        compiler_params=_cparams(("parallel", "parallel")),
        name="in_proj_tok",
    )(xa, mod3, g1, w_a)


def _head_norm_rope(x, g, cos, sin, rope):
    ss = jnp.sum(x * x, axis=0, keepdims=True)
    y = x * lax.rsqrt(ss * (1.0 / HEAD_DIM) + EPS) * g
    if not rope:
        return y
    half = HEAD_DIM // 2
    y1, y2 = y[:half], y[half:]
    return jnp.concatenate([y1 * cos - y2 * sin, y1 * sin + y2 * cos], axis=0)


def _in_b_kernel(x_ref, mod_ref, g_ref, wt_ref, gain_ref, cos_ref, sin_ref, q_ref, kt_ref, acc_sc):
    m = mod_ref[0]
    h = _norm_mod(x_ref[...], g_ref[...], m[1:2], m[0:1]).astype(BF)
    acc_sc[...] = _nt_dot(wt_ref[...], h)
    tm = h.shape[0]
    cos = cos_ref[...]
    sin = sin_ref[...]

    def pair(r0, rope):
        hs = []
        for e in range(2):
            r = r0 + e * HEAD_DIM
            g = _lane_tile(gain_ref[r:r + HEAD_DIM, :], tm)
            hs.append(_head_norm_rope(acc_sc[r:r + HEAD_DIM, :], g, cos, sin, rope))
        return jnp.concatenate(hs, axis=0)

    for p in range(Q_COLS // LANES):
        y = pair(p * LANES, rope=p < 8)
        q_ref[:, p * LANES:(p + 1) * LANES] = y.T.astype(BF)
    for p in range(KT_ROWS // LANES):
        y = pair(Q_COLS + p * LANES, rope=not (4 <= p < 8))
        kt_ref[p * LANES:(p + 1) * LANES, :] = y.astype(BF)


def _in_proj_b(xa, mod3, g1, w_bt, gain_b, cos_t, sin_t, group_of_tile, pos_of_tile):
    n, d = xa.shape
    return pl.pallas_call(
        _in_b_kernel,
        out_shape=(jax.ShapeDtypeStruct((n, Q_COLS), BF),
                   jax.ShapeDtypeStruct((KT_ROWS, n), BF)),
        grid=(n // TM,),
        in_specs=[pl.BlockSpec((TM, d), lambda i: (i, 0)),
                  pl.BlockSpec((1, 8, d), lambda i: (group_of_tile(i), 0, 0)),
                  pl.BlockSpec((1, d), lambda i: (0, 0)),
                  pl.BlockSpec((B_ROWS, d), lambda i: (0, 0)),
                  pl.BlockSpec((B_ROWS, LANES), lambda i: (0, 0)),
                  pl.BlockSpec((HEAD_DIM // 2, TM), lambda i: (0, pos_of_tile(i))),
                  pl.BlockSpec((HEAD_DIM // 2, TM), lambda i: (0, pos_of_tile(i)))],
        out_specs=(pl.BlockSpec((TM, Q_COLS), lambda i: (i, 0)),
                   pl.BlockSpec((KT_ROWS, TM), lambda i: (0, i))),
        scratch_shapes=[pltpu.VMEM((B_ROWS, TM), F32)],
        compiler_params=_cparams(("parallel",)),
        name="in_proj_heads",
    )(xa, mod3, g1, w_bt, gain_b, cos_t, sin_t)


def _mla_kernel(qa_ref, kva_ref, gqa_ref, gkva_ref, wqt_ref, wkt_ref, wv_ref, gq_ref, gk_ref,
                cos_ref, sin_ref, mq_ref, mkt_ref, mv_ref):
    tm = qa_ref.shape[0]
    cos = cos_ref[...]
    sin = sin_ref[...]
    rh = MLA_ROPE // 2

    def rms_rows(x, g):
        ms = jnp.mean(x * x, axis=-1, keepdims=True)
        return x * lax.rsqrt(ms + EPS) * g

    def rope_rows(x):
        x1, x2 = x[:rh], x[rh:]
        return jnp.concatenate([x1 * cos - x2 * sin, x1 * sin + x2 * cos], axis=0)

    qa = rms_rows(qa_ref[...].astype(F32), gqa_ref[...]).astype(BF)
    qt = _nt_dot(wqt_ref[...], qa)
    kva = kva_ref[...].astype(F32)
    cn = rms_rows(kva[:, :MLA_KV_LORA], gkva_ref[...]).astype(BF)
    knt = _nt_dot(wkt_ref[...], cn)
    mv_ref[...] = jnp.dot(cn, wv_ref[...], preferred_element_type=F32).astype(BF)
    krope = kva[:, MLA_KV_LORA:].T[:MLA_ROPE]
    kr_ss = jnp.sum(krope * krope, axis=0, keepdims=True)
    gq = _lane_tile(gq_ref[...], tm)
    gk = _lane_tile(gk_ref[...], tm)
    zpad = jnp.zeros((LANES - MLA_QK, tm), F32)
    for hd in range(MLA_HEADS):
        x = qt[hd * MLA_QK:(hd + 1) * MLA_QK]
        ss = jnp.sum(x * x, axis=0, keepdims=True)
        y = x * lax.rsqrt(ss * (1.0 / MLA_QK) + EPS) * gq
        y = jnp.concatenate([y[:MLA_NOPE], rope_rows(y[MLA_NOPE:]), zpad], axis=0)
        mq_ref[:, hd * LANES:(hd + 1) * LANES] = y.T.astype(BF)
        kn = knt[hd * MLA_NOPE:(hd + 1) * MLA_NOPE]
        ss = jnp.sum(kn * kn, axis=0, keepdims=True) + kr_ss
        r = lax.rsqrt(ss * (1.0 / MLA_QK) + EPS)
        yk = jnp.concatenate([kn * r * gk[:MLA_NOPE], rope_rows(krope * r * gk[MLA_NOPE:]), zpad], axis=0)
        mkt_ref[hd * LANES:(hd + 1) * LANES, :] = yk.astype(BF)


def _mla_prep(tok_a, gqa, gkva, wq_t, wk_t, wv, gq, gk, cos_t, sin_t, pos_of_tile):
    n = tok_a.shape[0]
    hw = MLA_HEADS * LANES
    return pl.pallas_call(
        _mla_kernel,
        out_shape=(jax.ShapeDtypeStruct((n, hw), BF),
                   jax.ShapeDtypeStruct((hw, n), BF),
                   jax.ShapeDtypeStruct((n, MLA_HEADS * MLA_V), BF)),
        grid=(n // TM,),
        in_specs=[pl.BlockSpec((TM, 256), lambda i: (i, A_MQA // 256)),
                  pl.BlockSpec((TM, 256), lambda i: (i, A_MKVA // 256)),
                  pl.BlockSpec((1, MLA_Q_LORA), lambda i: (0, 0)),
                  pl.BlockSpec((1, MLA_KV_LORA), lambda i: (0, 0)),
                  pl.BlockSpec(wq_t.shape, lambda i: (0, 0)),
                  pl.BlockSpec(wk_t.shape, lambda i: (0, 0)),
                  pl.BlockSpec(wv.shape, lambda i: (0, 0)),
                  pl.BlockSpec((MLA_QK, LANES), lambda i: (0, 0)),
                  pl.BlockSpec((MLA_QK, LANES), lambda i: (0, 0)),
                  pl.BlockSpec((MLA_ROPE // 2, TM), lambda i: (0, pos_of_tile(i))),
                  pl.BlockSpec((MLA_ROPE // 2, TM), lambda i: (0, pos_of_tile(i)))],
        out_specs=(pl.BlockSpec((TM, hw), lambda i: (i, 0)),
                   pl.BlockSpec((hw, TM), lambda i: (0, i)),
                   pl.BlockSpec((TM, MLA_HEADS * MLA_V), lambda i: (i, 0))),
        compiler_params=_cparams(("parallel",)),
        name="mla_prep",
    )(tok_a, tok_a, gqa, gkva, wq_t, wk_t, wv, gq, gk, cos_t, sin_t)


def _half_mask(shape):
    return lax.broadcasted_iota(jnp.int32, shape, 1) < (LANES // 2)


def _select_half(q, e, lo_mask):
    zero = jnp.zeros_like(q)
    return jnp.where(lo_mask, q, zero) if e == 0 else jnp.where(lo_mask, zero, q)


def _local_softmax_out(parts, extra_logit):
    m = parts[0][0].max(axis=-1, keepdims=True)
    for s, _ in parts[1:]:
        m = jnp.maximum(m, s.max(axis=-1, keepdims=True))
    if extra_logit is not None:
        m = jnp.maximum(m, extra_logit)
    z = None
    o = None
    for s, v in parts:
        p = jnp.exp2(s - m)
        zs = p.sum(axis=-1, keepdims=True)
        os_ = jnp.dot(p.astype(BF), v, preferred_element_type=F32)
        z = zs if z is None else z + zs
        o = os_ if o is None else o + os_
    if extra_logit is not None:
        z = z + jnp.exp2(extra_logit - m)
    return o / z


def _fixed_softmax_out(parts, ones_ref, extra_logit):
    acc = None
    for s, v in parts:
        aug = jnp.concatenate([v, ones_ref[:s.shape[1], :]], axis=1)
        t = jnp.dot(jnp.exp2(s).astype(BF), aug, preferred_element_type=F32)
        acc = t if acc is None else acc + t
    l = acc[:, LANES:]
    if extra_logit is not None:
        l = l + jnp.exp2(jnp.zeros_like(l) + extra_logit)
    unsafe = jnp.logical_not((l > 2.0 ** -SAFE_SUM_LOG2) & (l < 2.0 ** SAFE_SUM_LOG2))
    return acc[:, :LANES] / l, jnp.max(jnp.where(unsafe, 1.0, 0.0), axis=0, keepdims=True)


def _win_kernel(sink_ref, q_ref, *refs, band, fixed=False):
    if band:
        k0, k1, k2, k3, v0, v1, v2, v3, kc_ref, vc_ref, mask_ref = refs[:11]
        refs = refs[11:]
        kb = jnp.concatenate([k0[...], k1[...], k2[...], k3[...]], axis=1)
        vb = jnp.concatenate([v0[...], v1[...], v2[...], v3[...]], axis=0)
        mask = mask_ref[0]
    else:
        kc_ref, vc_ref = refs[:2]
        refs = refs[2:]
    if fixed:
        ones_ref, o_ref, flag_ref = refs
        ref_logit = sink_ref[0, WIN_HEADS]
    else:
        (o_ref,) = refs
    q = q_ref[...]
    lo = _half_mask((q.shape[0], LANES))
    group = WIN_HEADS // WIN_KV_HEADS
    bad = []
    for j in range(WIN_HEADS // 2):
        qp = q[:, j * LANES:(j + 1) * LANES]
        g = (2 * j) // group
        kc = kc_ref[g * LANES:(g + 1) * LANES, :]
        vc = vc_ref[:, g * LANES:(g + 1) * LANES]
        outs = []
        for e in range(2):
            qm = _select_half(qp, e, lo)
            parts = []
            if band:
                s = jnp.dot(qm, kb[g * LANES:(g + 1) * LANES, :], preferred_element_type=F32) + mask
                parts.append((s, vb[:, g * LANES:(g + 1) * LANES]))
            sc = jnp.dot(qm, kc, preferred_element_type=F32)
            if fixed:
                parts.append((sc - ref_logit, vc))
                o, u = _fixed_softmax_out(parts, ones_ref, sink_ref[0, 2 * j + e])
                bad.append(u)
                outs.append(o)
            else:
                parts.append((sc, vc))
                outs.append(_local_softmax_out(parts, sink_ref[0, 2 * j + e]))
        o_ref[:, j * LANES:(j + 1) * LANES] = jnp.where(lo, outs[0], outs[1]).astype(BF)
    if fixed:
        flag_ref[0] = jnp.concatenate(bad, axis=0)


def _win_attention(sink, q_tok, kt, tok_a, mask_tbl, n_batch, seq, n_ctx, latent, ones=None):
    ctx_blk = (n_batch * seq) // n_ctx
    kc_spec = lambda f: pl.BlockSpec((2 * LANES, n_ctx), f)
    vc_spec = lambda f: pl.BlockSpec((n_ctx, 2 * LANES), f)
    smem = pl.BlockSpec(memory_space=pltpu.SMEM)
    n_out = n_batch * (seq if latent else n_ctx)
    out_shape = jax.ShapeDtypeStruct((n_out, WIN_HEADS * HEAD_DIM), BF)
    if not latent:
        return pl.pallas_call(
            functools.partial(_win_kernel, band=False),
            out_shape=out_shape,
            grid=(n_batch,),
            in_specs=[smem,
                      pl.BlockSpec((n_ctx, 512), lambda b: (ctx_blk + b, 0)),
                      kc_spec(lambda b: (KT_WK // 256, ctx_blk + b)),
                      vc_spec(lambda b: (ctx_blk + b, A_WV // 256))],
            out_specs=pl.BlockSpec((n_ctx, 512), lambda b: (b, 0)),
            compiler_params=_cparams(("parallel",)),
            name="win_attn_ctx",
        )(sink, q_tok, kt, tok_a)
    nq = seq // WIN_TQ
    nkb = seq // LANES

    def kidx(j):
        return lambda b, i: (KT_WK // 256, b * nkb + jnp.clip(2 * i - 1 + j, 0, nkb - 1))

    def vidx(j):
        return lambda b, i: (b * nkb + jnp.clip(2 * i - 1 + j, 0, nkb - 1), A_WV // 256)

    def variant(b, i):
        return (jnp.where(i == 0, 0, jnp.where(i == nq - 1, 2, 1)), 0, 0)

    in_specs = ([smem, pl.BlockSpec((WIN_TQ, 512), lambda b, i: (b * nq + i, 0))]
                + [pl.BlockSpec((2 * LANES, LANES), kidx(j)) for j in range(4)]
                + [pl.BlockSpec((LANES, 2 * LANES), vidx(j)) for j in range(4)]
                + [kc_spec(lambda b, i: (KT_WK // 256, ctx_blk + b)),
                   vc_spec(lambda b, i: (ctx_blk + b, A_WV // 256)),
                   pl.BlockSpec((1, WIN_TQ, 4 * LANES), variant)])
    args = [sink, q_tok, kt, kt, kt, kt, tok_a, tok_a, tok_a, tok_a, kt, tok_a, mask_tbl]
    out_specs = pl.BlockSpec((WIN_TQ, 512), lambda b, i: (b * nq + i, 0))
    fixed = ones is not None
    if fixed:
        in_specs.append(pl.BlockSpec(ones.shape, lambda b, i: (0, 0)))
        args.append(ones)
        out_shape = (out_shape, jax.ShapeDtypeStruct((n_batch * nq, WIN_HEADS, LANES), F32))
        out_specs = (out_specs, pl.BlockSpec((1, WIN_HEADS, LANES), lambda b, i: (b * nq + i, 0, 0)))
    return pl.pallas_call(
        functools.partial(_win_kernel, band=True, fixed=fixed),
        out_shape=out_shape,
        grid=(n_batch, nq),
        in_specs=in_specs,
        out_specs=out_specs,
        compiler_params=_cparams(("parallel", "parallel")),
        name="win_attn_fixed" if fixed else "win_attn",
    )(*args)


def _nat_kernel(q_ref, k0, k1, k2, v0, v1, v2, kc_ref, vc_ref, bias_ref, *refs, fixed=False):
    if fixed:
        par_ref, ones_ref, o_ref, flag_ref = refs
        ref_logit = par_ref[0, 0]
    else:
        (o_ref,) = refs
    q = q_ref[...]
    kb = jnp.concatenate([k0[...], k1[...], k2[...]], axis=1)
    vb = jnp.concatenate([v0[...], v1[...], v2[...]], axis=0)
    lo = _half_mask((q.shape[0], LANES))
    bad = []
    for j in range(NAT_HEADS // 2):
        sl = slice(j * LANES, (j + 1) * LANES)
        qp = q[:, sl]
        outs = []
        for e in range(2):
            qm = _select_half(qp, e, lo)
            s = jnp.dot(qm, kb[sl, :], preferred_element_type=F32) + bias_ref[0, 2 * j + e]
            sc = jnp.dot(qm, kc_ref[sl, :], preferred_element_type=F32)
            if fixed:
                o, u = _fixed_softmax_out([(s, vb[:, sl]), (sc - ref_logit, vc_ref[:, sl])], ones_ref, None)
                bad.append(u)
                outs.append(o)
            else:
                outs.append(_local_softmax_out([(s, vb[:, sl]), (sc, vc_ref[:, sl])], None))
        o_ref[:, sl] = jnp.where(lo, outs[0], outs[1]).astype(BF)
    if fixed:
        flag_ref[0] = jnp.concatenate(bad, axis=0)


def _nat_attention(q_tok, kt, tok_a, bias_tbl, n_batch, seq, n_ctx, par=None, ones=None):
    tq = NAT_ROWS_PER_STEP * GRID_W
    nq = seq // tq
    rows = seq // GRID_W
    ctx_blk = (n_batch * seq) // n_ctx
    q_col = 2
    k_row = KT_NK // 512
    v_col = A_NV // 512

    def wstart(i):
        return jnp.clip(NAT_ROWS_PER_STEP * i - NAT_WIN_ROWS // 2, 0, rows - NAT_KEY_ROWS) // NAT_ROWS_PER_STEP

    def kidx(j):
        return lambda b, i: (k_row, b * nq + wstart(i) + j)

    def vidx(j):
        return lambda b, i: (b * nq + wstart(i) + j, v_col)

    def variant(b, i):
        return (jnp.where(i == 0, 0, jnp.where(i == nq - 1, 2, 1)), 0, 0, 0)

    nk = NAT_KEY_ROWS * GRID_W
    in_specs = ([pl.BlockSpec((tq, 512), lambda b, i: (b * nq + i, q_col))]
                + [pl.BlockSpec((512, tq), kidx(j)) for j in range(3)]
                + [pl.BlockSpec((tq, 512), vidx(j)) for j in range(3)]
                + [pl.BlockSpec((512, n_ctx), lambda b, i: (k_row, ctx_blk + b)),
                   pl.BlockSpec((n_ctx, 512), lambda b, i: (ctx_blk + b, v_col)),
                   pl.BlockSpec((1, NAT_HEADS, tq, nk), variant)])
    args = [q_tok, kt, kt, kt, tok_a, tok_a, tok_a, kt, tok_a, bias_tbl]
    out_shape = jax.ShapeDtypeStruct((n_batch * seq, NAT_HEADS * HEAD_DIM), BF)
    out_specs = pl.BlockSpec((tq, 512), lambda b, i: (b * nq + i, 0))
    fixed = ones is not None
    if fixed:
        in_specs += [pl.BlockSpec(memory_space=pltpu.SMEM), pl.BlockSpec(ones.shape, lambda b, i: (0, 0))]
        args += [par, ones]
        out_shape = (out_shape, jax.ShapeDtypeStruct((n_batch * nq, NAT_HEADS, LANES), F32))
        out_specs = (out_specs, pl.BlockSpec((1, NAT_HEADS, LANES), lambda b, i: (b * nq + i, 0, 0)))
    return pl.pallas_call(
        functools.partial(_nat_kernel, fixed=fixed),
        out_shape=out_shape,
        grid=(n_batch, nq),
        in_specs=in_specs,
        out_specs=out_specs,
        compiler_params=_cparams(("parallel", "parallel")),
        name="nat_attn_fixed" if fixed else "nat_attn",
    )(*args)


def _dense_kernel(lam_ref, q_ref, kc_ref, vc_ref, *refs, n_heads, packed, diff, latent, lam_scale):
    if latent:
        k_ref, v_ref = refs[0], refs[1]
        refs = refs[2:]
    if diff:
        subln_ref, o_ref, qm_sc, m_sc, l_sc, acc_sc, s_sc, p_sc = refs
    else:
        o_ref, qm_sc, m_sc, l_sc, acc_sc, s_sc, p_sc = refs
    kt_step = pl.program_id(2) if latent else 0
    tq = q_ref.shape[0]

    def kv_slices(h):
        blk = h // 2 if packed else h
        ks = slice(blk * LANES, (blk + 1) * LANES)
        vs = ks if packed else slice((h // 2) * LANES, (h // 2 + 1) * LANES)
        return ks, vs

    def step(h, k_ref_, v_ref_):
        ks, vs = kv_slices(h)
        nk = k_ref_.shape[1]
        slot = h % 2
        s_sc[slot, :, :nk] = jnp.dot(qm_sc[h], k_ref_[ks, :], preferred_element_type=F32)
        for r in range(tq // DENSE_RB):
            rows = slice(r * DENSE_RB, (r + 1) * DENSE_RB)
            mx = s_sc[slot, rows, 0:LANES]
            for c in range(1, nk // LANES):
                mx = jnp.maximum(mx, s_sc[slot, rows, c * LANES:(c + 1) * LANES])
            m_old = m_sc[h, rows, :]
            m_new = jnp.maximum(m_old, jnp.max(mx, axis=-1, keepdims=True))
            alpha = jnp.exp2(m_old - m_new)
            lsum = None
            for c in range(nk // LANES):
                cols = slice(c * LANES, (c + 1) * LANES)
                p = jnp.exp2(s_sc[slot, rows, cols] - m_new)
                lsum = p if lsum is None else lsum + p
                p_sc[slot, rows, cols] = p.astype(BF)
            m_sc[h, rows, :] = m_new
            l_sc[h, rows, :] = alpha * l_sc[h, rows, :] + jnp.sum(lsum, axis=-1, keepdims=True)
            acc_sc[h, rows, :] = alpha * acc_sc[h, rows, :]
        acc_sc[h] += jnp.dot(p_sc[slot, :, :nk], v_ref_[:, vs], preferred_element_type=F32)

    @pl.when(kt_step == 0)
    def _():
        q = q_ref[...]
        lo = _half_mask((tq, LANES))
        for h in range(n_heads):
            if packed:
                qp = q[:, (h // 2) * LANES:(h // 2 + 1) * LANES]
                qm_sc[h] = _select_half(qp, h % 2, lo)
            else:
                qm_sc[h] = q[:, h * LANES:(h + 1) * LANES]
        m_sc[...] = jnp.full(m_sc.shape, NEG, F32)
        l_sc[...] = jnp.zeros(l_sc.shape, F32)
        acc_sc[...] = jnp.zeros(acc_sc.shape, F32)
        for h in range(n_heads):
            step(h, kc_ref, vc_ref)

    if latent:
        for h in range(n_heads):
            step(h, k_ref, v_ref)
        last = kt_step == pl.num_programs(2) - 1
    else:
        last = True

    def finish():
        lo = _half_mask((tq, LANES))
        if diff:
            lam = lam_ref[0, 0]
            for hv in range(n_heads // 2):
                y = (acc_sc[2 * hv] / l_sc[2 * hv]
                     - lam * (acc_sc[2 * hv + 1] / l_sc[2 * hv + 1]))
                ms = jnp.mean(y * y, axis=-1, keepdims=True)
                y = y * lax.rsqrt(ms + EPS) * subln_ref[...] * lam_scale
                o_ref[:, hv * LANES:(hv + 1) * LANES] = y.astype(BF)
        else:
            for hp in range(n_heads // 2):
                o0 = acc_sc[2 * hp] / l_sc[2 * hp]
                o1 = acc_sc[2 * hp + 1] / l_sc[2 * hp + 1]
                o_ref[:, hp * LANES:(hp + 1) * LANES] = jnp.where(lo, o0, o1).astype(BF)

    if latent:
        pl.when(last)(finish)
    else:
        finish()


def _dense_attention(lam, q_arr, q_col, q_w, k_arr, k_row, k_w, v_arr, v_col, v_w, subln,
                     n_batch, seq, n_ctx, *, n_heads, packed, diff, latent, lam_scale, name):
    n = n_batch * (seq if latent else n_ctx)
    ctx_blk = (n_batch * seq) // n_ctx
    out_w = v_w
    smem = pl.BlockSpec(memory_space=pltpu.SMEM)
    kern = functools.partial(_dense_kernel, n_heads=n_heads, packed=packed, diff=diff,
                             latent=latent, lam_scale=lam_scale)
    tq = DENSE_TQ_ONLINE if latent else n_ctx
    max_nk = DENSE_TK if latent else n_ctx
    scratch = [pltpu.VMEM((n_heads, tq, LANES), BF),
               pltpu.VMEM((n_heads, tq, LANES), F32),
               pltpu.VMEM((n_heads, tq, LANES), F32),
               pltpu.VMEM((n_heads, tq, LANES), F32),
               pltpu.VMEM((2, tq, max_nk), F32),
               pltpu.VMEM((2, tq, max_nk), BF)]
    out_shape = jax.ShapeDtypeStruct((n, out_w), BF)
    if latent:
        nq = seq // tq
        nk = seq // DENSE_TK
        grid = (n_batch, nq, nk)
        in_specs = [smem,
                    pl.BlockSpec((tq, q_w), lambda b, i, k: (b * nq + i, q_col)),
                    pl.BlockSpec((k_w, n_ctx), lambda b, i, k: (k_row, ctx_blk + b)),
                    pl.BlockSpec((n_ctx, v_w), lambda b, i, k: (ctx_blk + b, v_col)),
                    pl.BlockSpec((k_w, DENSE_TK), lambda b, i, k: (k_row, b * nk + k)),
                    pl.BlockSpec((DENSE_TK, v_w), lambda b, i, k: (b * nk + k, v_col))]
        args = [lam, q_arr, k_arr, v_arr, k_arr, v_arr]
        if diff:
            in_specs.append(pl.BlockSpec((1, LANES), lambda b, i, k: (0, 0)))
            args.append(subln)
        out_specs = pl.BlockSpec((tq, out_w), lambda b, i, k: (b * nq + i, 0))
        sem = ("parallel", "parallel", "arbitrary")
    else:
        grid = (n_batch,)
        in_specs = [smem,
                    pl.BlockSpec((tq, q_w), lambda b: (ctx_blk + b, q_col)),
                    pl.BlockSpec((k_w, n_ctx), lambda b: (k_row, ctx_blk + b)),
                    pl.BlockSpec((n_ctx, v_w), lambda b: (ctx_blk + b, v_col))]
        args = [lam, q_arr, k_arr, v_arr]
        if diff:
            in_specs.append(pl.BlockSpec((1, LANES), lambda b: (0, 0)))
            args.append(subln)
        out_specs = pl.BlockSpec((tq, out_w), lambda b: (b, 0))
        sem = ("parallel",)
    return pl.pallas_call(
        kern, out_shape=out_shape, grid=grid, in_specs=in_specs, out_specs=out_specs,
        scratch_shapes=scratch, compiler_params=_cparams(sem), name=name,
    )(*args)


SAFE_SUM_LOG2 = 100.0


def _dense_fixed_kernel(par_ref, q_ref, kc_ref, vc_ref, *refs, n_heads, packed, diff, latent, lam_scale):
    if latent:
        k_ref, v_ref = refs[0], refs[1]
        refs = refs[2:]
    ones_ref = refs[0]
    refs = refs[1:]
    if diff:
        subln_ref, o_ref, flag_ref, qm_sc, acc_sc, p_sc = refs
    else:
        o_ref, flag_ref, qm_sc, acc_sc, p_sc = refs
    kt_step = pl.program_id(2) if latent else 0
    tq = q_ref.shape[0]
    ref_logit = par_ref[0, 1]

    def step(h, k_ref_, v_ref_, first):
        blk = h // 2 if packed else h
        ks = slice(blk * LANES, (blk + 1) * LANES)
        vs = ks if packed else slice((h // 2) * LANES, (h // 2 + 1) * LANES)
        nk = k_ref_.shape[1]
        slot = h % 2
        for c in range(nk // DENSE_KC):
            cols = slice(c * DENSE_KC, (c + 1) * DENSE_KC)
            s = jnp.dot(qm_sc[h], k_ref_[ks, cols], preferred_element_type=F32)
            p_sc[slot, :, cols] = jnp.exp2(s - ref_logit).astype(BF)
        v_aug = jnp.concatenate([v_ref_[:, vs], ones_ref[:nk, :]], axis=1)
        pv = jnp.dot(p_sc[slot, :, :nk], v_aug, preferred_element_type=F32)
        if first:
            acc_sc[h] = pv
        else:
            acc_sc[h] += pv

    @pl.when(kt_step == 0)
    def _():
        q = q_ref[...]
        lo = _half_mask((tq, LANES))
        for h in range(n_heads):
            if packed:
                qp = q[:, (h // 2) * LANES:(h // 2 + 1) * LANES]
                qm_sc[h] = _select_half(qp, h % 2, lo)
            else:
                qm_sc[h] = q[:, h * LANES:(h + 1) * LANES]
        for h in range(n_heads):
            step(h, kc_ref, vc_ref, True)

    if latent:
        for h in range(n_heads):
            step(h, k_ref, v_ref, False)
        last = kt_step == pl.num_programs(2) - 1

    def finish():
        lo = _half_mask((tq, LANES))
        outs, bad = [], []
        for h in range(n_heads):
            a = acc_sc[h]
            l = a[:, LANES:]
            unsafe = jnp.logical_not((l > 2.0 ** -SAFE_SUM_LOG2) & (l < 2.0 ** SAFE_SUM_LOG2))
            bad.append(jnp.max(jnp.where(unsafe, 1.0, 0.0), axis=0, keepdims=True))
            outs.append(a[:, :LANES] / l)
        flag_ref[0] = jnp.concatenate(bad, axis=0)
        if diff:
            lam = par_ref[0, 0]
            for hv in range(n_heads // 2):
                y = outs[2 * hv] - lam * outs[2 * hv + 1]
                ms = jnp.mean(y * y, axis=-1, keepdims=True)
                y = y * lax.rsqrt(ms + EPS) * subln_ref[...] * lam_scale
                o_ref[:, hv * LANES:(hv + 1) * LANES] = y.astype(BF)
        else:
            for hp in range(n_heads // 2):
                o_ref[:, hp * LANES:(hp + 1) * LANES] = jnp.where(lo, outs[2 * hp], outs[2 * hp + 1]).astype(BF)

    if latent:
        pl.when(last)(finish)
    else:
        finish()


def _dense_fixed_attention(par, q_arr, q_col, q_w, k_arr, k_row, k_w, v_arr, v_col, v_w, subln, ones,
                           n_batch, seq, n_ctx, *, n_heads, packed, diff, latent, lam_scale, name):
    n = n_batch * (seq if latent else n_ctx)
    ctx_blk = (n_batch * seq) // n_ctx
    smem = pl.BlockSpec(memory_space=pltpu.SMEM)
    kern = functools.partial(_dense_fixed_kernel, n_heads=n_heads, packed=packed, diff=diff,
                             latent=latent, lam_scale=lam_scale)
    tq = DENSE_TQ if latent else n_ctx
    max_nk = DENSE_TK if latent else n_ctx
    scratch = [pltpu.VMEM((n_heads, tq, LANES), BF),
               pltpu.VMEM((n_heads, tq, 2 * LANES), F32),
               pltpu.VMEM((2, tq, max_nk), BF)]
    if latent:
        nq = seq // tq
        nk = seq // DENSE_TK
        grid = (n_batch, nq, nk)
        ix = lambda f: (lambda b, i, k: f(b, i, k))
        q_ix = ix(lambda b, i, k: (b * nq + i, q_col))
        in_specs = [smem,
                    pl.BlockSpec((tq, q_w), q_ix),
                    pl.BlockSpec((k_w, n_ctx), ix(lambda b, i, k: (k_row, ctx_blk + b))),
                    pl.BlockSpec((n_ctx, v_w), ix(lambda b, i, k: (ctx_blk + b, v_col))),
                    pl.BlockSpec((k_w, DENSE_TK), ix(lambda b, i, k: (k_row, b * nk + k))),
                    pl.BlockSpec((DENSE_TK, v_w), ix(lambda b, i, k: (b * nk + k, v_col))),
                    pl.BlockSpec((DENSE_TK, LANES), ix(lambda b, i, k: (0, 0)))]
        args = [par, q_arr, k_arr, v_arr, k_arr, v_arr, ones]
        const_ix = ix(lambda b, i, k: (0, 0))
        out_specs = (pl.BlockSpec((tq, v_w), ix(lambda b, i, k: (b * nq + i, 0))),
                     pl.BlockSpec((1, n_heads, LANES), ix(lambda b, i, k: (b * nq + i, 0, 0))))
        n_flag = n_batch * nq
        sem = ("parallel", "parallel", "arbitrary")
    else:
        grid = (n_batch,)
        in_specs = [smem,
                    pl.BlockSpec((tq, q_w), lambda b: (ctx_blk + b, q_col)),
                    pl.BlockSpec((k_w, n_ctx), lambda b: (k_row, ctx_blk + b)),
                    pl.BlockSpec((n_ctx, v_w), lambda b: (ctx_blk + b, v_col)),
                    pl.BlockSpec((DENSE_TK, LANES), lambda b: (0, 0))]
        args = [par, q_arr, k_arr, v_arr, ones]
        const_ix = lambda b: (0, 0)
        out_specs = (pl.BlockSpec((tq, v_w), lambda b: (b, 0)),
                     pl.BlockSpec((1, n_heads, LANES), lambda b: (b, 0, 0)))
        n_flag = n_batch
        sem = ("parallel",)
    if diff:
        in_specs.append(pl.BlockSpec((1, LANES), const_ix))
        args.append(subln)
    return pl.pallas_call(
        kern,
        out_shape=(jax.ShapeDtypeStruct((n, v_w), BF), jax.ShapeDtypeStruct((n_flag, n_heads, LANES), F32)),
        grid=grid, in_specs=in_specs, out_specs=out_specs,
        scratch_shapes=scratch, compiler_params=_cparams(sem), name=name + "_fixed",
    )(*args)


def _merge_kernel(x_ref, mod_ref, g2_ref, *refs, n_lat_tiles, has_ctx):
    ys = refs[:N_BRANCH]
    refs = refs[N_BRANCH:]
    if has_ctx:
        ycs = refs[:N_BRANCH]
        refs = refs[N_BRANCH:]
        is_ctx = pl.program_id(0) >= n_lat_tiles
    gts = refs[:N_BRANCH]
    wb_ref, wo_ref, rw_ref, xo_ref, h2_ref, sc_ref = refs[N_BRANCH:]
    m = mod_ref[0]
    mix = None
    for n_ in range(N_BRANCH):
        y = ys[n_][...]
        if has_ctx:
            y = jnp.where(is_ctx, ycs[n_][...], y)
        yb = jnp.dot(y, wb_ref[n_], preferred_element_type=F32)
        t = gts[n_][...].astype(F32) * yb
        mix = t if mix is None else mix + t
    att = jnp.dot(mix.astype(BF), wo_ref[...], preferred_element_type=F32)
    xn = x_ref[...] + m[2:3] * att
    xo_ref[...] = xn
    h2 = _norm_mod(xn, g2_ref[...], m[4:5], m[3:4])
    h2_ref[...] = h2
    logits = jnp.dot(h2, rw_ref[...], preferred_element_type=F32, precision=lax.Precision.HIGHEST)
    sc_ref[...] = jax.nn.sigmoid(logits)


def _merge(xa, mod3, g2, ys, ys_ctx, tok_a, wb, wo, rw, n_rows, group_of_tile):
    d = xa.shape[1]
    n_lat_tiles = ys[0].shape[0] // TM
    has_ctx = ys_ctx is not None
    row = lambda w, c: pl.BlockSpec((TM, w), lambda i, c=c: (i, c))
    lat_row = pl.BlockSpec((TM, BRANCH_W), lambda i: (jnp.minimum(i, n_lat_tiles - 1), 0))
    in_specs = ([row(d, 0),
                 pl.BlockSpec((1, 8, d), lambda i: (group_of_tile(i), 0, 0)),
                 pl.BlockSpec((1, d), lambda i: (0, 0))]
                + [lat_row for _ in range(N_BRANCH)]
                + ([pl.BlockSpec((TM, BRANCH_W), lambda i: (0, 0)) for _ in range(N_BRANCH)] if has_ctx else [])
                + [row(d, c) for c in range(N_BRANCH)]
                + [pl.BlockSpec(wb.shape, lambda i: (0, 0, 0)),
                   pl.BlockSpec(wo.shape, lambda i: (0, 0)),
                   pl.BlockSpec(rw.shape, lambda i: (0, 0))])
    ys = list(ys) + (list(ys_ctx) if has_ctx else [])
    return pl.pallas_call(
        functools.partial(_merge_kernel, n_lat_tiles=n_lat_tiles, has_ctx=has_ctx),
        out_shape=(jax.ShapeDtypeStruct((n_rows, d), F32),
                   jax.ShapeDtypeStruct((n_rows, d), F32),
                   jax.ShapeDtypeStruct((n_rows, LANES), F32)),
        grid=(n_rows // TM,),
        in_specs=in_specs,
        out_specs=(row(d, 0), row(d, 0), row(LANES, 0)),
        compiler_params=_cparams(("parallel",)),
        name="merge",
    )(xa, mod3, g2, *ys, tok_a, tok_a, tok_a, tok_a, wb, wo, rw)


def _route(scores, router_b, n):
    per_group = N_EXPERTS // N_GROUPS
    st = scores.T
    biased = st + router_b.astype(F32)[:, None]

    def top2(v, axis):
        pos = lax.broadcasted_iota(jnp.int32, v.shape, axis)
        i0 = jnp.argmax(v, axis=axis).astype(jnp.int32)
        v0 = jnp.max(v, axis=axis)
        rest = jnp.where(pos == jnp.expand_dims(i0, axis), -jnp.inf, v)
        i1 = jnp.argmax(rest, axis=axis).astype(jnp.int32)
        v1 = jnp.max(rest, axis=axis)
        return (v0, v1), (i0, i1)

    (g0, g1), _ = top2(biased.reshape(N_GROUPS, per_group, n), 1)
    group = jnp.argmax(g0 + g1, axis=0)
    expert = jnp.arange(N_EXPERTS, dtype=jnp.int32)[:, None]
    in_group = (expert // per_group) == group[None, :]
    _, (e0, e1) = top2(jnp.where(in_group, biased, -jnp.inf), 0)
    oh0 = (expert == e0[None, :]).astype(F32)
    oh1 = (expert == e1[None, :]).astype(F32)
    w0 = (oh0 * st).sum(0)
    w1 = (oh1 * st).sum(0)
    wts = jnp.stack([w0, w1], axis=-1) / (w0 + w1)[:, None]
    n_asg = n * TOP_K
    cnt = (oh0 + oh1).reshape(N_EXPERTS, n // MOE_BLOCK, MOE_BLOCK)
    tri = jnp.triu(jnp.ones((MOE_BLOCK, MOE_BLOCK), F32), 1)
    within = jnp.einsum('ebj,ji->ebi', cnt, tri)
    blk_tot = cnt.sum(axis=-1)
    nb = blk_tot.shape[1]
    blk_off = jnp.einsum('eb,bc->ec', blk_tot, jnp.triu(jnp.ones((nb, nb), F32), 1),
                         precision=lax.Precision.HIGHEST)
    prefix = (within + blk_off[:, :, None]).reshape(N_EXPERTS, n)
    counts = blk_tot.sum(axis=-1).astype(jnp.int32)
    padded = (counts + MOE_BLOCK - 1) // MOE_BLOCK * MOE_BLOCK
    pad_end = jnp.cumsum(padded)
    pad_start = pad_end - padded
    slot = prefix + pad_start.astype(F32)[:, None]
    dest = jnp.stack([(oh0 * slot).sum(0), (oh1 * slot).sum(0)], axis=-1).reshape(n_asg)
    n_blk = (n_asg + N_EXPERTS * (MOE_BLOCK - 1) + MOE_BLOCK - 1) // MOE_BLOCK
    cap = n_blk * MOE_BLOCK
    blk_start = jnp.arange(n_blk, dtype=jnp.int32) * MOE_BLOCK
    blk_e = jnp.minimum((blk_start[:, None] >= pad_end[None, :]).sum(-1), N_EXPERTS - 1).astype(jnp.int32)
    pad_lo = jnp.concatenate([pad_start + counts, pad_end[-1:]]).astype(jnp.int32)
    pad_hi = jnp.concatenate([pad_end, jnp.full((1,), cap)]).astype(jnp.int32)
    n_used = (pad_end[-1:] // MOE_BLOCK).astype(jnp.int32)
    return blk_e, dest.astype(jnp.int32), wts.astype(F32), pad_lo, pad_hi, n_used, cap


def _dispatch_kernel(dest_ref, pad_lo_ref, pad_hi_ref, h2_ref, xs_hbm, zrow, sem, zsem):
    i = pl.program_id(0)

    @pl.when(i == 0)
    def _():
        zrow[...] = jnp.zeros(zrow.shape, F32)
        for e in range(N_EXPERTS + 1):
            lo, hi = pad_lo_ref[e], pad_hi_ref[e]

            def zero_row(r, c):
                pltpu.make_async_copy(zrow.at[pl.ds(0, 1)], xs_hbm.at[pl.ds(r, 1)], zsem).start()
                return c

            def zero_wait(r, c):
                pltpu.make_async_copy(zrow.at[pl.ds(0, 1)], xs_hbm.at[pl.ds(r, 1)], zsem).wait()
                return c

            lax.fori_loop(lo, hi, zero_row, 0)
            lax.fori_loop(lo, hi, zero_wait, 0)

    tm = h2_ref.shape[0]

    def copy_row(r, c):
        a = (i * tm + r) * TOP_K
        for k in range(TOP_K):
            pltpu.make_async_copy(h2_ref.at[pl.ds(r, 1)], xs_hbm.at[pl.ds(dest_ref[a + k], 1)], sem).start()
        return c

    lax.fori_loop(0, tm, copy_row, 0, unroll=4)
    for k in range(TOP_K):
        pltpu.make_async_copy(h2_ref, xs_hbm.at[pl.ds(0, tm)], sem).wait()


def _dispatch(dest, pad_lo, pad_hi, h2, cap):
    n, d = h2.shape
    grid_spec = pltpu.PrefetchScalarGridSpec(
        num_scalar_prefetch=3,
        grid=(n // TM,),
        in_specs=[pl.BlockSpec((TM, d), lambda i, de, lo, hi: (i, 0))],
        out_specs=pl.BlockSpec(memory_space=pl.ANY),
        scratch_shapes=[pltpu.VMEM((8, d), F32), pltpu.SemaphoreType.DMA, pltpu.SemaphoreType.DMA],
    )
    return pl.pallas_call(
        _dispatch_kernel,
        out_shape=jax.ShapeDtypeStruct((cap, d), F32),
        grid_spec=grid_spec,
        compiler_params=_cparams(("arbitrary",)),
        name="moe_dispatch",
    )(dest, pad_lo, pad_hi, h2)


def _grouped_ffn_kernel(blk_e_ref, n_used_ref, x_ref, w1_ref, w3_ref, w2_ref, o_ref):
    i = pl.program_id(0)

    @pl.when(i < n_used_ref[0])
    def _():
        xb = x_ref[...].astype(BF)
        a = jnp.dot(xb, w1_ref[0], preferred_element_type=F32)
        b = jnp.dot(xb, w3_ref[0], preferred_element_type=F32)
        hmid = (a * jax.nn.sigmoid(a) * b).astype(BF)
        o_ref[...] = jnp.dot(hmid, w2_ref[0], preferred_element_type=F32)

    @pl.when(i >= n_used_ref[0])
    def _():
        o_ref[...] = jnp.zeros(o_ref.shape, F32)


def _grouped_ffn(blk_e, n_used, xs, w1, w3, w2):
    cap, d = xs.shape
    wspec = lambda shp: pl.BlockSpec((1,) + shp, lambda i, be, nu: (be[i], 0, 0))
    grid_spec = pltpu.PrefetchScalarGridSpec(
        num_scalar_prefetch=2,
        grid=(cap // MOE_BLOCK,),
        in_specs=[pl.BlockSpec((MOE_BLOCK, d), lambda i, be, nu: (jnp.minimum(i, nu[0] - 1), 0)),
                  wspec((d, D_EXPERT)), wspec((d, D_EXPERT)), wspec((D_EXPERT, d))],
        out_specs=pl.BlockSpec((MOE_BLOCK, d), lambda i, be, nu: (i, 0)),
    )
    return pl.pallas_call(
        _grouped_ffn_kernel,
        out_shape=jax.ShapeDtypeStruct((cap, d), F32),
        grid_spec=grid_spec,
        compiler_params=_cparams(("arbitrary",)),
        name="moe_ffn",
    )(blk_e, n_used, xs, w1, w3, w2)


def _gather_combine_kernel(dest_ref, x_ref, mod_ref, w_ref, ys_hbm, o_ref, buf, sem):
    i = pl.program_id(0)
    n_steps = pl.num_programs(0)
    slot = i % 2
    tm = x_ref.shape[0]

    def start_gather(tile, s):
        def body(r, c):
            a = (tile * tm + r) * TOP_K
            for k in range(TOP_K):
                pltpu.make_async_copy(ys_hbm.at[pl.ds(dest_ref[a + k], 1)], buf.at[s, k, pl.ds(r, 1)],
                                      sem.at[s]).start()
            return c
        lax.fori_loop(0, tm, body, 0, unroll=4)

    @pl.when(i == 0)
    def _():
        start_gather(0, 0)

    @pl.when(i + 1 < n_steps)
    def _():
        start_gather(i + 1, 1 - slot)

    for k in range(TOP_K):
        pltpu.make_async_copy(ys_hbm.at[pl.ds(0, tm)], buf.at[slot, k], sem.at[slot]).wait()
    w = w_ref[...]
    f = w[:, 0:1] * buf[slot, 0] + w[:, 1:2] * buf[slot, 1]
    o_ref[...] = x_ref[...] + mod_ref[0][5:6] * f


def _gather_combine(dest, xn, mod3, wts, ys, n_rows, group_of_tile):
    d = xn.shape[1]
    grid_spec = pltpu.PrefetchScalarGridSpec(
        num_scalar_prefetch=1,
        grid=(n_rows // TM,),
        in_specs=[pl.BlockSpec((TM, d), lambda i, de: (i, 0)),
                  pl.BlockSpec((1, 8, d), lambda i, de: (group_of_tile(i), 0, 0)),
                  pl.BlockSpec((TM, TOP_K), lambda i, de: (i, 0)),
                  pl.BlockSpec(memory_space=pl.ANY)],
        out_specs=pl.BlockSpec((TM, d), lambda i, de: (i, 0)),
        scratch_shapes=[pltpu.VMEM((2, TOP_K, TM, d), F32), pltpu.SemaphoreType.DMA((2,))],
    )
    return pl.pallas_call(
        _gather_combine_kernel,
        out_shape=jax.ShapeDtypeStruct((n_rows, d), F32),
        grid_spec=grid_spec,
        compiler_params=_cparams(("arbitrary",)),
        name="moe_combine",
    )(dest, xn, mod3, wts, ys)


def _rope_tables(seq, dim, pad):
    t = jnp.arange(seq)
    rows = (t // GRID_W).astype(F32)
    cols = (t % GRID_W).astype(F32)
    quarter = dim // 4
    inv_freq = jnp.exp(-math.log(ROPE_BASE) * jnp.arange(quarter, dtype=F32) / quarter)
    ang = jnp.concatenate([inv_freq[:, None] * rows[None, :], inv_freq[:, None] * cols[None, :]], axis=0)
    cos = jnp.concatenate([jnp.cos(ang), jnp.ones((dim // 2, pad), F32)], axis=1)
    sin = jnp.concatenate([jnp.sin(ang), jnp.zeros((dim // 2, pad), F32)], axis=1)
    return cos, sin


def _win_mask_table(seq):
    nkb = seq // LANES
    nq = seq // WIN_TQ
    tabs = []
    for i in (0, 1, nq - 1):
        t = i * WIN_TQ + np.arange(WIN_TQ)[:, None]
        blk = np.clip(2 * i - 1 + np.arange(4), 0, nkb - 1)
        want = 2 * i - 1 + np.arange(4)
        s = (blk[:, None] * LANES + np.arange(LANES)[None, :]).reshape(-1)[None, :]
        ok = (np.abs(t - s) <= WIN_RADIUS) & np.repeat(blk == want, LANES)[None, :]
        tabs.append(np.where(ok, 0.0, NEG))
    return jnp.asarray(np.stack(tabs), F32)


def _nat_bias_table(rpb, seq, shift):
    rows = seq // GRID_W
    nq = rows // NAT_ROWS_PER_STEP
    wc = NAT_WIN_COLS
    col = np.arange(GRID_W)
    col_start = np.clip(col - wc // 2, 0, GRID_W - wc)
    col_ok = (col[None, :] >= col_start[:, None]) & (col[None, :] < col_start[:, None] + wc)
    d_col = np.clip(col[None, :] - col[:, None] + (wc - 1), 0, 2 * wc - 2)
    sel_r, oks = [], []
    for i in (0, 1, nq - 1):
        r0 = NAT_ROWS_PER_STEP * i
        ws = np.clip(r0 - NAT_WIN_ROWS // 2, 0, rows - NAT_KEY_ROWS)
        r = r0 + np.arange(NAT_ROWS_PER_STEP)
        rs = np.clip(r - NAT_WIN_ROWS // 2, 0, rows - NAT_WIN_ROWS)
        krow = ws + np.arange(NAT_KEY_ROWS)
        row_ok = (krow[None, :] >= rs[:, None]) & (krow[None, :] < rs[:, None] + NAT_WIN_ROWS)
        d_row = np.clip(krow[None, :] - r[:, None] + (NAT_WIN_ROWS - 1), 0, 2 * NAT_WIN_ROWS - 2)
        oks.append(row_ok[:, None, :, None] & col_ok[None, :, None, :])
        sel_r.append(d_row[:, :, None] == np.arange(2 * NAT_WIN_ROWS - 1)[None, None, :])
    sel_r = jnp.asarray(np.stack(sel_r), F32)
    sel_c = jnp.asarray(d_col[:, :, None] == np.arange(2 * wc - 1)[None, None, :], F32)
    cols = jnp.einsum('hrc,vkc->hrvk', rpb.astype(F32), sel_c, precision=lax.Precision.HIGHEST)
    bias = jnp.einsum('tuar,hrvk->thuvak', sel_r, cols, precision=lax.Precision.HIGHEST)
    bias = jnp.where(jnp.asarray(np.stack(oks))[:, None], bias - shift, NEG)
    return bias.reshape(3, rpb.shape[0], NAT_ROWS_PER_STEP * GRID_W, NAT_KEY_ROWS * GRID_W)


def _bcast_rows(v, reps=1):
    return jnp.tile(jnp.broadcast_to(v.astype(F32)[:, None], (v.shape[0], LANES)), (reps, 1))


def _layer_params(l, p):
    w = p['w_in'][l]
    sizes = (512, 128, 128, 512, 512, 512, 512, 512, 512, 256, 160, 4096)
    offs = np.concatenate([[0], np.cumsum(sizes)])
    seg = lambda k: w[:, offs[k]:offs[k + 1]]
    wq, wk, wv, dq, dk, dv, nq, nk, nv, mqa, mkva, gates = [seg(k) for k in range(12)]
    d = w.shape[0]
    dup = lambda m: jnp.concatenate([m[:, :64], m[:, :64], m[:, 64:], m[:, 64:]], axis=1)
    mkva_p = jnp.concatenate([mkva, jnp.zeros((d, 256 - mkva.shape[1]), F32)], axis=1)
    w_a = jnp.concatenate([gates, dv, nv, mqa, mkva_p, dup(wv)], axis=1).astype(BF)
    w_bt = jnp.concatenate([wq, dq, nq, dk, nk, dup(wk)], axis=1).T.astype(BF)
    scale = HEAD_DIM ** -0.5 * LOG2E
    gain_b = jnp.concatenate([
        _bcast_rows(p['win_q_norm'][l] * scale, 8), _bcast_rows(p['dif_q_norm'][l] * scale, 8),
        _bcast_rows(p['nat_q_norm'][l] * scale, 8), _bcast_rows(p['dif_k_norm'][l], 8),
        _bcast_rows(p['nat_k_norm'][l], 8), _bcast_rows(p['win_k_norm'][l], 4)], axis=0)
    wkv = p['mla_wkv_b'][l].reshape(MLA_KV_LORA, MLA_HEADS, MLA_NOPE + MLA_V)
    wk_t = wkv[:, :, :MLA_NOPE].reshape(MLA_KV_LORA, -1).T.astype(BF)
    wv_m = wkv[:, :, MLA_NOPE:].reshape(MLA_KV_LORA, -1).astype(BF)
    def logit_bound(gq, gk, dim):
        return dim * jnp.max(jnp.abs(gq)) * jnp.max(jnp.abs(gk)) * (1.0 + 2.0 ** -7)

    lam_f = p['dif_lambda'][l].astype(F32)
    lam_init = 0.8 - 0.6 * math.exp(-0.3 * l)
    lam = jnp.exp(jnp.sum(lam_f[0] * lam_f[1])) - jnp.exp(jnp.sum(lam_f[2] * lam_f[3])) + lam_init
    return dict(
        w_a=w_a, w_bt=w_bt, gain_b=gain_b,
        g1=p['norm1_g'][l].reshape(1, d), g2=p['norm2_g'][l].reshape(1, d),
        sink=(p['win_sink'][l].astype(F32) * LOG2E).reshape(1, WIN_HEADS),
        gqa=p['mla_q_a_norm'][l].reshape(1, -1), gkva=p['mla_kv_a_norm'][l].reshape(1, -1),
        wq_t=p['mla_wq_b'][l].T.astype(BF), wk_t=wk_t, wv_m=wv_m,
        gq=_bcast_rows(p['mla_q_norm'][l] * (MLA_QK ** -0.5 * LOG2E)), gk=_bcast_rows(p['mla_k_norm'][l]),
        lam=lam.reshape(1, 1).astype(F32), lam_scale=1.0 - lam_init,
        m_win=logit_bound(p['win_q_norm'][l] * scale, p['win_k_norm'][l], HEAD_DIM),
        m_dif=logit_bound(p['dif_q_norm'][l] * scale, p['dif_k_norm'][l], HEAD_DIM),
        m_nat=logit_bound(p['nat_q_norm'][l] * scale, p['nat_k_norm'][l], HEAD_DIM),
        m_mla=logit_bound(p['mla_q_norm'][l] * (MLA_QK ** -0.5 * LOG2E), p['mla_k_norm'][l], MLA_QK),
        subln=p['dif_subln'][l].astype(F32).reshape(1, DIF_V_DIM),
        wb=p['w_branch'][l].astype(BF), wo=p['w_out'][l].astype(BF),
        w1=p['moe_w1'][l].astype(BF), w3=p['moe_w3'][l].astype(BF), w2=p['moe_w2'][l].astype(BF),
    )


def kernel(x, c, ctx, c_ctx, ada_w, ada_b, norm1_g, norm2_g, w_in, win_q_norm, win_k_norm, win_sink,
           dif_q_norm, dif_k_norm, dif_lambda, dif_subln, nat_q_norm, nat_k_norm, nat_rpb,
           mla_q_a_norm, mla_wq_b, mla_kv_a_norm, mla_wkv_b, mla_q_norm, mla_k_norm,
           w_branch, w_out, router_w, router_b, moe_w1, moe_w3, moe_w2):
    p = dict(norm1_g=norm1_g, norm2_g=norm2_g, w_in=w_in, win_q_norm=win_q_norm, win_k_norm=win_k_norm,
             win_sink=win_sink, dif_q_norm=dif_q_norm, dif_k_norm=dif_k_norm, dif_lambda=dif_lambda,
             dif_subln=dif_subln, nat_q_norm=nat_q_norm, nat_k_norm=nat_k_norm,
             mla_q_a_norm=mla_q_a_norm, mla_wq_b=mla_wq_b, mla_kv_a_norm=mla_kv_a_norm,
             mla_wkv_b=mla_wkv_b, mla_q_norm=mla_q_norm, mla_k_norm=mla_k_norm,
             w_branch=w_branch, w_out=w_out, moe_w1=moe_w1, moe_w3=moe_w3, moe_w2=moe_w2)
    n_batch, seq, d = x.shape
    n_ctx = ctx.shape[1]
    depth = ada_w.shape[0]
    n_lat = n_batch * seq
    n_all = n_lat + n_batch * n_ctx
    assert seq % DENSE_TK == 0 and seq % DENSE_TQ == 0 and (n_batch * n_ctx) == TM and seq % TM == 0
    tiles_per_batch = seq // TM
    group_of_tile = lambda i: jnp.minimum(i // tiles_per_batch, n_batch)
    pos_of_tile = lambda i: jnp.where(i < n_batch * tiles_per_batch, i % tiles_per_batch, tiles_per_batch)

    cc = jnp.concatenate([c, c_ctx[None, :], jnp.zeros((8 - n_batch - 1, d), F32)], axis=0)
    mod = _modulation(cc, ada_w, ada_b)
    mod = mod[:, :n_batch + 1].reshape(depth, n_batch + 1, 6, d)
    mod = jnp.pad(mod, ((0, 0), (0, 0), (0, 2), (0, 0)))

    cos_h, sin_h = _rope_tables(seq, HEAD_DIM, TM)
    cos_m, sin_m = _rope_tables(seq, MLA_ROPE, TM)
    win_mask = _win_mask_table(seq)
    rw = jnp.pad(router_w.astype(F32), ((0, 0), (0, LANES - N_EXPERTS)))

    xa = jnp.concatenate([x.reshape(n_lat, d), ctx.reshape(n_batch * n_ctx, d)], axis=0)
    ones = jnp.ones((DENSE_TK, LANES), BF)

    def dense(lam, logit_bound, q_arr, q_col, q_w, k_arr, k_row, k_w, v_arr, v_col, v_w, subln, **kw):
        operands = (q_arr, q_col, q_w, k_arr, k_row, k_w, v_arr, v_col, v_w, subln)
        par = jnp.concatenate([lam, logit_bound.reshape(1, 1).astype(F32)], axis=1)
        y, flag = _dense_fixed_attention(par, *operands, ones, n_batch, seq, n_ctx, **kw)
        return lax.cond(jnp.max(flag) > 0.0,
                        lambda: _dense_attention(lam, *operands, n_batch, seq, n_ctx, **kw),
                        lambda: y)
    for l in range(depth):
        lp = _layer_params(l, p)
        want_ctx = l < depth - 1
        mod3 = mod[l]
        tok_a = _in_proj_a(xa, mod3, lp['g1'], lp['w_a'], group_of_tile)
        q_tok, kt = _in_proj_b(xa, mod3, lp['g1'], lp['w_bt'], lp['gain_b'], cos_h, sin_h,
                               group_of_tile, pos_of_tile)
        mq, mkt, mv = _mla_prep(tok_a, lp['gqa'], lp['gkva'], lp['wq_t'], lp['wk_t'], lp['wv_m'],
                                lp['gq'], lp['gk'], cos_m, sin_m, pos_of_tile)
        rpb2 = nat_rpb[l].astype(F32) * LOG2E
        m_nl = lp['m_nat'] + jnp.maximum(jnp.max(rpb2), 0.0)
        nat_bias_rel = _nat_bias_table(rpb2, seq, m_nl)

        def branches(latent):
            win_args = (q_tok, kt, tok_a)
            if latent:
                m_win = lp['m_win']
                sink_rel = jnp.concatenate([lp['sink'] - m_win, jnp.full((1, 8), m_win, F32)], axis=1)
                y_win, flag = _win_attention(sink_rel, *win_args, win_mask - m_win, n_batch, seq, n_ctx, True,
                                             ones=ones)
                y_win = lax.cond(jnp.max(flag) > 0.0,
                                 lambda: _win_attention(lp['sink'], *win_args, win_mask, n_batch, seq, n_ctx, True),
                                 lambda: y_win)
            else:
                y_win = _win_attention(lp['sink'], *win_args, win_mask, n_batch, seq, n_ctx, False)
            y_dif = dense(lp['lam'], lp['m_dif'], q_tok, 1, 512, kt, KT_DK // 512, 512, tok_a, A_DV // 512, 512,
                          lp['subln'], n_heads=2 * DIF_HEADS, packed=True, diff=True, latent=latent,
                          lam_scale=lp['lam_scale'], name="dif_attn" if latent else "dif_attn_ctx")
            if latent:
                y_nat, flag = _nat_attention(q_tok, kt, tok_a, nat_bias_rel, n_batch, seq, n_ctx,
                                             par=m_nl.reshape(1, 1).astype(F32), ones=ones)
                y_nat = lax.cond(jnp.max(flag) > 0.0,
                                 lambda: _nat_attention(q_tok, kt, tok_a, _nat_bias_table(rpb2, seq, 0.0),
                                                        n_batch, seq, n_ctx),
                                 lambda: y_nat)
            else:
                y_nat = dense(lp['lam'], lp['m_nat'], q_tok, 2, 512, kt, KT_NK // 512, 512, tok_a, A_NV // 512, 512,
                              None, n_heads=NAT_HEADS, packed=True, diff=False, latent=False,
                              lam_scale=1.0, name="nat_attn_ctx")
            y_mla = dense(lp['lam'], lp['m_mla'], mq, 0, 1024, mkt, 0, 1024, mv, 0, 512, None,
                          n_heads=MLA_HEADS, packed=False, diff=False, latent=latent, lam_scale=1.0,
                          name="mla_attn" if latent else "mla_attn_ctx")
            return [y_win, y_dif, y_nat, y_mla]

        ys = branches(True)
        ys_c = branches(False) if want_ctx else None
        n_rows = n_all if want_ctx else n_lat
        xn, h2, scores = _merge(xa, mod3, lp['g2'], ys, ys_c, tok_a, lp['wb'], lp['wo'], rw, n_rows,
                                group_of_tile)
        blk_e, dest, wts, pad_lo, pad_hi, n_used, cap = _route(scores[:, :N_EXPERTS], router_b, n_rows)
        xs = _dispatch(dest, pad_lo, pad_hi, h2, cap)
        ys = _grouped_ffn(blk_e, n_used, xs, lp['w1'], lp['w3'], lp['w2'])
        xa = _gather_combine(dest, xn, mod3, wts, ys, n_rows, group_of_tile)
    return xa[:n_lat].reshape(n_batch, seq, d)
```

```python
import functools
import math

import jax
import jax.numpy as jnp
import numpy as np
from jax import lax
from jax.experimental import pallas as pl
from jax.experimental.pallas import tpu as pltpu

F32 = jnp.float32
BF = jnp.bfloat16

GRID_W = 64
HEAD_DIM = 64
N_BRANCH = 4
BRANCH_W = 512
ROPE_BASE = 10000.0
EPS = 1e-6
NEG = -1e30
LOG2E = math.log2(math.e)
WIN_HEADS, WIN_KV_HEADS, WIN_RADIUS = 8, 2, 128
DIF_HEADS, DIF_QK_DIM, DIF_V_DIM = 4, 64, 128
NAT_HEADS, NAT_WIN_ROWS, NAT_WIN_COLS = 8, 8, 16
MLA_HEADS, MLA_NOPE, MLA_ROPE, MLA_V, MLA_Q_LORA, MLA_KV_LORA = 8, 64, 32, 64, 256, 128
MLA_QK = MLA_NOPE + MLA_ROPE
N_EXPERTS, N_GROUPS, TOP_K, D_EXPERT, MOE_BLOCK = 16, 4, 2, 512, 256

LANES = 128
TM = 512
WIN_TQ = 256
NAT_ROWS_PER_STEP = 4
NAT_KEY_ROWS = 12
DENSE_TQ = 1024
DENSE_TQ_ONLINE = 512
DENSE_TK = 1024
DENSE_RB = 64
DENSE_KC = 256
VMEM_LIMIT = 48 * 1024 * 1024

A_GATES, A_DV, A_NV, A_MQA, A_MKVA, A_WV = 0, 4096, 4608, 5120, 5376, 5632
A_COLS = 5888
Q_COLS = 1536
KT_DK, KT_NK, KT_WK = 0, 512, 1024
KT_ROWS = 1280
B_ROWS = Q_COLS + KT_ROWS


def _cparams(sem, vmem=VMEM_LIMIT):
    return pltpu.CompilerParams(dimension_semantics=sem, vmem_limit_bytes=vmem)


def _nt_dot(a, b):
    return lax.dot_general(a, b, (((1,), (1,)), ((), ())), preferred_element_type=F32)


def _norm_mod(x, g, sc, sh):
    ms = jnp.mean(x * x, axis=-1, keepdims=True)
    return (x * lax.rsqrt(ms + EPS) * g) * (1.0 + sc) + sh


def _lane_tile(a, n):
    reps = n // a.shape[1]
    return a if reps == 1 else jnp.concatenate([a] * reps, axis=1)


def _mod_kernel(c_ref, w_ref, b_ref, o_ref):
    cc = c_ref[...]
    a = cc * jax.nn.sigmoid(cc)
    o_ref[0] = jnp.dot(a, w_ref[0], preferred_element_type=F32,
                       precision=lax.Precision.HIGHEST) + b_ref[0]


def _modulation(cc, ada_w, ada_b):
    n_layers, d, d6 = ada_w.shape
    tn = 1536
    return pl.pallas_call(
        _mod_kernel,
        out_shape=jax.ShapeDtypeStruct((n_layers, 8, d6), F32),
        grid=(n_layers, d6 // tn),
        in_specs=[pl.BlockSpec((8, d), lambda l, j: (0, 0)),
                  pl.BlockSpec((1, d, tn), lambda l, j: (l, 0, j)),
                  pl.BlockSpec((1, 1, tn), lambda l, j: (l, 0, j))],
        out_specs=pl.BlockSpec((1, 8, tn), lambda l, j: (l, 0, j)),
        compiler_params=_cparams(("parallel", "parallel")),
        name="adaln_mod",
    )(cc, ada_w, ada_b.reshape(n_layers, 1, d6))


def _in_a_kernel(x_ref, mod_ref, g_ref, w_ref, o_ref):
    m = mod_ref[0]
    h = _norm_mod(x_ref[...], g_ref[...], m[1:2], m[0:1]).astype(BF)
    o_ref[...] = jnp.dot(h, w_ref[...], preferred_element_type=F32).astype(BF)


def _in_proj_a(xa, mod3, g1, w_a, group_of_tile):
    n, d = xa.shape
    tn = A_COLS // 2
    return pl.pallas_call(
        _in_a_kernel,
        out_shape=jax.ShapeDtypeStruct((n, A_COLS), BF),
        grid=(A_COLS // tn, n // TM),
        in_specs=[pl.BlockSpec((TM, d), lambda j, i: (i, 0)),
                  pl.BlockSpec((1, 8, d), lambda j, i: (group_of_tile(i), 0, 0)),
                  pl.BlockSpec((1, d), lambda j, i: (0, 0)),
                  pl.BlockSpec((d, tn), lambda j, i: (0, j))],
        out_specs=pl.BlockSpec((TM, tn), lambda j, i: (i, j)),
        compiler_params=_cparams(("parallel", "parallel")),
        name="in_proj_tok",
    )(xa, mod3, g1, w_a)


def _head_norm_rope(x, g, cos, sin, rope):
    ss = jnp.sum(x * x, axis=0, keepdims=True)
    y = x * lax.rsqrt(ss * (1.0 / HEAD_DIM) + EPS) * g
    if not rope:
        return y
    half = HEAD_DIM // 2
    y1, y2 = y[:half], y[half:]
    return jnp.concatenate([y1 * cos - y2 * sin, y1 * sin + y2 * cos], axis=0)


def _in_b_kernel(x_ref, mod_ref, g_ref, wt_ref, gain_ref, cos_ref, sin_ref, q_ref, kt_ref, acc_sc):
    m = mod_ref[0]
    h = _norm_mod(x_ref[...], g_ref[...], m[1:2], m[0:1]).astype(BF)
    acc_sc[...] = _nt_dot(wt_ref[...], h)
    tm = h.shape[0]
    cos = cos_ref[...]
    sin = sin_ref[...]

    def pair(r0, rope):
        hs = []
        for e in range(2):
            r = r0 + e * HEAD_DIM
            g = _lane_tile(gain_ref[r:r + HEAD_DIM, :], tm)
            hs.append(_head_norm_rope(acc_sc[r:r + HEAD_DIM, :], g, cos, sin, rope))
        return jnp.concatenate(hs, axis=0)

    for p in range(Q_COLS // LANES):
        y = pair(p * LANES, rope=p < 8)
        q_ref[:, p * LANES:(p + 1) * LANES] = y.T.astype(BF)
    for p in range(KT_ROWS // LANES):
        y = pair(Q_COLS + p * LANES, rope=not (4 <= p < 8))
        kt_ref[p * LANES:(p + 1) * LANES, :] = y.astype(BF)


def _in_proj_b(xa, mod3, g1, w_bt, gain_b, cos_t, sin_t, group_of_tile, pos_of_tile):
    n, d = xa.shape
    return pl.pallas_call(
        _in_b_kernel,
        out_shape=(jax.ShapeDtypeStruct((n, Q_COLS), BF),
                   jax.ShapeDtypeStruct((KT_ROWS, n), BF)),
        grid=(n // TM,),
        in_specs=[pl.BlockSpec((TM, d), lambda i: (i, 0)),
                  pl.BlockSpec((1, 8, d), lambda i: (group_of_tile(i), 0, 0)),
                  pl.BlockSpec((1, d), lambda i: (0, 0)),
                  pl.BlockSpec((B_ROWS, d), lambda i: (0, 0)),
                  pl.BlockSpec((B_ROWS, LANES), lambda i: (0, 0)),
                  pl.BlockSpec((HEAD_DIM // 2, TM), lambda i: (0, pos_of_tile(i))),
                  pl.BlockSpec((HEAD_DIM // 2, TM), lambda i: (0, pos_of_tile(i)))],
        out_specs=(pl.BlockSpec((TM, Q_COLS), lambda i: (i, 0)),
                   pl.BlockSpec((KT_ROWS, TM), lambda i: (0, i))),
        scratch_shapes=[pltpu.VMEM((B_ROWS, TM), F32)],
        compiler_params=_cparams(("parallel",)),
        name="in_proj_heads",
    )(xa, mod3, g1, w_bt, gain_b, cos_t, sin_t)


def _mla_kernel(qa_ref, kva_ref, gqa_ref, gkva_ref, wqt_ref, wkt_ref, wv_ref, gq_ref, gk_ref,
                cos_ref, sin_ref, mq_ref, mkt_ref, mv_ref):
    tm = qa_ref.shape[0]
    cos = cos_ref[...]
    sin = sin_ref[...]
    rh = MLA_ROPE // 2

    def rms_rows(x, g):
        ms = jnp.mean(x * x, axis=-1, keepdims=True)
        return x * lax.rsqrt(ms + EPS) * g

    def rope_rows(x):
        x1, x2 = x[:rh], x[rh:]
        return jnp.concatenate([x1 * cos - x2 * sin, x1 * sin + x2 * cos], axis=0)

    qa = rms_rows(qa_ref[...].astype(F32), gqa_ref[...]).astype(BF)
    qt = _nt_dot(wqt_ref[...], qa)
    kva = kva_ref[...].astype(F32)
    cn = rms_rows(kva[:, :MLA_KV_LORA], gkva_ref[...]).astype(BF)
    knt = _nt_dot(wkt_ref[...], cn)
    mv_ref[...] = jnp.dot(cn, wv_ref[...], preferred_element_type=F32).astype(BF)
    krope = kva[:, MLA_KV_LORA:].T[:MLA_ROPE]
    kr_ss = jnp.sum(krope * krope, axis=0, keepdims=True)
    gq = _lane_tile(gq_ref[...], tm)
    gk = _lane_tile(gk_ref[...], tm)
    zpad = jnp.zeros((LANES - MLA_QK, tm), F32)
    for hd in range(MLA_HEADS):
        x = qt[hd * MLA_QK:(hd + 1) * MLA_QK]
        ss = jnp.sum(x * x, axis=0, keepdims=True)
        y = x * lax.rsqrt(ss * (1.0 / MLA_QK) + EPS) * gq
        y = jnp.concatenate([y[:MLA_NOPE], rope_rows(y[MLA_NOPE:]), zpad], axis=0)
        mq_ref[:, hd * LANES:(hd + 1) * LANES] = y.T.astype(BF)
        kn = knt[hd * MLA_NOPE:(hd + 1) * MLA_NOPE]
        ss = jnp.sum(kn * kn, axis=0, keepdims=True) + kr_ss
        r = lax.rsqrt(ss * (1.0 / MLA_QK) + EPS)
        yk = jnp.concatenate([kn * r * gk[:MLA_NOPE], rope_rows(krope * r * gk[MLA_NOPE:]), zpad], axis=0)
        mkt_ref[hd * LANES:(hd + 1) * LANES, :] = yk.astype(BF)


def _mla_prep(tok_a, gqa, gkva, wq_t, wk_t, wv, gq, gk, cos_t, sin_t, pos_of_tile):
    n = tok_a.shape[0]
    hw = MLA_HEADS * LANES
    return pl.pallas_call(
        _mla_kernel,
        out_shape=(jax.ShapeDtypeStruct((n, hw), BF),
                   jax.ShapeDtypeStruct((hw, n), BF),
                   jax.ShapeDtypeStruct((n, MLA_HEADS * MLA_V), BF)),
        grid=(n // TM,),
        in_specs=[pl.BlockSpec((TM, 256), lambda i: (i, A_MQA // 256)),
                  pl.BlockSpec((TM, 256), lambda i: (i, A_MKVA // 256)),
                  pl.BlockSpec((1, MLA_Q_LORA), lambda i: (0, 0)),
                  pl.BlockSpec((1, MLA_KV_LORA), lambda i: (0, 0)),
                  pl.BlockSpec(wq_t.shape, lambda i: (0, 0)),
                  pl.BlockSpec(wk_t.shape, lambda i: (0, 0)),
                  pl.BlockSpec(wv.shape, lambda i: (0, 0)),
                  pl.BlockSpec((MLA_QK, LANES), lambda i: (0, 0)),
                  pl.BlockSpec((MLA_QK, LANES), lambda i: (0, 0)),
                  pl.BlockSpec((MLA_ROPE // 2, TM), lambda i: (0, pos_of_tile(i))),
                  pl.BlockSpec((MLA_ROPE // 2, TM), lambda i: (0, pos_of_tile(i)))],
        out_specs=(pl.BlockSpec((TM, hw), lambda i: (i, 0)),
                   pl.BlockSpec((hw, TM), lambda i: (0, i)),
                   pl.BlockSpec((TM, MLA_HEADS * MLA_V), lambda i: (i, 0))),
        compiler_params=_cparams(("parallel",)),
        name="mla_prep",
    )(tok_a, tok_a, gqa, gkva, wq_t, wk_t, wv, gq, gk, cos_t, sin_t)


def _half_mask(shape):
    return lax.broadcasted_iota(jnp.int32, shape, 1) < (LANES // 2)


def _select_half(q, e, lo_mask):
    zero = jnp.zeros_like(q)
    return jnp.where(lo_mask, q, zero) if e == 0 else jnp.where(lo_mask, zero, q)


def _local_softmax_out(parts, extra_logit):
    m = parts[0][0].max(axis=-1, keepdims=True)
    for s, _ in parts[1:]:
        m = jnp.maximum(m, s.max(axis=-1, keepdims=True))
    if extra_logit is not None:
        m = jnp.maximum(m, extra_logit)
    z = None
    o = None
    for s, v in parts:
        p = jnp.exp2(s - m)
        zs = p.sum(axis=-1, keepdims=True)
        os_ = jnp.dot(p.astype(BF), v, preferred_element_type=F32)
        z = zs if z is None else z + zs
        o = os_ if o is None else o + os_
    if extra_logit is not None:
        z = z + jnp.exp2(extra_logit - m)
    return o / z


def _fixed_softmax_out(parts, ones_ref, extra_logit):
    acc = None
    for s, v in parts:
        aug = jnp.concatenate([v, ones_ref[:s.shape[1], :]], axis=1)
        t = jnp.dot(jnp.exp2(s).astype(BF), aug, preferred_element_type=F32)
        acc = t if acc is None else acc + t
    l = acc[:, LANES:]
    if extra_logit is not None:
        l = l + jnp.exp2(jnp.zeros_like(l) + extra_logit)
    unsafe = jnp.logical_not((l > 2.0 ** -SAFE_SUM_LOG2) & (l < 2.0 ** SAFE_SUM_LOG2))
    return acc[:, :LANES] / l, jnp.max(jnp.where(unsafe, 1.0, 0.0), axis=0, keepdims=True)


def _win_kernel(sink_ref, q_ref, *refs, band, fixed=False):
    if band:
        k0, k1, k2, k3, v0, v1, v2, v3, kc_ref, vc_ref, mask_ref = refs[:11]
        refs = refs[11:]
        kb = jnp.concatenate([k0[...], k1[...], k2[...], k3[...]], axis=1)
        vb = jnp.concatenate([v0[...], v1[...], v2[...], v3[...]], axis=0)
        mask = mask_ref[0]
    else:
        kc_ref, vc_ref = refs[:2]
        refs = refs[2:]
    if fixed:
        ones_ref, o_ref, flag_ref = refs
        ref_logit = sink_ref[0, WIN_HEADS]
    else:
        (o_ref,) = refs
    q = q_ref[...]
    lo = _half_mask((q.shape[0], LANES))
    group = WIN_HEADS // WIN_KV_HEADS
    bad = []
    for j in range(WIN_HEADS // 2):
        qp = q[:, j * LANES:(j + 1) * LANES]
        g = (2 * j) // group
        kc = kc_ref[g * LANES:(g + 1) * LANES, :]
        vc = vc_ref[:, g * LANES:(g + 1) * LANES]
        outs = []
        for e in range(2):
            qm = _select_half(qp, e, lo)
            parts = []
            if band:
                s = jnp.dot(qm, kb[g * LANES:(g + 1) * LANES, :], preferred_element_type=F32) + mask
                parts.append((s, vb[:, g * LANES:(g + 1) * LANES]))
            sc = jnp.dot(qm, kc, preferred_element_type=F32)
            if fixed:
                parts.append((sc - ref_logit, vc))
                o, u = _fixed_softmax_out(parts, ones_ref, sink_ref[0, 2 * j + e])
                bad.append(u)
                outs.append(o)
            else:
                parts.append((sc, vc))
                outs.append(_local_softmax_out(parts, sink_ref[0, 2 * j + e]))
        o_ref[:, j * LANES:(j + 1) * LANES] = jnp.where(lo, outs[0], outs[1]).astype(BF)
    if fixed:
        flag_ref[0] = jnp.concatenate(bad, axis=0)


def _win_attention(sink, q_tok, kt, tok_a, mask_tbl, n_batch, seq, n_ctx, latent, ones=None):
    ctx_blk = (n_batch * seq) // n_ctx
    kc_spec = lambda f: pl.BlockSpec((2 * LANES, n_ctx), f)
    vc_spec = lambda f: pl.BlockSpec((n_ctx, 2 * LANES), f)
    smem = pl.BlockSpec(memory_space=pltpu.SMEM)
    n_out = n_batch * (seq if latent else n_ctx)
    out_shape = jax.ShapeDtypeStruct((n_out, WIN_HEADS * HEAD_DIM), BF)
    if not latent:
        return pl.pallas_call(
            functools.partial(_win_kernel, band=False),
            out_shape=out_shape,
            grid=(n_batch,),
            in_specs=[smem,
                      pl.BlockSpec((n_ctx, 512), lambda b: (ctx_blk + b, 0)),
                      kc_spec(lambda b: (KT_WK // 256, ctx_blk + b)),
                      vc_spec(lambda b: (ctx_blk + b, A_WV // 256))],
            out_specs=pl.BlockSpec((n_ctx, 512), lambda b: (b, 0)),
            compiler_params=_cparams(("parallel",)),
            name="win_attn_ctx",
        )(sink, q_tok, kt, tok_a)
    nq = seq // WIN_TQ
    nkb = seq // LANES

    def kidx(j):
        return lambda b, i: (KT_WK // 256, b * nkb + jnp.clip(2 * i - 1 + j, 0, nkb - 1))

    def vidx(j):
        return lambda b, i: (b * nkb + jnp.clip(2 * i - 1 + j, 0, nkb - 1), A_WV // 256)

    def variant(b, i):
        return (jnp.where(i == 0, 0, jnp.where(i == nq - 1, 2, 1)), 0, 0)

    in_specs = ([smem, pl.BlockSpec((WIN_TQ, 512), lambda b, i: (b * nq + i, 0))]
                + [pl.BlockSpec((2 * LANES, LANES), kidx(j)) for j in range(4)]
                + [pl.BlockSpec((LANES, 2 * LANES), vidx(j)) for j in range(4)]
                + [kc_spec(lambda b, i: (KT_WK // 256, ctx_blk + b)),
                   vc_spec(lambda b, i: (ctx_blk + b, A_WV // 256)),
                   pl.BlockSpec((1, WIN_TQ, 4 * LANES), variant)])
    args = [sink, q_tok, kt, kt, kt, kt, tok_a, tok_a, tok_a, tok_a, kt, tok_a, mask_tbl]
    out_specs = pl.BlockSpec((WIN_TQ, 512), lambda b, i: (b * nq + i, 0))
    fixed = ones is not None
    if fixed:
        in_specs.append(pl.BlockSpec(ones.shape, lambda b, i: (0, 0)))
        args.append(ones)
        out_shape = (out_shape, jax.ShapeDtypeStruct((n_batch * nq, WIN_HEADS, LANES), F32))
        out_specs = (out_specs, pl.BlockSpec((1, WIN_HEADS, LANES), lambda b, i: (b * nq + i, 0, 0)))
    return pl.pallas_call(
        functools.partial(_win_kernel, band=True, fixed=fixed),
        out_shape=out_shape,
        grid=(n_batch, nq),
        in_specs=in_specs,
        out_specs=out_specs,
        compiler_params=_cparams(("parallel", "parallel")),
        name="win_attn_fixed" if fixed else "win_attn",
    )(*args)


def _nat_kernel(q_ref, k0, k1, k2, v0, v1, v2, kc_ref, vc_ref, bias_ref, *refs, fixed=False):
    if fixed:
        par_ref, ones_ref, o_ref, flag_ref = refs
        ref_logit = par_ref[0, 0]
    else:
        (o_ref,) = refs
    q = q_ref[...]
    kb = jnp.concatenate([k0[...], k1[...], k2[...]], axis=1)
    vb = jnp.concatenate([v0[...], v1[...], v2[...]], axis=0)
    lo = _half_mask((q.shape[0], LANES))
    bad = []
    for j in range(NAT_HEADS // 2):
        sl = slice(j * LANES, (j + 1) * LANES)
        qp = q[:, sl]
        outs = []
        for e in range(2):
            qm = _select_half(qp, e, lo)
            s = jnp.dot(qm, kb[sl, :], preferred_element_type=F32) + bias_ref[0, 2 * j + e]
            sc = jnp.dot(qm, kc_ref[sl, :], preferred_element_type=F32)
            if fixed:
                o, u = _fixed_softmax_out([(s, vb[:, sl]), (sc - ref_logit, vc_ref[:, sl])], ones_ref, None)
                bad.append(u)
                outs.append(o)
            else:
                outs.append(_local_softmax_out([(s, vb[:, sl]), (sc, vc_ref[:, sl])], None))
        o_ref[:, sl] = jnp.where(lo, outs[0], outs[1]).astype(BF)
    if fixed:
        flag_ref[0] = jnp.concatenate(bad, axis=0)


def _nat_attention(q_tok, kt, tok_a, bias_tbl, n_batch, seq, n_ctx, par=None, ones=None):
    tq = NAT_ROWS_PER_STEP * GRID_W
    nq = seq // tq
    rows = seq // GRID_W
    ctx_blk = (n_batch * seq) // n_ctx
    q_col = 2
    k_row = KT_NK // 512
    v_col = A_NV // 512

    def wstart(i):
        return jnp.clip(NAT_ROWS_PER_STEP * i - NAT_WIN_ROWS // 2, 0, rows - NAT_KEY_ROWS) // NAT_ROWS_PER_STEP

    def kidx(j):
        return lambda b, i: (k_row, b * nq + wstart(i) + j)

    def vidx(j):
        return lambda b, i: (b * nq + wstart(i) + j, v_col)

    def variant(b, i):
        return (jnp.where(i == 0, 0, jnp.where(i == nq - 1, 2, 1)), 0, 0, 0)

    nk = NAT_KEY_ROWS * GRID_W
    in_specs = ([pl.BlockSpec((tq, 512), lambda b, i: (b * nq + i, q_col))]
                + [pl.BlockSpec((512, tq), kidx(j)) for j in range(3)]
                + [pl.BlockSpec((tq, 512), vidx(j)) for j in range(3)]
                + [pl.BlockSpec((512, n_ctx), lambda b, i: (k_row, ctx_blk + b)),
                   pl.BlockSpec((n_ctx, 512), lambda b, i: (ctx_blk + b, v_col)),
                   pl.BlockSpec((1, NAT_HEADS, tq, nk), variant)])
    args = [q_tok, kt, kt, kt, tok_a, tok_a, tok_a, kt, tok_a, bias_tbl]
    out_shape = jax.ShapeDtypeStruct((n_batch * seq, NAT_HEADS * HEAD_DIM), BF)
    out_specs = pl.BlockSpec((tq, 512), lambda b, i: (b * nq + i, 0))
    fixed = ones is not None
    if fixed:
        in_specs += [pl.BlockSpec(memory_space=pltpu.SMEM), pl.BlockSpec(ones.shape, lambda b, i: (0, 0))]
        args += [par, ones]
        out_shape = (out_shape, jax.ShapeDtypeStruct((n_batch * nq, NAT_HEADS, LANES), F32))
        out_specs = (out_specs, pl.BlockSpec((1, NAT_HEADS, LANES), lambda b, i: (b * nq + i, 0, 0)))
    return pl.pallas_call(
        functools.partial(_nat_kernel, fixed=fixed),
        out_shape=out_shape,
        grid=(n_batch, nq),
        in_specs=in_specs,
        out_specs=out_specs,
        compiler_params=_cparams(("parallel", "parallel")),
        name="nat_attn_fixed" if fixed else "nat_attn",
    )(*args)


def _dense_kernel(lam_ref, q_ref, kc_ref, vc_ref, *refs, n_heads, packed, diff, latent, lam_scale):
    if latent:
        k_ref, v_ref = refs[0], refs[1]
        refs = refs[2:]
    if diff:
        subln_ref, o_ref, qm_sc, m_sc, l_sc, acc_sc, s_sc, p_sc = refs
    else:
        o_ref, qm_sc, m_sc, l_sc, acc_sc, s_sc, p_sc = refs
    kt_step = pl.program_id(2) if latent else 0
    tq = q_ref.shape[0]

    def kv_slices(h):
        blk = h // 2 if packed else h
        ks = slice(blk * LANES, (blk + 1) * LANES)
        vs = ks if packed else slice((h // 2) * LANES, (h // 2 + 1) * LANES)
        return ks, vs

    def step(h, k_ref_, v_ref_):
        ks, vs = kv_slices(h)
        nk = k_ref_.shape[1]
        slot = h % 2
        s_sc[slot, :, :nk] = jnp.dot(qm_sc[h], k_ref_[ks, :], preferred_element_type=F32)
        for r in range(tq // DENSE_RB):
            rows = slice(r * DENSE_RB, (r + 1) * DENSE_RB)
            mx = s_sc[slot, rows, 0:LANES]
            for c in range(1, nk // LANES):
                mx = jnp.maximum(mx, s_sc[slot, rows, c * LANES:(c + 1) * LANES])
            m_old = m_sc[h, rows, :]
            m_new = jnp.maximum(m_old, jnp.max(mx, axis=-1, keepdims=True))
            alpha = jnp.exp2(m_old - m_new)
            lsum = None
            for c in range(nk // LANES):
                cols = slice(c * LANES, (c + 1) * LANES)
                p = jnp.exp2(s_sc[slot, rows, cols] - m_new)
                lsum = p if lsum is None else lsum + p
                p_sc[slot, rows, cols] = p.astype(BF)
            m_sc[h, rows, :] = m_new
            l_sc[h, rows, :] = alpha * l_sc[h, rows, :] + jnp.sum(lsum, axis=-1, keepdims=True)
            acc_sc[h, rows, :] = alpha * acc_sc[h, rows, :]
        acc_sc[h] += jnp.dot(p_sc[slot, :, :nk], v_ref_[:, vs], preferred_element_type=F32)

    @pl.when(kt_step == 0)
    def _():
        q = q_ref[...]
        lo = _half_mask((tq, LANES))
        for h in range(n_heads):
            if packed:
                qp = q[:, (h // 2) * LANES:(h // 2 + 1) * LANES]
                qm_sc[h] = _select_half(qp, h % 2, lo)
            else:
                qm_sc[h] = q[:, h * LANES:(h + 1) * LANES]
        m_sc[...] = jnp.full(m_sc.shape, NEG, F32)
        l_sc[...] = jnp.zeros(l_sc.shape, F32)
        acc_sc[...] = jnp.zeros(acc_sc.shape, F32)
        for h in range(n_heads):
            step(h, kc_ref, vc_ref)

    if latent:
        for h in range(n_heads):
            step(h, k_ref, v_ref)
        last = kt_step == pl.num_programs(2) - 1
    else:
        last = True

    def finish():
        lo = _half_mask((tq, LANES))
        if diff:
            lam = lam_ref[0, 0]
            for hv in range(n_heads // 2):
                y = (acc_sc[2 * hv] / l_sc[2 * hv]
                     - lam * (acc_sc[2 * hv + 1] / l_sc[2 * hv + 1]))
                ms = jnp.mean(y * y, axis=-1, keepdims=True)
                y = y * lax.rsqrt(ms + EPS) * subln_ref[...] * lam_scale
                o_ref[:, hv * LANES:(hv + 1) * LANES] = y.astype(BF)
        else:
            for hp in range(n_heads // 2):
                o0 = acc_sc[2 * hp] / l_sc[2 * hp]
                o1 = acc_sc[2 * hp + 1] / l_sc[2 * hp + 1]
                o_ref[:, hp * LANES:(hp + 1) * LANES] = jnp.where(lo, o0, o1).astype(BF)

    if latent:
        pl.when(last)(finish)
    else:
        finish()


def _dense_attention(lam, q_arr, q_col, q_w, k_arr, k_row, k_w, v_arr, v_col, v_w, subln,
                     n_batch, seq, n_ctx, *, n_heads, packed, diff, latent, lam_scale, name):
    n = n_batch * (seq if latent else n_ctx)
    ctx_blk = (n_batch * seq) // n_ctx
    out_w = v_w
    smem = pl.BlockSpec(memory_space=pltpu.SMEM)
    kern = functools.partial(_dense_kernel, n_heads=n_heads, packed=packed, diff=diff,
                             latent=latent, lam_scale=lam_scale)
    tq = DENSE_TQ_ONLINE if latent else n_ctx
    max_nk = DENSE_TK if latent else n_ctx
    scratch = [pltpu.VMEM((n_heads, tq, LANES), BF),
               pltpu.VMEM((n_heads, tq, LANES), F32),
               pltpu.VMEM((n_heads, tq, LANES), F32),
               pltpu.VMEM((n_heads, tq, LANES), F32),
               pltpu.VMEM((2, tq, max_nk), F32),
               pltpu.VMEM((2, tq, max_nk), BF)]
    out_shape = jax.ShapeDtypeStruct((n, out_w), BF)
    if latent:
        nq = seq // tq
        nk = seq // DENSE_TK
        grid = (n_batch, nq, nk)
        in_specs = [smem,
                    pl.BlockSpec((tq, q_w), lambda b, i, k: (b * nq + i, q_col)),
                    pl.BlockSpec((k_w, n_ctx), lambda b, i, k: (k_row, ctx_blk + b)),
                    pl.BlockSpec((n_ctx, v_w), lambda b, i, k: (ctx_blk + b, v_col)),
                    pl.BlockSpec((k_w, DENSE_TK), lambda b, i, k: (k_row, b * nk + k)),
                    pl.BlockSpec((DENSE_TK, v_w), lambda b, i, k: (b * nk + k, v_col))]
        args = [lam, q_arr, k_arr, v_arr, k_arr, v_arr]
        if diff:
            in_specs.append(pl.BlockSpec((1, LANES), lambda b, i, k: (0, 0)))
            args.append(subln)
        out_specs = pl.BlockSpec((tq, out_w), lambda b, i, k: (b * nq + i, 0))
        sem = ("parallel", "parallel", "arbitrary")
    else:
        grid = (n_batch,)
        in_specs = [smem,
                    pl.BlockSpec((tq, q_w), lambda b: (ctx_blk + b, q_col)),
                    pl.BlockSpec((k_w, n_ctx), lambda b: (k_row, ctx_blk + b)),
                    pl.BlockSpec((n_ctx, v_w), lambda b: (ctx_blk + b, v_col))]
        args = [lam, q_arr, k_arr, v_arr]
        if diff:
            in_specs.append(pl.BlockSpec((1, LANES), lambda b: (0, 0)))
            args.append(subln)
        out_specs = pl.BlockSpec((tq, out_w), lambda b: (b, 0))
        sem = ("parallel",)
    return pl.pallas_call(
        kern, out_shape=out_shape, grid=grid, in_specs=in_specs, out_specs=out_specs,
        scratch_shapes=scratch, compiler_params=_cparams(sem), name=name,
    )(*args)


SAFE_SUM_LOG2 = 100.0


def _dense_fixed_kernel(par_ref, q_ref, kc_ref, vc_ref, *refs, n_heads, packed, diff, latent, lam_scale):
    if latent:
        k_ref, v_ref = refs[0], refs[1]
        refs = refs[2:]
    ones_ref = refs[0]
    refs = refs[1:]
    if diff:
        subln_ref, o_ref, flag_ref, qm_sc, acc_sc, p_sc = refs
    else:
        o_ref, flag_ref, qm_sc, acc_sc, p_sc = refs
    kt_step = pl.program_id(2) if latent else 0
    tq = q_ref.shape[0]
    ref_logit = par_ref[0, 1]

    def step(h, k_ref_, v_ref_, first):
        blk = h // 2 if packed else h
        ks = slice(blk * LANES, (blk + 1) * LANES)
        vs = ks if packed else slice((h // 2) * LANES, (h // 2 + 1) * LANES)
        nk = k_ref_.shape[1]
        slot = h % 2
        for c in range(nk // DENSE_KC):
            cols = slice(c * DENSE_KC, (c + 1) * DENSE_KC)
            s = jnp.dot(qm_sc[h], k_ref_[ks, cols], preferred_element_type=F32)
            p_sc[slot, :, cols] = jnp.exp2(s - ref_logit).astype(BF)
        v_aug = jnp.concatenate([v_ref_[:, vs], ones_ref[:nk, :]], axis=1)
        pv = jnp.dot(p_sc[slot, :, :nk], v_aug, preferred_element_type=F32)
        if first:
            acc_sc[h] = pv
        else:
            acc_sc[h] += pv

    @pl.when(kt_step == 0)
    def _():
        q = q_ref[...]
        lo = _half_mask((tq, LANES))
        for h in range(n_heads):
            if packed:
                qp = q[:, (h // 2) * LANES:(h // 2 + 1) * LANES]
                qm_sc[h] = _select_half(qp, h % 2, lo)
            else:
                qm_sc[h] = q[:, h * LANES:(h + 1) * LANES]
        for h in range(n_heads):
            step(h, kc_ref, vc_ref, True)

    if latent:
        for h in range(n_heads):
            step(h, k_ref, v_ref, False)
        last = kt_step == pl.num_programs(2) - 1

    def finish():
        lo = _half_mask((tq, LANES))
        outs, bad = [], []
        for h in range(n_heads):
            a = acc_sc[h]
            l = a[:, LANES:]
            unsafe = jnp.logical_not((l > 2.0 ** -SAFE_SUM_LOG2) & (l < 2.0 ** SAFE_SUM_LOG2))
            bad.append(jnp.max(jnp.where(unsafe, 1.0, 0.0), axis=0, keepdims=True))
            outs.append(a[:, :LANES] / l)
        flag_ref[0] = jnp.concatenate(bad, axis=0)
        if diff:
            lam = par_ref[0, 0]
            for hv in range(n_heads // 2):
                y = outs[2 * hv] - lam * outs[2 * hv + 1]
                ms = jnp.mean(y * y, axis=-1, keepdims=True)
                y = y * lax.rsqrt(ms + EPS) * subln_ref[...] * lam_scale
                o_ref[:, hv * LANES:(hv + 1) * LANES] = y.astype(BF)
        else:
            for hp in range(n_heads // 2):
                o_ref[:, hp * LANES:(hp + 1) * LANES] = jnp.where(lo, outs[2 * hp], outs[2 * hp + 1]).astype(BF)

    if latent:
        pl.when(last)(finish)
    else:
        finish()


def _dense_fixed_attention(par, q_arr, q_col, q_w, k_arr, k_row, k_w, v_arr, v_col, v_w, subln, ones,
                           n_batch, seq, n_ctx, *, n_heads, packed, diff, latent, lam_scale, name):
    n = n_batch * (seq if latent else n_ctx)
    ctx_blk = (n_batch * seq) // n_ctx
    smem = pl.BlockSpec(memory_space=pltpu.SMEM)
    kern = functools.partial(_dense_fixed_kernel, n_heads=n_heads, packed=packed, diff=diff,
                             latent=latent, lam_scale=lam_scale)
    tq = DENSE_TQ if latent else n_ctx
    max_nk = DENSE_TK if latent else n_ctx
    scratch = [pltpu.VMEM((n_heads, tq, LANES), BF),
               pltpu.VMEM((n_heads, tq, 2 * LANES), F32),
               pltpu.VMEM((2, tq, max_nk), BF)]
    if latent:
        nq = seq // tq
        nk = seq // DENSE_TK
        grid = (n_batch, nq, nk)
        ix = lambda f: (lambda b, i, k: f(b, i, k))
        q_ix = ix(lambda b, i, k: (b * nq + i, q_col))
        in_specs = [smem,
                    pl.BlockSpec((tq, q_w), q_ix),
                    pl.BlockSpec((k_w, n_ctx), ix(lambda b, i, k: (k_row, ctx_blk + b))),
                    pl.BlockSpec((n_ctx, v_w), ix(lambda b, i, k: (ctx_blk + b, v_col))),
                    pl.BlockSpec((k_w, DENSE_TK), ix(lambda b, i, k: (k_row, b * nk + k))),
                    pl.BlockSpec((DENSE_TK, v_w), ix(lambda b, i, k: (b * nk + k, v_col))),
                    pl.BlockSpec((DENSE_TK, LANES), ix(lambda b, i, k: (0, 0)))]
        args = [par, q_arr, k_arr, v_arr, k_arr, v_arr, ones]
        const_ix = ix(lambda b, i, k: (0, 0))
        out_specs = (pl.BlockSpec((tq, v_w), ix(lambda b, i, k: (b * nq + i, 0))),
                     pl.BlockSpec((1, n_heads, LANES), ix(lambda b, i, k: (b * nq + i, 0, 0))))
        n_flag = n_batch * nq
        sem = ("parallel", "parallel", "arbitrary")
    else:
        grid = (n_batch,)
        in_specs = [smem,
                    pl.BlockSpec((tq, q_w), lambda b: (ctx_blk + b, q_col)),
                    pl.BlockSpec((k_w, n_ctx), lambda b: (k_row, ctx_blk + b)),
                    pl.BlockSpec((n_ctx, v_w), lambda b: (ctx_blk + b, v_col)),
                    pl.BlockSpec((DENSE_TK, LANES), lambda b: (0, 0))]
        args = [par, q_arr, k_arr, v_arr, ones]
        const_ix = lambda b: (0, 0)
        out_specs = (pl.BlockSpec((tq, v_w), lambda b: (b, 0)),
                     pl.BlockSpec((1, n_heads, LANES), lambda b: (b, 0, 0)))
        n_flag = n_batch
        sem = ("parallel",)
    if diff:
        in_specs.append(pl.BlockSpec((1, LANES), const_ix))
        args.append(subln)
    return pl.pallas_call(
        kern,
        out_shape=(jax.ShapeDtypeStruct((n, v_w), BF), jax.ShapeDtypeStruct((n_flag, n_heads, LANES), F32)),
        grid=grid, in_specs=in_specs, out_specs=out_specs,
        scratch_shapes=scratch, compiler_params=_cparams(sem), name=name + "_fixed",
    )(*args)


def _merge_kernel(x_ref, mod_ref, g2_ref, *refs, n_lat_tiles, has_ctx):
    ys = refs[:N_BRANCH]
    refs = refs[N_BRANCH:]
    if has_ctx:
        ycs = refs[:N_BRANCH]
        refs = refs[N_BRANCH:]
        is_ctx = pl.program_id(0) >= n_lat_tiles
    gts = refs[:N_BRANCH]
    wb_ref, wo_ref, rw_ref, xo_ref, h2_ref, sc_ref = refs[N_BRANCH:]
    m = mod_ref[0]
    mix = None
    for n_ in range(N_BRANCH):
        y = ys[n_][...]
        if has_ctx:
            y = jnp.where(is_ctx, ycs[n_][...], y)
        yb = jnp.dot(y, wb_ref[n_], preferred_element_type=F32)
        t = jax.nn.sigmoid(gts[n_][...].astype(F32)) * yb
        mix = t if mix is None else mix + t
    att = jnp.dot(mix.astype(BF), wo_ref[...], preferred_element_type=F32)
    xn = x_ref[...] + m[2:3] * att
    xo_ref[...] = xn
    h2 = _norm_mod(xn, g2_ref[...], m[4:5], m[3:4])
    h2_ref[...] = h2
    h_hi = h2.astype(BF)
    h_lo = (h2 - h_hi.astype(F32)).astype(BF)
    logits = (jnp.dot(h_hi, rw_ref[0], preferred_element_type=F32)
              + (jnp.dot(h_lo, rw_ref[0], preferred_element_type=F32)
                 + jnp.dot(h_hi, rw_ref[1], preferred_element_type=F32)))
    sc_ref[...] = jax.nn.sigmoid(logits)


def _merge(xa, mod3, g2, ys, ys_ctx, tok_a, wb, wo, rw, n_rows, group_of_tile):
    d = xa.shape[1]
    n_lat_tiles = ys[0].shape[0] // TM
    has_ctx = ys_ctx is not None
    row = lambda w, c: pl.BlockSpec((TM, w), lambda i, c=c: (i, c))
    lat_row = pl.BlockSpec((TM, BRANCH_W), lambda i: (jnp.minimum(i, n_lat_tiles - 1), 0))
    in_specs = ([row(d, 0),
                 pl.BlockSpec((1, 8, d), lambda i: (group_of_tile(i), 0, 0)),
                 pl.BlockSpec((1, d), lambda i: (0, 0))]
                + [lat_row for _ in range(N_BRANCH)]
                + ([pl.BlockSpec((TM, BRANCH_W), lambda i: (0, 0)) for _ in range(N_BRANCH)] if has_ctx else [])
                + [row(d, c) for c in range(N_BRANCH)]
                + [pl.BlockSpec(wb.shape, lambda i: (0, 0, 0)),
                   pl.BlockSpec(wo.shape, lambda i: (0, 0)),
                   pl.BlockSpec(rw.shape, lambda i: (0, 0, 0))])
    ys = list(ys) + (list(ys_ctx) if has_ctx else [])
    return pl.pallas_call(
        functools.partial(_merge_kernel, n_lat_tiles=n_lat_tiles, has_ctx=has_ctx),
        out_shape=(jax.ShapeDtypeStruct((n_rows, d), F32),
                   jax.ShapeDtypeStruct((n_rows, d), F32),
                   jax.ShapeDtypeStruct((n_rows, LANES), F32)),
        grid=(n_rows // TM,),
        in_specs=in_specs,
        out_specs=(row(d, 0), row(d, 0), row(LANES, 0)),
        compiler_params=_cparams(("parallel",)),
        name="merge",
    )(xa, mod3, g2, *ys, tok_a, tok_a, tok_a, tok_a, wb, wo, rw)


def _route(scores, router_b, n):
    per_group = N_EXPERTS // N_GROUPS
    st = scores.T
    biased = st + router_b.astype(F32)[:, None]

    def top2(v, axis):
        pos = lax.broadcasted_iota(jnp.int32, v.shape, axis)
        i0 = jnp.argmax(v, axis=axis).astype(jnp.int32)
        v0 = jnp.max(v, axis=axis)
        rest = jnp.where(pos == jnp.expand_dims(i0, axis), -jnp.inf, v)
        i1 = jnp.argmax(rest, axis=axis).astype(jnp.int32)
        v1 = jnp.max(rest, axis=axis)
        return (v0, v1), (i0, i1)

    (g0, g1), _ = top2(biased.reshape(N_GROUPS, per_group, n), 1)
    group = jnp.argmax(g0 + g1, axis=0)
    expert = jnp.arange(N_EXPERTS, dtype=jnp.int32)[:, None]
    in_group = (expert // per_group) == group[None, :]
    _, (e0, e1) = top2(jnp.where(in_group, biased, -jnp.inf), 0)
    oh0 = (expert == e0[None, :]).astype(F32)
    oh1 = (expert == e1[None, :]).astype(F32)
    w0 = (oh0 * st).sum(0)
    w1 = (oh1 * st).sum(0)
    wts = jnp.stack([w0, w1], axis=-1) / (w0 + w1)[:, None]
    n_asg = n * TOP_K
    cnt = (oh0 + oh1).reshape(N_EXPERTS, n // MOE_BLOCK, MOE_BLOCK)
    tri = jnp.triu(jnp.ones((MOE_BLOCK, MOE_BLOCK), F32), 1)
    within = jnp.einsum('ebj,ji->ebi', cnt, tri)
    blk_tot = cnt.sum(axis=-1)
    nb = blk_tot.shape[1]
    blk_off = jnp.einsum('eb,bc->ec', blk_tot, jnp.triu(jnp.ones((nb, nb), F32), 1),
                         precision=lax.Precision.HIGHEST)
    prefix = (within + blk_off[:, :, None]).reshape(N_EXPERTS, n)
    counts = blk_tot.sum(axis=-1).astype(jnp.int32)
    padded = (counts + MOE_BLOCK - 1) // MOE_BLOCK * MOE_BLOCK
    pad_end = jnp.cumsum(padded)
    pad_start = pad_end - padded
    slot = prefix + pad_start.astype(F32)[:, None]
    dest = jnp.stack([(oh0 * slot).sum(0), (oh1 * slot).sum(0)], axis=-1).reshape(n_asg)
    n_blk = (n_asg + N_EXPERTS * (MOE_BLOCK - 1) + MOE_BLOCK - 1) // MOE_BLOCK
    cap = n_blk * MOE_BLOCK
    blk_start = jnp.arange(n_blk, dtype=jnp.int32) * MOE_BLOCK
    blk_e = jnp.minimum((blk_start[:, None] >= pad_end[None, :]).sum(-1), N_EXPERTS - 1).astype(jnp.int32)
    pad_lo = jnp.concatenate([pad_start + counts, pad_end[-1:]]).astype(jnp.int32)
    pad_hi = jnp.concatenate([pad_end, jnp.full((1,), cap)]).astype(jnp.int32)
    n_used = (pad_end[-1:] // MOE_BLOCK).astype(jnp.int32)
    return blk_e, dest.astype(jnp.int32), wts.astype(F32), pad_lo, pad_hi, n_used, cap


def _dispatch_kernel(dest_ref, pad_lo_ref, pad_hi_ref, h2_ref, xs_hbm, zrow, sem, zsem):
    i = pl.program_id(0)

    @pl.when(i == 0)
    def _():
        zrow[...] = jnp.zeros(zrow.shape, F32)
        for e in range(N_EXPERTS + 1):
            lo, hi = pad_lo_ref[e], pad_hi_ref[e]

            def zero_row(r, c):
                pltpu.make_async_copy(zrow.at[pl.ds(0, 1)], xs_hbm.at[pl.ds(r, 1)], zsem).start()
                return c

            def zero_wait(r, c):
                pltpu.make_async_copy(zrow.at[pl.ds(0, 1)], xs_hbm.at[pl.ds(r, 1)], zsem).wait()
                return c

            lax.fori_loop(lo, hi, zero_row, 0)
            lax.fori_loop(lo, hi, zero_wait, 0)

    tm = h2_ref.shape[0]

    def copy_row(r, c):
        a = (i * tm + r) * TOP_K
        for k in range(TOP_K):
            pltpu.make_async_copy(h2_ref.at[pl.ds(r, 1)], xs_hbm.at[pl.ds(dest_ref[a + k], 1)], sem).start()
        return c

    lax.fori_loop(0, tm, copy_row, 0, unroll=4)
    for k in range(TOP_K):
        pltpu.make_async_copy(h2_ref, xs_hbm.at[pl.ds(0, tm)], sem).wait()


def _dispatch(dest, pad_lo, pad_hi, h2, cap):
    n, d = h2.shape
    grid_spec = pltpu.PrefetchScalarGridSpec(
        num_scalar_prefetch=3,
        grid=(n // TM,),
        in_specs=[pl.BlockSpec((TM, d), lambda i, de, lo, hi: (i, 0))],
        out_specs=pl.BlockSpec(memory_space=pl.ANY),
        scratch_shapes=[pltpu.VMEM((8, d), F32), pltpu.SemaphoreType.DMA, pltpu.SemaphoreType.DMA],
    )
    return pl.pallas_call(
        _dispatch_kernel,
        out_shape=jax.ShapeDtypeStruct((cap, d), F32),
        grid_spec=grid_spec,
        compiler_params=_cparams(("arbitrary",)),
        name="moe_dispatch",
    )(dest, pad_lo, pad_hi, h2)


def _grouped_ffn_kernel(blk_e_ref, n_used_ref, x_ref, w1_ref, w3_ref, w2_ref, o_ref):
    i = pl.program_id(0)

    @pl.when(i < n_used_ref[0])
    def _():
        xb = x_ref[...].astype(BF)
        a = jnp.dot(xb, w1_ref[0], preferred_element_type=F32)
        b = jnp.dot(xb, w3_ref[0], preferred_element_type=F32)
        hmid = (a * jax.nn.sigmoid(a) * b).astype(BF)
        o_ref[...] = jnp.dot(hmid, w2_ref[0], preferred_element_type=F32)

    @pl.when(i >= n_used_ref[0])
    def _():
        o_ref[...] = jnp.zeros(o_ref.shape, F32)


def _grouped_ffn(blk_e, n_used, xs, w1, w3, w2):
    cap, d = xs.shape
    wspec = lambda shp: pl.BlockSpec((1,) + shp, lambda i, be, nu: (be[i], 0, 0))
    grid_spec = pltpu.PrefetchScalarGridSpec(
        num_scalar_prefetch=2,
        grid=(cap // MOE_BLOCK,),
        in_specs=[pl.BlockSpec((MOE_BLOCK, d), lambda i, be, nu: (jnp.minimum(i, nu[0] - 1), 0)),
                  wspec((d, D_EXPERT)), wspec((d, D_EXPERT)), wspec((D_EXPERT, d))],
        out_specs=pl.BlockSpec((MOE_BLOCK, d), lambda i, be, nu: (i, 0)),
    )
    return pl.pallas_call(
        _grouped_ffn_kernel,
        out_shape=jax.ShapeDtypeStruct((cap, d), F32),
        grid_spec=grid_spec,
        compiler_params=_cparams(("arbitrary",)),
        name="moe_ffn",
    )(blk_e, n_used, xs, w1, w3, w2)


def _gather_combine_kernel(dest_ref, x_ref, mod_ref, w_ref, ys_hbm, o_ref, buf, sem):
    i = pl.program_id(0)
    n_steps = pl.num_programs(0)
    slot = i % 2
    tm = x_ref.shape[0]

    def start_gather(tile, s):
        def body(r, c):
            a = (tile * tm + r) * TOP_K
            for k in range(TOP_K):
                pltpu.make_async_copy(ys_hbm.at[pl.ds(dest_ref[a + k], 1)], buf.at[s, k, pl.ds(r, 1)],
                                      sem.at[s]).start()
            return c
        lax.fori_loop(0, tm, body, 0, unroll=4)

    @pl.when(i == 0)
    def _():
        start_gather(0, 0)

    @pl.when(i + 1 < n_steps)
    def _():
        start_gather(i + 1, 1 - slot)

    for k in range(TOP_K):
        pltpu.make_async_copy(ys_hbm.at[pl.ds(0, tm)], buf.at[slot, k], sem.at[slot]).wait()
    w = w_ref[...]
    f = w[:, 0:1] * buf[slot, 0] + w[:, 1:2] * buf[slot, 1]
    o_ref[...] = x_ref[...] + mod_ref[0][5:6] * f


def _gather_combine(dest, xn, mod3, wts, ys, n_rows, group_of_tile):
    d = xn.shape[1]
    grid_spec = pltpu.PrefetchScalarGridSpec(
        num_scalar_prefetch=1,
        grid=(n_rows // TM,),
        in_specs=[pl.BlockSpec((TM, d), lambda i, de: (i, 0)),
                  pl.BlockSpec((1, 8, d), lambda i, de: (group_of_tile(i), 0, 0)),
                  pl.BlockSpec((TM, TOP_K), lambda i, de: (i, 0)),
                  pl.BlockSpec(memory_space=pl.ANY)],
        out_specs=pl.BlockSpec((TM, d), lambda i, de: (i, 0)),
        scratch_shapes=[pltpu.VMEM((2, TOP_K, TM, d), F32), pltpu.SemaphoreType.DMA((2,))],
    )
    return pl.pallas_call(
        _gather_combine_kernel,
        out_shape=jax.ShapeDtypeStruct((n_rows, d), F32),
        grid_spec=grid_spec,
        compiler_params=_cparams(("arbitrary",)),
        name="moe_combine",
    )(dest, xn, mod3, wts, ys)


def _rope_tables(seq, dim, pad):
    t = jnp.arange(seq)
    rows = (t // GRID_W).astype(F32)
    cols = (t % GRID_W).astype(F32)
    quarter = dim // 4
    inv_freq = jnp.exp(-math.log(ROPE_BASE) * jnp.arange(quarter, dtype=F32) / quarter)
    ang = jnp.concatenate([inv_freq[:, None] * rows[None, :], inv_freq[:, None] * cols[None, :]], axis=0)
    cos = jnp.concatenate([jnp.cos(ang), jnp.ones((dim // 2, pad), F32)], axis=1)
    sin = jnp.concatenate([jnp.sin(ang), jnp.zeros((dim // 2, pad), F32)], axis=1)
    return cos, sin


def _win_mask_table(seq):
    nkb = seq // LANES
    nq = seq // WIN_TQ
    tabs = []
    for i in (0, 1, nq - 1):
        t = i * WIN_TQ + np.arange(WIN_TQ)[:, None]
        blk = np.clip(2 * i - 1 + np.arange(4), 0, nkb - 1)
        want = 2 * i - 1 + np.arange(4)
        s = (blk[:, None] * LANES + np.arange(LANES)[None, :]).reshape(-1)[None, :]
        ok = (np.abs(t - s) <= WIN_RADIUS) & np.repeat(blk == want, LANES)[None, :]
        tabs.append(np.where(ok, 0.0, NEG))
    return jnp.asarray(np.stack(tabs), F32)


def _nat_bias_table(rpb, seq, shift):
    rows = seq // GRID_W
    nq = rows // NAT_ROWS_PER_STEP
    wc = NAT_WIN_COLS
    col = np.arange(GRID_W)
    col_start = np.clip(col - wc // 2, 0, GRID_W - wc)
    col_ok = (col[None, :] >= col_start[:, None]) & (col[None, :] < col_start[:, None] + wc)
    d_col = np.clip(col[None, :] - col[:, None] + (wc - 1), 0, 2 * wc - 2)
    sel_r, oks = [], []
    for i in (0, 1, nq - 1):
        r0 = NAT_ROWS_PER_STEP * i
        ws = np.clip(r0 - NAT_WIN_ROWS // 2, 0, rows - NAT_KEY_ROWS)
        r = r0 + np.arange(NAT_ROWS_PER_STEP)
        rs = np.clip(r - NAT_WIN_ROWS // 2, 0, rows - NAT_WIN_ROWS)
        krow = ws + np.arange(NAT_KEY_ROWS)
        row_ok = (krow[None, :] >= rs[:, None]) & (krow[None, :] < rs[:, None] + NAT_WIN_ROWS)
        d_row = np.clip(krow[None, :] - r[:, None] + (NAT_WIN_ROWS - 1), 0, 2 * NAT_WIN_ROWS - 2)
        oks.append(row_ok[:, None, :, None] & col_ok[None, :, None, :])
        sel_r.append(d_row[:, :, None] == np.arange(2 * NAT_WIN_ROWS - 1)[None, None, :])
    sel_r = jnp.asarray(np.stack(sel_r), F32)
    sel_c = jnp.asarray(d_col[:, :, None] == np.arange(2 * wc - 1)[None, None, :], F32)
    cols = jnp.einsum('hrc,vkc->hrvk', rpb.astype(F32), sel_c, precision=lax.Precision.HIGHEST)
    bias = jnp.einsum('tuar,hrvk->thuvak', sel_r, cols, precision=lax.Precision.HIGHEST)
    bias = jnp.where(jnp.asarray(np.stack(oks))[:, None], bias - shift, NEG)
    return bias.reshape(3, rpb.shape[0], NAT_ROWS_PER_STEP * GRID_W, NAT_KEY_ROWS * GRID_W)


def _bcast_rows(v, reps=1):
    return jnp.tile(jnp.broadcast_to(v.astype(F32)[:, None], (v.shape[0], LANES)), (reps, 1))


def _layer_params(l, p):
    w = p['w_in'][l]
    sizes = (512, 128, 128, 512, 512, 512, 512, 512, 512, 256, 160, 4096)
    offs = np.concatenate([[0], np.cumsum(sizes)])
    seg = lambda k: w[:, offs[k]:offs[k + 1]]
    wq, wk, wv, dq, dk, dv, nq, nk, nv, mqa, mkva, gates = [seg(k) for k in range(12)]
    d = w.shape[0]
    dup = lambda m: jnp.concatenate([m[:, :64], m[:, :64], m[:, 64:], m[:, 64:]], axis=1)
    mkva_p = jnp.concatenate([mkva, jnp.zeros((d, 256 - mkva.shape[1]), F32)], axis=1)
    w_a = jnp.concatenate([gates, dv, nv, mqa, mkva_p, dup(wv)], axis=1).astype(BF)
    w_bt = jnp.concatenate([wq, dq, nq, dk, nk, dup(wk)], axis=1).T.astype(BF)
    scale = HEAD_DIM ** -0.5 * LOG2E
    gain_b = jnp.concatenate([
        _bcast_rows(p['win_q_norm'][l] * scale, 8), _bcast_rows(p['dif_q_norm'][l] * scale, 8),
        _bcast_rows(p['nat_q_norm'][l] * scale, 8), _bcast_rows(p['dif_k_norm'][l], 8),
        _bcast_rows(p['nat_k_norm'][l], 8), _bcast_rows(p['win_k_norm'][l], 4)], axis=0)
    wkv = p['mla_wkv_b'][l].reshape(MLA_KV_LORA, MLA_HEADS, MLA_NOPE + MLA_V)
    wk_t = wkv[:, :, :MLA_NOPE].reshape(MLA_KV_LORA, -1).T.astype(BF)
    wv_m = wkv[:, :, MLA_NOPE:].reshape(MLA_KV_LORA, -1).astype(BF)
    def logit_bound(gq, gk, dim):
        return dim * jnp.max(jnp.abs(gq)) * jnp.max(jnp.abs(gk)) * (1.0 + 2.0 ** -7)

    lam_f = p['dif_lambda'][l].astype(F32)
    lam_init = 0.8 - 0.6 * math.exp(-0.3 * l)
    lam = jnp.exp(jnp.sum(lam_f[0] * lam_f[1])) - jnp.exp(jnp.sum(lam_f[2] * lam_f[3])) + lam_init
    return dict(
        w_a=w_a, w_bt=w_bt, gain_b=gain_b,
        g1=p['norm1_g'][l].reshape(1, d), g2=p['norm2_g'][l].reshape(1, d),
        sink=(p['win_sink'][l].astype(F32) * LOG2E).reshape(1, WIN_HEADS),
        gqa=p['mla_q_a_norm'][l].reshape(1, -1), gkva=p['mla_kv_a_norm'][l].reshape(1, -1),
        wq_t=p['mla_wq_b'][l].T.astype(BF), wk_t=wk_t, wv_m=wv_m,
        gq=_bcast_rows(p['mla_q_norm'][l] * (MLA_QK ** -0.5 * LOG2E)), gk=_bcast_rows(p['mla_k_norm'][l]),
        lam=lam.reshape(1, 1).astype(F32), lam_scale=1.0 - lam_init,
        m_win=logit_bound(p['win_q_norm'][l] * scale, p['win_k_norm'][l], HEAD_DIM),
        m_dif=logit_bound(p['dif_q_norm'][l] * scale, p['dif_k_norm'][l], HEAD_DIM),
        m_nat=logit_bound(p['nat_q_norm'][l] * scale, p['nat_k_norm'][l], HEAD_DIM),
        m_mla=logit_bound(p['mla_q_norm'][l] * (MLA_QK ** -0.5 * LOG2E), p['mla_k_norm'][l], MLA_QK),
        subln=p['dif_subln'][l].astype(F32).reshape(1, DIF_V_DIM),
        wb=p['w_branch'][l].astype(BF), wo=p['w_out'][l].astype(BF),
        w1=p['moe_w1'][l].astype(BF), w3=p['moe_w3'][l].astype(BF), w2=p['moe_w2'][l].astype(BF),
    )


def kernel(x, c, ctx, c_ctx, ada_w, ada_b, norm1_g, norm2_g, w_in, win_q_norm, win_k_norm, win_sink,
           dif_q_norm, dif_k_norm, dif_lambda, dif_subln, nat_q_norm, nat_k_norm, nat_rpb,
           mla_q_a_norm, mla_wq_b, mla_kv_a_norm, mla_wkv_b, mla_q_norm, mla_k_norm,
           w_branch, w_out, router_w, router_b, moe_w1, moe_w3, moe_w2):
    p = dict(norm1_g=norm1_g, norm2_g=norm2_g, w_in=w_in, win_q_norm=win_q_norm, win_k_norm=win_k_norm,
             win_sink=win_sink, dif_q_norm=dif_q_norm, dif_k_norm=dif_k_norm, dif_lambda=dif_lambda,
             dif_subln=dif_subln, nat_q_norm=nat_q_norm, nat_k_norm=nat_k_norm,
             mla_q_a_norm=mla_q_a_norm, mla_wq_b=mla_wq_b, mla_kv_a_norm=mla_kv_a_norm,
             mla_wkv_b=mla_wkv_b, mla_q_norm=mla_q_norm, mla_k_norm=mla_k_norm,
             w_branch=w_branch, w_out=w_out, moe_w1=moe_w1, moe_w3=moe_w3, moe_w2=moe_w2)
    n_batch, seq, d = x.shape
    n_ctx = ctx.shape[1]
    depth = ada_w.shape[0]
    n_lat = n_batch * seq
    n_all = n_lat + n_batch * n_ctx
    assert seq % DENSE_TK == 0 and seq % DENSE_TQ == 0 and (n_batch * n_ctx) == TM and seq % TM == 0
    tiles_per_batch = seq // TM
    group_of_tile = lambda i: jnp.minimum(i // tiles_per_batch, n_batch)
    pos_of_tile = lambda i: jnp.where(i < n_batch * tiles_per_batch, i % tiles_per_batch, tiles_per_batch)

    cc = jnp.concatenate([c, c_ctx[None, :], jnp.zeros((8 - n_batch - 1, d), F32)], axis=0)
    mod = _modulation(cc, ada_w, ada_b)
    mod = mod[:, :n_batch + 1].reshape(depth, n_batch + 1, 6, d)
    mod = jnp.pad(mod, ((0, 0), (0, 0), (0, 2), (0, 0)))

    cos_h, sin_h = _rope_tables(seq, HEAD_DIM, TM)
    cos_m, sin_m = _rope_tables(seq, MLA_ROPE, TM)
    win_mask = _win_mask_table(seq)
    rw = jnp.pad(router_w.astype(F32), ((0, 0), (0, LANES - N_EXPERTS)))
    rw_hi = rw.astype(BF)
    rw = jnp.stack([rw_hi, (rw - rw_hi.astype(F32)).astype(BF)])

    xa = jnp.concatenate([x.reshape(n_lat, d), ctx.reshape(n_batch * n_ctx, d)], axis=0)
    ones = jnp.ones((DENSE_TK, LANES), BF)

    def dense(lam, logit_bound, q_arr, q_col, q_w, k_arr, k_row, k_w, v_arr, v_col, v_w, subln, **kw):
        operands = (q_arr, q_col, q_w, k_arr, k_row, k_w, v_arr, v_col, v_w, subln)
        par = jnp.concatenate([lam, logit_bound.reshape(1, 1).astype(F32)], axis=1)
        y, flag = _dense_fixed_attention(par, *operands, ones, n_batch, seq, n_ctx, **kw)
        return lax.cond(jnp.max(flag) > 0.0,
                        lambda: _dense_attention(lam, *operands, n_batch, seq, n_ctx, **kw),
                        lambda: y)
    for l in range(depth):
        lp = _layer_params(l, p)
        want_ctx = l < depth - 1
        mod3 = mod[l]
        tok_a = _in_proj_a(xa, mod3, lp['g1'], lp['w_a'], group_of_tile)
        q_tok, kt = _in_proj_b(xa, mod3, lp['g1'], lp['w_bt'], lp['gain_b'], cos_h, sin_h,
                               group_of_tile, pos_of_tile)
        mq, mkt, mv = _mla_prep(tok_a, lp['gqa'], lp['gkva'], lp['wq_t'], lp['wk_t'], lp['wv_m'],
                                lp['gq'], lp['gk'], cos_m, sin_m, pos_of_tile)
        rpb2 = nat_rpb[l].astype(F32) * LOG2E
        m_nl = lp['m_nat'] + jnp.maximum(jnp.max(rpb2), 0.0)
        nat_bias_rel = _nat_bias_table(rpb2, seq, m_nl)

        def branches(latent):
            win_args = (q_tok, kt, tok_a)
            if latent:
                m_win = lp['m_win']
                sink_rel = jnp.concatenate([lp['sink'] - m_win, jnp.full((1, 8), m_win, F32)], axis=1)
                y_win, flag = _win_attention(sink_rel, *win_args, win_mask - m_win, n_batch, seq, n_ctx, True,
                                             ones=ones)
                y_win = lax.cond(jnp.max(flag) > 0.0,
                                 lambda: _win_attention(lp['sink'], *win_args, win_mask, n_batch, seq, n_ctx, True),
                                 lambda: y_win)
            else:
                y_win = _win_attention(lp['sink'], *win_args, win_mask, n_batch, seq, n_ctx, False)
            y_dif = dense(lp['lam'], lp['m_dif'], q_tok, 1, 512, kt, KT_DK // 512, 512, tok_a, A_DV // 512, 512,
                          lp['subln'], n_heads=2 * DIF_HEADS, packed=True, diff=True, latent=latent,
                          lam_scale=lp['lam_scale'], name="dif_attn" if latent else "dif_attn_ctx")
            if latent:
                y_nat, flag = _nat_attention(q_tok, kt, tok_a, nat_bias_rel, n_batch, seq, n_ctx,
                                             par=m_nl.reshape(1, 1).astype(F32), ones=ones)
                y_nat = lax.cond(jnp.max(flag) > 0.0,
                                 lambda: _nat_attention(q_tok, kt, tok_a, _nat_bias_table(rpb2, seq, 0.0),
                                                        n_batch, seq, n_ctx),
                                 lambda: y_nat)
            else:
                y_nat = dense(lp['lam'], lp['m_nat'], q_tok, 2, 512, kt, KT_NK // 512, 512, tok_a, A_NV // 512, 512,
                              None, n_heads=NAT_HEADS, packed=True, diff=False, latent=False,
                              lam_scale=1.0, name="nat_attn_ctx")
            y_mla = dense(lp['lam'], lp['m_mla'], mq, 0, 1024, mkt, 0, 1024, mv, 0, 512, None,
                          n_heads=MLA_HEADS, packed=False, diff=False, latent=latent, lam_scale=1.0,
                          name="mla_attn" if latent else "mla_attn_ctx")
            return [y_win, y_dif, y_nat, y_mla]

        ys = branches(True)
        ys_c = branches(False) if want_ctx else None
        n_rows = n_all if want_ctx else n_lat
        xn, h2, scores = _merge(xa, mod3, lp['g2'], ys, ys_c, tok_a, lp['wb'], lp['wo'], rw, n_rows,
                                group_of_tile)
        blk_e, dest, wts, pad_lo, pad_hi, n_used, cap = _route(scores[:, :N_EXPERTS], router_b, n_rows)
        xs = _dispatch(dest, pad_lo, pad_hi, h2, cap)
        ys = _grouped_ffn(blk_e, n_used, xs, lp['w1'], lp['w3'], lp['w2'])
        xa = _gather_combine(dest, xn, mod3, wts, ys, n_rows, group_of_tile)
    return xa[:n_lat].reshape(n_batch, seq, d)
```

```python
import functools
import math

import jax
import jax.numpy as jnp
import numpy as np
from jax import lax
from jax.experimental import pallas as pl
from jax.experimental.pallas import tpu as pltpu

F32 = jnp.float32
BF = jnp.bfloat16

GRID_W = 64
HEAD_DIM = 64
N_BRANCH = 4
BRANCH_W = 512
ROPE_BASE = 10000.0
EPS = 1e-6
NEG = -1e30
LOG2E = math.log2(math.e)
WIN_HEADS, WIN_KV_HEADS, WIN_RADIUS = 8, 2, 128
DIF_HEADS, DIF_QK_DIM, DIF_V_DIM = 4, 64, 128
NAT_HEADS, NAT_WIN_ROWS, NAT_WIN_COLS = 8, 8, 16
MLA_HEADS, MLA_NOPE, MLA_ROPE, MLA_V, MLA_Q_LORA, MLA_KV_LORA = 8, 64, 32, 64, 256, 128
MLA_QK = MLA_NOPE + MLA_ROPE
N_EXPERTS, N_GROUPS, TOP_K, D_EXPERT, MOE_BLOCK = 16, 4, 2, 512, 256

LANES = 128
TM = 512
WIN_TQ = 256
NAT_ROWS_PER_STEP = 4
NAT_KEY_ROWS = 12
DENSE_TQ = 1024
DENSE_TQ_ONLINE = 512
DENSE_TK = 2048
DENSE_TK_ONLINE = 1024
DENSE_RB = 64
DENSE_KC = 256
VMEM_LIMIT = 48 * 1024 * 1024

A_GATES, A_DV, A_NV, A_MQA, A_MKVA, A_WV = 0, 4096, 4608, 5120, 5376, 5632
A_COLS = 5888
Q_COLS = 1536
KT_DK, KT_NK, KT_WK = 0, 512, 1024
KT_ROWS = 1280
B_ROWS = Q_COLS + KT_ROWS


def _cparams(sem, vmem=VMEM_LIMIT):
    return pltpu.CompilerParams(dimension_semantics=sem, vmem_limit_bytes=vmem)


def _nt_dot(a, b):
    return lax.dot_general(a, b, (((1,), (1,)), ((), ())), preferred_element_type=F32)


def _norm_mod(x, g, sc, sh):
    ms = jnp.mean(x * x, axis=-1, keepdims=True)
    return (x * lax.rsqrt(ms + EPS) * g) * (1.0 + sc) + sh


def _lane_tile(a, n):
    reps = n // a.shape[1]
    return a if reps == 1 else jnp.concatenate([a] * reps, axis=1)


def _mod_kernel(c_ref, w_ref, b_ref, o_ref):
    cc = c_ref[...]
    a = cc * jax.nn.sigmoid(cc)
    o_ref[0] = jnp.dot(a, w_ref[0], preferred_element_type=F32,
                       precision=lax.Precision.HIGHEST) + b_ref[0]


def _modulation(cc, ada_w, ada_b):
    n_layers, d, d6 = ada_w.shape
    tn = 1536
    return pl.pallas_call(
        _mod_kernel,
        out_shape=jax.ShapeDtypeStruct((n_layers, 8, d6), F32),
        grid=(n_layers, d6 // tn),
        in_specs=[pl.BlockSpec((8, d), lambda l, j: (0, 0)),
                  pl.BlockSpec((1, d, tn), lambda l, j: (l, 0, j)),
                  pl.BlockSpec((1, 1, tn), lambda l, j: (l, 0, j))],
        out_specs=pl.BlockSpec((1, 8, tn), lambda l, j: (l, 0, j)),
        compiler_params=_cparams(("parallel", "parallel")),
        name="adaln_mod",
    )(cc, ada_w, ada_b.reshape(n_layers, 1, d6))


def _in_a_kernel(x_ref, mod_ref, g_ref, w_ref, o_ref):
    m = mod_ref[0]
    h = _norm_mod(x_ref[...], g_ref[...], m[1:2], m[0:1]).astype(BF)
    o_ref[...] = jnp.dot(h, w_ref[...], preferred_element_type=F32).astype(BF)


def _in_proj_a(xa, mod3, g1, w_a, group_of_tile):
    n, d = xa.shape
    tn = A_COLS // 2
    return pl.pallas_call(
        _in_a_kernel,
        out_shape=jax.ShapeDtypeStruct((n, A_COLS), BF),
        grid=(A_COLS // tn, n // TM),
        in_specs=[pl.BlockSpec((TM, d), lambda j, i: (i, 0)),
                  pl.BlockSpec((1, 8, d), lambda j, i: (group_of_tile(i), 0, 0)),
                  pl.BlockSpec((1, d), lambda j, i: (0, 0)),
                  pl.BlockSpec((d, tn), lambda j, i: (0, j))],
        out_specs=pl.BlockSpec((TM, tn), lambda j, i: (i, j)),
        compiler_params=_cparams(("parallel", "parallel")),
        name="in_proj_tok",
    )(xa, mod3, g1, w_a)


def _head_norm_rope(x, g, cos, sin, rope):
    ss = jnp.sum(x * x, axis=0, keepdims=True)
    y = x * lax.rsqrt(ss * (1.0 / HEAD_DIM) + EPS) * g
    if not rope:
        return y
    half = HEAD_DIM // 2
    y1, y2 = y[:half], y[half:]
    return jnp.concatenate([y1 * cos - y2 * sin, y1 * sin + y2 * cos], axis=0)


def _in_b_kernel(x_ref, mod_ref, g_ref, wt_ref, gain_ref, cos_ref, sin_ref, q_ref, kt_ref, acc_sc):
    m = mod_ref[0]
    h = _norm_mod(x_ref[...], g_ref[...], m[1:2], m[0:1]).astype(BF)
    acc_sc[...] = _nt_dot(wt_ref[...], h)
    tm = h.shape[0]
    cos = cos_ref[...]
    sin = sin_ref[...]

    def pair(r0, rope):
        hs = []
        for e in range(2):
            r = r0 + e * HEAD_DIM
            g = _lane_tile(gain_ref[r:r + HEAD_DIM, :], tm)
            hs.append(_head_norm_rope(acc_sc[r:r + HEAD_DIM, :], g, cos, sin, rope))
        return jnp.concatenate(hs, axis=0)

    for p in range(Q_COLS // LANES):
        y = pair(p * LANES, rope=p < 8)
        q_ref[:, p * LANES:(p + 1) * LANES] = y.T.astype(BF)
    for p in range(KT_ROWS // LANES):
        y = pair(Q_COLS + p * LANES, rope=not (4 <= p < 8))
        kt_ref[p * LANES:(p + 1) * LANES, :] = y.astype(BF)


def _in_proj_b(xa, mod3, g1, w_bt, gain_b, cos_t, sin_t, group_of_tile, pos_of_tile):
    n, d = xa.shape
    return pl.pallas_call(
        _in_b_kernel,
        out_shape=(jax.ShapeDtypeStruct((n, Q_COLS), BF),
                   jax.ShapeDtypeStruct((KT_ROWS, n), BF)),
        grid=(n // TM,),
        in_specs=[pl.BlockSpec((TM, d), lambda i: (i, 0)),
                  pl.BlockSpec((1, 8, d), lambda i: (group_of_tile(i), 0, 0)),
                  pl.BlockSpec((1, d), lambda i: (0, 0)),
                  pl.BlockSpec((B_ROWS, d), lambda i: (0, 0)),
                  pl.BlockSpec((B_ROWS, LANES), lambda i: (0, 0)),
                  pl.BlockSpec((HEAD_DIM // 2, TM), lambda i: (0, pos_of_tile(i))),
                  pl.BlockSpec((HEAD_DIM // 2, TM), lambda i: (0, pos_of_tile(i)))],
        out_specs=(pl.BlockSpec((TM, Q_COLS), lambda i: (i, 0)),
                   pl.BlockSpec((KT_ROWS, TM), lambda i: (0, i))),
        scratch_shapes=[pltpu.VMEM((B_ROWS, TM), F32)],
        compiler_params=_cparams(("parallel",)),
        name="in_proj_heads",
    )(xa, mod3, g1, w_bt, gain_b, cos_t, sin_t)


def _mla_kernel(qa_ref, kva_ref, gqa_ref, gkva_ref, wqt_ref, wkt_ref, wv_ref, gq_ref, gk_ref,
                cos_ref, sin_ref, mq_ref, mkt_ref, mv_ref):
    tm = qa_ref.shape[0]
    cos = cos_ref[...]
    sin = sin_ref[...]
    rh = MLA_ROPE // 2

    def rms_rows(x, g):
        ms = jnp.mean(x * x, axis=-1, keepdims=True)
        return x * lax.rsqrt(ms + EPS) * g

    def rope_rows(x):
        x1, x2 = x[:rh], x[rh:]
        return jnp.concatenate([x1 * cos - x2 * sin, x1 * sin + x2 * cos], axis=0)

    qa = rms_rows(qa_ref[...].astype(F32), gqa_ref[...]).astype(BF)
    qt = _nt_dot(wqt_ref[...], qa)
    kva = kva_ref[...].astype(F32)
    cn = rms_rows(kva[:, :MLA_KV_LORA], gkva_ref[...]).astype(BF)
    knt = _nt_dot(wkt_ref[...], cn)
    mv_ref[...] = jnp.dot(cn, wv_ref[...], preferred_element_type=F32).astype(BF)
    krope = kva[:, MLA_KV_LORA:].T[:MLA_ROPE]
    kr_ss = jnp.sum(krope * krope, axis=0, keepdims=True)
    gq = _lane_tile(gq_ref[...], tm)
    gk = _lane_tile(gk_ref[...], tm)
    zpad = jnp.zeros((LANES - MLA_QK, tm), F32)
    for hd in range(MLA_HEADS):
        x = qt[hd * MLA_QK:(hd + 1) * MLA_QK]
        ss = jnp.sum(x * x, axis=0, keepdims=True)
        y = x * lax.rsqrt(ss * (1.0 / MLA_QK) + EPS) * gq
        y = jnp.concatenate([y[:MLA_NOPE], rope_rows(y[MLA_NOPE:]), zpad], axis=0)
        mq_ref[:, hd * LANES:(hd + 1) * LANES] = y.T.astype(BF)
        kn = knt[hd * MLA_NOPE:(hd + 1) * MLA_NOPE]
        ss = jnp.sum(kn * kn, axis=0, keepdims=True) + kr_ss
        r = lax.rsqrt(ss * (1.0 / MLA_QK) + EPS)
        yk = jnp.concatenate([kn * r * gk[:MLA_NOPE], rope_rows(krope * r * gk[MLA_NOPE:]), zpad], axis=0)
        mkt_ref[hd * LANES:(hd + 1) * LANES, :] = yk.astype(BF)


def _mla_prep(tok_a, gqa, gkva, wq_t, wk_t, wv, gq, gk, cos_t, sin_t, pos_of_tile):
    n = tok_a.shape[0]
    hw = MLA_HEADS * LANES
    return pl.pallas_call(
        _mla_kernel,
        out_shape=(jax.ShapeDtypeStruct((n, hw), BF),
                   jax.ShapeDtypeStruct((hw, n), BF),
                   jax.ShapeDtypeStruct((n, MLA_HEADS * MLA_V), BF)),
        grid=(n // TM,),
        in_specs=[pl.BlockSpec((TM, 256), lambda i: (i, A_MQA // 256)),
                  pl.BlockSpec((TM, 256), lambda i: (i, A_MKVA // 256)),
                  pl.BlockSpec((1, MLA_Q_LORA), lambda i: (0, 0)),
                  pl.BlockSpec((1, MLA_KV_LORA), lambda i: (0, 0)),
                  pl.BlockSpec(wq_t.shape, lambda i: (0, 0)),
                  pl.BlockSpec(wk_t.shape, lambda i: (0, 0)),
                  pl.BlockSpec(wv.shape, lambda i: (0, 0)),
                  pl.BlockSpec((MLA_QK, LANES), lambda i: (0, 0)),
                  pl.BlockSpec((MLA_QK, LANES), lambda i: (0, 0)),
                  pl.BlockSpec((MLA_ROPE // 2, TM), lambda i: (0, pos_of_tile(i))),
                  pl.BlockSpec((MLA_ROPE // 2, TM), lambda i: (0, pos_of_tile(i)))],
        out_specs=(pl.BlockSpec((TM, hw), lambda i: (i, 0)),
                   pl.BlockSpec((hw, TM), lambda i: (0, i)),
                   pl.BlockSpec((TM, MLA_HEADS * MLA_V), lambda i: (i, 0))),
        compiler_params=_cparams(("parallel",)),
        name="mla_prep",
    )(tok_a, tok_a, gqa, gkva, wq_t, wk_t, wv, gq, gk, cos_t, sin_t)


def _half_mask(shape):
    return lax.broadcasted_iota(jnp.int32, shape, 1) < (LANES // 2)


def _select_half(q, e, lo_mask):
    zero = jnp.zeros_like(q)
    return jnp.where(lo_mask, q, zero) if e == 0 else jnp.where(lo_mask, zero, q)


def _local_softmax_out(parts, extra_logit):
    m = parts[0][0].max(axis=-1, keepdims=True)
    for s, _ in parts[1:]:
        m = jnp.maximum(m, s.max(axis=-1, keepdims=True))
    if extra_logit is not None:
        m = jnp.maximum(m, extra_logit)
    z = None
    o = None
    for s, v in parts:
        p = jnp.exp2(s - m)
        zs = p.sum(axis=-1, keepdims=True)
        os_ = jnp.dot(p.astype(BF), v, preferred_element_type=F32)
        z = zs if z is None else z + zs
        o = os_ if o is None else o + os_
    if extra_logit is not None:
        z = z + jnp.exp2(extra_logit - m)
    return o / z


def _fixed_softmax_out(parts, ones_ref, extra_logit):
    acc = None
    for s, v in parts:
        aug = jnp.concatenate([v, ones_ref[:s.shape[1], :]], axis=1)
        t = jnp.dot(jnp.exp2(s).astype(BF), aug, preferred_element_type=F32)
        acc = t if acc is None else acc + t
    l = acc[:, LANES:]
    if extra_logit is not None:
        l = l + jnp.exp2(jnp.zeros_like(l) + extra_logit)
    unsafe = jnp.logical_not((l > 2.0 ** -SAFE_SUM_LOG2) & (l < 2.0 ** SAFE_SUM_LOG2))
    return acc[:, :LANES] / l, jnp.max(jnp.where(unsafe, 1.0, 0.0), axis=0, keepdims=True)


def _win_kernel(sink_ref, q_ref, *refs, band, fixed=False):
    if band:
        k0, k1, k2, k3, v0, v1, v2, v3, kc_ref, vc_ref, mask_ref = refs[:11]
        refs = refs[11:]
        kb = jnp.concatenate([k0[...], k1[...], k2[...], k3[...]], axis=1)
        vb = jnp.concatenate([v0[...], v1[...], v2[...], v3[...]], axis=0)
        mask = mask_ref[0]
    else:
        kc_ref, vc_ref = refs[:2]
        refs = refs[2:]
    if fixed:
        ones_ref, o_ref, flag_ref = refs
        ref_logit = sink_ref[0, WIN_HEADS]
    else:
        (o_ref,) = refs
    q = q_ref[...]
    lo = _half_mask((q.shape[0], LANES))
    group = WIN_HEADS // WIN_KV_HEADS
    bad = []
    for j in range(WIN_HEADS // 2):
        qp = q[:, j * LANES:(j + 1) * LANES]
        g = (2 * j) // group
        kc = kc_ref[g * LANES:(g + 1) * LANES, :]
        vc = vc_ref[:, g * LANES:(g + 1) * LANES]
        outs = []
        for e in range(2):
            qm = _select_half(qp, e, lo)
            parts = []
            if band:
                s = jnp.dot(qm, kb[g * LANES:(g + 1) * LANES, :], preferred_element_type=F32) + mask
                parts.append((s, vb[:, g * LANES:(g + 1) * LANES]))
            sc = jnp.dot(qm, kc, preferred_element_type=F32)
            if fixed:
                parts.append((sc - ref_logit, vc))
                o, u = _fixed_softmax_out(parts, ones_ref, sink_ref[0, 2 * j + e])
                bad.append(u)
                outs.append(o)
            else:
                parts.append((sc, vc))
                outs.append(_local_softmax_out(parts, sink_ref[0, 2 * j + e]))
        o_ref[:, j * LANES:(j + 1) * LANES] = jnp.where(lo, outs[0], outs[1]).astype(BF)
    if fixed:
        flag_ref[0] = jnp.concatenate(bad, axis=0)


def _win_attention(sink, q_tok, kt, tok_a, mask_tbl, n_batch, seq, n_ctx, latent, ones=None):
    ctx_blk = (n_batch * seq) // n_ctx
    kc_spec = lambda f: pl.BlockSpec((2 * LANES, n_ctx), f)
    vc_spec = lambda f: pl.BlockSpec((n_ctx, 2 * LANES), f)
    smem = pl.BlockSpec(memory_space=pltpu.SMEM)
    n_out = n_batch * (seq if latent else n_ctx)
    out_shape = jax.ShapeDtypeStruct((n_out, WIN_HEADS * HEAD_DIM), BF)
    if not latent:
        return pl.pallas_call(
            functools.partial(_win_kernel, band=False),
            out_shape=out_shape,
            grid=(n_batch,),
            in_specs=[smem,
                      pl.BlockSpec((n_ctx, 512), lambda b: (ctx_blk + b, 0)),
                      kc_spec(lambda b: (KT_WK // 256, ctx_blk + b)),
                      vc_spec(lambda b: (ctx_blk + b, A_WV // 256))],
            out_specs=pl.BlockSpec((n_ctx, 512), lambda b: (b, 0)),
            compiler_params=_cparams(("parallel",)),
            name="win_attn_ctx",
        )(sink, q_tok, kt, tok_a)
    nq = seq // WIN_TQ
    nkb = seq // LANES

    def kidx(j):
        return lambda b, i: (KT_WK // 256, b * nkb + jnp.clip(2 * i - 1 + j, 0, nkb - 1))

    def vidx(j):
        return lambda b, i: (b * nkb + jnp.clip(2 * i - 1 + j, 0, nkb - 1), A_WV // 256)

    def variant(b, i):
        return (jnp.where(i == 0, 0, jnp.where(i == nq - 1, 2, 1)), 0, 0)

    in_specs = ([smem, pl.BlockSpec((WIN_TQ, 512), lambda b, i: (b * nq + i, 0))]
                + [pl.BlockSpec((2 * LANES, LANES), kidx(j)) for j in range(4)]
                + [pl.BlockSpec((LANES, 2 * LANES), vidx(j)) for j in range(4)]
                + [kc_spec(lambda b, i: (KT_WK // 256, ctx_blk + b)),
                   vc_spec(lambda b, i: (ctx_blk + b, A_WV // 256)),
                   pl.BlockSpec((1, WIN_TQ, 4 * LANES), variant)])
    args = [sink, q_tok, kt, kt, kt, kt, tok_a, tok_a, tok_a, tok_a, kt, tok_a, mask_tbl]
    out_specs = pl.BlockSpec((WIN_TQ, 512), lambda b, i: (b * nq + i, 0))
    fixed = ones is not None
    if fixed:
        in_specs.append(pl.BlockSpec(ones.shape, lambda b, i: (0, 0)))
        args.append(ones)
        out_shape = (out_shape, jax.ShapeDtypeStruct((n_batch * nq, WIN_HEADS, LANES), F32))
        out_specs = (out_specs, pl.BlockSpec((1, WIN_HEADS, LANES), lambda b, i: (b * nq + i, 0, 0)))
    return pl.pallas_call(
        functools.partial(_win_kernel, band=True, fixed=fixed),
        out_shape=out_shape,
        grid=(n_batch, nq),
        in_specs=in_specs,
        out_specs=out_specs,
        compiler_params=_cparams(("parallel", "parallel")),
        name="win_attn_fixed" if fixed else "win_attn",
    )(*args)


def _nat_kernel(q_ref, k0, k1, k2, v0, v1, v2, kc_ref, vc_ref, bias_ref, *refs, fixed=False):
    if fixed:
        par_ref, ones_ref, o_ref, flag_ref = refs
        ref_logit = par_ref[0, 0]
    else:
        (o_ref,) = refs
    q = q_ref[...]
    kb = jnp.concatenate([k0[...], k1[...], k2[...]], axis=1)
    vb = jnp.concatenate([v0[...], v1[...], v2[...]], axis=0)
    lo = _half_mask((q.shape[0], LANES))
    bad = []
    for j in range(NAT_HEADS // 2):
        sl = slice(j * LANES, (j + 1) * LANES)
        qp = q[:, sl]
        outs = []
        for e in range(2):
            qm = _select_half(qp, e, lo)
            s = jnp.dot(qm, kb[sl, :], preferred_element_type=F32) + bias_ref[0, 2 * j + e]
            sc = jnp.dot(qm, kc_ref[sl, :], preferred_element_type=F32)
            if fixed:
                o, u = _fixed_softmax_out([(s, vb[:, sl]), (sc - ref_logit, vc_ref[:, sl])], ones_ref, None)
                bad.append(u)
                outs.append(o)
            else:
                outs.append(_local_softmax_out([(s, vb[:, sl]), (sc, vc_ref[:, sl])], None))
        o_ref[:, sl] = jnp.where(lo, outs[0], outs[1]).astype(BF)
    if fixed:
        flag_ref[0] = jnp.concatenate(bad, axis=0)


def _nat_attention(q_tok, kt, tok_a, bias_tbl, n_batch, seq, n_ctx, par=None, ones=None):
    tq = NAT_ROWS_PER_STEP * GRID_W
    nq = seq // tq
    rows = seq // GRID_W
    ctx_blk = (n_batch * seq) // n_ctx
    q_col = 2
    k_row = KT_NK // 512
    v_col = A_NV // 512

    def wstart(i):
        return jnp.clip(NAT_ROWS_PER_STEP * i - NAT_WIN_ROWS // 2, 0, rows - NAT_KEY_ROWS) // NAT_ROWS_PER_STEP

    def kidx(j):
        return lambda b, i: (k_row, b * nq + wstart(i) + j)

    def vidx(j):
        return lambda b, i: (b * nq + wstart(i) + j, v_col)

    def variant(b, i):
        return (jnp.where(i == 0, 0, jnp.where(i == nq - 1, 2, 1)), 0, 0, 0)

    nk = NAT_KEY_ROWS * GRID_W
    in_specs = ([pl.BlockSpec((tq, 512), lambda b, i: (b * nq + i, q_col))]
                + [pl.BlockSpec((512, tq), kidx(j)) for j in range(3)]
                + [pl.BlockSpec((tq, 512), vidx(j)) for j in range(3)]
                + [pl.BlockSpec((512, n_ctx), lambda b, i: (k_row, ctx_blk + b)),
                   pl.BlockSpec((n_ctx, 512), lambda b, i: (ctx_blk + b, v_col)),
                   pl.BlockSpec((1, NAT_HEADS, tq, nk), variant)])
    args = [q_tok, kt, kt, kt, tok_a, tok_a, tok_a, kt, tok_a, bias_tbl]
    out_shape = jax.ShapeDtypeStruct((n_batch * seq, NAT_HEADS * HEAD_DIM), BF)
    out_specs = pl.BlockSpec((tq, 512), lambda b, i: (b * nq + i, 0))
    fixed = ones is not None
    if fixed:
        in_specs += [pl.BlockSpec(memory_space=pltpu.SMEM), pl.BlockSpec(ones.shape, lambda b, i: (0, 0))]
        args += [par, ones]
        out_shape = (out_shape, jax.ShapeDtypeStruct((n_batch * nq, NAT_HEADS, LANES), F32))
        out_specs = (out_specs, pl.BlockSpec((1, NAT_HEADS, LANES), lambda b, i: (b * nq + i, 0, 0)))
    return pl.pallas_call(
        functools.partial(_nat_kernel, fixed=fixed),
        out_shape=out_shape,
        grid=(n_batch, nq),
        in_specs=in_specs,
        out_specs=out_specs,
        compiler_params=_cparams(("parallel", "parallel")),
        name="nat_attn_fixed" if fixed else "nat_attn",
    )(*args)


def _dense_kernel(lam_ref, q_ref, kc_ref, vc_ref, *refs, n_heads, packed, diff, latent, lam_scale):
    if latent:
        k_ref, v_ref = refs[0], refs[1]
        refs = refs[2:]
    if diff:
        subln_ref, o_ref, qm_sc, m_sc, l_sc, acc_sc, s_sc, p_sc = refs
    else:
        o_ref, qm_sc, m_sc, l_sc, acc_sc, s_sc, p_sc = refs
    kt_step = pl.program_id(2) if latent else 0
    tq = q_ref.shape[0]

    def kv_slices(h):
        blk = h // 2 if packed else h
        ks = slice(blk * LANES, (blk + 1) * LANES)
        vs = ks if packed else slice((h // 2) * LANES, (h // 2 + 1) * LANES)
        return ks, vs

    def step(h, k_ref_, v_ref_):
        ks, vs = kv_slices(h)
        nk = k_ref_.shape[1]
        slot = h % 2
        s_sc[slot, :, :nk] = jnp.dot(qm_sc[h], k_ref_[ks, :], preferred_element_type=F32)
        for r in range(tq // DENSE_RB):
            rows = slice(r * DENSE_RB, (r + 1) * DENSE_RB)
            mx = s_sc[slot, rows, 0:LANES]
            for c in range(1, nk // LANES):
                mx = jnp.maximum(mx, s_sc[slot, rows, c * LANES:(c + 1) * LANES])
            m_old = m_sc[h, rows, :]
            m_new = jnp.maximum(m_old, jnp.max(mx, axis=-1, keepdims=True))
            alpha = jnp.exp2(m_old - m_new)
            lsum = None
            for c in range(nk // LANES):
                cols = slice(c * LANES, (c + 1) * LANES)
                p = jnp.exp2(s_sc[slot, rows, cols] - m_new)
                lsum = p if lsum is None else lsum + p
                p_sc[slot, rows, cols] = p.astype(BF)
            m_sc[h, rows, :] = m_new
            l_sc[h, rows, :] = alpha * l_sc[h, rows, :] + jnp.sum(lsum, axis=-1, keepdims=True)
            acc_sc[h, rows, :] = alpha * acc_sc[h, rows, :]
        acc_sc[h] += jnp.dot(p_sc[slot, :, :nk], v_ref_[:, vs], preferred_element_type=F32)

    @pl.when(kt_step == 0)
    def _():
        q = q_ref[...]
        lo = _half_mask((tq, LANES))
        for h in range(n_heads):
            if packed:
                qp = q[:, (h // 2) * LANES:(h // 2 + 1) * LANES]
                qm_sc[h] = _select_half(qp, h % 2, lo)
            else:
                qm_sc[h] = q[:, h * LANES:(h + 1) * LANES]
        m_sc[...] = jnp.full(m_sc.shape, NEG, F32)
        l_sc[...] = jnp.zeros(l_sc.shape, F32)
        acc_sc[...] = jnp.zeros(acc_sc.shape, F32)
        for h in range(n_heads):
            step(h, kc_ref, vc_ref)

    if latent:
        for h in range(n_heads):
            step(h, k_ref, v_ref)
        last = kt_step == pl.num_programs(2) - 1
    else:
        last = True

    def finish():
        lo = _half_mask((tq, LANES))
        if diff:
            lam = lam_ref[0, 0]
            for hv in range(n_heads // 2):
                y = (acc_sc[2 * hv] / l_sc[2 * hv]
                     - lam * (acc_sc[2 * hv + 1] / l_sc[2 * hv + 1]))
                ms = jnp.mean(y * y, axis=-1, keepdims=True)
                y = y * lax.rsqrt(ms + EPS) * subln_ref[...] * lam_scale
                o_ref[:, hv * LANES:(hv + 1) * LANES] = y.astype(BF)
        else:
            for hp in range(n_heads // 2):
                o0 = acc_sc[2 * hp] / l_sc[2 * hp]
                o1 = acc_sc[2 * hp + 1] / l_sc[2 * hp + 1]
                o_ref[:, hp * LANES:(hp + 1) * LANES] = jnp.where(lo, o0, o1).astype(BF)

    if latent:
        pl.when(last)(finish)
    else:
        finish()


def _dense_attention(lam, q_arr, q_col, q_w, k_arr, k_row, k_w, v_arr, v_col, v_w, subln,
                     n_batch, seq, n_ctx, *, n_heads, packed, diff, latent, lam_scale, name):
    n = n_batch * (seq if latent else n_ctx)
    ctx_blk = (n_batch * seq) // n_ctx
    out_w = v_w
    smem = pl.BlockSpec(memory_space=pltpu.SMEM)
    kern = functools.partial(_dense_kernel, n_heads=n_heads, packed=packed, diff=diff,
                             latent=latent, lam_scale=lam_scale)
    tq = DENSE_TQ_ONLINE if latent else n_ctx
    tk = DENSE_TK_ONLINE
    max_nk = tk if latent else n_ctx
    scratch = [pltpu.VMEM((n_heads, tq, LANES), BF),
               pltpu.VMEM((n_heads, tq, LANES), F32),
               pltpu.VMEM((n_heads, tq, LANES), F32),
               pltpu.VMEM((n_heads, tq, LANES), F32),
               pltpu.VMEM((2, tq, max_nk), F32),
               pltpu.VMEM((2, tq, max_nk), BF)]
    out_shape = jax.ShapeDtypeStruct((n, out_w), BF)
    if latent:
        nq = seq // tq
        nk = seq // tk
        grid = (n_batch, nq, nk)
        in_specs = [smem,
                    pl.BlockSpec((tq, q_w), lambda b, i, k: (b * nq + i, q_col)),
                    pl.BlockSpec((k_w, n_ctx), lambda b, i, k: (k_row, ctx_blk + b)),
                    pl.BlockSpec((n_ctx, v_w), lambda b, i, k: (ctx_blk + b, v_col)),
                    pl.BlockSpec((k_w, tk), lambda b, i, k: (k_row, b * nk + k)),
                    pl.BlockSpec((tk, v_w), lambda b, i, k: (b * nk + k, v_col))]
        args = [lam, q_arr, k_arr, v_arr, k_arr, v_arr]
        if diff:
            in_specs.append(pl.BlockSpec((1, LANES), lambda b, i, k: (0, 0)))
            args.append(subln)
        out_specs = pl.BlockSpec((tq, out_w), lambda b, i, k: (b * nq + i, 0))
        sem = ("parallel", "parallel", "arbitrary")
    else:
        grid = (n_batch,)
        in_specs = [smem,
                    pl.BlockSpec((tq, q_w), lambda b: (ctx_blk + b, q_col)),
                    pl.BlockSpec((k_w, n_ctx), lambda b: (k_row, ctx_blk + b)),
                    pl.BlockSpec((n_ctx, v_w), lambda b: (ctx_blk + b, v_col))]
        args = [lam, q_arr, k_arr, v_arr]
        if diff:
            in_specs.append(pl.BlockSpec((1, LANES), lambda b: (0, 0)))
            args.append(subln)
        out_specs = pl.BlockSpec((tq, out_w), lambda b: (b, 0))
        sem = ("parallel",)
    return pl.pallas_call(
        kern, out_shape=out_shape, grid=grid, in_specs=in_specs, out_specs=out_specs,
        scratch_shapes=scratch, compiler_params=_cparams(sem), name=name,
    )(*args)


SAFE_SUM_LOG2 = 100.0


def _dense_fixed_kernel(par_ref, q_ref, kc_ref, vc_ref, *refs, n_heads, packed, diff, latent, lam_scale):
    if latent:
        k_ref, v_ref = refs[0], refs[1]
        refs = refs[2:]
    ones_ref = refs[0]
    refs = refs[1:]
    if diff:
        subln_ref, o_ref, flag_ref, qm_sc, acc_sc, p_sc = refs
    else:
        o_ref, flag_ref, qm_sc, acc_sc, p_sc = refs
    kt_step = pl.program_id(2) if latent else 0
    tq = q_ref.shape[0]
    ref_logit = par_ref[0, 1]

    def step(h, k_ref_, v_ref_, first):
        blk = h // 2 if packed else h
        ks = slice(blk * LANES, (blk + 1) * LANES)
        vs = ks if packed else slice((h // 2) * LANES, (h // 2 + 1) * LANES)
        nk = k_ref_.shape[1]
        slot = h % 2
        for c in range(nk // DENSE_KC):
            cols = slice(c * DENSE_KC, (c + 1) * DENSE_KC)
            s = jnp.dot(qm_sc[h], k_ref_[ks, cols], preferred_element_type=F32)
            p_sc[slot, :, cols] = jnp.exp2(s - ref_logit).astype(BF)
        v_aug = jnp.concatenate([v_ref_[:, vs], ones_ref[:nk, :]], axis=1)
        pv = jnp.dot(p_sc[slot, :, :nk], v_aug, preferred_element_type=F32)
        if first:
            acc_sc[h] = pv
        else:
            acc_sc[h] += pv

    @pl.when(kt_step == 0)
    def _():
        q = q_ref[...]
        lo = _half_mask((tq, LANES))
        for h in range(n_heads):
            if packed:
                qp = q[:, (h // 2) * LANES:(h // 2 + 1) * LANES]
                qm_sc[h] = _select_half(qp, h % 2, lo)
            else:
                qm_sc[h] = q[:, h * LANES:(h + 1) * LANES]
        for h in range(n_heads):
            step(h, kc_ref, vc_ref, True)

    if latent:
        for h in range(n_heads):
            step(h, k_ref, v_ref, False)
        last = kt_step == pl.num_programs(2) - 1

    def finish():
        lo = _half_mask((tq, LANES))
        outs, bad = [], []
        for h in range(n_heads):
            a = acc_sc[h]
            l = a[:, LANES:]
            unsafe = jnp.logical_not((l > 2.0 ** -SAFE_SUM_LOG2) & (l < 2.0 ** SAFE_SUM_LOG2))
            bad.append(jnp.max(jnp.where(unsafe, 1.0, 0.0), axis=0, keepdims=True))
            outs.append(a[:, :LANES] / l)
        flag_ref[0] = jnp.concatenate(bad, axis=0)
        if diff:
            lam = par_ref[0, 0]
            for hv in range(n_heads // 2):
                y = outs[2 * hv] - lam * outs[2 * hv + 1]
                ms = jnp.mean(y * y, axis=-1, keepdims=True)
                y = y * lax.rsqrt(ms + EPS) * subln_ref[...] * lam_scale
                o_ref[:, hv * LANES:(hv + 1) * LANES] = y.astype(BF)
        else:
            for hp in range(n_heads // 2):
                o_ref[:, hp * LANES:(hp + 1) * LANES] = jnp.where(lo, outs[2 * hp], outs[2 * hp + 1]).astype(BF)

    if latent:
        pl.when(last)(finish)
    else:
        finish()


def _dense_fixed_attention(par, q_arr, q_col, q_w, k_arr, k_row, k_w, v_arr, v_col, v_w, subln, ones,
                           n_batch, seq, n_ctx, *, n_heads, packed, diff, latent, lam_scale, name):
    n = n_batch * (seq if latent else n_ctx)
    ctx_blk = (n_batch * seq) // n_ctx
    smem = pl.BlockSpec(memory_space=pltpu.SMEM)
    kern = functools.partial(_dense_fixed_kernel, n_heads=n_heads, packed=packed, diff=diff,
                             latent=latent, lam_scale=lam_scale)
    tq = DENSE_TQ if latent else n_ctx
    max_nk = DENSE_TK if latent else n_ctx
    scratch = [pltpu.VMEM((n_heads, tq, LANES), BF),
               pltpu.VMEM((n_heads, tq, 2 * LANES), F32),
               pltpu.VMEM((2, tq, max_nk), BF)]
    if latent:
        nq = seq // tq
        nk = seq // DENSE_TK
        grid = (n_batch, nq, nk)
        ix = lambda f: (lambda b, i, k: f(b, i, k))
        q_ix = ix(lambda b, i, k: (b * nq + i, q_col))
        in_specs = [smem,
                    pl.BlockSpec((tq, q_w), q_ix),
                    pl.BlockSpec((k_w, n_ctx), ix(lambda b, i, k: (k_row, ctx_blk + b))),
                    pl.BlockSpec((n_ctx, v_w), ix(lambda b, i, k: (ctx_blk + b, v_col))),
                    pl.BlockSpec((k_w, DENSE_TK), ix(lambda b, i, k: (k_row, b * nk + k))),
                    pl.BlockSpec((DENSE_TK, v_w), ix(lambda b, i, k: (b * nk + k, v_col))),
                    pl.BlockSpec((DENSE_TK, LANES), ix(lambda b, i, k: (0, 0)))]
        args = [par, q_arr, k_arr, v_arr, k_arr, v_arr, ones]
        const_ix = ix(lambda b, i, k: (0, 0))
        out_specs = (pl.BlockSpec((tq, v_w), ix(lambda b, i, k: (b * nq + i, 0))),
                     pl.BlockSpec((1, n_heads, LANES), ix(lambda b, i, k: (b * nq + i, 0, 0))))
        n_flag = n_batch * nq
        sem = ("parallel", "parallel", "arbitrary")
    else:
        grid = (n_batch,)
        in_specs = [smem,
                    pl.BlockSpec((tq, q_w), lambda b: (ctx_blk + b, q_col)),
                    pl.BlockSpec((k_w, n_ctx), lambda b: (k_row, ctx_blk + b)),
                    pl.BlockSpec((n_ctx, v_w), lambda b: (ctx_blk + b, v_col)),
                    pl.BlockSpec((DENSE_TK, LANES), lambda b: (0, 0))]
        args = [par, q_arr, k_arr, v_arr, ones]
        const_ix = lambda b: (0, 0)
        out_specs = (pl.BlockSpec((tq, v_w), lambda b: (b, 0)),
                     pl.BlockSpec((1, n_heads, LANES), lambda b: (b, 0, 0)))
        n_flag = n_batch
        sem = ("parallel",)
    if diff:
        in_specs.append(pl.BlockSpec((1, LANES), const_ix))
        args.append(subln)
    return pl.pallas_call(
        kern,
        out_shape=(jax.ShapeDtypeStruct((n, v_w), BF), jax.ShapeDtypeStruct((n_flag, n_heads, LANES), F32)),
        grid=grid, in_specs=in_specs, out_specs=out_specs,
        scratch_shapes=scratch, compiler_params=_cparams(sem), name=name + "_fixed",
    )(*args)


def _merge_kernel(x_ref, mod_ref, g2_ref, *refs, n_lat_tiles, has_ctx):
    ys = refs[:N_BRANCH]
    refs = refs[N_BRANCH:]
    if has_ctx:
        ycs = refs[:N_BRANCH]
        refs = refs[N_BRANCH:]
        is_ctx = pl.program_id(0) >= n_lat_tiles
    gts = refs[:N_BRANCH]
    wb_ref, wo_ref, rw_ref, xo_ref, h2_ref, sc_ref = refs[N_BRANCH:]
    m = mod_ref[0]
    mix = None
    for n_ in range(N_BRANCH):
        y = ys[n_][...]
        if has_ctx:
            y = jnp.where(is_ctx, ycs[n_][...], y)
        yb = jnp.dot(y, wb_ref[n_], preferred_element_type=F32)
        t = jax.nn.sigmoid(gts[n_][...].astype(F32)) * yb
        mix = t if mix is None else mix + t
    att = jnp.dot(mix.astype(BF), wo_ref[...], preferred_element_type=F32)
    xn = x_ref[...] + m[2:3] * att
    xo_ref[...] = xn
    h2 = _norm_mod(xn, g2_ref[...], m[4:5], m[3:4])
    h2_ref[...] = h2
    h_hi = h2.astype(BF)
    h_lo = (h2 - h_hi.astype(F32)).astype(BF)
    logits = (jnp.dot(h_hi, rw_ref[0], preferred_element_type=F32)
              + (jnp.dot(h_lo, rw_ref[0], preferred_element_type=F32)
                 + jnp.dot(h_hi, rw_ref[1], preferred_element_type=F32)))
    sc_ref[...] = jax.nn.sigmoid(logits)


def _merge(xa, mod3, g2, ys, ys_ctx, tok_a, wb, wo, rw, n_rows, group_of_tile):
    d = xa.shape[1]
    n_lat_tiles = ys[0].shape[0] // TM
    has_ctx = ys_ctx is not None
    row = lambda w, c: pl.BlockSpec((TM, w), lambda i, c=c: (i, c))
    lat_row = pl.BlockSpec((TM, BRANCH_W), lambda i: (jnp.minimum(i, n_lat_tiles - 1), 0))
    in_specs = ([row(d, 0),
                 pl.BlockSpec((1, 8, d), lambda i: (group_of_tile(i), 0, 0)),
                 pl.BlockSpec((1, d), lambda i: (0, 0))]
                + [lat_row for _ in range(N_BRANCH)]
                + ([pl.BlockSpec((TM, BRANCH_W), lambda i: (0, 0)) for _ in range(N_BRANCH)] if has_ctx else [])
                + [row(d, c) for c in range(N_BRANCH)]
                + [pl.BlockSpec(wb.shape, lambda i: (0, 0, 0)),
                   pl.BlockSpec(wo.shape, lambda i: (0, 0)),
                   pl.BlockSpec(rw.shape, lambda i: (0, 0, 0))])
    ys = list(ys) + (list(ys_ctx) if has_ctx else [])
    return pl.pallas_call(
        functools.partial(_merge_kernel, n_lat_tiles=n_lat_tiles, has_ctx=has_ctx),
        out_shape=(jax.ShapeDtypeStruct((n_rows, d), F32),
                   jax.ShapeDtypeStruct((n_rows, d), F32),
                   jax.ShapeDtypeStruct((n_rows, LANES), F32)),
        grid=(n_rows // TM,),
        in_specs=in_specs,
        out_specs=(row(d, 0), row(d, 0), row(LANES, 0)),
        compiler_params=_cparams(("parallel",)),
        name="merge",
    )(xa, mod3, g2, *ys, tok_a, tok_a, tok_a, tok_a, wb, wo, rw)


def _route(scores, router_b, n):
    per_group = N_EXPERTS // N_GROUPS
    st = scores.T
    biased = st + router_b.astype(F32)[:, None]

    def top2(v, axis):
        pos = lax.broadcasted_iota(jnp.int32, v.shape, axis)
        i0 = jnp.argmax(v, axis=axis).astype(jnp.int32)
        v0 = jnp.max(v, axis=axis)
        rest = jnp.where(pos == jnp.expand_dims(i0, axis), -jnp.inf, v)
        i1 = jnp.argmax(rest, axis=axis).astype(jnp.int32)
        v1 = jnp.max(rest, axis=axis)
        return (v0, v1), (i0, i1)

    (g0, g1), _ = top2(biased.reshape(N_GROUPS, per_group, n), 1)
    group = jnp.argmax(g0 + g1, axis=0)
    expert = jnp.arange(N_EXPERTS, dtype=jnp.int32)[:, None]
    in_group = (expert // per_group) == group[None, :]
    _, (e0, e1) = top2(jnp.where(in_group, biased, -jnp.inf), 0)
    oh0 = (expert == e0[None, :]).astype(F32)
    oh1 = (expert == e1[None, :]).astype(F32)
    w0 = (oh0 * st).sum(0)
    w1 = (oh1 * st).sum(0)
    wts = jnp.stack([w0, w1], axis=-1) / (w0 + w1)[:, None]
    n_asg = n * TOP_K
    cnt = (oh0 + oh1).reshape(N_EXPERTS, n // MOE_BLOCK, MOE_BLOCK)
    tri = jnp.triu(jnp.ones((MOE_BLOCK, MOE_BLOCK), F32), 1)
    within = jnp.einsum('ebj,ji->ebi', cnt, tri)
    blk_tot = cnt.sum(axis=-1)
    nb = blk_tot.shape[1]
    blk_off = jnp.einsum('eb,bc->ec', blk_tot, jnp.triu(jnp.ones((nb, nb), F32), 1),
                         precision=lax.Precision.HIGHEST)
    prefix = (within + blk_off[:, :, None]).reshape(N_EXPERTS, n)
    counts = blk_tot.sum(axis=-1).astype(jnp.int32)
    padded = (counts + MOE_BLOCK - 1) // MOE_BLOCK * MOE_BLOCK
    pad_end = jnp.cumsum(padded)
    pad_start = pad_end - padded
    slot = prefix + pad_start.astype(F32)[:, None]
    dest = jnp.stack([(oh0 * slot).sum(0), (oh1 * slot).sum(0)], axis=-1).reshape(n_asg)
    n_blk = (n_asg + N_EXPERTS * (MOE_BLOCK - 1) + MOE_BLOCK - 1) // MOE_BLOCK
    cap = n_blk * MOE_BLOCK
    blk_start = jnp.arange(n_blk, dtype=jnp.int32) * MOE_BLOCK
    blk_e = jnp.minimum((blk_start[:, None] >= pad_end[None, :]).sum(-1), N_EXPERTS - 1).astype(jnp.int32)
    pad_lo = jnp.concatenate([pad_start + counts, pad_end[-1:]]).astype(jnp.int32)
    pad_hi = jnp.concatenate([pad_end, jnp.full((1,), cap)]).astype(jnp.int32)
    n_used = (pad_end[-1:] // MOE_BLOCK).astype(jnp.int32)
    return blk_e, dest.astype(jnp.int32), wts.astype(F32), pad_lo, pad_hi, n_used, cap


def _dispatch_kernel(dest_ref, pad_lo_ref, pad_hi_ref, h2_ref, xs_hbm, zrow, sem, zsem):
    i = pl.program_id(0)

    @pl.when(i == 0)
    def _():
        zrow[...] = jnp.zeros(zrow.shape, F32)
        for e in range(N_EXPERTS + 1):
            lo, hi = pad_lo_ref[e], pad_hi_ref[e]

            def zero_row(r, c):
                pltpu.make_async_copy(zrow.at[pl.ds(0, 1)], xs_hbm.at[pl.ds(r, 1)], zsem).start()
                return c

            def zero_wait(r, c):
                pltpu.make_async_copy(zrow.at[pl.ds(0, 1)], xs_hbm.at[pl.ds(r, 1)], zsem).wait()
                return c

            lax.fori_loop(lo, hi, zero_row, 0)
            lax.fori_loop(lo, hi, zero_wait, 0)

    tm = h2_ref.shape[0]

    def copy_row(r, c):
        a = (i * tm + r) * TOP_K
        for k in range(TOP_K):
            pltpu.make_async_copy(h2_ref.at[pl.ds(r, 1)], xs_hbm.at[pl.ds(dest_ref[a + k], 1)], sem).start()
        return c

    lax.fori_loop(0, tm, copy_row, 0, unroll=4)
    for k in range(TOP_K):
        pltpu.make_async_copy(h2_ref, xs_hbm.at[pl.ds(0, tm)], sem).wait()


def _dispatch(dest, pad_lo, pad_hi, h2, cap):
    n, d = h2.shape
    grid_spec = pltpu.PrefetchScalarGridSpec(
        num_scalar_prefetch=3,
        grid=(n // TM,),
        in_specs=[pl.BlockSpec((TM, d), lambda i, de, lo, hi: (i, 0))],
        out_specs=pl.BlockSpec(memory_space=pl.ANY),
        scratch_shapes=[pltpu.VMEM((8, d), F32), pltpu.SemaphoreType.DMA, pltpu.SemaphoreType.DMA],
    )
    return pl.pallas_call(
        _dispatch_kernel,
        out_shape=jax.ShapeDtypeStruct((cap, d), F32),
        grid_spec=grid_spec,
        compiler_params=_cparams(("arbitrary",)),
        name="moe_dispatch",
    )(dest, pad_lo, pad_hi, h2)


def _grouped_ffn_kernel(blk_e_ref, n_used_ref, x_ref, w1_ref, w3_ref, w2_ref, o_ref):
    i = pl.program_id(0)

    @pl.when(i < n_used_ref[0])
    def _():
        xb = x_ref[...].astype(BF)
        a = jnp.dot(xb, w1_ref[0], preferred_element_type=F32)
        b = jnp.dot(xb, w3_ref[0], preferred_element_type=F32)
        hmid = (a * jax.nn.sigmoid(a) * b).astype(BF)
        o_ref[...] = jnp.dot(hmid, w2_ref[0], preferred_element_type=F32)

    @pl.when(i >= n_used_ref[0])
    def _():
        o_ref[...] = jnp.zeros(o_ref.shape, F32)


def _grouped_ffn(blk_e, n_used, xs, w1, w3, w2):
    cap, d = xs.shape
    wspec = lambda shp: pl.BlockSpec((1,) + shp, lambda i, be, nu: (be[i], 0, 0))
    grid_spec = pltpu.PrefetchScalarGridSpec(
        num_scalar_prefetch=2,
        grid=(cap // MOE_BLOCK,),
        in_specs=[pl.BlockSpec((MOE_BLOCK, d), lambda i, be, nu: (jnp.minimum(i, nu[0] - 1), 0)),
                  wspec((d, D_EXPERT)), wspec((d, D_EXPERT)), wspec((D_EXPERT, d))],
        out_specs=pl.BlockSpec((MOE_BLOCK, d), lambda i, be, nu: (i, 0)),
    )
    return pl.pallas_call(
        _grouped_ffn_kernel,
        out_shape=jax.ShapeDtypeStruct((cap, d), F32),
        grid_spec=grid_spec,
        compiler_params=_cparams(("arbitrary",)),
        name="moe_ffn",
    )(blk_e, n_used, xs, w1, w3, w2)


def _gather_combine_kernel(dest_ref, x_ref, mod_ref, w_ref, ys_hbm, o_ref, buf, sem):
    i = pl.program_id(0)
    n_steps = pl.num_programs(0)
    slot = i % 2
    tm = x_ref.shape[0]

    def start_gather(tile, s):
        def body(r, c):
            a = (tile * tm + r) * TOP_K
            for k in range(TOP_K):
                pltpu.make_async_copy(ys_hbm.at[pl.ds(dest_ref[a + k], 1)], buf.at[s, k, pl.ds(r, 1)],
                                      sem.at[s]).start()
            return c
        lax.fori_loop(0, tm, body, 0, unroll=4)

    @pl.when(i == 0)
    def _():
        start_gather(0, 0)

    @pl.when(i + 1 < n_steps)
    def _():
        start_gather(i + 1, 1 - slot)

    for k in range(TOP_K):
        pltpu.make_async_copy(ys_hbm.at[pl.ds(0, tm)], buf.at[slot, k], sem.at[slot]).wait()
    w = w_ref[...]
    f = w[:, 0:1] * buf[slot, 0] + w[:, 1:2] * buf[slot, 1]
    o_ref[...] = x_ref[...] + mod_ref[0][5:6] * f


def _gather_combine(dest, xn, mod3, wts, ys, n_rows, group_of_tile):
    d = xn.shape[1]
    grid_spec = pltpu.PrefetchScalarGridSpec(
        num_scalar_prefetch=1,
        grid=(n_rows // TM,),
        in_specs=[pl.BlockSpec((TM, d), lambda i, de: (i, 0)),
                  pl.BlockSpec((1, 8, d), lambda i, de: (group_of_tile(i), 0, 0)),
                  pl.BlockSpec((TM, TOP_K), lambda i, de: (i, 0)),
                  pl.BlockSpec(memory_space=pl.ANY)],
        out_specs=pl.BlockSpec((TM, d), lambda i, de: (i, 0)),
        scratch_shapes=[pltpu.VMEM((2, TOP_K, TM, d), F32), pltpu.SemaphoreType.DMA((2,))],
    )
    return pl.pallas_call(
        _gather_combine_kernel,
        out_shape=jax.ShapeDtypeStruct((n_rows, d), F32),
        grid_spec=grid_spec,
        compiler_params=_cparams(("arbitrary",)),
        name="moe_combine",
    )(dest, xn, mod3, wts, ys)


def _rope_tables(seq, dim, pad):
    t = jnp.arange(seq)
    rows = (t // GRID_W).astype(F32)
    cols = (t % GRID_W).astype(F32)
    quarter = dim // 4
    inv_freq = jnp.exp(-math.log(ROPE_BASE) * jnp.arange(quarter, dtype=F32) / quarter)
    ang = jnp.concatenate([inv_freq[:, None] * rows[None, :], inv_freq[:, None] * cols[None, :]], axis=0)
    cos = jnp.concatenate([jnp.cos(ang), jnp.ones((dim // 2, pad), F32)], axis=1)
    sin = jnp.concatenate([jnp.sin(ang), jnp.zeros((dim // 2, pad), F32)], axis=1)
    return cos, sin


def _win_mask_table(seq):
    nkb = seq // LANES
    nq = seq // WIN_TQ
    tabs = []
    for i in (0, 1, nq - 1):
        t = i * WIN_TQ + np.arange(WIN_TQ)[:, None]
        blk = np.clip(2 * i - 1 + np.arange(4), 0, nkb - 1)
        want = 2 * i - 1 + np.arange(4)
        s = (blk[:, None] * LANES + np.arange(LANES)[None, :]).reshape(-1)[None, :]
        ok = (np.abs(t - s) <= WIN_RADIUS) & np.repeat(blk == want, LANES)[None, :]
        tabs.append(np.where(ok, 0.0, NEG))
    return jnp.asarray(np.stack(tabs), F32)


def _nat_bias_table(rpb, seq, shift):
    rows = seq // GRID_W
    nq = rows // NAT_ROWS_PER_STEP
    wc = NAT_WIN_COLS
    col = np.arange(GRID_W)
    col_start = np.clip(col - wc // 2, 0, GRID_W - wc)
    col_ok = (col[None, :] >= col_start[:, None]) & (col[None, :] < col_start[:, None] + wc)
    d_col = np.clip(col[None, :] - col[:, None] + (wc - 1), 0, 2 * wc - 2)
    sel_r, oks = [], []
    for i in (0, 1, nq - 1):
        r0 = NAT_ROWS_PER_STEP * i
        ws = np.clip(r0 - NAT_WIN_ROWS // 2, 0, rows - NAT_KEY_ROWS)
        r = r0 + np.arange(NAT_ROWS_PER_STEP)
        rs = np.clip(r - NAT_WIN_ROWS // 2, 0, rows - NAT_WIN_ROWS)
        krow = ws + np.arange(NAT_KEY_ROWS)
        row_ok = (krow[None, :] >= rs[:, None]) & (krow[None, :] < rs[:, None] + NAT_WIN_ROWS)
        d_row = np.clip(krow[None, :] - r[:, None] + (NAT_WIN_ROWS - 1), 0, 2 * NAT_WIN_ROWS - 2)
        oks.append(row_ok[:, None, :, None] & col_ok[None, :, None, :])
        sel_r.append(d_row[:, :, None] == np.arange(2 * NAT_WIN_ROWS - 1)[None, None, :])
    sel_r = jnp.asarray(np.stack(sel_r), F32)
    sel_c = jnp.asarray(d_col[:, :, None] == np.arange(2 * wc - 1)[None, None, :], F32)
    cols = jnp.einsum('hrc,vkc->hrvk', rpb.astype(F32), sel_c, precision=lax.Precision.HIGHEST)
    bias = jnp.einsum('tuar,hrvk->thuvak', sel_r, cols, precision=lax.Precision.HIGHEST)
    bias = jnp.where(jnp.asarray(np.stack(oks))[:, None], bias - shift, NEG)
    return bias.reshape(3, rpb.shape[0], NAT_ROWS_PER_STEP * GRID_W, NAT_KEY_ROWS * GRID_W)


def _bcast_rows(v, reps=1):
    return jnp.tile(jnp.broadcast_to(v.astype(F32)[:, None], (v.shape[0], LANES)), (reps, 1))


def _layer_params(l, p):
    w = p['w_in'][l]
    sizes = (512, 128, 128, 512, 512, 512, 512, 512, 512, 256, 160, 4096)
    offs = np.concatenate([[0], np.cumsum(sizes)])
    seg = lambda k: w[:, offs[k]:offs[k + 1]]
    wq, wk, wv, dq, dk, dv, nq, nk, nv, mqa, mkva, gates = [seg(k) for k in range(12)]
    d = w.shape[0]
    dup = lambda m: jnp.concatenate([m[:, :64], m[:, :64], m[:, 64:], m[:, 64:]], axis=1)
    mkva_p = jnp.concatenate([mkva, jnp.zeros((d, 256 - mkva.shape[1]), F32)], axis=1)
    w_a = jnp.concatenate([gates, dv, nv, mqa, mkva_p, dup(wv)], axis=1).astype(BF)
    w_bt = jnp.concatenate([wq, dq, nq, dk, nk, dup(wk)], axis=1).T.astype(BF)
    scale = HEAD_DIM ** -0.5 * LOG2E
    gain_b = jnp.concatenate([
        _bcast_rows(p['win_q_norm'][l] * scale, 8), _bcast_rows(p['dif_q_norm'][l] * scale, 8),
        _bcast_rows(p['nat_q_norm'][l] * scale, 8), _bcast_rows(p['dif_k_norm'][l], 8),
        _bcast_rows(p['nat_k_norm'][l], 8), _bcast_rows(p['win_k_norm'][l], 4)], axis=0)
    wkv = p['mla_wkv_b'][l].reshape(MLA_KV_LORA, MLA_HEADS, MLA_NOPE + MLA_V)
    wk_t = wkv[:, :, :MLA_NOPE].reshape(MLA_KV_LORA, -1).T.astype(BF)
    wv_m = wkv[:, :, MLA_NOPE:].reshape(MLA_KV_LORA, -1).astype(BF)
    def logit_bound(gq, gk, dim):
        return dim * jnp.max(jnp.abs(gq)) * jnp.max(jnp.abs(gk)) * (1.0 + 2.0 ** -7)

    lam_f = p['dif_lambda'][l].astype(F32)
    lam_init = 0.8 - 0.6 * math.exp(-0.3 * l)
    lam = jnp.exp(jnp.sum(lam_f[0] * lam_f[1])) - jnp.exp(jnp.sum(lam_f[2] * lam_f[3])) + lam_init
    return dict(
        w_a=w_a, w_bt=w_bt, gain_b=gain_b,
        g1=p['norm1_g'][l].reshape(1, d), g2=p['norm2_g'][l].reshape(1, d),
        sink=(p['win_sink'][l].astype(F32) * LOG2E).reshape(1, WIN_HEADS),
        gqa=p['mla_q_a_norm'][l].reshape(1, -1), gkva=p['mla_kv_a_norm'][l].reshape(1, -1),
        wq_t=p['mla_wq_b'][l].T.astype(BF), wk_t=wk_t, wv_m=wv_m,
        gq=_bcast_rows(p['mla_q_norm'][l] * (MLA_QK ** -0.5 * LOG2E)), gk=_bcast_rows(p['mla_k_norm'][l]),
        lam=lam.reshape(1, 1).astype(F32), lam_scale=1.0 - lam_init,
        m_win=logit_bound(p['win_q_norm'][l] * scale, p['win_k_norm'][l], HEAD_DIM),
        m_dif=logit_bound(p['dif_q_norm'][l] * scale, p['dif_k_norm'][l], HEAD_DIM),
        m_nat=logit_bound(p['nat_q_norm'][l] * scale, p['nat_k_norm'][l], HEAD_DIM),
        m_mla=logit_bound(p['mla_q_norm'][l] * (MLA_QK ** -0.5 * LOG2E), p['mla_k_norm'][l], MLA_QK),
        subln=p['dif_subln'][l].astype(F32).reshape(1, DIF_V_DIM),
        wb=p['w_branch'][l].astype(BF), wo=p['w_out'][l].astype(BF),
        w1=p['moe_w1'][l].astype(BF), w3=p['moe_w3'][l].astype(BF), w2=p['moe_w2'][l].astype(BF),
    )


def kernel(x, c, ctx, c_ctx, ada_w, ada_b, norm1_g, norm2_g, w_in, win_q_norm, win_k_norm, win_sink,
           dif_q_norm, dif_k_norm, dif_lambda, dif_subln, nat_q_norm, nat_k_norm, nat_rpb,
           mla_q_a_norm, mla_wq_b, mla_kv_a_norm, mla_wkv_b, mla_q_norm, mla_k_norm,
           w_branch, w_out, router_w, router_b, moe_w1, moe_w3, moe_w2):
    p = dict(norm1_g=norm1_g, norm2_g=norm2_g, w_in=w_in, win_q_norm=win_q_norm, win_k_norm=win_k_norm,
             win_sink=win_sink, dif_q_norm=dif_q_norm, dif_k_norm=dif_k_norm, dif_lambda=dif_lambda,
             dif_subln=dif_subln, nat_q_norm=nat_q_norm, nat_k_norm=nat_k_norm,
             mla_q_a_norm=mla_q_a_norm, mla_wq_b=mla_wq_b, mla_kv_a_norm=mla_kv_a_norm,
             mla_wkv_b=mla_wkv_b, mla_q_norm=mla_q_norm, mla_k_norm=mla_k_norm,
             w_branch=w_branch, w_out=w_out, moe_w1=moe_w1, moe_w3=moe_w3, moe_w2=moe_w2)
    n_batch, seq, d = x.shape
    n_ctx = ctx.shape[1]
    depth = ada_w.shape[0]
    n_lat = n_batch * seq
    n_all = n_lat + n_batch * n_ctx
    assert seq % DENSE_TK == 0 and seq % DENSE_TQ == 0 and (n_batch * n_ctx) == TM and seq % TM == 0
    tiles_per_batch = seq // TM
    group_of_tile = lambda i: jnp.minimum(i // tiles_per_batch, n_batch)
    pos_of_tile = lambda i: jnp.where(i < n_batch * tiles_per_batch, i % tiles_per_batch, tiles_per_batch)

    cc = jnp.concatenate([c, c_ctx[None, :], jnp.zeros((8 - n_batch - 1, d), F32)], axis=0)
    mod = _modulation(cc, ada_w, ada_b)
    mod = mod[:, :n_batch + 1].reshape(depth, n_batch + 1, 6, d)
    mod = jnp.pad(mod, ((0, 0), (0, 0), (0, 2), (0, 0)))

    cos_h, sin_h = _rope_tables(seq, HEAD_DIM, TM)
    cos_m, sin_m = _rope_tables(seq, MLA_ROPE, TM)
    win_mask = _win_mask_table(seq)
    rw = jnp.pad(router_w.astype(F32), ((0, 0), (0, LANES - N_EXPERTS)))
    rw_hi = rw.astype(BF)
    rw = jnp.stack([rw_hi, (rw - rw_hi.astype(F32)).astype(BF)])

    xa = jnp.concatenate([x.reshape(n_lat, d), ctx.reshape(n_batch * n_ctx, d)], axis=0)
    ones = jnp.ones((DENSE_TK, LANES), BF)

    def dense(lam, logit_bound, q_arr, q_col, q_w, k_arr, k_row, k_w, v_arr, v_col, v_w, subln, **kw):
        operands = (q_arr, q_col, q_w, k_arr, k_row, k_w, v_arr, v_col, v_w, subln)
        par = jnp.concatenate([lam, logit_bound.reshape(1, 1).astype(F32)], axis=1)
        y, flag = _dense_fixed_attention(par, *operands, ones, n_batch, seq, n_ctx, **kw)
        return lax.cond(jnp.max(flag) > 0.0,
                        lambda: _dense_attention(lam, *operands, n_batch, seq, n_ctx, **kw),
                        lambda: y)
    for l in range(depth):
        lp = _layer_params(l, p)
        want_ctx = l < depth - 1
        mod3 = mod[l]
        tok_a = _in_proj_a(xa, mod3, lp['g1'], lp['w_a'], group_of_tile)
        q_tok, kt = _in_proj_b(xa, mod3, lp['g1'], lp['w_bt'], lp['gain_b'], cos_h, sin_h,
                               group_of_tile, pos_of_tile)
        mq, mkt, mv = _mla_prep(tok_a, lp['gqa'], lp['gkva'], lp['wq_t'], lp['wk_t'], lp['wv_m'],
                                lp['gq'], lp['gk'], cos_m, sin_m, pos_of_tile)
        rpb2 = nat_rpb[l].astype(F32) * LOG2E
        m_nl = lp['m_nat'] + jnp.maximum(jnp.max(rpb2), 0.0)
        nat_bias_rel = _nat_bias_table(rpb2, seq, m_nl)

        def branches(latent):
            win_args = (q_tok, kt, tok_a)
            if latent:
                m_win = lp['m_win']
                sink_rel = jnp.concatenate([lp['sink'] - m_win, jnp.full((1, 8), m_win, F32)], axis=1)
                y_win, flag = _win_attention(sink_rel, *win_args, win_mask - m_win, n_batch, seq, n_ctx, True,
                                             ones=ones)
                y_win = lax.cond(jnp.max(flag) > 0.0,
                                 lambda: _win_attention(lp['sink'], *win_args, win_mask, n_batch, seq, n_ctx, True),
                                 lambda: y_win)
            else:
                y_win = _win_attention(lp['sink'], *win_args, win_mask, n_batch, seq, n_ctx, False)
            y_dif = dense(lp['lam'], lp['m_dif'], q_tok, 1, 512, kt, KT_DK // 512, 512, tok_a, A_DV // 512, 512,
                          lp['subln'], n_heads=2 * DIF_HEADS, packed=True, diff=True, latent=latent,
                          lam_scale=lp['lam_scale'], name="dif_attn" if latent else "dif_attn_ctx")
            if latent:
                y_nat, flag = _nat_attention(q_tok, kt, tok_a, nat_bias_rel, n_batch, seq, n_ctx,
                                             par=m_nl.reshape(1, 1).astype(F32), ones=ones)
                y_nat = lax.cond(jnp.max(flag) > 0.0,
                                 lambda: _nat_attention(q_tok, kt, tok_a, _nat_bias_table(rpb2, seq, 0.0),
                                                        n_batch, seq, n_ctx),
                                 lambda: y_nat)
            else:
                y_nat = dense(lp['lam'], lp['m_nat'], q_tok, 2, 512, kt, KT_NK // 512, 512, tok_a, A_NV // 512, 512,
                              None, n_heads=NAT_HEADS, packed=True, diff=False, latent=False,
                              lam_scale=1.0, name="nat_attn_ctx")
            y_mla = dense(lp['lam'], lp['m_mla'], mq, 0, 1024, mkt, 0, 1024, mv, 0, 512, None,
                          n_heads=MLA_HEADS, packed=False, diff=False, latent=latent, lam_scale=1.0,
                          name="mla_attn" if latent else "mla_attn_ctx")
            return [y_win, y_dif, y_nat, y_mla]

        ys = branches(True)
        ys_c = branches(False) if want_ctx else None
        n_rows = n_all if want_ctx else n_lat
        xn, h2, scores = _merge(xa, mod3, lp['g2'], ys, ys_c, tok_a, lp['wb'], lp['wo'], rw, n_rows,
                                group_of_tile)
        blk_e, dest, wts, pad_lo, pad_hi, n_used, cap = _route(scores[:, :N_EXPERTS], router_b, n_rows)
        xs = _dispatch(dest, pad_lo, pad_hi, h2, cap)
        ys = _grouped_ffn(blk_e, n_used, xs, lp['w1'], lp['w3'], lp['w2'])
        xa = _gather_combine(dest, xn, mod3, wts, ys, n_rows, group_of_tile)
    return xa[:n_lat].reshape(n_batch, seq, d)
```

```python
import functools
import math

import jax
import jax.numpy as jnp
import numpy as np
from jax import lax
from jax.experimental import pallas as pl
from jax.experimental.pallas import tpu as pltpu

F32 = jnp.float32
BF = jnp.bfloat16

GRID_W = 64
HEAD_DIM = 64
N_BRANCH = 4
BRANCH_W = 512
ROPE_BASE = 10000.0
EPS = 1e-6
NEG = -1e30
LOG2E = math.log2(math.e)
WIN_HEADS, WIN_KV_HEADS, WIN_RADIUS = 8, 2, 128
DIF_HEADS, DIF_QK_DIM, DIF_V_DIM = 4, 64, 128
NAT_HEADS, NAT_WIN_ROWS, NAT_WIN_COLS = 8, 8, 16
MLA_HEADS, MLA_NOPE, MLA_ROPE, MLA_V, MLA_Q_LORA, MLA_KV_LORA = 8, 64, 32, 64, 256, 128
MLA_QK = MLA_NOPE + MLA_ROPE
N_EXPERTS, N_GROUPS, TOP_K, D_EXPERT, MOE_BLOCK = 16, 4, 2, 512, 256

LANES = 128
TM = 512
WIN_TQ = 256
NAT_ROWS_PER_STEP = 4
NAT_KEY_ROWS = 12
DENSE_TQ = 1024
DENSE_TQ_ONLINE = 512
DENSE_TK = 2048
DENSE_TK_ONLINE = 1024
DENSE_RB = 64
DENSE_KC = 256
VMEM_LIMIT = 48 * 1024 * 1024

A_GATES, A_DV, A_NV, A_MQA, A_MKVA, A_WV = 0, 4096, 4608, 5120, 5376, 5632
A_COLS = 5888
Q_COLS = 1536
KT_DK, KT_NK, KT_WK = 0, 512, 1024
KT_ROWS = 1280
B_ROWS = Q_COLS + KT_ROWS


def _cparams(sem, vmem=VMEM_LIMIT):
    return pltpu.CompilerParams(dimension_semantics=sem, vmem_limit_bytes=vmem)


def _nt_dot(a, b):
    return lax.dot_general(a, b, (((1,), (1,)), ((), ())), preferred_element_type=F32)


def _norm_mod(x, g, sc, sh):
    ms = jnp.mean(x * x, axis=-1, keepdims=True)
    return (x * lax.rsqrt(ms + EPS) * g) * (1.0 + sc) + sh


def _lane_tile(a, n):
    reps = n // a.shape[1]
    return a if reps == 1 else jnp.concatenate([a] * reps, axis=1)


def _mod_kernel(c_ref, w_ref, b_ref, o_ref):
    cc = c_ref[...]
    a = cc * jax.nn.sigmoid(cc)
    o_ref[0] = jnp.dot(a, w_ref[0], preferred_element_type=F32,
                       precision=lax.Precision.HIGHEST) + b_ref[0]


def _modulation(cc, ada_w, ada_b):
    n_layers, d, d6 = ada_w.shape
    tn = 1536
    return pl.pallas_call(
        _mod_kernel,
        out_shape=jax.ShapeDtypeStruct((n_layers, 8, d6), F32),
        grid=(n_layers, d6 // tn),
        in_specs=[pl.BlockSpec((8, d), lambda l, j: (0, 0)),
                  pl.BlockSpec((1, d, tn), lambda l, j: (l, 0, j)),
                  pl.BlockSpec((1, 1, tn), lambda l, j: (l, 0, j))],
        out_specs=pl.BlockSpec((1, 8, tn), lambda l, j: (l, 0, j)),
        compiler_params=_cparams(("parallel", "parallel")),
        name="adaln_mod",
    )(cc, ada_w, ada_b.reshape(n_layers, 1, d6))


def _in_a_kernel(x_ref, mod_ref, g_ref, w_ref, o_ref):
    m = mod_ref[0]
    h = _norm_mod(x_ref[...], g_ref[...], m[1:2], m[0:1]).astype(BF)
    o_ref[...] = jnp.dot(h, w_ref[...], preferred_element_type=F32).astype(BF)


def _in_proj_a(xa, mod3, g1, w_a, group_of_tile):
    n, d = xa.shape
    tn = A_COLS // 2
    return pl.pallas_call(
        _in_a_kernel,
        out_shape=jax.ShapeDtypeStruct((n, A_COLS), BF),
        grid=(A_COLS // tn, n // TM),
        in_specs=[pl.BlockSpec((TM, d), lambda j, i: (i, 0)),
                  pl.BlockSpec((1, 8, d), lambda j, i: (group_of_tile(i), 0, 0)),
                  pl.BlockSpec((1, d), lambda j, i: (0, 0)),
                  pl.BlockSpec((d, tn), lambda j, i: (0, j))],
        out_specs=pl.BlockSpec((TM, tn), lambda j, i: (i, j)),
        compiler_params=_cparams(("parallel", "parallel")),
        name="in_proj_tok",
    )(xa, mod3, g1, w_a)


def _head_norm_rope(x, g, cos, sin, rope):
    ss = jnp.sum(x * x, axis=0, keepdims=True)
    y = x * lax.rsqrt(ss * (1.0 / HEAD_DIM) + EPS) * g
    if not rope:
        return y
    half = HEAD_DIM // 2
    y1, y2 = y[:half], y[half:]
    return jnp.concatenate([y1 * cos - y2 * sin, y1 * sin + y2 * cos], axis=0)


def _in_b_kernel(x_ref, mod_ref, g_ref, wt_ref, gain_ref, cos_ref, sin_ref, q_ref, kt_ref, acc_sc):
    m = mod_ref[0]
    h = _norm_mod(x_ref[...], g_ref[...], m[1:2], m[0:1]).astype(BF)
    acc_sc[...] = _nt_dot(wt_ref[...], h)
    tm = h.shape[0]
    cos = cos_ref[...]
    sin = sin_ref[...]

    def pair(r0, rope):
        hs = []
        for e in range(2):
            r = r0 + e * HEAD_DIM
            g = _lane_tile(gain_ref[r:r + HEAD_DIM, :], tm)
            hs.append(_head_norm_rope(acc_sc[r:r + HEAD_DIM, :], g, cos, sin, rope))
        return jnp.concatenate(hs, axis=0)

    for p in range(Q_COLS // LANES):
        y = pair(p * LANES, rope=p < 8)
        q_ref[:, p * LANES:(p + 1) * LANES] = y.T.astype(BF)
    for p in range(KT_ROWS // LANES):
        y = pair(Q_COLS + p * LANES, rope=not (4 <= p < 8))
        kt_ref[p * LANES:(p + 1) * LANES, :] = y.astype(BF)


def _in_proj_b(xa, mod3, g1, w_bt, gain_b, cos_t, sin_t, group_of_tile, pos_of_tile):
    n, d = xa.shape
    return pl.pallas_call(
        _in_b_kernel,
        out_shape=(jax.ShapeDtypeStruct((n, Q_COLS), BF),
                   jax.ShapeDtypeStruct((KT_ROWS, n), BF)),
        grid=(n // TM,),
        in_specs=[pl.BlockSpec((TM, d), lambda i: (i, 0)),
                  pl.BlockSpec((1, 8, d), lambda i: (group_of_tile(i), 0, 0)),
                  pl.BlockSpec((1, d), lambda i: (0, 0)),
                  pl.BlockSpec((B_ROWS, d), lambda i: (0, 0)),
                  pl.BlockSpec((B_ROWS, LANES), lambda i: (0, 0)),
                  pl.BlockSpec((HEAD_DIM // 2, TM), lambda i: (0, pos_of_tile(i))),
                  pl.BlockSpec((HEAD_DIM // 2, TM), lambda i: (0, pos_of_tile(i)))],
        out_specs=(pl.BlockSpec((TM, Q_COLS), lambda i: (i, 0)),
                   pl.BlockSpec((KT_ROWS, TM), lambda i: (0, i))),
        scratch_shapes=[pltpu.VMEM((B_ROWS, TM), F32)],
        compiler_params=_cparams(("parallel",)),
        name="in_proj_heads",
    )(xa, mod3, g1, w_bt, gain_b, cos_t, sin_t)


def _mla_kernel(qa_ref, kva_ref, gqa_ref, gkva_ref, wqt_ref, wkt_ref, wv_ref, gq_ref, gk_ref,
                cos_ref, sin_ref, mq_ref, mkt_ref, mv_ref):
    tm = qa_ref.shape[0]
    cos = cos_ref[...]
    sin = sin_ref[...]
    rh = MLA_ROPE // 2

    def rms_rows(x, g):
        ms = jnp.mean(x * x, axis=-1, keepdims=True)
        return x * lax.rsqrt(ms + EPS) * g

    def rope_rows(x):
        x1, x2 = x[:rh], x[rh:]
        return jnp.concatenate([x1 * cos - x2 * sin, x1 * sin + x2 * cos], axis=0)

    qa = rms_rows(qa_ref[...].astype(F32), gqa_ref[...]).astype(BF)
    qt = _nt_dot(wqt_ref[...], qa)
    kva = kva_ref[...].astype(F32)
    cn = rms_rows(kva[:, :MLA_KV_LORA], gkva_ref[...]).astype(BF)
    knt = _nt_dot(wkt_ref[...], cn)
    mv_ref[...] = jnp.dot(cn, wv_ref[...], preferred_element_type=F32).astype(BF)
    krope = kva[:, MLA_KV_LORA:].T[:MLA_ROPE]
    kr_ss = jnp.sum(krope * krope, axis=0, keepdims=True)
    gq = _lane_tile(gq_ref[...], tm)
    gk = _lane_tile(gk_ref[...], tm)
    zpad = jnp.zeros((LANES - MLA_QK, tm), F32)
    for hd in range(MLA_HEADS):
        x = qt[hd * MLA_QK:(hd + 1) * MLA_QK]
        ss = jnp.sum(x * x, axis=0, keepdims=True)
        y = x * lax.rsqrt(ss * (1.0 / MLA_QK) + EPS) * gq
        y = jnp.concatenate([y[:MLA_NOPE], rope_rows(y[MLA_NOPE:]), zpad], axis=0)
        mq_ref[:, hd * LANES:(hd + 1) * LANES] = y.T.astype(BF)
        kn = knt[hd * MLA_NOPE:(hd + 1) * MLA_NOPE]
        ss = jnp.sum(kn * kn, axis=0, keepdims=True) + kr_ss
        r = lax.rsqrt(ss * (1.0 / MLA_QK) + EPS)
        yk = jnp.concatenate([kn * r * gk[:MLA_NOPE], rope_rows(krope * r * gk[MLA_NOPE:]), zpad], axis=0)
        mkt_ref[hd * LANES:(hd + 1) * LANES, :] = yk.astype(BF)


def _mla_prep(tok_a, gqa, gkva, wq_t, wk_t, wv, gq, gk, cos_t, sin_t, pos_of_tile):
    n = tok_a.shape[0]
    hw = MLA_HEADS * LANES
    return pl.pallas_call(
        _mla_kernel,
        out_shape=(jax.ShapeDtypeStruct((n, hw), BF),
                   jax.ShapeDtypeStruct((hw, n), BF),
                   jax.ShapeDtypeStruct((n, MLA_HEADS * MLA_V), BF)),
        grid=(n // TM,),
        in_specs=[pl.BlockSpec((TM, 256), lambda i: (i, A_MQA // 256)),
                  pl.BlockSpec((TM, 256), lambda i: (i, A_MKVA // 256)),
                  pl.BlockSpec((1, MLA_Q_LORA), lambda i: (0, 0)),
                  pl.BlockSpec((1, MLA_KV_LORA), lambda i: (0, 0)),
                  pl.BlockSpec(wq_t.shape, lambda i: (0, 0)),
                  pl.BlockSpec(wk_t.shape, lambda i: (0, 0)),
                  pl.BlockSpec(wv.shape, lambda i: (0, 0)),
                  pl.BlockSpec((MLA_QK, LANES), lambda i: (0, 0)),
                  pl.BlockSpec((MLA_QK, LANES), lambda i: (0, 0)),
                  pl.BlockSpec((MLA_ROPE // 2, TM), lambda i: (0, pos_of_tile(i))),
                  pl.BlockSpec((MLA_ROPE // 2, TM), lambda i: (0, pos_of_tile(i)))],
        out_specs=(pl.BlockSpec((TM, hw), lambda i: (i, 0)),
                   pl.BlockSpec((hw, TM), lambda i: (0, i)),
                   pl.BlockSpec((TM, MLA_HEADS * MLA_V), lambda i: (i, 0))),
        compiler_params=_cparams(("parallel",)),
        name="mla_prep",
    )(tok_a, tok_a, gqa, gkva, wq_t, wk_t, wv, gq, gk, cos_t, sin_t)


def _half_mask(shape):
    return lax.broadcasted_iota(jnp.int32, shape, 1) < (LANES // 2)


def _select_half(q, e, lo_mask):
    zero = jnp.zeros_like(q)
    return jnp.where(lo_mask, q, zero) if e == 0 else jnp.where(lo_mask, zero, q)


def _local_softmax_out(parts, extra_logit):
    m = parts[0][0].max(axis=-1, keepdims=True)
    for s, _ in parts[1:]:
        m = jnp.maximum(m, s.max(axis=-1, keepdims=True))
    if extra_logit is not None:
        m = jnp.maximum(m, extra_logit)
    z = None
    o = None
    for s, v in parts:
        p = jnp.exp2(s - m)
        zs = p.sum(axis=-1, keepdims=True)
        os_ = jnp.dot(p.astype(BF), v, preferred_element_type=F32)
        z = zs if z is None else z + zs
        o = os_ if o is None else o + os_
    if extra_logit is not None:
        z = z + jnp.exp2(extra_logit - m)
    return o / z


def _fixed_softmax_out(parts, ones_ref, extra_logit):
    acc = None
    for s, v in parts:
        aug = jnp.concatenate([v, ones_ref[:s.shape[1], :]], axis=1)
        t = jnp.dot(jnp.exp2(s).astype(BF), aug, preferred_element_type=F32)
        acc = t if acc is None else acc + t
    l = acc[:, LANES:]
    if extra_logit is not None:
        l = l + jnp.exp2(jnp.zeros_like(l) + extra_logit)
    unsafe = jnp.logical_not((l > 2.0 ** -SAFE_SUM_LOG2) & (l < 2.0 ** SAFE_SUM_LOG2))
    return acc[:, :LANES] / l, jnp.max(jnp.where(unsafe, 1.0, 0.0), axis=0, keepdims=True)


def _win_kernel(sink_ref, q_ref, *refs, band, fixed=False):
    if band:
        k0, k1, k2, k3, v0, v1, v2, v3, kc_ref, vc_ref, mask_ref = refs[:11]
        refs = refs[11:]
        kb = jnp.concatenate([k0[...], k1[...], k2[...], k3[...]], axis=1)
        vb = jnp.concatenate([v0[...], v1[...], v2[...], v3[...]], axis=0)
        mask = mask_ref[0]
    else:
        kc_ref, vc_ref = refs[:2]
        refs = refs[2:]
    if fixed:
        ones_ref, o_ref, flag_ref = refs
        ref_logit = sink_ref[0, WIN_HEADS]
    else:
        (o_ref,) = refs
    q = q_ref[...]
    lo = _half_mask((q.shape[0], LANES))
    group = WIN_HEADS // WIN_KV_HEADS
    bad = []
    for j in range(WIN_HEADS // 2):
        qp = q[:, j * LANES:(j + 1) * LANES]
        g = (2 * j) // group
        kc = kc_ref[g * LANES:(g + 1) * LANES, :]
        vc = vc_ref[:, g * LANES:(g + 1) * LANES]
        outs = []
        for e in range(2):
            qm = _select_half(qp, e, lo)
            parts = []
            if band:
                s = jnp.dot(qm, kb[g * LANES:(g + 1) * LANES, :], preferred_element_type=F32) + mask
                parts.append((s, vb[:, g * LANES:(g + 1) * LANES]))
            sc = jnp.dot(qm, kc, preferred_element_type=F32)
            if fixed:
                parts.append((sc - ref_logit, vc))
                o, u = _fixed_softmax_out(parts, ones_ref, sink_ref[0, 2 * j + e])
                bad.append(u)
                outs.append(o)
            else:
                parts.append((sc, vc))
                outs.append(_local_softmax_out(parts, sink_ref[0, 2 * j + e]))
        o_ref[:, j * LANES:(j + 1) * LANES] = jnp.where(lo, outs[0], outs[1]).astype(BF)
    if fixed:
        flag_ref[0] = jnp.concatenate(bad, axis=0)


def _win_attention(sink, q_tok, kt, tok_a, mask_tbl, n_batch, seq, n_ctx, latent, ones=None):
    ctx_blk = (n_batch * seq) // n_ctx
    kc_spec = lambda f: pl.BlockSpec((2 * LANES, n_ctx), f)
    vc_spec = lambda f: pl.BlockSpec((n_ctx, 2 * LANES), f)
    smem = pl.BlockSpec(memory_space=pltpu.SMEM)
    n_out = n_batch * (seq if latent else n_ctx)
    out_shape = jax.ShapeDtypeStruct((n_out, WIN_HEADS * HEAD_DIM), BF)
    if not latent:
        return pl.pallas_call(
            functools.partial(_win_kernel, band=False),
            out_shape=out_shape,
            grid=(n_batch,),
            in_specs=[smem,
                      pl.BlockSpec((n_ctx, 512), lambda b: (ctx_blk + b, 0)),
                      kc_spec(lambda b: (KT_WK // 256, ctx_blk + b)),
                      vc_spec(lambda b: (ctx_blk + b, A_WV // 256))],
            out_specs=pl.BlockSpec((n_ctx, 512), lambda b: (b, 0)),
            compiler_params=_cparams(("parallel",)),
            name="win_attn_ctx",
        )(sink, q_tok, kt, tok_a)
    nq = seq // WIN_TQ
    nkb = seq // LANES

    def kidx(j):
        return lambda b, i: (KT_WK // 256, b * nkb + jnp.clip(2 * i - 1 + j, 0, nkb - 1))

    def vidx(j):
        return lambda b, i: (b * nkb + jnp.clip(2 * i - 1 + j, 0, nkb - 1), A_WV // 256)

    def variant(b, i):
        return (jnp.where(i == 0, 0, jnp.where(i == nq - 1, 2, 1)), 0, 0)

    in_specs = ([smem, pl.BlockSpec((WIN_TQ, 512), lambda b, i: (b * nq + i, 0))]
                + [pl.BlockSpec((2 * LANES, LANES), kidx(j)) for j in range(4)]
                + [pl.BlockSpec((LANES, 2 * LANES), vidx(j)) for j in range(4)]
                + [kc_spec(lambda b, i: (KT_WK // 256, ctx_blk + b)),
                   vc_spec(lambda b, i: (ctx_blk + b, A_WV // 256)),
                   pl.BlockSpec((1, WIN_TQ, 4 * LANES), variant)])
    args = [sink, q_tok, kt, kt, kt, kt, tok_a, tok_a, tok_a, tok_a, kt, tok_a, mask_tbl]
    out_specs = pl.BlockSpec((WIN_TQ, 512), lambda b, i: (b * nq + i, 0))
    fixed = ones is not None
    if fixed:
        in_specs.append(pl.BlockSpec(ones.shape, lambda b, i: (0, 0)))
        args.append(ones)
        out_shape = (out_shape, jax.ShapeDtypeStruct((n_batch * nq, WIN_HEADS, LANES), F32))
        out_specs = (out_specs, pl.BlockSpec((1, WIN_HEADS, LANES), lambda b, i: (b * nq + i, 0, 0)))
    return pl.pallas_call(
        functools.partial(_win_kernel, band=True, fixed=fixed),
        out_shape=out_shape,
        grid=(n_batch, nq),
        in_specs=in_specs,
        out_specs=out_specs,
        compiler_params=_cparams(("parallel", "parallel")),
        name="win_attn_fixed" if fixed else "win_attn",
    )(*args)


def _nat_kernel(q_ref, k0, k1, k2, v0, v1, v2, kc_ref, vc_ref, bias_ref, *refs, fixed=False):
    if fixed:
        par_ref, ones_ref, o_ref, flag_ref = refs
        ref_logit = par_ref[0, 0]
    else:
        (o_ref,) = refs
    q = q_ref[...]
    kb = jnp.concatenate([k0[...], k1[...], k2[...]], axis=1)
    vb = jnp.concatenate([v0[...], v1[...], v2[...]], axis=0)
    lo = _half_mask((q.shape[0], LANES))
    bad = []
    for j in range(NAT_HEADS // 2):
        sl = slice(j * LANES, (j + 1) * LANES)
        qp = q[:, sl]
        outs = []
        for e in range(2):
            qm = _select_half(qp, e, lo)
            s = jnp.dot(qm, kb[sl, :], preferred_element_type=F32) + bias_ref[0, 2 * j + e]
            sc = jnp.dot(qm, kc_ref[sl, :], preferred_element_type=F32)
            if fixed:
                o, u = _fixed_softmax_out([(s, vb[:, sl]), (sc - ref_logit, vc_ref[:, sl])], ones_ref, None)
                bad.append(u)
                outs.append(o)
            else:
                outs.append(_local_softmax_out([(s, vb[:, sl]), (sc, vc_ref[:, sl])], None))
        o_ref[:, sl] = jnp.where(lo, outs[0], outs[1]).astype(BF)
    if fixed:
        flag_ref[0] = jnp.concatenate(bad, axis=0)


def _nat_attention(q_tok, kt, tok_a, bias_tbl, n_batch, seq, n_ctx, par=None, ones=None):
    tq = NAT_ROWS_PER_STEP * GRID_W
    nq = seq // tq
    rows = seq // GRID_W
    ctx_blk = (n_batch * seq) // n_ctx
    q_col = 2
    k_row = KT_NK // 512
    v_col = A_NV // 512

    def wstart(i):
        return jnp.clip(NAT_ROWS_PER_STEP * i - NAT_WIN_ROWS // 2, 0, rows - NAT_KEY_ROWS) // NAT_ROWS_PER_STEP

    def kidx(j):
        return lambda b, i: (k_row, b * nq + wstart(i) + j)

    def vidx(j):
        return lambda b, i: (b * nq + wstart(i) + j, v_col)

    def variant(b, i):
        return (jnp.where(i == 0, 0, jnp.where(i == nq - 1, 2, 1)), 0, 0, 0)

    nk = NAT_KEY_ROWS * GRID_W
    in_specs = ([pl.BlockSpec((tq, 512), lambda b, i: (b * nq + i, q_col))]
                + [pl.BlockSpec((512, tq), kidx(j)) for j in range(3)]
                + [pl.BlockSpec((tq, 512), vidx(j)) for j in range(3)]
                + [pl.BlockSpec((512, n_ctx), lambda b, i: (k_row, ctx_blk + b)),
                   pl.BlockSpec((n_ctx, 512), lambda b, i: (ctx_blk + b, v_col)),
                   pl.BlockSpec((1, NAT_HEADS, tq, nk), variant)])
    args = [q_tok, kt, kt, kt, tok_a, tok_a, tok_a, kt, tok_a, bias_tbl]
    out_shape = jax.ShapeDtypeStruct((n_batch * seq, NAT_HEADS * HEAD_DIM), BF)
    out_specs = pl.BlockSpec((tq, 512), lambda b, i: (b * nq + i, 0))
    fixed = ones is not None
    if fixed:
        in_specs += [pl.BlockSpec(memory_space=pltpu.SMEM), pl.BlockSpec(ones.shape, lambda b, i: (0, 0))]
        args += [par, ones]
        out_shape = (out_shape, jax.ShapeDtypeStruct((n_batch * nq, NAT_HEADS, LANES), F32))
        out_specs = (out_specs, pl.BlockSpec((1, NAT_HEADS, LANES), lambda b, i: (b * nq + i, 0, 0)))
    return pl.pallas_call(
        functools.partial(_nat_kernel, fixed=fixed),
        out_shape=out_shape,
        grid=(n_batch, nq),
        in_specs=in_specs,
        out_specs=out_specs,
        compiler_params=_cparams(("parallel", "parallel")),
        name="nat_attn_fixed" if fixed else "nat_attn",
    )(*args)


def _dense_kernel(lam_ref, q_ref, kc_ref, vc_ref, *refs, n_heads, packed, diff, latent, lam_scale):
    if latent:
        k_ref, v_ref = refs[0], refs[1]
        refs = refs[2:]
    if diff:
        subln_ref, o_ref, qm_sc, m_sc, l_sc, acc_sc, s_sc, p_sc = refs
    else:
        o_ref, qm_sc, m_sc, l_sc, acc_sc, s_sc, p_sc = refs
    kt_step = pl.program_id(2) if latent else 0
    tq = q_ref.shape[0]

    def kv_slices(h):
        blk = h // 2 if packed else h
        ks = slice(blk * LANES, (blk + 1) * LANES)
        vs = ks if packed else slice((h // 2) * LANES, (h // 2 + 1) * LANES)
        return ks, vs

    def step(h, k_ref_, v_ref_):
        ks, vs = kv_slices(h)
        nk = k_ref_.shape[1]
        slot = h % 2
        s_sc[slot, :, :nk] = jnp.dot(qm_sc[h], k_ref_[ks, :], preferred_element_type=F32)
        for r in range(tq // DENSE_RB):
            rows = slice(r * DENSE_RB, (r + 1) * DENSE_RB)
            mx = s_sc[slot, rows, 0:LANES]
            for c in range(1, nk // LANES):
                mx = jnp.maximum(mx, s_sc[slot, rows, c * LANES:(c + 1) * LANES])
            m_old = m_sc[h, rows, :]
            m_new = jnp.maximum(m_old, jnp.max(mx, axis=-1, keepdims=True))
            alpha = jnp.exp2(m_old - m_new)
            lsum = None
            for c in range(nk // LANES):
                cols = slice(c * LANES, (c + 1) * LANES)
                p = jnp.exp2(s_sc[slot, rows, cols] - m_new)
                lsum = p if lsum is None else lsum + p
                p_sc[slot, rows, cols] = p.astype(BF)
            m_sc[h, rows, :] = m_new
            l_sc[h, rows, :] = alpha * l_sc[h, rows, :] + jnp.sum(lsum, axis=-1, keepdims=True)
            acc_sc[h, rows, :] = alpha * acc_sc[h, rows, :]
        acc_sc[h] += jnp.dot(p_sc[slot, :, :nk], v_ref_[:, vs], preferred_element_type=F32)

    @pl.when(kt_step == 0)
    def _():
        q = q_ref[...]
        lo = _half_mask((tq, LANES))
        for h in range(n_heads):
            if packed:
                qp = q[:, (h // 2) * LANES:(h // 2 + 1) * LANES]
                qm_sc[h] = _select_half(qp, h % 2, lo)
            else:
                qm_sc[h] = q[:, h * LANES:(h + 1) * LANES]
        m_sc[...] = jnp.full(m_sc.shape, NEG, F32)
        l_sc[...] = jnp.zeros(l_sc.shape, F32)
        acc_sc[...] = jnp.zeros(acc_sc.shape, F32)
        for h in range(n_heads):
            step(h, kc_ref, vc_ref)

    if latent:
        for h in range(n_heads):
            step(h, k_ref, v_ref)
        last = kt_step == pl.num_programs(2) - 1
    else:
        last = True

    def finish():
        lo = _half_mask((tq, LANES))
        if diff:
            lam = lam_ref[0, 0]
            for hv in range(n_heads // 2):
                y = (acc_sc[2 * hv] / l_sc[2 * hv]
                     - lam * (acc_sc[2 * hv + 1] / l_sc[2 * hv + 1]))
                ms = jnp.mean(y * y, axis=-1, keepdims=True)
                y = y * lax.rsqrt(ms + EPS) * subln_ref[...] * lam_scale
                o_ref[:, hv * LANES:(hv + 1) * LANES] = y.astype(BF)
        else:
            for hp in range(n_heads // 2):
                o0 = acc_sc[2 * hp] / l_sc[2 * hp]
                o1 = acc_sc[2 * hp + 1] / l_sc[2 * hp + 1]
                o_ref[:, hp * LANES:(hp + 1) * LANES] = jnp.where(lo, o0, o1).astype(BF)

    if latent:
        pl.when(last)(finish)
    else:
        finish()


def _dense_attention(lam, q_arr, q_col, q_w, k_arr, k_row, k_w, v_arr, v_col, v_w, subln,
                     n_batch, seq, n_ctx, *, n_heads, packed, diff, latent, lam_scale, name):
    n = n_batch * (seq if latent else n_ctx)
    ctx_blk = (n_batch * seq) // n_ctx
    out_w = v_w
    smem = pl.BlockSpec(memory_space=pltpu.SMEM)
    kern = functools.partial(_dense_kernel, n_heads=n_heads, packed=packed, diff=diff,
                             latent=latent, lam_scale=lam_scale)
    tq = DENSE_TQ_ONLINE if latent else n_ctx
    tk = DENSE_TK_ONLINE
    max_nk = tk if latent else n_ctx
    scratch = [pltpu.VMEM((n_heads, tq, LANES), BF),
               pltpu.VMEM((n_heads, tq, LANES), F32),
               pltpu.VMEM((n_heads, tq, LANES), F32),
               pltpu.VMEM((n_heads, tq, LANES), F32),
               pltpu.VMEM((2, tq, max_nk), F32),
               pltpu.VMEM((2, tq, max_nk), BF)]
    out_shape = jax.ShapeDtypeStruct((n, out_w), BF)
    if latent:
        nq = seq // tq
        nk = seq // tk
        grid = (n_batch, nq, nk)
        in_specs = [smem,
                    pl.BlockSpec((tq, q_w), lambda b, i, k: (b * nq + i, q_col)),
                    pl.BlockSpec((k_w, n_ctx), lambda b, i, k: (k_row, ctx_blk + b)),
                    pl.BlockSpec((n_ctx, v_w), lambda b, i, k: (ctx_blk + b, v_col)),
                    pl.BlockSpec((k_w, tk), lambda b, i, k: (k_row, b * nk + k)),
                    pl.BlockSpec((tk, v_w), lambda b, i, k: (b * nk + k, v_col))]
        args = [lam, q_arr, k_arr, v_arr, k_arr, v_arr]
        if diff:
            in_specs.append(pl.BlockSpec((1, LANES), lambda b, i, k: (0, 0)))
            args.append(subln)
        out_specs = pl.BlockSpec((tq, out_w), lambda b, i, k: (b * nq + i, 0))
        sem = ("parallel", "parallel", "arbitrary")
    else:
        grid = (n_batch,)
        in_specs = [smem,
                    pl.BlockSpec((tq, q_w), lambda b: (ctx_blk + b, q_col)),
                    pl.BlockSpec((k_w, n_ctx), lambda b: (k_row, ctx_blk + b)),
                    pl.BlockSpec((n_ctx, v_w), lambda b: (ctx_blk + b, v_col))]
        args = [lam, q_arr, k_arr, v_arr]
        if diff:
            in_specs.append(pl.BlockSpec((1, LANES), lambda b: (0, 0)))
            args.append(subln)
        out_specs = pl.BlockSpec((tq, out_w), lambda b: (b, 0))
        sem = ("parallel",)
    return pl.pallas_call(
        kern, out_shape=out_shape, grid=grid, in_specs=in_specs, out_specs=out_specs,
        scratch_shapes=scratch, compiler_params=_cparams(sem), name=name,
    )(*args)


SAFE_SUM_LOG2 = 100.0


def _dense_fixed_kernel(par_ref, q_ref, kc_ref, vc_ref, *refs, n_heads, packed, diff, latent, lam_scale):
    if latent:
        k_ref, v_ref = refs[0], refs[1]
        refs = refs[2:]
    ones_ref = refs[0]
    refs = refs[1:]
    if diff:
        subln_ref, o_ref, flag_ref, qm_sc, acc_sc, p_sc = refs
    else:
        o_ref, flag_ref, qm_sc, acc_sc, p_sc = refs
    kt_step = pl.program_id(2) if latent else 0
    tq = q_ref.shape[0]
    ref_logit = par_ref[0, 1]

    def step(h, k_ref_, v_ref_, first):
        blk = h // 2 if packed else h
        ks = slice(blk * LANES, (blk + 1) * LANES)
        vs = ks if packed else slice((h // 2) * LANES, (h // 2 + 1) * LANES)
        nk = k_ref_.shape[1]
        slot = h % 2
        for c in range(nk // DENSE_KC):
            cols = slice(c * DENSE_KC, (c + 1) * DENSE_KC)
            s = jnp.dot(qm_sc[h], k_ref_[ks, cols], preferred_element_type=F32)
            p_sc[slot, :, cols] = jnp.exp2(s - ref_logit).astype(BF)
        v_aug = jnp.concatenate([v_ref_[:, vs], ones_ref[:nk, :]], axis=1)
        pv = jnp.dot(p_sc[slot, :, :nk], v_aug, preferred_element_type=F32)
        if first:
            acc_sc[h] = pv
        else:
            acc_sc[h] += pv

    @pl.when(kt_step == 0)
    def _():
        q = q_ref[...]
        lo = _half_mask((tq, LANES))
        for h in range(n_heads):
            if packed:
                qp = q[:, (h // 2) * LANES:(h // 2 + 1) * LANES]
                qm_sc[h] = _select_half(qp, h % 2, lo)
            else:
                qm_sc[h] = q[:, h * LANES:(h + 1) * LANES]
        for h in range(n_heads):
            step(h, kc_ref, vc_ref, True)

    if latent:
        for h in range(n_heads):
            step(h, k_ref, v_ref, False)
        last = kt_step == pl.num_programs(2) - 1

    def finish():
        lo = _half_mask((tq, LANES))
        outs, bad = [], []
        for h in range(n_heads):
            a = acc_sc[h]
            l = a[:, LANES:]
            unsafe = jnp.logical_not((l > 2.0 ** -SAFE_SUM_LOG2) & (l < 2.0 ** SAFE_SUM_LOG2))
            bad.append(jnp.max(jnp.where(unsafe, 1.0, 0.0), axis=0, keepdims=True))
            outs.append(a[:, :LANES] / l)
        flag_ref[0] = jnp.concatenate(bad, axis=0)
        if diff:
            lam = par_ref[0, 0]
            for hv in range(n_heads // 2):
                y = outs[2 * hv] - lam * outs[2 * hv + 1]
                ms = jnp.mean(y * y, axis=-1, keepdims=True)
                y = y * lax.rsqrt(ms + EPS) * subln_ref[...] * lam_scale
                o_ref[:, hv * LANES:(hv + 1) * LANES] = y.astype(BF)
        else:
            for hp in range(n_heads // 2):
                o_ref[:, hp * LANES:(hp + 1) * LANES] = jnp.where(lo, outs[2 * hp], outs[2 * hp + 1]).astype(BF)

    if latent:
        pl.when(last)(finish)
    else:
        finish()


def _dense_fixed_attention(par, q_arr, q_col, q_w, k_arr, k_row, k_w, v_arr, v_col, v_w, subln, ones,
                           n_batch, seq, n_ctx, *, n_heads, packed, diff, latent, lam_scale, name):
    n = n_batch * (seq if latent else n_ctx)
    ctx_blk = (n_batch * seq) // n_ctx
    smem = pl.BlockSpec(memory_space=pltpu.SMEM)
    kern = functools.partial(_dense_fixed_kernel, n_heads=n_heads, packed=packed, diff=diff,
                             latent=latent, lam_scale=lam_scale)
    tq = DENSE_TQ if latent else n_ctx
    max_nk = DENSE_TK if latent else n_ctx
    scratch = [pltpu.VMEM((n_heads, tq, LANES), BF),
               pltpu.VMEM((n_heads, tq, 2 * LANES), F32),
               pltpu.VMEM((2, tq, max_nk), BF)]
    if latent:
        nq = seq // tq
        nk = seq // DENSE_TK
        grid = (n_batch, nq, nk)
        ix = lambda f: (lambda b, i, k: f(b, i, k))
        q_ix = ix(lambda b, i, k: (b * nq + i, q_col))
        in_specs = [smem,
                    pl.BlockSpec((tq, q_w), q_ix),
                    pl.BlockSpec((k_w, n_ctx), ix(lambda b, i, k: (k_row, ctx_blk + b))),
                    pl.BlockSpec((n_ctx, v_w), ix(lambda b, i, k: (ctx_blk + b, v_col))),
                    pl.BlockSpec((k_w, DENSE_TK), ix(lambda b, i, k: (k_row, b * nk + k))),
                    pl.BlockSpec((DENSE_TK, v_w), ix(lambda b, i, k: (b * nk + k, v_col))),
                    pl.BlockSpec((DENSE_TK, LANES), ix(lambda b, i, k: (0, 0)))]
        args = [par, q_arr, k_arr, v_arr, k_arr, v_arr, ones]
        const_ix = ix(lambda b, i, k: (0, 0))
        out_specs = (pl.BlockSpec((tq, v_w), ix(lambda b, i, k: (b * nq + i, 0))),
                     pl.BlockSpec((1, n_heads, LANES), ix(lambda b, i, k: (b * nq + i, 0, 0))))
        n_flag = n_batch * nq
        sem = ("parallel", "parallel", "arbitrary")
    else:
        grid = (n_batch,)
        in_specs = [smem,
                    pl.BlockSpec((tq, q_w), lambda b: (ctx_blk + b, q_col)),
                    pl.BlockSpec((k_w, n_ctx), lambda b: (k_row, ctx_blk + b)),
                    pl.BlockSpec((n_ctx, v_w), lambda b: (ctx_blk + b, v_col)),
                    pl.BlockSpec((DENSE_TK, LANES), lambda b: (0, 0))]
        args = [par, q_arr, k_arr, v_arr, ones]
        const_ix = lambda b: (0, 0)
        out_specs = (pl.BlockSpec((tq, v_w), lambda b: (b, 0)),
                     pl.BlockSpec((1, n_heads, LANES), lambda b: (b, 0, 0)))
        n_flag = n_batch
        sem = ("parallel",)
    if diff:
        in_specs.append(pl.BlockSpec((1, LANES), const_ix))
        args.append(subln)
    return pl.pallas_call(
        kern,
        out_shape=(jax.ShapeDtypeStruct((n, v_w), BF), jax.ShapeDtypeStruct((n_flag, n_heads, LANES), F32)),
        grid=grid, in_specs=in_specs, out_specs=out_specs,
        scratch_shapes=scratch, compiler_params=_cparams(sem), name=name + "_fixed",
    )(*args)


def _merge_kernel(x_ref, mod_ref, g2_ref, *refs, n_lat_tiles, has_ctx):
    ys = refs[:N_BRANCH]
    refs = refs[N_BRANCH:]
    if has_ctx:
        ycs = refs[:N_BRANCH]
        refs = refs[N_BRANCH:]
        is_ctx = pl.program_id(0) >= n_lat_tiles
    gts = refs[:N_BRANCH]
    wb_ref, wo_ref, rw_ref, xo_ref, h2_ref, sc_ref = refs[N_BRANCH:]
    m = mod_ref[0]
    mix = None
    for n_ in range(N_BRANCH):
        y = ys[n_][...]
        if has_ctx:
            y = jnp.where(is_ctx, ycs[n_][...], y)
        yb = jnp.dot(y, wb_ref[n_], preferred_element_type=F32)
        t = (0.5 + 0.5 * jnp.tanh(0.5 * gts[n_][...].astype(F32))) * yb
        mix = t if mix is None else mix + t
    att = jnp.dot(mix.astype(BF), wo_ref[...], preferred_element_type=F32)
    xn = x_ref[...] + m[2:3] * att
    xo_ref[...] = xn
    h2 = _norm_mod(xn, g2_ref[...], m[4:5], m[3:4])
    h2_ref[...] = h2
    h_hi = h2.astype(BF)
    h_lo = (h2 - h_hi.astype(F32)).astype(BF)
    logits = (jnp.dot(h_hi, rw_ref[0], preferred_element_type=F32)
              + (jnp.dot(h_lo, rw_ref[0], preferred_element_type=F32)
                 + jnp.dot(h_hi, rw_ref[1], preferred_element_type=F32)))
    sc_ref[...] = jax.nn.sigmoid(logits)


def _merge(xa, mod3, g2, ys, ys_ctx, tok_a, wb, wo, rw, n_rows, group_of_tile):
    d = xa.shape[1]
    n_lat_tiles = ys[0].shape[0] // TM
    has_ctx = ys_ctx is not None
    row = lambda w, c: pl.BlockSpec((TM, w), lambda i, c=c: (i, c))
    lat_row = pl.BlockSpec((TM, BRANCH_W), lambda i: (jnp.minimum(i, n_lat_tiles - 1), 0))
    in_specs = ([row(d, 0),
                 pl.BlockSpec((1, 8, d), lambda i: (group_of_tile(i), 0, 0)),
                 pl.BlockSpec((1, d), lambda i: (0, 0))]
                + [lat_row for _ in range(N_BRANCH)]
                + ([pl.BlockSpec((TM, BRANCH_W), lambda i: (0, 0)) for _ in range(N_BRANCH)] if has_ctx else [])
                + [row(d, c) for c in range(N_BRANCH)]
                + [pl.BlockSpec(wb.shape, lambda i: (0, 0, 0)),
                   pl.BlockSpec(wo.shape, lambda i: (0, 0)),
                   pl.BlockSpec(rw.shape, lambda i: (0, 0, 0))])
    ys = list(ys) + (list(ys_ctx) if has_ctx else [])
    return pl.pallas_call(
        functools.partial(_merge_kernel, n_lat_tiles=n_lat_tiles, has_ctx=has_ctx),
        out_shape=(jax.ShapeDtypeStruct((n_rows, d), F32),
                   jax.ShapeDtypeStruct((n_rows, d), F32),
                   jax.ShapeDtypeStruct((n_rows, LANES), F32)),
        grid=(n_rows // TM,),
        in_specs=in_specs,
        out_specs=(row(d, 0), row(d, 0), row(LANES, 0)),
        compiler_params=_cparams(("parallel",)),
        name="merge",
    )(xa, mod3, g2, *ys, tok_a, tok_a, tok_a, tok_a, wb, wo, rw)


def _route(scores, router_b, n):
    per_group = N_EXPERTS // N_GROUPS
    st = scores.T
    biased = st + router_b.astype(F32)[:, None]

    def top2(v, axis):
        pos = lax.broadcasted_iota(jnp.int32, v.shape, axis)
        i0 = jnp.argmax(v, axis=axis).astype(jnp.int32)
        v0 = jnp.max(v, axis=axis)
        rest = jnp.where(pos == jnp.expand_dims(i0, axis), -jnp.inf, v)
        i1 = jnp.argmax(rest, axis=axis).astype(jnp.int32)
        v1 = jnp.max(rest, axis=axis)
        return (v0, v1), (i0, i1)

    (g0, g1), _ = top2(biased.reshape(N_GROUPS, per_group, n), 1)
    group = jnp.argmax(g0 + g1, axis=0)
    expert = jnp.arange(N_EXPERTS, dtype=jnp.int32)[:, None]
    in_group = (expert // per_group) == group[None, :]
    _, (e0, e1) = top2(jnp.where(in_group, biased, -jnp.inf), 0)
    oh0 = (expert == e0[None, :]).astype(F32)
    oh1 = (expert == e1[None, :]).astype(F32)
    w0 = (oh0 * st).sum(0)
    w1 = (oh1 * st).sum(0)
    wts = jnp.stack([w0, w1], axis=-1) / (w0 + w1)[:, None]
    n_asg = n * TOP_K
    cnt = (oh0 + oh1).reshape(N_EXPERTS, n // MOE_BLOCK, MOE_BLOCK)
    tri = jnp.triu(jnp.ones((MOE_BLOCK, MOE_BLOCK), F32), 1)
    within = jnp.einsum('ebj,ji->ebi', cnt, tri)
    blk_tot = cnt.sum(axis=-1)
    nb = blk_tot.shape[1]
    blk_off = jnp.einsum('eb,bc->ec', blk_tot, jnp.triu(jnp.ones((nb, nb), F32), 1),
                         precision=lax.Precision.HIGHEST)
    prefix = (within + blk_off[:, :, None]).reshape(N_EXPERTS, n)
    counts = blk_tot.sum(axis=-1).astype(jnp.int32)
    padded = (counts + MOE_BLOCK - 1) // MOE_BLOCK * MOE_BLOCK
    pad_end = jnp.cumsum(padded)
    pad_start = pad_end - padded
    slot = prefix + pad_start.astype(F32)[:, None]
    dest = jnp.stack([(oh0 * slot).sum(0), (oh1 * slot).sum(0)], axis=-1).reshape(n_asg)
    n_blk = (n_asg + N_EXPERTS * (MOE_BLOCK - 1) + MOE_BLOCK - 1) // MOE_BLOCK
    cap = n_blk * MOE_BLOCK
    blk_start = jnp.arange(n_blk, dtype=jnp.int32) * MOE_BLOCK
    blk_e = jnp.minimum((blk_start[:, None] >= pad_end[None, :]).sum(-1), N_EXPERTS - 1).astype(jnp.int32)
    pad_lo = jnp.concatenate([pad_start + counts, pad_end[-1:]]).astype(jnp.int32)
    pad_hi = jnp.concatenate([pad_end, jnp.full((1,), cap)]).astype(jnp.int32)
    n_used = (pad_end[-1:] // MOE_BLOCK).astype(jnp.int32)
    return blk_e, dest.astype(jnp.int32), wts.astype(F32), pad_lo, pad_hi, n_used, cap


def _dispatch_kernel(dest_ref, pad_lo_ref, pad_hi_ref, h2_ref, xs_hbm, zrow, sem, zsem):
    i = pl.program_id(0)

    @pl.when(i == 0)
    def _():
        zrow[...] = jnp.zeros(zrow.shape, F32)
        for e in range(N_EXPERTS + 1):
            lo, hi = pad_lo_ref[e], pad_hi_ref[e]

            def zero_row(r, c):
                pltpu.make_async_copy(zrow.at[pl.ds(0, 1)], xs_hbm.at[pl.ds(r, 1)], zsem).start()
                return c

            def zero_wait(r, c):
                pltpu.make_async_copy(zrow.at[pl.ds(0, 1)], xs_hbm.at[pl.ds(r, 1)], zsem).wait()
                return c

            lax.fori_loop(lo, hi, zero_row, 0)
            lax.fori_loop(lo, hi, zero_wait, 0)

    tm = h2_ref.shape[0]

    def copy_row(r, c):
        a = (i * tm + r) * TOP_K
        for k in range(TOP_K):
            pltpu.make_async_copy(h2_ref.at[pl.ds(r, 1)], xs_hbm.at[pl.ds(dest_ref[a + k], 1)], sem).start()
        return c

    lax.fori_loop(0, tm, copy_row, 0, unroll=4)
    for k in range(TOP_K):
        pltpu.make_async_copy(h2_ref, xs_hbm.at[pl.ds(0, tm)], sem).wait()


def _dispatch(dest, pad_lo, pad_hi, h2, cap):
    n, d = h2.shape
    grid_spec = pltpu.PrefetchScalarGridSpec(
        num_scalar_prefetch=3,
        grid=(n // TM,),
        in_specs=[pl.BlockSpec((TM, d), lambda i, de, lo, hi: (i, 0))],
        out_specs=pl.BlockSpec(memory_space=pl.ANY),
        scratch_shapes=[pltpu.VMEM((8, d), F32), pltpu.SemaphoreType.DMA, pltpu.SemaphoreType.DMA],
    )
    return pl.pallas_call(
        _dispatch_kernel,
        out_shape=jax.ShapeDtypeStruct((cap, d), F32),
        grid_spec=grid_spec,
        compiler_params=_cparams(("arbitrary",)),
        name="moe_dispatch",
    )(dest, pad_lo, pad_hi, h2)


def _grouped_ffn_kernel(blk_e_ref, n_used_ref, x_ref, w1_ref, w3_ref, w2_ref, o_ref):
    i = pl.program_id(0)

    @pl.when(i < n_used_ref[0])
    def _():
        xb = x_ref[...].astype(BF)
        a = jnp.dot(xb, w1_ref[0], preferred_element_type=F32)
        b = jnp.dot(xb, w3_ref[0], preferred_element_type=F32)
        hmid = (a * jax.nn.sigmoid(a) * b).astype(BF)
        o_ref[...] = jnp.dot(hmid, w2_ref[0], preferred_element_type=F32)

    @pl.when(i >= n_used_ref[0])
    def _():
        o_ref[...] = jnp.zeros(o_ref.shape, F32)


def _grouped_ffn(blk_e, n_used, xs, w1, w3, w2):
    cap, d = xs.shape
    wspec = lambda shp: pl.BlockSpec((1,) + shp, lambda i, be, nu: (be[i], 0, 0))
    grid_spec = pltpu.PrefetchScalarGridSpec(
        num_scalar_prefetch=2,
        grid=(cap // MOE_BLOCK,),
        in_specs=[pl.BlockSpec((MOE_BLOCK, d), lambda i, be, nu: (jnp.minimum(i, nu[0] - 1), 0)),
                  wspec((d, D_EXPERT)), wspec((d, D_EXPERT)), wspec((D_EXPERT, d))],
        out_specs=pl.BlockSpec((MOE_BLOCK, d), lambda i, be, nu: (i, 0)),
    )
    return pl.pallas_call(
        _grouped_ffn_kernel,
        out_shape=jax.ShapeDtypeStruct((cap, d), F32),
        grid_spec=grid_spec,
        compiler_params=_cparams(("arbitrary",)),
        name="moe_ffn",
    )(blk_e, n_used, xs, w1, w3, w2)


def _gather_combine_kernel(dest_ref, x_ref, mod_ref, w_ref, ys_hbm, o_ref, buf, sem):
    i = pl.program_id(0)
    n_steps = pl.num_programs(0)
    slot = i % 2
    tm = x_ref.shape[0]

    def start_gather(tile, s):
        def body(r, c):
            a = (tile * tm + r) * TOP_K
            for k in range(TOP_K):
                pltpu.make_async_copy(ys_hbm.at[pl.ds(dest_ref[a + k], 1)], buf.at[s, k, pl.ds(r, 1)],
                                      sem.at[s]).start()
            return c
        lax.fori_loop(0, tm, body, 0, unroll=4)

    @pl.when(i == 0)
    def _():
        start_gather(0, 0)

    @pl.when(i + 1 < n_steps)
    def _():
        start_gather(i + 1, 1 - slot)

    for k in range(TOP_K):
        pltpu.make_async_copy(ys_hbm.at[pl.ds(0, tm)], buf.at[slot, k], sem.at[slot]).wait()
    w = w_ref[...]
    f = w[:, 0:1] * buf[slot, 0] + w[:, 1:2] * buf[slot, 1]
    o_ref[...] = x_ref[...] + mod_ref[0][5:6] * f


def _gather_combine(dest, xn, mod3, wts, ys, n_rows, group_of_tile):
    d = xn.shape[1]
    grid_spec = pltpu.PrefetchScalarGridSpec(
        num_scalar_prefetch=1,
        grid=(n_rows // TM,),
        in_specs=[pl.BlockSpec((TM, d), lambda i, de: (i, 0)),
                  pl.BlockSpec((1, 8, d), lambda i, de: (group_of_tile(i), 0, 0)),
                  pl.BlockSpec((TM, TOP_K), lambda i, de: (i, 0)),
                  pl.BlockSpec(memory_space=pl.ANY)],
        out_specs=pl.BlockSpec((TM, d), lambda i, de: (i, 0)),
        scratch_shapes=[pltpu.VMEM((2, TOP_K, TM, d), F32), pltpu.SemaphoreType.DMA((2,))],
    )
    return pl.pallas_call(
        _gather_combine_kernel,
        out_shape=jax.ShapeDtypeStruct((n_rows, d), F32),
        grid_spec=grid_spec,
        compiler_params=_cparams(("arbitrary",)),
        name="moe_combine",
    )(dest, xn, mod3, wts, ys)


def _rope_tables(seq, dim, pad):
    t = jnp.arange(seq)
    rows = (t // GRID_W).astype(F32)
    cols = (t % GRID_W).astype(F32)
    quarter = dim // 4
    inv_freq = jnp.exp(-math.log(ROPE_BASE) * jnp.arange(quarter, dtype=F32) / quarter)
    ang = jnp.concatenate([inv_freq[:, None] * rows[None, :], inv_freq[:, None] * cols[None, :]], axis=0)
    cos = jnp.concatenate([jnp.cos(ang), jnp.ones((dim // 2, pad), F32)], axis=1)
    sin = jnp.concatenate([jnp.sin(ang), jnp.zeros((dim // 2, pad), F32)], axis=1)
    return cos, sin


def _win_mask_table(seq):
    nkb = seq // LANES
    nq = seq // WIN_TQ
    tabs = []
    for i in (0, 1, nq - 1):
        t = i * WIN_TQ + np.arange(WIN_TQ)[:, None]
        blk = np.clip(2 * i - 1 + np.arange(4), 0, nkb - 1)
        want = 2 * i - 1 + np.arange(4)
        s = (blk[:, None] * LANES + np.arange(LANES)[None, :]).reshape(-1)[None, :]
        ok = (np.abs(t - s) <= WIN_RADIUS) & np.repeat(blk == want, LANES)[None, :]
        tabs.append(np.where(ok, 0.0, NEG))
    return jnp.asarray(np.stack(tabs), F32)


def _nat_bias_table(rpb, seq, shift):
    rows = seq // GRID_W
    nq = rows // NAT_ROWS_PER_STEP
    wc = NAT_WIN_COLS
    col = np.arange(GRID_W)
    col_start = np.clip(col - wc // 2, 0, GRID_W - wc)
    col_ok = (col[None, :] >= col_start[:, None]) & (col[None, :] < col_start[:, None] + wc)
    d_col = np.clip(col[None, :] - col[:, None] + (wc - 1), 0, 2 * wc - 2)
    sel_r, oks = [], []
    for i in (0, 1, nq - 1):
        r0 = NAT_ROWS_PER_STEP * i
        ws = np.clip(r0 - NAT_WIN_ROWS // 2, 0, rows - NAT_KEY_ROWS)
        r = r0 + np.arange(NAT_ROWS_PER_STEP)
        rs = np.clip(r - NAT_WIN_ROWS // 2, 0, rows - NAT_WIN_ROWS)
        krow = ws + np.arange(NAT_KEY_ROWS)
        row_ok = (krow[None, :] >= rs[:, None]) & (krow[None, :] < rs[:, None] + NAT_WIN_ROWS)
        d_row = np.clip(krow[None, :] - r[:, None] + (NAT_WIN_ROWS - 1), 0, 2 * NAT_WIN_ROWS - 2)
        oks.append(row_ok[:, None, :, None] & col_ok[None, :, None, :])
        sel_r.append(d_row[:, :, None] == np.arange(2 * NAT_WIN_ROWS - 1)[None, None, :])
    sel_r = jnp.asarray(np.stack(sel_r), F32)
    sel_c = jnp.asarray(d_col[:, :, None] == np.arange(2 * wc - 1)[None, None, :], F32)
    cols = jnp.einsum('hrc,vkc->hrvk', rpb.astype(F32), sel_c, precision=lax.Precision.HIGHEST)
    bias = jnp.einsum('tuar,hrvk->thuvak', sel_r, cols, precision=lax.Precision.HIGHEST)
    bias = jnp.where(jnp.asarray(np.stack(oks))[:, None], bias - shift, NEG)
    return bias.reshape(3, rpb.shape[0], NAT_ROWS_PER_STEP * GRID_W, NAT_KEY_ROWS * GRID_W)


def _bcast_rows(v, reps=1):
    return jnp.tile(jnp.broadcast_to(v.astype(F32)[:, None], (v.shape[0], LANES)), (reps, 1))


def _layer_params(l, p):
    w = p['w_in'][l]
    sizes = (512, 128, 128, 512, 512, 512, 512, 512, 512, 256, 160, 4096)
    offs = np.concatenate([[0], np.cumsum(sizes)])
    seg = lambda k: w[:, offs[k]:offs[k + 1]]
    wq, wk, wv, dq, dk, dv, nq, nk, nv, mqa, mkva, gates = [seg(k) for k in range(12)]
    d = w.shape[0]
    dup = lambda m: jnp.concatenate([m[:, :64], m[:, :64], m[:, 64:], m[:, 64:]], axis=1)
    mkva_p = jnp.concatenate([mkva, jnp.zeros((d, 256 - mkva.shape[1]), F32)], axis=1)
    w_a = jnp.concatenate([gates, dv, nv, mqa, mkva_p, dup(wv)], axis=1).astype(BF)
    w_bt = jnp.concatenate([wq, dq, nq, dk, nk, dup(wk)], axis=1).T.astype(BF)
    scale = HEAD_DIM ** -0.5 * LOG2E
    gain_b = jnp.concatenate([
        _bcast_rows(p['win_q_norm'][l] * scale, 8), _bcast_rows(p['dif_q_norm'][l] * scale, 8),
        _bcast_rows(p['nat_q_norm'][l] * scale, 8), _bcast_rows(p['dif_k_norm'][l], 8),
        _bcast_rows(p['nat_k_norm'][l], 8), _bcast_rows(p['win_k_norm'][l], 4)], axis=0)
    wkv = p['mla_wkv_b'][l].reshape(MLA_KV_LORA, MLA_HEADS, MLA_NOPE + MLA_V)
    wk_t = wkv[:, :, :MLA_NOPE].reshape(MLA_KV_LORA, -1).T.astype(BF)
    wv_m = wkv[:, :, MLA_NOPE:].reshape(MLA_KV_LORA, -1).astype(BF)
    def logit_bound(gq, gk, dim):
        return dim * jnp.max(jnp.abs(gq)) * jnp.max(jnp.abs(gk)) * (1.0 + 2.0 ** -7)

    lam_f = p['dif_lambda'][l].astype(F32)
    lam_init = 0.8 - 0.6 * math.exp(-0.3 * l)
    lam = jnp.exp(jnp.sum(lam_f[0] * lam_f[1])) - jnp.exp(jnp.sum(lam_f[2] * lam_f[3])) + lam_init
    return dict(
        w_a=w_a, w_bt=w_bt, gain_b=gain_b,
        g1=p['norm1_g'][l].reshape(1, d), g2=p['norm2_g'][l].reshape(1, d),
        sink=(p['win_sink'][l].astype(F32) * LOG2E).reshape(1, WIN_HEADS),
        gqa=p['mla_q_a_norm'][l].reshape(1, -1), gkva=p['mla_kv_a_norm'][l].reshape(1, -1),
        wq_t=p['mla_wq_b'][l].T.astype(BF), wk_t=wk_t, wv_m=wv_m,
        gq=_bcast_rows(p['mla_q_norm'][l] * (MLA_QK ** -0.5 * LOG2E)), gk=_bcast_rows(p['mla_k_norm'][l]),
        lam=lam.reshape(1, 1).astype(F32), lam_scale=1.0 - lam_init,
        m_win=logit_bound(p['win_q_norm'][l] * scale, p['win_k_norm'][l], HEAD_DIM),
        m_dif=logit_bound(p['dif_q_norm'][l] * scale, p['dif_k_norm'][l], HEAD_DIM),
        m_nat=logit_bound(p['nat_q_norm'][l] * scale, p['nat_k_norm'][l], HEAD_DIM),
        m_mla=logit_bound(p['mla_q_norm'][l] * (MLA_QK ** -0.5 * LOG2E), p['mla_k_norm'][l], MLA_QK),
        subln=p['dif_subln'][l].astype(F32).reshape(1, DIF_V_DIM),
        wb=p['w_branch'][l].astype(BF), wo=p['w_out'][l].astype(BF),
        w1=p['moe_w1'][l].astype(BF), w3=p['moe_w3'][l].astype(BF), w2=p['moe_w2'][l].astype(BF),
    )


def kernel(x, c, ctx, c_ctx, ada_w, ada_b, norm1_g, norm2_g, w_in, win_q_norm, win_k_norm, win_sink,
           dif_q_norm, dif_k_norm, dif_lambda, dif_subln, nat_q_norm, nat_k_norm, nat_rpb,
           mla_q_a_norm, mla_wq_b, mla_kv_a_norm, mla_wkv_b, mla_q_norm, mla_k_norm,
           w_branch, w_out, router_w, router_b, moe_w1, moe_w3, moe_w2):
    p = dict(norm1_g=norm1_g, norm2_g=norm2_g, w_in=w_in, win_q_norm=win_q_norm, win_k_norm=win_k_norm,
             win_sink=win_sink, dif_q_norm=dif_q_norm, dif_k_norm=dif_k_norm, dif_lambda=dif_lambda,
             dif_subln=dif_subln, nat_q_norm=nat_q_norm, nat_k_norm=nat_k_norm,
             mla_q_a_norm=mla_q_a_norm, mla_wq_b=mla_wq_b, mla_kv_a_norm=mla_kv_a_norm,
             mla_wkv_b=mla_wkv_b, mla_q_norm=mla_q_norm, mla_k_norm=mla_k_norm,
             w_branch=w_branch, w_out=w_out, moe_w1=moe_w1, moe_w3=moe_w3, moe_w2=moe_w2)
    n_batch, seq, d = x.shape
    n_ctx = ctx.shape[1]
    depth = ada_w.shape[0]
    n_lat = n_batch * seq
    n_all = n_lat + n_batch * n_ctx
    assert seq % DENSE_TK == 0 and seq % DENSE_TQ == 0 and (n_batch * n_ctx) == TM and seq % TM == 0
    tiles_per_batch = seq // TM
    group_of_tile = lambda i: jnp.minimum(i // tiles_per_batch, n_batch)
    pos_of_tile = lambda i: jnp.where(i < n_batch * tiles_per_batch, i % tiles_per_batch, tiles_per_batch)

    cc = jnp.concatenate([c, c_ctx[None, :], jnp.zeros((8 - n_batch - 1, d), F32)], axis=0)
    mod = _modulation(cc, ada_w, ada_b)
    mod = mod[:, :n_batch + 1].reshape(depth, n_batch + 1, 6, d)
    mod = jnp.pad(mod, ((0, 0), (0, 0), (0, 2), (0, 0)))

    cos_h, sin_h = _rope_tables(seq, HEAD_DIM, TM)
    cos_m, sin_m = _rope_tables(seq, MLA_ROPE, TM)
    win_mask = _win_mask_table(seq)
    rw = jnp.pad(router_w.astype(F32), ((0, 0), (0, LANES - N_EXPERTS)))
    rw_hi = rw.astype(BF)
    rw = jnp.stack([rw_hi, (rw - rw_hi.astype(F32)).astype(BF)])

    xa = jnp.concatenate([x.reshape(n_lat, d), ctx.reshape(n_batch * n_ctx, d)], axis=0)
    ones = jnp.ones((DENSE_TK, LANES), BF)

    def dense(lam, logit_bound, q_arr, q_col, q_w, k_arr, k_row, k_w, v_arr, v_col, v_w, subln, **kw):
        operands = (q_arr, q_col, q_w, k_arr, k_row, k_w, v_arr, v_col, v_w, subln)
        par = jnp.concatenate([lam, logit_bound.reshape(1, 1).astype(F32)], axis=1)
        y, flag = _dense_fixed_attention(par, *operands, ones, n_batch, seq, n_ctx, **kw)
        return lax.cond(jnp.max(flag) > 0.0,
                        lambda: _dense_attention(lam, *operands, n_batch, seq, n_ctx, **kw),
                        lambda: y)
    for l in range(depth):
        lp = _layer_params(l, p)
        want_ctx = l < depth - 1
        mod3 = mod[l]
        tok_a = _in_proj_a(xa, mod3, lp['g1'], lp['w_a'], group_of_tile)
        q_tok, kt = _in_proj_b(xa, mod3, lp['g1'], lp['w_bt'], lp['gain_b'], cos_h, sin_h,
                               group_of_tile, pos_of_tile)
        mq, mkt, mv = _mla_prep(tok_a, lp['gqa'], lp['gkva'], lp['wq_t'], lp['wk_t'], lp['wv_m'],
                                lp['gq'], lp['gk'], cos_m, sin_m, pos_of_tile)
        rpb2 = nat_rpb[l].astype(F32) * LOG2E
        m_nl = lp['m_nat'] + jnp.maximum(jnp.max(rpb2), 0.0)
        nat_bias_rel = _nat_bias_table(rpb2, seq, m_nl)

        def branches(latent):
            win_args = (q_tok, kt, tok_a)
            if latent:
                m_win = lp['m_win']
                sink_rel = jnp.concatenate([lp['sink'] - m_win, jnp.full((1, 8), m_win, F32)], axis=1)
                y_win, flag = _win_attention(sink_rel, *win_args, win_mask - m_win, n_batch, seq, n_ctx, True,
                                             ones=ones)
                y_win = lax.cond(jnp.max(flag) > 0.0,
                                 lambda: _win_attention(lp['sink'], *win_args, win_mask, n_batch, seq, n_ctx, True),
                                 lambda: y_win)
            else:
                y_win = _win_attention(lp['sink'], *win_args, win_mask, n_batch, seq, n_ctx, False)
            y_dif = dense(lp['lam'], lp['m_dif'], q_tok, 1, 512, kt, KT_DK // 512, 512, tok_a, A_DV // 512, 512,
                          lp['subln'], n_heads=2 * DIF_HEADS, packed=True, diff=True, latent=latent,
                          lam_scale=lp['lam_scale'], name="dif_attn" if latent else "dif_attn_ctx")
            if latent:
                y_nat, flag = _nat_attention(q_tok, kt, tok_a, nat_bias_rel, n_batch, seq, n_ctx,
                                             par=m_nl.reshape(1, 1).astype(F32), ones=ones)
                y_nat = lax.cond(jnp.max(flag) > 0.0,
                                 lambda: _nat_attention(q_tok, kt, tok_a, _nat_bias_table(rpb2, seq, 0.0),
                                                        n_batch, seq, n_ctx),
                                 lambda: y_nat)
            else:
                y_nat = dense(lp['lam'], lp['m_nat'], q_tok, 2, 512, kt, KT_NK // 512, 512, tok_a, A_NV // 512, 512,
                              None, n_heads=NAT_HEADS, packed=True, diff=False, latent=False,
                              lam_scale=1.0, name="nat_attn_ctx")
            y_mla = dense(lp['lam'], lp['m_mla'], mq, 0, 1024, mkt, 0, 1024, mv, 0, 512, None,
                          n_heads=MLA_HEADS, packed=False, diff=False, latent=latent, lam_scale=1.0,
                          name="mla_attn" if latent else "mla_attn_ctx")
            return [y_win, y_dif, y_nat, y_mla]

        ys = branches(True)
        ys_c = branches(False) if want_ctx else None
        n_rows = n_all if want_ctx else n_lat
        xn, h2, scores = _merge(xa, mod3, lp['g2'], ys, ys_c, tok_a, lp['wb'], lp['wo'], rw, n_rows,
                                group_of_tile)
        blk_e, dest, wts, pad_lo, pad_hi, n_used, cap = _route(scores[:, :N_EXPERTS], router_b, n_rows)
        xs = _dispatch(dest, pad_lo, pad_hi, h2, cap)
        ys = _grouped_ffn(blk_e, n_used, xs, lp['w1'], lp['w3'], lp['w2'])
        xa = _gather_combine(dest, xn, mod3, wts, ys, n_rows, group_of_tile)
    return xa[:n_lat].reshape(n_batch, seq, d)
```

```python
import functools
import math

import jax
import jax.numpy as jnp
import numpy as np
from jax import lax
from jax.experimental import pallas as pl
from jax.experimental.pallas import tpu as pltpu

F32 = jnp.float32
BF = jnp.bfloat16

GRID_W = 64
HEAD_DIM = 64
N_BRANCH = 4
BRANCH_W = 512
ROPE_BASE = 10000.0
EPS = 1e-6
NEG = -1e30
LOG2E = math.log2(math.e)
WIN_HEADS, WIN_KV_HEADS, WIN_RADIUS = 8, 2, 128
DIF_HEADS, DIF_QK_DIM, DIF_V_DIM = 4, 64, 128
NAT_HEADS, NAT_WIN_ROWS, NAT_WIN_COLS = 8, 8, 16
MLA_HEADS, MLA_NOPE, MLA_ROPE, MLA_V, MLA_Q_LORA, MLA_KV_LORA = 8, 64, 32, 64, 256, 128
MLA_QK = MLA_NOPE + MLA_ROPE
N_EXPERTS, N_GROUPS, TOP_K, D_EXPERT, MOE_BLOCK = 16, 4, 2, 512, 256

LANES = 128
TM = 512
WIN_TQ = 256
NAT_ROWS_PER_STEP = 4
NAT_KEY_ROWS = 12
DENSE_TQ = 1024
DENSE_TQ_ONLINE = 512
DENSE_TK = 2048
DENSE_TK_ONLINE = 1024
DENSE_RB = 64
DENSE_KC = 256
VMEM_LIMIT = 48 * 1024 * 1024

A_GATES, A_DV, A_NV, A_MQA, A_MKVA, A_WV = 0, 4096, 4608, 5120, 5376, 5632
A_COLS = 5888
Q_COLS = 1536
KT_DK, KT_NK, KT_WK = 0, 512, 1024
KT_ROWS = 1280
B_ROWS = Q_COLS + KT_ROWS


def _cparams(sem, vmem=VMEM_LIMIT):
    return pltpu.CompilerParams(dimension_semantics=sem, vmem_limit_bytes=vmem)


def _nt_dot(a, b):
    return lax.dot_general(a, b, (((1,), (1,)), ((), ())), preferred_element_type=F32)


def _norm_mod(x, g, sc, sh):
    ms = jnp.mean(x * x, axis=-1, keepdims=True)
    return (x * lax.rsqrt(ms + EPS) * g) * (1.0 + sc) + sh


def _lane_tile(a, n):
    reps = n // a.shape[1]
    return a if reps == 1 else jnp.concatenate([a] * reps, axis=1)


def _mod_kernel(c_ref, w_ref, b_ref, o_ref):
    cc = c_ref[...]
    a = cc * jax.nn.sigmoid(cc)
    o_ref[0] = jnp.dot(a, w_ref[0], preferred_element_type=F32,
                       precision=lax.Precision.HIGHEST) + b_ref[0]


def _modulation(cc, ada_w, ada_b):
    n_layers, d, d6 = ada_w.shape
    tn = 1536
    return pl.pallas_call(
        _mod_kernel,
        out_shape=jax.ShapeDtypeStruct((n_layers, 8, d6), F32),
        grid=(n_layers, d6 // tn),
        in_specs=[pl.BlockSpec((8, d), lambda l, j: (0, 0)),
                  pl.BlockSpec((1, d, tn), lambda l, j: (l, 0, j)),
                  pl.BlockSpec((1, 1, tn), lambda l, j: (l, 0, j))],
        out_specs=pl.BlockSpec((1, 8, tn), lambda l, j: (l, 0, j)),
        compiler_params=_cparams(("parallel", "parallel")),
        name="adaln_mod",
    )(cc, ada_w, ada_b.reshape(n_layers, 1, d6))


def _in_a_kernel(x_ref, mod_ref, g_ref, w_ref, o_ref):
    m = mod_ref[0]
    h = _norm_mod(x_ref[...], g_ref[...], m[1:2], m[0:1]).astype(BF)
    o_ref[...] = jnp.dot(h, w_ref[...], preferred_element_type=F32).astype(BF)


def _in_proj_a(xa, mod3, g1, w_a, group_of_tile):
    n, d = xa.shape
    tn = A_COLS // 2
    return pl.pallas_call(
        _in_a_kernel,
        out_shape=jax.ShapeDtypeStruct((n, A_COLS), BF),
        grid=(A_COLS // tn, n // TM),
        in_specs=[pl.BlockSpec((TM, d), lambda j, i: (i, 0)),
                  pl.BlockSpec((1, 8, d), lambda j, i: (group_of_tile(i), 0, 0)),
                  pl.BlockSpec((1, d), lambda j, i: (0, 0)),
                  pl.BlockSpec((d, tn), lambda j, i: (0, j))],
        out_specs=pl.BlockSpec((TM, tn), lambda j, i: (i, j)),
        compiler_params=_cparams(("parallel", "parallel")),
        name="in_proj_tok",
    )(xa, mod3, g1, w_a)


def _head_norm_rope(x, g, cos, sin, rope):
    ss = jnp.sum(x * x, axis=0, keepdims=True)
    y = x * lax.rsqrt(ss * (1.0 / HEAD_DIM) + EPS) * g
    if not rope:
        return y
    half = HEAD_DIM // 2
    y1, y2 = y[:half], y[half:]
    return jnp.concatenate([y1 * cos - y2 * sin, y1 * sin + y2 * cos], axis=0)


def _in_b_kernel(x_ref, mod_ref, g_ref, wt_ref, gain_ref, cos_ref, sin_ref, q_ref, kt_ref, acc_sc):
    m = mod_ref[0]
    h = _norm_mod(x_ref[...], g_ref[...], m[1:2], m[0:1]).astype(BF)
    acc_sc[...] = _nt_dot(wt_ref[...], h)
    tm = h.shape[0]
    cos = cos_ref[...]
    sin = sin_ref[...]

    def pair(r0, rope):
        hs = []
        for e in range(2):
            r = r0 + e * HEAD_DIM
            g = _lane_tile(gain_ref[r:r + HEAD_DIM, :], tm)
            hs.append(_head_norm_rope(acc_sc[r:r + HEAD_DIM, :], g, cos, sin, rope))
        return jnp.concatenate(hs, axis=0)

    for p in range(Q_COLS // LANES):
        y = pair(p * LANES, rope=p < 8)
        q_ref[:, p * LANES:(p + 1) * LANES] = y.T.astype(BF)
    for p in range(KT_ROWS // LANES):
        y = pair(Q_COLS + p * LANES, rope=not (4 <= p < 8))
        kt_ref[p * LANES:(p + 1) * LANES, :] = y.astype(BF)


def _in_proj_b(xa, mod3, g1, w_bt, gain_b, cos_t, sin_t, group_of_tile, pos_of_tile):
    n, d = xa.shape
    return pl.pallas_call(
        _in_b_kernel,
        out_shape=(jax.ShapeDtypeStruct((n, Q_COLS), BF),
                   jax.ShapeDtypeStruct((KT_ROWS, n), BF)),
        grid=(n // TM,),
        in_specs=[pl.BlockSpec((TM, d), lambda i: (i, 0)),
                  pl.BlockSpec((1, 8, d), lambda i: (group_of_tile(i), 0, 0)),
                  pl.BlockSpec((1, d), lambda i: (0, 0)),
                  pl.BlockSpec((B_ROWS, d), lambda i: (0, 0)),
                  pl.BlockSpec((B_ROWS, LANES), lambda i: (0, 0)),
                  pl.BlockSpec((HEAD_DIM // 2, TM), lambda i: (0, pos_of_tile(i))),
                  pl.BlockSpec((HEAD_DIM // 2, TM), lambda i: (0, pos_of_tile(i)))],
        out_specs=(pl.BlockSpec((TM, Q_COLS), lambda i: (i, 0)),
                   pl.BlockSpec((KT_ROWS, TM), lambda i: (0, i))),
        scratch_shapes=[pltpu.VMEM((B_ROWS, TM), F32)],
        compiler_params=_cparams(("parallel",)),
        name="in_proj_heads",
    )(xa, mod3, g1, w_bt, gain_b, cos_t, sin_t)


def _mla_kernel(qa_ref, kva_ref, gqa_ref, gkva_ref, wqt_ref, wkt_ref, wv_ref, gq_ref, gk_ref,
                cos_ref, sin_ref, mq_ref, mkt_ref, mv_ref):
    tm = qa_ref.shape[0]
    cos = cos_ref[...]
    sin = sin_ref[...]
    rh = MLA_ROPE // 2

    def rms_rows(x, g):
        ms = jnp.mean(x * x, axis=-1, keepdims=True)
        return x * lax.rsqrt(ms + EPS) * g

    def rope_rows(x):
        x1, x2 = x[:rh], x[rh:]
        return jnp.concatenate([x1 * cos - x2 * sin, x1 * sin + x2 * cos], axis=0)

    qa = rms_rows(qa_ref[...].astype(F32), gqa_ref[...]).astype(BF)
    qt = _nt_dot(wqt_ref[...], qa)
    kva = kva_ref[...].astype(F32)
    cn = rms_rows(kva[:, :MLA_KV_LORA], gkva_ref[...]).astype(BF)
    knt = _nt_dot(wkt_ref[...], cn)
    mv_ref[...] = jnp.dot(cn, wv_ref[...], preferred_element_type=F32).astype(BF)
    krope = kva[:, MLA_KV_LORA:].T[:MLA_ROPE]
    kr_ss = jnp.sum(krope * krope, axis=0, keepdims=True)
    gq = _lane_tile(gq_ref[...], tm)
    gk = _lane_tile(gk_ref[...], tm)
    zpad = jnp.zeros((LANES - MLA_QK, tm), F32)
    for hd in range(MLA_HEADS):
        x = qt[hd * MLA_QK:(hd + 1) * MLA_QK]
        ss = jnp.sum(x * x, axis=0, keepdims=True)
        y = x * lax.rsqrt(ss * (1.0 / MLA_QK) + EPS) * gq
        y = jnp.concatenate([y[:MLA_NOPE], rope_rows(y[MLA_NOPE:]), zpad], axis=0)
        mq_ref[:, hd * LANES:(hd + 1) * LANES] = y.T.astype(BF)
        kn = knt[hd * MLA_NOPE:(hd + 1) * MLA_NOPE]
        ss = jnp.sum(kn * kn, axis=0, keepdims=True) + kr_ss
        r = lax.rsqrt(ss * (1.0 / MLA_QK) + EPS)
        yk = jnp.concatenate([kn * r * gk[:MLA_NOPE], rope_rows(krope * r * gk[MLA_NOPE:]), zpad], axis=0)
        mkt_ref[hd * LANES:(hd + 1) * LANES, :] = yk.astype(BF)


def _mla_prep(tok_a, gqa, gkva, wq_t, wk_t, wv, gq, gk, cos_t, sin_t, pos_of_tile):
    n = tok_a.shape[0]
    hw = MLA_HEADS * LANES
    return pl.pallas_call(
        _mla_kernel,
        out_shape=(jax.ShapeDtypeStruct((n, hw), BF),
                   jax.ShapeDtypeStruct((hw, n), BF),
                   jax.ShapeDtypeStruct((n, MLA_HEADS * MLA_V), BF)),
        grid=(n // TM,),
        in_specs=[pl.BlockSpec((TM, 256), lambda i: (i, A_MQA // 256)),
                  pl.BlockSpec((TM, 256), lambda i: (i, A_MKVA // 256)),
                  pl.BlockSpec((1, MLA_Q_LORA), lambda i: (0, 0)),
                  pl.BlockSpec((1, MLA_KV_LORA), lambda i: (0, 0)),
                  pl.BlockSpec(wq_t.shape, lambda i: (0, 0)),
                  pl.BlockSpec(wk_t.shape, lambda i: (0, 0)),
                  pl.BlockSpec(wv.shape, lambda i: (0, 0)),
                  pl.BlockSpec((MLA_QK, LANES), lambda i: (0, 0)),
                  pl.BlockSpec((MLA_QK, LANES), lambda i: (0, 0)),
                  pl.BlockSpec((MLA_ROPE // 2, TM), lambda i: (0, pos_of_tile(i))),
                  pl.BlockSpec((MLA_ROPE // 2, TM), lambda i: (0, pos_of_tile(i)))],
        out_specs=(pl.BlockSpec((TM, hw), lambda i: (i, 0)),
                   pl.BlockSpec((hw, TM), lambda i: (0, i)),
                   pl.BlockSpec((TM, MLA_HEADS * MLA_V), lambda i: (i, 0))),
        compiler_params=_cparams(("parallel",)),
        name="mla_prep",
    )(tok_a, tok_a, gqa, gkva, wq_t, wk_t, wv, gq, gk, cos_t, sin_t)


def _half_mask(shape):
    return lax.broadcasted_iota(jnp.int32, shape, 1) < (LANES // 2)


def _select_half(q, e, lo_mask):
    zero = jnp.zeros_like(q)
    return jnp.where(lo_mask, q, zero) if e == 0 else jnp.where(lo_mask, zero, q)


def _local_softmax_out(parts, extra_logit):
    m = parts[0][0].max(axis=-1, keepdims=True)
    for s, _ in parts[1:]:
        m = jnp.maximum(m, s.max(axis=-1, keepdims=True))
    if extra_logit is not None:
        m = jnp.maximum(m, extra_logit)
    z = None
    o = None
    for s, v in parts:
        p = jnp.exp2(s - m)
        zs = p.sum(axis=-1, keepdims=True)
        os_ = jnp.dot(p.astype(BF), v, preferred_element_type=F32)
        z = zs if z is None else z + zs
        o = os_ if o is None else o + os_
    if extra_logit is not None:
        z = z + jnp.exp2(extra_logit - m)
    return o / z


def _fixed_softmax_out(parts, ones_ref, extra_logit):
    acc = None
    for s, v in parts:
        aug = jnp.concatenate([v, ones_ref[:s.shape[1], :]], axis=1)
        t = jnp.dot(jnp.exp2(s).astype(BF), aug, preferred_element_type=F32)
        acc = t if acc is None else acc + t
    l = acc[:, LANES:]
    if extra_logit is not None:
        l = l + jnp.exp2(jnp.zeros_like(l) + extra_logit)
    unsafe = jnp.logical_not((l > 2.0 ** -SAFE_SUM_LOG2) & (l < 2.0 ** SAFE_SUM_LOG2))
    return acc[:, :LANES] / l, jnp.max(jnp.where(unsafe, 1.0, 0.0), axis=0, keepdims=True)


def _win_kernel(sink_ref, q_ref, *refs, band, fixed=False):
    if band:
        k0, k1, k2, k3, v0, v1, v2, v3, kc_ref, vc_ref, mask_ref = refs[:11]
        refs = refs[11:]
        kb = jnp.concatenate([k0[...], k1[...], k2[...], k3[...]], axis=1)
        vb = jnp.concatenate([v0[...], v1[...], v2[...], v3[...]], axis=0)
        mask = mask_ref[0]
    else:
        kc_ref, vc_ref = refs[:2]
        refs = refs[2:]
    if fixed:
        ones_ref, o_ref, flag_ref = refs
        ref_logit = sink_ref[0, WIN_HEADS]
    else:
        (o_ref,) = refs
    q = q_ref[...]
    lo = _half_mask((q.shape[0], LANES))
    group = WIN_HEADS // WIN_KV_HEADS
    bad = []
    for j in range(WIN_HEADS // 2):
        qp = q[:, j * LANES:(j + 1) * LANES]
        g = (2 * j) // group
        kc = kc_ref[g * LANES:(g + 1) * LANES, :]
        vc = vc_ref[:, g * LANES:(g + 1) * LANES]
        outs = []
        for e in range(2):
            qm = _select_half(qp, e, lo)
            parts = []
            if band:
                s = jnp.dot(qm, kb[g * LANES:(g + 1) * LANES, :], preferred_element_type=F32) + mask
                parts.append((s, vb[:, g * LANES:(g + 1) * LANES]))
            sc = jnp.dot(qm, kc, preferred_element_type=F32)
            if fixed:
                parts.append((sc - ref_logit, vc))
                o, u = _fixed_softmax_out(parts, ones_ref, sink_ref[0, 2 * j + e])
                bad.append(u)
                outs.append(o)
            else:
                parts.append((sc, vc))
                outs.append(_local_softmax_out(parts, sink_ref[0, 2 * j + e]))
        o_ref[:, j * LANES:(j + 1) * LANES] = jnp.where(lo, outs[0], outs[1]).astype(BF)
    if fixed:
        flag_ref[0] = jnp.concatenate(bad, axis=0)


def _win_attention(sink, q_tok, kt, tok_a, mask_tbl, n_batch, seq, n_ctx, latent, ones=None):
    ctx_blk = (n_batch * seq) // n_ctx
    kc_spec = lambda f: pl.BlockSpec((2 * LANES, n_ctx), f)
    vc_spec = lambda f: pl.BlockSpec((n_ctx, 2 * LANES), f)
    smem = pl.BlockSpec(memory_space=pltpu.SMEM)
    n_out = n_batch * (seq if latent else n_ctx)
    out_shape = jax.ShapeDtypeStruct((n_out, WIN_HEADS * HEAD_DIM), BF)
    if not latent:
        return pl.pallas_call(
            functools.partial(_win_kernel, band=False),
            out_shape=out_shape,
            grid=(n_batch,),
            in_specs=[smem,
                      pl.BlockSpec((n_ctx, 512), lambda b: (ctx_blk + b, 0)),
                      kc_spec(lambda b: (KT_WK // 256, ctx_blk + b)),
                      vc_spec(lambda b: (ctx_blk + b, A_WV // 256))],
            out_specs=pl.BlockSpec((n_ctx, 512), lambda b: (b, 0)),
            compiler_params=_cparams(("parallel",)),
            name="win_attn_ctx",
        )(sink, q_tok, kt, tok_a)
    nq = seq // WIN_TQ
    nkb = seq // LANES

    def kidx(j):
        return lambda b, i: (KT_WK // 256, b * nkb + jnp.clip(2 * i - 1 + j, 0, nkb - 1))

    def vidx(j):
        return lambda b, i: (b * nkb + jnp.clip(2 * i - 1 + j, 0, nkb - 1), A_WV // 256)

    def variant(b, i):
        return (jnp.where(i == 0, 0, jnp.where(i == nq - 1, 2, 1)), 0, 0)

    in_specs = ([smem, pl.BlockSpec((WIN_TQ, 512), lambda b, i: (b * nq + i, 0))]
                + [pl.BlockSpec((2 * LANES, LANES), kidx(j)) for j in range(4)]
                + [pl.BlockSpec((LANES, 2 * LANES), vidx(j)) for j in range(4)]
                + [kc_spec(lambda b, i: (KT_WK // 256, ctx_blk + b)),
                   vc_spec(lambda b, i: (ctx_blk + b, A_WV // 256)),
                   pl.BlockSpec((1, WIN_TQ, 4 * LANES), variant)])
    args = [sink, q_tok, kt, kt, kt, kt, tok_a, tok_a, tok_a, tok_a, kt, tok_a, mask_tbl]
    out_specs = pl.BlockSpec((WIN_TQ, 512), lambda b, i: (b * nq + i, 0))
    fixed = ones is not None
    if fixed:
        in_specs.append(pl.BlockSpec(ones.shape, lambda b, i: (0, 0)))
        args.append(ones)
        out_shape = (out_shape, jax.ShapeDtypeStruct((n_batch * nq, WIN_HEADS, LANES), F32))
        out_specs = (out_specs, pl.BlockSpec((1, WIN_HEADS, LANES), lambda b, i: (b * nq + i, 0, 0)))
    return pl.pallas_call(
        functools.partial(_win_kernel, band=True, fixed=fixed),
        out_shape=out_shape,
        grid=(n_batch, nq),
        in_specs=in_specs,
        out_specs=out_specs,
        compiler_params=_cparams(("parallel", "parallel")),
        name="win_attn_fixed" if fixed else "win_attn",
    )(*args)


def _nat_kernel(q_ref, k0, k1, k2, v0, v1, v2, kc_ref, vc_ref, bias_ref, *refs, fixed=False):
    if fixed:
        par_ref, ones_ref, o_ref, flag_ref = refs
        ref_logit = par_ref[0, 0]
    else:
        (o_ref,) = refs
    q = q_ref[...]
    kb = jnp.concatenate([k0[...], k1[...], k2[...]], axis=1)
    vb = jnp.concatenate([v0[...], v1[...], v2[...]], axis=0)
    lo = _half_mask((q.shape[0], LANES))
    bad = []
    for j in range(NAT_HEADS // 2):
        sl = slice(j * LANES, (j + 1) * LANES)
        qp = q[:, sl]
        outs = []
        for e in range(2):
            qm = _select_half(qp, e, lo)
            s = jnp.dot(qm, kb[sl, :], preferred_element_type=F32) + bias_ref[0, 2 * j + e]
            sc = jnp.dot(qm, kc_ref[sl, :], preferred_element_type=F32)
            if fixed:
                o, u = _fixed_softmax_out([(s, vb[:, sl]), (sc - ref_logit, vc_ref[:, sl])], ones_ref, None)
                bad.append(u)
                outs.append(o)
            else:
                outs.append(_local_softmax_out([(s, vb[:, sl]), (sc, vc_ref[:, sl])], None))
        o_ref[:, sl] = jnp.where(lo, outs[0], outs[1]).astype(BF)
    if fixed:
        flag_ref[0] = jnp.concatenate(bad, axis=0)


def _nat_attention(q_tok, kt, tok_a, bias_tbl, n_batch, seq, n_ctx, par=None, ones=None):
    tq = NAT_ROWS_PER_STEP * GRID_W
    nq = seq // tq
    rows = seq // GRID_W
    ctx_blk = (n_batch * seq) // n_ctx
    q_col = 2
    k_row = KT_NK // 512
    v_col = A_NV // 512

    def wstart(i):
        return jnp.clip(NAT_ROWS_PER_STEP * i - NAT_WIN_ROWS // 2, 0, rows - NAT_KEY_ROWS) // NAT_ROWS_PER_STEP

    def kidx(j):
        return lambda b, i: (k_row, b * nq + wstart(i) + j)

    def vidx(j):
        return lambda b, i: (b * nq + wstart(i) + j, v_col)

    def variant(b, i):
        return (jnp.where(i == 0, 0, jnp.where(i == nq - 1, 2, 1)), 0, 0, 0)

    nk = NAT_KEY_ROWS * GRID_W
    in_specs = ([pl.BlockSpec((tq, 512), lambda b, i: (b * nq + i, q_col))]
                + [pl.BlockSpec((512, tq), kidx(j)) for j in range(3)]
                + [pl.BlockSpec((tq, 512), vidx(j)) for j in range(3)]
                + [pl.BlockSpec((512, n_ctx), lambda b, i: (k_row, ctx_blk + b)),
                   pl.BlockSpec((n_ctx, 512), lambda b, i: (ctx_blk + b, v_col)),
                   pl.BlockSpec((1, NAT_HEADS, tq, nk), variant)])
    args = [q_tok, kt, kt, kt, tok_a, tok_a, tok_a, kt, tok_a, bias_tbl]
    out_shape = jax.ShapeDtypeStruct((n_batch * seq, NAT_HEADS * HEAD_DIM), BF)
    out_specs = pl.BlockSpec((tq, 512), lambda b, i: (b * nq + i, 0))
    fixed = ones is not None
    if fixed:
        in_specs += [pl.BlockSpec(memory_space=pltpu.SMEM), pl.BlockSpec(ones.shape, lambda b, i: (0, 0))]
        args += [par, ones]
        out_shape = (out_shape, jax.ShapeDtypeStruct((n_batch * nq, NAT_HEADS, LANES), F32))
        out_specs = (out_specs, pl.BlockSpec((1, NAT_HEADS, LANES), lambda b, i: (b * nq + i, 0, 0)))
    return pl.pallas_call(
        functools.partial(_nat_kernel, fixed=fixed),
        out_shape=out_shape,
        grid=(n_batch, nq),
        in_specs=in_specs,
        out_specs=out_specs,
        compiler_params=_cparams(("parallel", "parallel")),
        name="nat_attn_fixed" if fixed else "nat_attn",
    )(*args)


def _dense_kernel(lam_ref, q_ref, kc_ref, vc_ref, *refs, n_heads, packed, diff, latent, lam_scale):
    if latent:
        k_ref, v_ref = refs[0], refs[1]
        refs = refs[2:]
    if diff:
        subln_ref, o_ref, qm_sc, m_sc, l_sc, acc_sc, s_sc, p_sc = refs
    else:
        o_ref, qm_sc, m_sc, l_sc, acc_sc, s_sc, p_sc = refs
    kt_step = pl.program_id(2) if latent else 0
    tq = q_ref.shape[0]

    def kv_slices(h):
        blk = h // 2 if packed else h
        ks = slice(blk * LANES, (blk + 1) * LANES)
        vs = ks if packed else slice((h // 2) * LANES, (h // 2 + 1) * LANES)
        return ks, vs

    def step(h, k_ref_, v_ref_):
        ks, vs = kv_slices(h)
        nk = k_ref_.shape[1]
        slot = h % 2
        s_sc[slot, :, :nk] = jnp.dot(qm_sc[h], k_ref_[ks, :], preferred_element_type=F32)
        for r in range(tq // DENSE_RB):
            rows = slice(r * DENSE_RB, (r + 1) * DENSE_RB)
            mx = s_sc[slot, rows, 0:LANES]
            for c in range(1, nk // LANES):
                mx = jnp.maximum(mx, s_sc[slot, rows, c * LANES:(c + 1) * LANES])
            m_old = m_sc[h, rows, :]
            m_new = jnp.maximum(m_old, jnp.max(mx, axis=-1, keepdims=True))
            alpha = jnp.exp2(m_old - m_new)
            lsum = None
            for c in range(nk // LANES):
                cols = slice(c * LANES, (c + 1) * LANES)
                p = jnp.exp2(s_sc[slot, rows, cols] - m_new)
                lsum = p if lsum is None else lsum + p
                p_sc[slot, rows, cols] = p.astype(BF)
            m_sc[h, rows, :] = m_new
            l_sc[h, rows, :] = alpha * l_sc[h, rows, :] + jnp.sum(lsum, axis=-1, keepdims=True)
            acc_sc[h, rows, :] = alpha * acc_sc[h, rows, :]
        acc_sc[h] += jnp.dot(p_sc[slot, :, :nk], v_ref_[:, vs], preferred_element_type=F32)

    @pl.when(kt_step == 0)
    def _():
        q = q_ref[...]
        lo = _half_mask((tq, LANES))
        for h in range(n_heads):
            if packed:
                qp = q[:, (h // 2) * LANES:(h // 2 + 1) * LANES]
                qm_sc[h] = _select_half(qp, h % 2, lo)
            else:
                qm_sc[h] = q[:, h * LANES:(h + 1) * LANES]
        m_sc[...] = jnp.full(m_sc.shape, NEG, F32)
        l_sc[...] = jnp.zeros(l_sc.shape, F32)
        acc_sc[...] = jnp.zeros(acc_sc.shape, F32)
        for h in range(n_heads):
            step(h, kc_ref, vc_ref)

    if latent:
        for h in range(n_heads):
            step(h, k_ref, v_ref)
        last = kt_step == pl.num_programs(2) - 1
    else:
        last = True

    def finish():
        lo = _half_mask((tq, LANES))
        if diff:
            lam = lam_ref[0, 0]
            for hv in range(n_heads // 2):
                y = (acc_sc[2 * hv] / l_sc[2 * hv]
                     - lam * (acc_sc[2 * hv + 1] / l_sc[2 * hv + 1]))
                ms = jnp.mean(y * y, axis=-1, keepdims=True)
                y = y * lax.rsqrt(ms + EPS) * subln_ref[...] * lam_scale
                o_ref[:, hv * LANES:(hv + 1) * LANES] = y.astype(BF)
        else:
            for hp in range(n_heads // 2):
                o0 = acc_sc[2 * hp] / l_sc[2 * hp]
                o1 = acc_sc[2 * hp + 1] / l_sc[2 * hp + 1]
                o_ref[:, hp * LANES:(hp + 1) * LANES] = jnp.where(lo, o0, o1).astype(BF)

    if latent:
        pl.when(last)(finish)
    else:
        finish()


def _dense_attention(lam, q_arr, q_col, q_w, k_arr, k_row, k_w, v_arr, v_col, v_w, subln,
                     n_batch, seq, n_ctx, *, n_heads, packed, diff, latent, lam_scale, name):
    n = n_batch * (seq if latent else n_ctx)
    ctx_blk = (n_batch * seq) // n_ctx
    out_w = v_w
    smem = pl.BlockSpec(memory_space=pltpu.SMEM)
    kern = functools.partial(_dense_kernel, n_heads=n_heads, packed=packed, diff=diff,
                             latent=latent, lam_scale=lam_scale)
    tq = DENSE_TQ_ONLINE if latent else n_ctx
    tk = DENSE_TK_ONLINE
    max_nk = tk if latent else n_ctx
    scratch = [pltpu.VMEM((n_heads, tq, LANES), BF),
               pltpu.VMEM((n_heads, tq, LANES), F32),
               pltpu.VMEM((n_heads, tq, LANES), F32),
               pltpu.VMEM((n_heads, tq, LANES), F32),
               pltpu.VMEM((2, tq, max_nk), F32),
               pltpu.VMEM((2, tq, max_nk), BF)]
    out_shape = jax.ShapeDtypeStruct((n, out_w), BF)
    if latent:
        nq = seq // tq
        nk = seq // tk
        grid = (n_batch, nq, nk)
        in_specs = [smem,
                    pl.BlockSpec((tq, q_w), lambda b, i, k: (b * nq + i, q_col)),
                    pl.BlockSpec((k_w, n_ctx), lambda b, i, k: (k_row, ctx_blk + b)),
                    pl.BlockSpec((n_ctx, v_w), lambda b, i, k: (ctx_blk + b, v_col)),
                    pl.BlockSpec((k_w, tk), lambda b, i, k: (k_row, b * nk + k)),
                    pl.BlockSpec((tk, v_w), lambda b, i, k: (b * nk + k, v_col))]
        args = [lam, q_arr, k_arr, v_arr, k_arr, v_arr]
        if diff:
            in_specs.append(pl.BlockSpec((1, LANES), lambda b, i, k: (0, 0)))
            args.append(subln)
        out_specs = pl.BlockSpec((tq, out_w), lambda b, i, k: (b * nq + i, 0))
        sem = ("parallel", "parallel", "arbitrary")
    else:
        grid = (n_batch,)
        in_specs = [smem,
                    pl.BlockSpec((tq, q_w), lambda b: (ctx_blk + b, q_col)),
                    pl.BlockSpec((k_w, n_ctx), lambda b: (k_row, ctx_blk + b)),
                    pl.BlockSpec((n_ctx, v_w), lambda b: (ctx_blk + b, v_col))]
        args = [lam, q_arr, k_arr, v_arr]
        if diff:
            in_specs.append(pl.BlockSpec((1, LANES), lambda b: (0, 0)))
            args.append(subln)
        out_specs = pl.BlockSpec((tq, out_w), lambda b: (b, 0))
        sem = ("parallel",)
    return pl.pallas_call(
        kern, out_shape=out_shape, grid=grid, in_specs=in_specs, out_specs=out_specs,
        scratch_shapes=scratch, compiler_params=_cparams(sem), name=name,
    )(*args)


SAFE_SUM_LOG2 = 100.0


def _dense_fixed_kernel(par_ref, q_ref, kc_ref, vc_ref, *refs, n_heads, packed, diff, latent, lam_scale):
    if latent:
        k_ref, v_ref = refs[0], refs[1]
        refs = refs[2:]
    ones_ref = refs[0]
    refs = refs[1:]
    if diff:
        subln_ref, o_ref, flag_ref, qm_sc, acc_sc, p_sc = refs
    else:
        o_ref, flag_ref, qm_sc, acc_sc, p_sc = refs
    kt_step = pl.program_id(2) if latent else 0
    tq = q_ref.shape[0]
    ref_logit = par_ref[0, 1]

    def step(h, k_ref_, v_ref_, first):
        blk = h // 2 if packed else h
        ks = slice(blk * LANES, (blk + 1) * LANES)
        vs = ks if packed else slice((h // 2) * LANES, (h // 2 + 1) * LANES)
        nk = k_ref_.shape[1]
        slot = h % 2
        for c in range(nk // DENSE_KC):
            cols = slice(c * DENSE_KC, (c + 1) * DENSE_KC)
            s = jnp.dot(qm_sc[h], k_ref_[ks, cols], preferred_element_type=F32)
            p_sc[slot, :, cols] = jnp.exp2(s - ref_logit).astype(BF)
        v_aug = jnp.concatenate([v_ref_[:, vs], ones_ref[:nk, :]], axis=1)
        pv = jnp.dot(p_sc[slot, :, :nk], v_aug, preferred_element_type=F32)
        if first:
            acc_sc[h] = pv
        else:
            acc_sc[h] += pv

    @pl.when(kt_step == 0)
    def _():
        q = q_ref[...]
        lo = _half_mask((tq, LANES))
        for h in range(n_heads):
            if packed:
                qp = q[:, (h // 2) * LANES:(h // 2 + 1) * LANES]
                qm_sc[h] = _select_half(qp, h % 2, lo)
            else:
                qm_sc[h] = q[:, h * LANES:(h + 1) * LANES]
        for h in range(n_heads):
            step(h, kc_ref, vc_ref, True)

    if latent:
        for h in range(n_heads):
            step(h, k_ref, v_ref, False)
        last = kt_step == pl.num_programs(2) - 1

    def finish():
        lo = _half_mask((tq, LANES))
        outs, bad = [], []
        for h in range(n_heads):
            a = acc_sc[h]
            l = a[:, LANES:]
            unsafe = jnp.logical_not((l > 2.0 ** -SAFE_SUM_LOG2) & (l < 2.0 ** SAFE_SUM_LOG2))
            bad.append(jnp.max(jnp.where(unsafe, 1.0, 0.0), axis=0, keepdims=True))
            outs.append(a[:, :LANES] / l)
        flag_ref[0] = jnp.concatenate(bad, axis=0)
        if diff:
            lam = par_ref[0, 0]
            for hv in range(n_heads // 2):
                y = outs[2 * hv] - lam * outs[2 * hv + 1]
                ms = jnp.mean(y * y, axis=-1, keepdims=True)
                y = y * lax.rsqrt(ms + EPS) * subln_ref[...] * lam_scale
                o_ref[:, hv * LANES:(hv + 1) * LANES] = y.astype(BF)
        else:
            for hp in range(n_heads // 2):
                o_ref[:, hp * LANES:(hp + 1) * LANES] = jnp.where(lo, outs[2 * hp], outs[2 * hp + 1]).astype(BF)

    if latent:
        pl.when(last)(finish)
    else:
        finish()


def _dense_fixed_attention(par, q_arr, q_col, q_w, k_arr, k_row, k_w, v_arr, v_col, v_w, subln, ones,
                           n_batch, seq, n_ctx, *, n_heads, packed, diff, latent, lam_scale, name):
    n = n_batch * (seq if latent else n_ctx)
    ctx_blk = (n_batch * seq) // n_ctx
    smem = pl.BlockSpec(memory_space=pltpu.SMEM)
    kern = functools.partial(_dense_fixed_kernel, n_heads=n_heads, packed=packed, diff=diff,
                             latent=latent, lam_scale=lam_scale)
    tq = DENSE_TQ if latent else n_ctx
    max_nk = DENSE_TK if latent else n_ctx
    scratch = [pltpu.VMEM((n_heads, tq, LANES), BF),
               pltpu.VMEM((n_heads, tq, 2 * LANES), F32),
               pltpu.VMEM((2, tq, max_nk), BF)]
    if latent:
        nq = seq // tq
        nk = seq // DENSE_TK
        grid = (n_batch, nq, nk)
        ix = lambda f: (lambda b, i, k: f(b, i, k))
        q_ix = ix(lambda b, i, k: (b * nq + i, q_col))
        in_specs = [smem,
                    pl.BlockSpec((tq, q_w), q_ix),
                    pl.BlockSpec((k_w, n_ctx), ix(lambda b, i, k: (k_row, ctx_blk + b))),
                    pl.BlockSpec((n_ctx, v_w), ix(lambda b, i, k: (ctx_blk + b, v_col))),
                    pl.BlockSpec((k_w, DENSE_TK), ix(lambda b, i, k: (k_row, b * nk + k))),
                    pl.BlockSpec((DENSE_TK, v_w), ix(lambda b, i, k: (b * nk + k, v_col))),
                    pl.BlockSpec((DENSE_TK, LANES), ix(lambda b, i, k: (0, 0)))]
        args = [par, q_arr, k_arr, v_arr, k_arr, v_arr, ones]
        const_ix = ix(lambda b, i, k: (0, 0))
        out_specs = (pl.BlockSpec((tq, v_w), ix(lambda b, i, k: (b * nq + i, 0))),
                     pl.BlockSpec((1, n_heads, LANES), ix(lambda b, i, k: (b * nq + i, 0, 0))))
        n_flag = n_batch * nq
        sem = ("parallel", "parallel", "arbitrary")
    else:
        grid = (n_batch,)
        in_specs = [smem,
                    pl.BlockSpec((tq, q_w), lambda b: (ctx_blk + b, q_col)),
                    pl.BlockSpec((k_w, n_ctx), lambda b: (k_row, ctx_blk + b)),
                    pl.BlockSpec((n_ctx, v_w), lambda b: (ctx_blk + b, v_col)),
                    pl.BlockSpec((DENSE_TK, LANES), lambda b: (0, 0))]
        args = [par, q_arr, k_arr, v_arr, ones]
        const_ix = lambda b: (0, 0)
        out_specs = (pl.BlockSpec((tq, v_w), lambda b: (b, 0)),
                     pl.BlockSpec((1, n_heads, LANES), lambda b: (b, 0, 0)))
        n_flag = n_batch
        sem = ("parallel",)
    if diff:
        in_specs.append(pl.BlockSpec((1, LANES), const_ix))
        args.append(subln)
    return pl.pallas_call(
        kern,
        out_shape=(jax.ShapeDtypeStruct((n, v_w), BF), jax.ShapeDtypeStruct((n_flag, n_heads, LANES), F32)),
        grid=grid, in_specs=in_specs, out_specs=out_specs,
        scratch_shapes=scratch, compiler_params=_cparams(sem), name=name + "_fixed",
    )(*args)


def _merge_kernel(x_ref, mod_ref, g2_ref, *refs, n_lat_tiles, has_ctx):
    ys = refs[:N_BRANCH]
    refs = refs[N_BRANCH:]
    if has_ctx:
        ycs = refs[:N_BRANCH]
        refs = refs[N_BRANCH:]
        is_ctx = pl.program_id(0) >= n_lat_tiles
    gts = refs[:N_BRANCH]
    wb_ref, wo_ref, rw_ref, xo_ref, h2_ref, sc_ref = refs[N_BRANCH:]
    m = mod_ref[0]
    mix = None
    for n_ in range(N_BRANCH):
        y = ys[n_][...]
        if has_ctx:
            y = jnp.where(is_ctx, ycs[n_][...], y)
        yb = jnp.dot(y, wb_ref[n_], preferred_element_type=F32)
        t = jax.nn.sigmoid(gts[n_][...].astype(F32)) * yb
        mix = t if mix is None else mix + t
    att = jnp.dot(mix.astype(BF), wo_ref[...], preferred_element_type=F32)
    xn = x_ref[...] + m[2:3] * att
    xo_ref[...] = xn
    h2 = _norm_mod(xn, g2_ref[...], m[4:5], m[3:4])
    h2_ref[...] = h2
    h_hi = h2.astype(BF)
    h_lo = (h2 - h_hi.astype(F32)).astype(BF)
    logits = (jnp.dot(h_hi, rw_ref[0], preferred_element_type=F32)
              + (jnp.dot(h_lo, rw_ref[0], preferred_element_type=F32)
                 + jnp.dot(h_hi, rw_ref[1], preferred_element_type=F32)))
    sc_ref[...] = jax.nn.sigmoid(logits)


def _merge(xa, mod3, g2, ys, ys_ctx, tok_a, wb, wo, rw, n_rows, group_of_tile):
    d = xa.shape[1]
    n_lat_tiles = ys[0].shape[0] // TM
    has_ctx = ys_ctx is not None
    row = lambda w, c: pl.BlockSpec((TM, w), lambda i, c=c: (i, c))
    lat_row = pl.BlockSpec((TM, BRANCH_W), lambda i: (jnp.minimum(i, n_lat_tiles - 1), 0))
    in_specs = ([row(d, 0),
                 pl.BlockSpec((1, 8, d), lambda i: (group_of_tile(i), 0, 0)),
                 pl.BlockSpec((1, d), lambda i: (0, 0))]
                + [lat_row for _ in range(N_BRANCH)]
                + ([pl.BlockSpec((TM, BRANCH_W), lambda i: (0, 0)) for _ in range(N_BRANCH)] if has_ctx else [])
                + [row(d, c) for c in range(N_BRANCH)]
                + [pl.BlockSpec(wb.shape, lambda i: (0, 0, 0)),
                   pl.BlockSpec(wo.shape, lambda i: (0, 0)),
                   pl.BlockSpec(rw.shape, lambda i: (0, 0, 0))])
    ys = list(ys) + (list(ys_ctx) if has_ctx else [])
    return pl.pallas_call(
        functools.partial(_merge_kernel, n_lat_tiles=n_lat_tiles, has_ctx=has_ctx),
        out_shape=(jax.ShapeDtypeStruct((n_rows, d), F32),
                   jax.ShapeDtypeStruct((n_rows, d), F32),
                   jax.ShapeDtypeStruct((n_rows, LANES), F32)),
        grid=(n_rows // TM,),
        in_specs=in_specs,
        out_specs=(row(d, 0), row(d, 0), row(LANES, 0)),
        compiler_params=_cparams(("parallel",)),
        name="merge",
    )(xa, mod3, g2, *ys, tok_a, tok_a, tok_a, tok_a, wb, wo, rw)


def _route(scores, router_b, n):
    per_group = N_EXPERTS // N_GROUPS
    st = scores.T
    biased = st + router_b.astype(F32)[:, None]

    def top2(v, axis):
        pos = lax.broadcasted_iota(jnp.int32, v.shape, axis)
        i0 = jnp.argmax(v, axis=axis).astype(jnp.int32)
        v0 = jnp.max(v, axis=axis)
        rest = jnp.where(pos == jnp.expand_dims(i0, axis), -jnp.inf, v)
        i1 = jnp.argmax(rest, axis=axis).astype(jnp.int32)
        v1 = jnp.max(rest, axis=axis)
        return (v0, v1), (i0, i1)

    (g0, g1), _ = top2(biased.reshape(N_GROUPS, per_group, n), 1)
    group = jnp.argmax(g0 + g1, axis=0)
    expert = jnp.arange(N_EXPERTS, dtype=jnp.int32)[:, None]
    in_group = (expert // per_group) == group[None, :]
    _, (e0, e1) = top2(jnp.where(in_group, biased, -jnp.inf), 0)
    oh0 = (expert == e0[None, :]).astype(F32)
    oh1 = (expert == e1[None, :]).astype(F32)
    w0 = (oh0 * st).sum(0)
    w1 = (oh1 * st).sum(0)
    wts = jnp.stack([w0, w1], axis=-1) / (w0 + w1)[:, None]
    n_asg = n * TOP_K
    cnt = (oh0 + oh1).reshape(N_EXPERTS, n // MOE_BLOCK, MOE_BLOCK)
    tri = jnp.triu(jnp.ones((MOE_BLOCK, MOE_BLOCK), F32), 1)
    within = jnp.einsum('ebj,ji->ebi', cnt, tri)
    blk_tot = cnt.sum(axis=-1)
    nb = blk_tot.shape[1]
    blk_off = jnp.einsum('eb,bc->ec', blk_tot, jnp.triu(jnp.ones((nb, nb), F32), 1),
                         precision=lax.Precision.HIGHEST)
    prefix = (within + blk_off[:, :, None]).reshape(N_EXPERTS, n)
    counts = blk_tot.sum(axis=-1).astype(jnp.int32)
    padded = (counts + MOE_BLOCK - 1) // MOE_BLOCK * MOE_BLOCK
    pad_end = jnp.cumsum(padded)
    pad_start = pad_end - padded
    slot = prefix + pad_start.astype(F32)[:, None]
    dest = jnp.stack([(oh0 * slot).sum(0), (oh1 * slot).sum(0)], axis=-1).reshape(n_asg)
    n_blk = (n_asg + N_EXPERTS * (MOE_BLOCK - 1) + MOE_BLOCK - 1) // MOE_BLOCK
    cap = n_blk * MOE_BLOCK
    blk_start = jnp.arange(n_blk, dtype=jnp.int32) * MOE_BLOCK
    blk_e = jnp.minimum((blk_start[:, None] >= pad_end[None, :]).sum(-1), N_EXPERTS - 1).astype(jnp.int32)
    pad_lo = jnp.concatenate([pad_start + counts, pad_end[-1:]]).astype(jnp.int32)
    pad_hi = jnp.concatenate([pad_end, jnp.full((1,), cap)]).astype(jnp.int32)
    n_used = (pad_end[-1:] // MOE_BLOCK).astype(jnp.int32)
    return blk_e, dest.astype(jnp.int32), wts.astype(F32), pad_lo, pad_hi, n_used, cap


def _dispatch_kernel(dest_ref, pad_lo_ref, pad_hi_ref, h2_ref, xs_hbm, zrow, sem, zsem):
    i = pl.program_id(0)

    @pl.when(i == 0)
    def _():
        zrow[...] = jnp.zeros(zrow.shape, F32)
        for e in range(N_EXPERTS + 1):
            lo, hi = pad_lo_ref[e], pad_hi_ref[e]

            def zero_row(r, c):
                pltpu.make_async_copy(zrow.at[pl.ds(0, 1)], xs_hbm.at[pl.ds(r, 1)], zsem).start()
                return c

            def zero_wait(r, c):
                pltpu.make_async_copy(zrow.at[pl.ds(0, 1)], xs_hbm.at[pl.ds(r, 1)], zsem).wait()
                return c

            lax.fori_loop(lo, hi, zero_row, 0)
            lax.fori_loop(lo, hi, zero_wait, 0)

    tm = h2_ref.shape[0]

    def copy_row(r, c):
        a = (i * tm + r) * TOP_K
        for k in range(TOP_K):
            pltpu.make_async_copy(h2_ref.at[pl.ds(r, 1)], xs_hbm.at[pl.ds(dest_ref[a + k], 1)],
                                  sem).start(priority=k % 2)
        return c

    lax.fori_loop(0, tm, copy_row, 0, unroll=4)
    for k in range(TOP_K):
        pltpu.make_async_copy(h2_ref, xs_hbm.at[pl.ds(0, tm)], sem).wait()


def _dispatch(dest, pad_lo, pad_hi, h2, cap):
    n, d = h2.shape
    grid_spec = pltpu.PrefetchScalarGridSpec(
        num_scalar_prefetch=3,
        grid=(n // TM,),
        in_specs=[pl.BlockSpec((TM, d), lambda i, de, lo, hi: (i, 0))],
        out_specs=pl.BlockSpec(memory_space=pl.ANY),
        scratch_shapes=[pltpu.VMEM((8, d), F32), pltpu.SemaphoreType.DMA, pltpu.SemaphoreType.DMA],
    )
    return pl.pallas_call(
        _dispatch_kernel,
        out_shape=jax.ShapeDtypeStruct((cap, d), F32),
        grid_spec=grid_spec,
        compiler_params=_cparams(("arbitrary",)),
        name="moe_dispatch",
    )(dest, pad_lo, pad_hi, h2)


def _grouped_ffn_kernel(blk_e_ref, n_used_ref, x_ref, w1_ref, w3_ref, w2_ref, o_ref):
    i = pl.program_id(0)

    @pl.when(i < n_used_ref[0])
    def _():
        xb = x_ref[...].astype(BF)
        a = jnp.dot(xb, w1_ref[0], preferred_element_type=F32)
        b = jnp.dot(xb, w3_ref[0], preferred_element_type=F32)
        hmid = (a * jax.nn.sigmoid(a) * b).astype(BF)
        o_ref[...] = jnp.dot(hmid, w2_ref[0], preferred_element_type=F32)

    @pl.when(i >= n_used_ref[0])
    def _():
        o_ref[...] = jnp.zeros(o_ref.shape, F32)


def _grouped_ffn(blk_e, n_used, xs, w1, w3, w2):
    cap, d = xs.shape
    wspec = lambda shp: pl.BlockSpec((1,) + shp, lambda i, be, nu: (be[i], 0, 0))
    grid_spec = pltpu.PrefetchScalarGridSpec(
        num_scalar_prefetch=2,
        grid=(cap // MOE_BLOCK,),
        in_specs=[pl.BlockSpec((MOE_BLOCK, d), lambda i, be, nu: (jnp.minimum(i, nu[0] - 1), 0)),
                  wspec((d, D_EXPERT)), wspec((d, D_EXPERT)), wspec((D_EXPERT, d))],
        out_specs=pl.BlockSpec((MOE_BLOCK, d), lambda i, be, nu: (i, 0)),
    )
    return pl.pallas_call(
        _grouped_ffn_kernel,
        out_shape=jax.ShapeDtypeStruct((cap, d), F32),
        grid_spec=grid_spec,
        compiler_params=_cparams(("arbitrary",)),
        name="moe_ffn",
    )(blk_e, n_used, xs, w1, w3, w2)


def _gather_combine_kernel(dest_ref, x_ref, mod_ref, w_ref, ys_hbm, o_ref, buf, sem):
    i = pl.program_id(0)
    n_steps = pl.num_programs(0)
    slot = i % 2
    tm = x_ref.shape[0]

    def start_gather(tile, s):
        def body(r, c):
            a = (tile * tm + r) * TOP_K
            for k in range(TOP_K):
                pltpu.make_async_copy(ys_hbm.at[pl.ds(dest_ref[a + k], 1)], buf.at[s, k, pl.ds(r, 1)],
                                      sem.at[s]).start(priority=k % 2)
            return c
        lax.fori_loop(0, tm, body, 0, unroll=4)

    @pl.when(i == 0)
    def _():
        start_gather(0, 0)

    @pl.when(i + 1 < n_steps)
    def _():
        start_gather(i + 1, 1 - slot)

    for k in range(TOP_K):
        pltpu.make_async_copy(ys_hbm.at[pl.ds(0, tm)], buf.at[slot, k], sem.at[slot]).wait()
    w = w_ref[...]
    f = w[:, 0:1] * buf[slot, 0] + w[:, 1:2] * buf[slot, 1]
    o_ref[...] = x_ref[...] + mod_ref[0][5:6] * f


def _gather_combine(dest, xn, mod3, wts, ys, n_rows, group_of_tile):
    d = xn.shape[1]
    grid_spec = pltpu.PrefetchScalarGridSpec(
        num_scalar_prefetch=1,
        grid=(n_rows // TM,),
        in_specs=[pl.BlockSpec((TM, d), lambda i, de: (i, 0)),
                  pl.BlockSpec((1, 8, d), lambda i, de: (group_of_tile(i), 0, 0)),
                  pl.BlockSpec((TM, TOP_K), lambda i, de: (i, 0)),
                  pl.BlockSpec(memory_space=pl.ANY)],
        out_specs=pl.BlockSpec((TM, d), lambda i, de: (i, 0)),
        scratch_shapes=[pltpu.VMEM((2, TOP_K, TM, d), F32), pltpu.SemaphoreType.DMA((2,))],
    )
    return pl.pallas_call(
        _gather_combine_kernel,
        out_shape=jax.ShapeDtypeStruct((n_rows, d), F32),
        grid_spec=grid_spec,
        compiler_params=_cparams(("arbitrary",)),
        name="moe_combine",
    )(dest, xn, mod3, wts, ys)


def _rope_tables(seq, dim, pad):
    t = jnp.arange(seq)
    rows = (t // GRID_W).astype(F32)
    cols = (t % GRID_W).astype(F32)
    quarter = dim // 4
    inv_freq = jnp.exp(-math.log(ROPE_BASE) * jnp.arange(quarter, dtype=F32) / quarter)
    ang = jnp.concatenate([inv_freq[:, None] * rows[None, :], inv_freq[:, None] * cols[None, :]], axis=0)
    cos = jnp.concatenate([jnp.cos(ang), jnp.ones((dim // 2, pad), F32)], axis=1)
    sin = jnp.concatenate([jnp.sin(ang), jnp.zeros((dim // 2, pad), F32)], axis=1)
    return cos, sin


def _win_mask_table(seq):
    nkb = seq // LANES
    nq = seq // WIN_TQ
    tabs = []
    for i in (0, 1, nq - 1):
        t = i * WIN_TQ + np.arange(WIN_TQ)[:, None]
        blk = np.clip(2 * i - 1 + np.arange(4), 0, nkb - 1)
        want = 2 * i - 1 + np.arange(4)
        s = (blk[:, None] * LANES + np.arange(LANES)[None, :]).reshape(-1)[None, :]
        ok = (np.abs(t - s) <= WIN_RADIUS) & np.repeat(blk == want, LANES)[None, :]
        tabs.append(np.where(ok, 0.0, NEG))
    return jnp.asarray(np.stack(tabs), F32)


def _nat_bias_table(rpb, seq, shift):
    rows = seq // GRID_W
    nq = rows // NAT_ROWS_PER_STEP
    wc = NAT_WIN_COLS
    col = np.arange(GRID_W)
    col_start = np.clip(col - wc // 2, 0, GRID_W - wc)
    col_ok = (col[None, :] >= col_start[:, None]) & (col[None, :] < col_start[:, None] + wc)
    d_col = np.clip(col[None, :] - col[:, None] + (wc - 1), 0, 2 * wc - 2)
    sel_r, oks = [], []
    for i in (0, 1, nq - 1):
        r0 = NAT_ROWS_PER_STEP * i
        ws = np.clip(r0 - NAT_WIN_ROWS // 2, 0, rows - NAT_KEY_ROWS)
        r = r0 + np.arange(NAT_ROWS_PER_STEP)
        rs = np.clip(r - NAT_WIN_ROWS // 2, 0, rows - NAT_WIN_ROWS)
        krow = ws + np.arange(NAT_KEY_ROWS)
        row_ok = (krow[None, :] >= rs[:, None]) & (krow[None, :] < rs[:, None] + NAT_WIN_ROWS)
        d_row = np.clip(krow[None, :] - r[:, None] + (NAT_WIN_ROWS - 1), 0, 2 * NAT_WIN_ROWS - 2)
        oks.append(row_ok[:, None, :, None] & col_ok[None, :, None, :])
        sel_r.append(d_row[:, :, None] == np.arange(2 * NAT_WIN_ROWS - 1)[None, None, :])
    sel_r = jnp.asarray(np.stack(sel_r), F32)
    sel_c = jnp.asarray(d_col[:, :, None] == np.arange(2 * wc - 1)[None, None, :], F32)
    cols = jnp.einsum('hrc,vkc->hrvk', rpb.astype(F32), sel_c, precision=lax.Precision.HIGHEST)
    bias = jnp.einsum('tuar,hrvk->thuvak', sel_r, cols, precision=lax.Precision.HIGHEST)
    bias = jnp.where(jnp.asarray(np.stack(oks))[:, None], bias - shift, NEG)
    return bias.reshape(3, rpb.shape[0], NAT_ROWS_PER_STEP * GRID_W, NAT_KEY_ROWS * GRID_W)


def _bcast_rows(v, reps=1):
    return jnp.tile(jnp.broadcast_to(v.astype(F32)[:, None], (v.shape[0], LANES)), (reps, 1))


def _layer_params(l, p):
    w = p['w_in'][l]
    sizes = (512, 128, 128, 512, 512, 512, 512, 512, 512, 256, 160, 4096)
    offs = np.concatenate([[0], np.cumsum(sizes)])
    seg = lambda k: w[:, offs[k]:offs[k + 1]]
    wq, wk, wv, dq, dk, dv, nq, nk, nv, mqa, mkva, gates = [seg(k) for k in range(12)]
    d = w.shape[0]
    dup = lambda m: jnp.concatenate([m[:, :64], m[:, :64], m[:, 64:], m[:, 64:]], axis=1)
    mkva_p = jnp.concatenate([mkva, jnp.zeros((d, 256 - mkva.shape[1]), F32)], axis=1)
    w_a = jnp.concatenate([gates, dv, nv, mqa, mkva_p, dup(wv)], axis=1).astype(BF)
    w_bt = jnp.concatenate([wq, dq, nq, dk, nk, dup(wk)], axis=1).T.astype(BF)
    scale = HEAD_DIM ** -0.5 * LOG2E
    gain_b = jnp.concatenate([
        _bcast_rows(p['win_q_norm'][l] * scale, 8), _bcast_rows(p['dif_q_norm'][l] * scale, 8),
        _bcast_rows(p['nat_q_norm'][l] * scale, 8), _bcast_rows(p['dif_k_norm'][l], 8),
        _bcast_rows(p['nat_k_norm'][l], 8), _bcast_rows(p['win_k_norm'][l], 4)], axis=0)
    wkv = p['mla_wkv_b'][l].reshape(MLA_KV_LORA, MLA_HEADS, MLA_NOPE + MLA_V)
    wk_t = wkv[:, :, :MLA_NOPE].reshape(MLA_KV_LORA, -1).T.astype(BF)
    wv_m = wkv[:, :, MLA_NOPE:].reshape(MLA_KV_LORA, -1).astype(BF)
    def logit_bound(gq, gk, dim):
        return dim * jnp.max(jnp.abs(gq)) * jnp.max(jnp.abs(gk)) * (1.0 + 2.0 ** -7)

    lam_f = p['dif_lambda'][l].astype(F32)
    lam_init = 0.8 - 0.6 * math.exp(-0.3 * l)
    lam = jnp.exp(jnp.sum(lam_f[0] * lam_f[1])) - jnp.exp(jnp.sum(lam_f[2] * lam_f[3])) + lam_init
    return dict(
        w_a=w_a, w_bt=w_bt, gain_b=gain_b,
        g1=p['norm1_g'][l].reshape(1, d), g2=p['norm2_g'][l].reshape(1, d),
        sink=(p['win_sink'][l].astype(F32) * LOG2E).reshape(1, WIN_HEADS),
        gqa=p['mla_q_a_norm'][l].reshape(1, -1), gkva=p['mla_kv_a_norm'][l].reshape(1, -1),
        wq_t=p['mla_wq_b'][l].T.astype(BF), wk_t=wk_t, wv_m=wv_m,
        gq=_bcast_rows(p['mla_q_norm'][l] * (MLA_QK ** -0.5 * LOG2E)), gk=_bcast_rows(p['mla_k_norm'][l]),
        lam=lam.reshape(1, 1).astype(F32), lam_scale=1.0 - lam_init,
        m_win=logit_bound(p['win_q_norm'][l] * scale, p['win_k_norm'][l], HEAD_DIM),
        m_dif=logit_bound(p['dif_q_norm'][l] * scale, p['dif_k_norm'][l], HEAD_DIM),
        m_nat=logit_bound(p['nat_q_norm'][l] * scale, p['nat_k_norm'][l], HEAD_DIM),
        m_mla=logit_bound(p['mla_q_norm'][l] * (MLA_QK ** -0.5 * LOG2E), p['mla_k_norm'][l], MLA_QK),
        subln=p['dif_subln'][l].astype(F32).reshape(1, DIF_V_DIM),
        wb=p['w_branch'][l].astype(BF), wo=p['w_out'][l].astype(BF),
        w1=p['moe_w1'][l].astype(BF), w3=p['moe_w3'][l].astype(BF), w2=p['moe_w2'][l].astype(BF),
    )


def kernel(x, c, ctx, c_ctx, ada_w, ada_b, norm1_g, norm2_g, w_in, win_q_norm, win_k_norm, win_sink,
           dif_q_norm, dif_k_norm, dif_lambda, dif_subln, nat_q_norm, nat_k_norm, nat_rpb,
           mla_q_a_norm, mla_wq_b, mla_kv_a_norm, mla_wkv_b, mla_q_norm, mla_k_norm,
           w_branch, w_out, router_w, router_b, moe_w1, moe_w3, moe_w2):
    p = dict(norm1_g=norm1_g, norm2_g=norm2_g, w_in=w_in, win_q_norm=win_q_norm, win_k_norm=win_k_norm,
             win_sink=win_sink, dif_q_norm=dif_q_norm, dif_k_norm=dif_k_norm, dif_lambda=dif_lambda,
             dif_subln=dif_subln, nat_q_norm=nat_q_norm, nat_k_norm=nat_k_norm,
             mla_q_a_norm=mla_q_a_norm, mla_wq_b=mla_wq_b, mla_kv_a_norm=mla_kv_a_norm,
             mla_wkv_b=mla_wkv_b, mla_q_norm=mla_q_norm, mla_k_norm=mla_k_norm,
             w_branch=w_branch, w_out=w_out, moe_w1=moe_w1, moe_w3=moe_w3, moe_w2=moe_w2)
    n_batch, seq, d = x.shape
    n_ctx = ctx.shape[1]
    depth = ada_w.shape[0]
    n_lat = n_batch * seq
    n_all = n_lat + n_batch * n_ctx
    assert seq % DENSE_TK == 0 and seq % DENSE_TQ == 0 and (n_batch * n_ctx) == TM and seq % TM == 0
    tiles_per_batch = seq // TM
    group_of_tile = lambda i: jnp.minimum(i // tiles_per_batch, n_batch)
    pos_of_tile = lambda i: jnp.where(i < n_batch * tiles_per_batch, i % tiles_per_batch, tiles_per_batch)

    cc = jnp.concatenate([c, c_ctx[None, :], jnp.zeros((8 - n_batch - 1, d), F32)], axis=0)
    mod = _modulation(cc, ada_w, ada_b)
    mod = mod[:, :n_batch + 1].reshape(depth, n_batch + 1, 6, d)
    mod = jnp.pad(mod, ((0, 0), (0, 0), (0, 2), (0, 0)))

    cos_h, sin_h = _rope_tables(seq, HEAD_DIM, TM)
    cos_m, sin_m = _rope_tables(seq, MLA_ROPE, TM)
    win_mask = _win_mask_table(seq)
    rw = jnp.pad(router_w.astype(F32), ((0, 0), (0, LANES - N_EXPERTS)))
    rw_hi = rw.astype(BF)
    rw = jnp.stack([rw_hi, (rw - rw_hi.astype(F32)).astype(BF)])

    xa = jnp.concatenate([x.reshape(n_lat, d), ctx.reshape(n_batch * n_ctx, d)], axis=0)
    ones = jnp.ones((DENSE_TK, LANES), BF)

    def dense(lam, logit_bound, q_arr, q_col, q_w, k_arr, k_row, k_w, v_arr, v_col, v_w, subln, **kw):
        operands = (q_arr, q_col, q_w, k_arr, k_row, k_w, v_arr, v_col, v_w, subln)
        par = jnp.concatenate([lam, logit_bound.reshape(1, 1).astype(F32)], axis=1)
        y, flag = _dense_fixed_attention(par, *operands, ones, n_batch, seq, n_ctx, **kw)
        return lax.cond(jnp.max(flag) > 0.0,
                        lambda: _dense_attention(lam, *operands, n_batch, seq, n_ctx, **kw),
                        lambda: y)
    for l in range(depth):
        lp = _layer_params(l, p)
        want_ctx = l < depth - 1
        mod3 = mod[l]
        tok_a = _in_proj_a(xa, mod3, lp['g1'], lp['w_a'], group_of_tile)
        q_tok, kt = _in_proj_b(xa, mod3, lp['g1'], lp['w_bt'], lp['gain_b'], cos_h, sin_h,
                               group_of_tile, pos_of_tile)
        mq, mkt, mv = _mla_prep(tok_a, lp['gqa'], lp['gkva'], lp['wq_t'], lp['wk_t'], lp['wv_m'],
                                lp['gq'], lp['gk'], cos_m, sin_m, pos_of_tile)
        rpb2 = nat_rpb[l].astype(F32) * LOG2E
        m_nl = lp['m_nat'] + jnp.maximum(jnp.max(rpb2), 0.0)
        nat_bias_rel = _nat_bias_table(rpb2, seq, m_nl)

        def branches(latent):
            win_args = (q_tok, kt, tok_a)
            if latent:
                m_win = lp['m_win']
                sink_rel = jnp.concatenate([lp['sink'] - m_win, jnp.full((1, 8), m_win, F32)], axis=1)
                y_win, flag = _win_attention(sink_rel, *win_args, win_mask - m_win, n_batch, seq, n_ctx, True,
                                             ones=ones)
                y_win = lax.cond(jnp.max(flag) > 0.0,
                                 lambda: _win_attention(lp['sink'], *win_args, win_mask, n_batch, seq, n_ctx, True),
                                 lambda: y_win)
            else:
                y_win = _win_attention(lp['sink'], *win_args, win_mask, n_batch, seq, n_ctx, False)
            y_dif = dense(lp['lam'], lp['m_dif'], q_tok, 1, 512, kt, KT_DK // 512, 512, tok_a, A_DV // 512, 512,
                          lp['subln'], n_heads=2 * DIF_HEADS, packed=True, diff=True, latent=latent,
                          lam_scale=lp['lam_scale'], name="dif_attn" if latent else "dif_attn_ctx")
            if latent:
                y_nat, flag = _nat_attention(q_tok, kt, tok_a, nat_bias_rel, n_batch, seq, n_ctx,
                                             par=m_nl.reshape(1, 1).astype(F32), ones=ones)
                y_nat = lax.cond(jnp.max(flag) > 0.0,
                                 lambda: _nat_attention(q_tok, kt, tok_a, _nat_bias_table(rpb2, seq, 0.0),
                                                        n_batch, seq, n_ctx),
                                 lambda: y_nat)
            else:
                y_nat = dense(lp['lam'], lp['m_nat'], q_tok, 2, 512, kt, KT_NK // 512, 512, tok_a, A_NV // 512, 512,
                              None, n_heads=NAT_HEADS, packed=True, diff=False, latent=False,
                              lam_scale=1.0, name="nat_attn_ctx")
            y_mla = dense(lp['lam'], lp['m_mla'], mq, 0, 1024, mkt, 0, 1024, mv, 0, 512, None,
                          n_heads=MLA_HEADS, packed=False, diff=False, latent=latent, lam_scale=1.0,
                          name="mla_attn" if latent else "mla_attn_ctx")
            return [y_win, y_dif, y_nat, y_mla]

        ys = branches(True)
        ys_c = branches(False) if want_ctx else None
        n_rows = n_all if want_ctx else n_lat
        xn, h2, scores = _merge(xa, mod3, lp['g2'], ys, ys_c, tok_a, lp['wb'], lp['wo'], rw, n_rows,
                                group_of_tile)
        blk_e, dest, wts, pad_lo, pad_hi, n_used, cap = _route(scores[:, :N_EXPERTS], router_b, n_rows)
        xs = _dispatch(dest, pad_lo, pad_hi, h2, cap)
        ys = _grouped_ffn(blk_e, n_used, xs, lp['w1'], lp['w3'], lp['w2'])
        xa = _gather_combine(dest, xn, mod3, wts, ys, n_rows, group_of_tile)
    return xa[:n_lat].reshape(n_batch, seq, d)
```
